```python
import math
import jax
import jax.numpy as jnp
from jax import lax
import numpy as np

D_MODEL = 2048
BATCH = 2
SEQ = 4096
DEPTH = 4
DEC_BATCH = 8
DEC_SEQ = 4
PAST_LEN = 16384
PAGE_SIZE = 128

N_MIXERS = 3
N_A = (DEPTH + 2) // 3
N_B = (DEPTH + 1) // 3
N_C = DEPTH // 3
D_FF = 5632
NORM_EPS = 1e-6

A_HEAD_DIM = 64
A_HEADS = D_MODEL // A_HEAD_DIM
A_DECAY_LORA = 96
A_ICL_LORA = 96
A_VRES_LORA = 64
A_GATE_LORA = 256
A_GN_EPS = 64e-5

B_WINDOWS = (128, 512, 2048)
B_DILATIONS = (1, 4, 16)
B_GROUPS = 3
B_HEADS = 16
B_HEAD_DIM = 128
B_INNER = B_HEADS * B_HEAD_DIM
N_BUCKETS = 32
BUCKET_MAX_DIST = 2048

C_HEADS = 8
C_QK_DIM = 128
C_V_DIM = 256
C_CHUNK = 64
C_GATE_CAP = 15.0
C_IN = C_HEADS * (2 * C_QK_DIM + 2 * C_V_DIM) + 2 * C_HEADS

kernel_name = 'hybrid_rwkv7_dilated_mlstm_step'


def rms_norm(x, g):
    xf = x.astype(jnp.float32)
    y = xf * lax.rsqrt(jnp.mean(xf * xf, axis=-1, keepdims=True) + NORM_EPS)
    return (y * g.astype(jnp.float32)).astype(x.dtype)


def swiglu(h, w_in, w_out):
    gate, up = jnp.split(h @ w_in, 2, axis=-1)
    return (jax.nn.silu(gate) * up) @ w_out


def rel_bucket(dist):
    exact = N_BUCKETS // 2
    d = jnp.maximum(dist, 1).astype(jnp.float32)
    log_b = exact + (jnp.log(d / exact) / math.log(BUCKET_MAX_DIST / exact) * (N_BUCKETS - exact)).astype(jnp.int32)
    return jnp.where(dist < exact, dist, jnp.minimum(log_b, N_BUCKETS - 1))


def rwkv7_recurrence(r, decay, k, v, kk, a, S0):
    def step(S, inp):
        r_t, w_t, k_t, v_t, kk_t, a_t = inp
        removal = jnp.einsum('bhvk,bhk->bhv', S, kk_t)
        S = (S * w_t[:, :, None, :]
             - removal[..., None] * (kk_t * a_t)[:, :, None, :]
             + v_t[..., None] * k_t[:, :, None, :])
        return S, jnp.einsum('bhvk,bhk->bhv', S, r_t)
    xs = tuple(jnp.moveaxis(t, 1, 0) for t in (r, decay, k, v, kk, a))
    S, ys = lax.scan(step, S0, xs)
    return jnp.moveaxis(ys, 0, 1), S


def rwkv7_mix(u, shift_prev, S0, j, v_first, P):
    Bn, T, D = u.shape
    H, N = A_HEADS, A_HEAD_DIM
    f32 = jnp.float32
    u_prev = jnp.concatenate([shift_prev[:, None, :].astype(u.dtype), u[:, :-1]], axis=1)
    du = u_prev - u
    mu = P['a_mu'][j]
    xr, xw, xk, xv, xa, xg = (u + du * mu[n] for n in range(6))
    r, k, v = jnp.einsum('nbtd,nde->nbte', jnp.stack([xr, xk, xv]), P['a_w_rkv'][j])
    w_pre = (P['a_w0'][j] + jnp.tanh(xw @ P['a_w1'][j]) @ P['a_w2'][j]).astype(f32)
    decay = jnp.exp(-jnp.exp(-jax.nn.softplus(-w_pre) - 0.5))
    a = jax.nn.sigmoid(P['a_a0'][j] + (xa @ P['a_a1'][j]) @ P['a_a2'][j])
    g = jax.nn.sigmoid(xg @ P['a_g1'][j]) @ P['a_g2'][j]
    if j == 0:
        v_first = v
    else:
        nu = jax.nn.sigmoid(P['a_v0'][j - 1] + (xv @ P['a_v1'][j - 1]) @ P['a_v2'][j - 1])
        v = v + (v_first - v) * nu
    heads = lambda t: t.reshape(Bn, T, H, N).astype(f32)
    kk = heads(k * P['a_k_k'][j])
    kk = kk / jnp.maximum(jnp.linalg.norm(kk, axis=-1, keepdims=True), 1e-12)
    k = k * (1 + (a - 1) * P['a_k_a'][j])
    rh, kh, vh, ah = heads(r), heads(k), heads(v), heads(a)
    y, S = rwkv7_recurrence(rh, heads(decay), kh, vh, kk, ah, S0.astype(f32))
    mean = jnp.mean(y, axis=-1, keepdims=True)
    var = jnp.mean(jnp.square(y - mean), axis=-1, keepdims=True)
    y = ((y - mean) * lax.rsqrt(var + A_GN_EPS)).reshape(Bn, T, D) * P['a_gn_w'][j] + P['a_gn_b'][j]
    bonus = jnp.sum(rh * kh * P['a_r_k'][j].astype(f32), axis=-1, keepdims=True) * vh
    y = (y + bonus.reshape(Bn, T, D)).astype(u.dtype) * g
    return y @ P['a_w_out'][j], u[:, -1], S, v_first


def dilated_group_prompt(q, k, v, dil, win, table):
    Bn, S, H, Dh = q.shape
    f32 = jnp.float32
    blk = win // dil
    span = blk * dil
    nb = -(-S // span)
    pad = nb * span - S

    def blocks(t):
        return jnp.pad(t, ((0, 0), (0, pad), (0, 0), (0, 0))).reshape(Bn, nb, blk, dil, H, Dh)

    def with_prev(t):
        prev = jnp.pad(t, ((0, 0), (1, 0), (0, 0), (0, 0), (0, 0), (0, 0)))[:, :-1]
        return jnp.concatenate([prev, t], axis=2)

    qb, kb, vb = blocks(q), with_prev(blocks(k)), with_prev(blocks(v))
    logits = jnp.einsum('bnidhe,bnjdhe->bndhij', qb, kb, preferred_element_type=f32)
    qi = jnp.arange(blk)[:, None]
    kj = jnp.arange(2 * blk)[None, :]
    step = qi + blk - kj
    bias = table[rel_bucket(jnp.clip(step, 0, blk) * dil)].astype(f32)
    logits = logits + jnp.transpose(bias, (2, 0, 1))
    valid = ((step >= 0) & (step <= blk))[None] & ((jnp.arange(nb)[:, None, None] > 0) | (kj >= blk)[None])
    logits = jnp.where(valid[None, :, None, None], logits, -jnp.inf)
    mx = jnp.max(logits, axis=-1, keepdims=True)
    p = jnp.exp(logits - mx)
    den = jnp.sum(p, axis=-1)
    o = jnp.einsum('bndhij,bnjdhe->bnidhe', p, vb.astype(f32))
    o = o / jnp.transpose(den, (0, 1, 4, 2, 3))[..., None]
    lse = jnp.transpose(mx[..., 0] + jnp.log(den), (0, 1, 4, 2, 3))
    return o.reshape(Bn, nb * span, H, Dh)[:, :S], lse.reshape(Bn, nb * span, H)[:, :S]


def dilated_group_sample(q, k, v, kv_buf, dil, win, table):
    f32 = jnp.float32
    L = kv_buf.shape[1]
    T = q.shape[1]
    steps = win // dil
    kall = jnp.concatenate([kv_buf[:, :, 0].astype(k.dtype), k], axis=1)
    vall = jnp.concatenate([kv_buf[:, :, 1].astype(v.dtype), v], axis=1)
    ms = jnp.arange(steps + 1)
    idx = L + jnp.arange(T)[:, None] - dil * ms[None, :]
    valid = idx >= 0
    idx = jnp.maximum(idx, 0)
    kg, vg = kall[:, idx], vall[:, idx]
    logits = jnp.einsum('bthe,btmhe->bthm', q, kg, preferred_element_type=f32)
    bias = table[rel_bucket(dil * ms)].astype(f32).T
    logits = jnp.where(valid[None, :, None, :], logits + bias, -jnp.inf)
    mx = jnp.max(logits, axis=-1, keepdims=True)
    p = jnp.exp(logits - mx)
    den = jnp.sum(p, axis=-1)
    o = jnp.einsum('bthm,btmhe->bthe', p, vg.astype(f32)) / den[..., None]
    return o, mx[..., 0] + jnp.log(den)


def dilated_mix(u, j, kv_bufs, P):
    Bn, T, _ = u.shape
    G, H, Dh = B_GROUPS, B_HEADS, B_HEAD_DIM
    qkv = (u @ P['b_w_qkv'][j]).reshape(Bn, T, G, 3, H, Dh)
    outs, lses, new_kv = [], [], []
    for g in range(G):
        q = qkv[:, :, g, 0] * (Dh ** -0.5)
        k, v = qkv[:, :, g, 1], qkv[:, :, g, 2]
        tab = P['rel_bias'][:, g * H:(g + 1) * H]
        if kv_bufs is None:
            o, lse = dilated_group_prompt(q, k, v, B_DILATIONS[g], B_WINDOWS[g], tab)
            keep = min(B_WINDOWS[g], T)
            new_kv.append(jnp.stack([k[:, T - keep:], v[:, T - keep:]], axis=2))
        else:
            o, lse = dilated_group_sample(q, k, v, kv_bufs[g], B_DILATIONS[g], B_WINDOWS[g], tab)
            new_kv.append(jnp.stack([k, v], axis=2))
        outs.append(o)
        lses.append(lse)
    wts = jax.nn.softmax(jnp.stack(lses), axis=0)
    o = jnp.sum(wts[..., None] * jnp.stack(outs), axis=0)
    y = o.reshape(Bn, T, B_INNER).astype(u.dtype) @ P['b_w_out'][j]
    return y, new_kv


def mlstm_chunk_step(carry, xs):
    C, n, m = carry
    q, k, v, ig, lf = xs
    L = q.shape[1]
    b = jnp.cumsum(lf, axis=1)
    causal = jnp.tril(jnp.ones((L, L), dtype=bool))
    Dm = b[:, :, None, :] - b[:, None, :, :] + ig[:, None, :, :]
    Dm = jnp.where(causal[None, :, :, None], Dm, -jnp.inf)
    inter = b + m[:, None, :]
    mt = jnp.maximum(inter, jnp.max(Dm, axis=2))
    wD = jnp.exp(Dm - mt[:, :, None, :])
    wI = jnp.exp(inter - mt)
    sc = jnp.einsum('bthe,bshe->btsh', q, k) * wD
    num = jnp.einsum('btsh,bshv->bthv', sc, v) + wI[..., None] * jnp.einsum('bhve,bthe->bthv', C, q)
    den = jnp.sum(sc, axis=2) + wI * jnp.einsum('bhe,bthe->bth', n, q)
    h = num / jnp.maximum(jnp.abs(den), jnp.exp(-mt))[..., None]
    m_new = mt[:, -1]
    wS = jnp.exp(b[:, -1:] - b + ig - m_new[:, None])
    dec = jnp.exp(b[:, -1] + m - m_new)
    C_new = dec[..., None, None] * C + jnp.einsum('bsh,bshv,bshe->bhve', wS, v, k)
    n_new = dec[..., None] * n + jnp.einsum('bsh,bshe->bhe', wS, k)
    return (C_new, n_new, m_new), h


def mlstm_mix(u, C0, n0, m0, j, P):
    Bn, T, _ = u.shape
    H, E, V = C_HEADS, C_QK_DIM, C_V_DIM
    f32 = jnp.float32
    proj = u @ P['c_w_in'][j]
    q, k, v, o, ig, fg = jnp.split(proj, [H * E, 2 * H * E, 2 * H * E + H * V, 2 * H * E + 2 * H * V, 2 * H * E + 2 * H * V + H], axis=-1)
    q = q.reshape(Bn, T, H, E).astype(f32)
    k = k.reshape(Bn, T, H, E).astype(f32) * (E ** -0.5)
    v = v.reshape(Bn, T, H, V).astype(f32)
    gb = P['c_b_gates'][j].astype(f32)
    ig = C_GATE_CAP * jnp.tanh((ig.astype(f32) + gb[:H]) / C_GATE_CAP)
    fg = C_GATE_CAP * jnp.tanh((fg.astype(f32) + gb[H:]) / C_GATE_CAP)
    lf = jax.nn.log_sigmoid(fg)
    L = min(C_CHUNK, T)
    nch = -(-T // L)
    pad = nch * L - T

    def padt(t, val):
        return jnp.pad(t, [(0, 0), (0, pad)] + [(0, 0)] * (t.ndim - 2), constant_values=val)

    def chunks(t):
        return jnp.moveaxis(t.reshape((Bn, nch, L) + t.shape[2:]), 1, 0)

    xs = (chunks(padt(q, 0.0)), chunks(padt(k, 0.0)), chunks(padt(v, 0.0)), chunks(padt(ig, -1e30)), chunks(padt(lf, 0.0)))
    (C, n, m), h = lax.scan(mlstm_chunk_step, (C0.astype(f32), n0.astype(f32), m0.astype(f32)), xs)
    h = jnp.moveaxis(h, 0, 1).reshape(Bn, nch * L, H, V)[:, :T]
    h = h * lax.rsqrt(jnp.mean(h * h, axis=-1, keepdims=True) + NORM_EPS)
    h = h.reshape(Bn, T, H * V) * P['c_norm_w'][j]
    y = (h.astype(u.dtype) * jax.nn.sigmoid(o)) @ P['c_w_out'][j]
    return y, C, n, m


def trunk(x, a_wkv, a_shift, b_kv, c_C, c_n, c_m, P):
    dt = x.dtype
    new_a_wkv, new_a_shift, new_c_C, new_c_n, new_c_m = [], [], [], [], []
    new_b = [[] for _ in range(B_GROUPS)]
    v_first = None
    for i in range(DEPTH):
        x = x + 0.5 * swiglu(rms_norm(x, P['norm_ffn1'][i]), P['ffn1_w_in'][i], P['ffn1_w_out'][i])
        u = rms_norm(x, P['norm_mix'][i])
        kind, j = i % N_MIXERS, i // N_MIXERS
        if kind == 0:
            y, shift, S, v_first = rwkv7_mix(u, a_shift[j], a_wkv[j], j, v_first, P)
            new_a_wkv.append(S.astype(dt))
            new_a_shift.append(shift.astype(dt))
        elif kind == 1:
            bufs = None if b_kv is None else [buf[j] for buf in b_kv]
            y, kvs = dilated_mix(u, j, bufs, P)
            for g in range(B_GROUPS):
                new_b[g].append(kvs[g].astype(dt))
        else:
            y, C, n, m = mlstm_mix(u, c_C[j], c_n[j], c_m[j], j, P)
            new_c_C.append(C.astype(dt))
            new_c_n.append(n.astype(dt))
            new_c_m.append(m.astype(dt))
        x = x + y
        x = x + 0.5 * swiglu(rms_norm(x, P['norm_ffn2'][i]), P['ffn2_w_in'][i], P['ffn2_w_out'][i])
    return (rms_norm(x, P['norm_final']), jnp.stack(new_a_wkv), jnp.stack(new_a_shift),
            [jnp.stack(bk) for bk in new_b], jnp.stack(new_c_C), jnp.stack(new_c_n), jnp.stack(new_c_m))


def setup_inputs(seed: int = 0) -> dict:
    key = jax.random.key(seed)
    ks = iter(jax.random.split(key, 64))
    f32 = jnp.float32
    D = D_MODEL

    def nrm(shape, scale=1.0):
        return scale * jax.random.normal(next(ks), shape, f32)

    def gain(shape):
        return 1.0 + nrm(shape, 0.02)

    Lg = [min(w, PAST_LEN) for w in B_WINDOWS]
    f_bias = jnp.broadcast_to(jnp.linspace(3.0, 6.0, C_HEADS, dtype=f32), (N_C, C_HEADS)) + nrm((N_C, C_HEADS), 0.1)
    c_b_gates = jnp.concatenate([nrm((N_C, C_HEADS), 0.1), f_bias], axis=-1)
    return {
        'x_prompt': nrm((BATCH, SEQ, D)),
        'x_sample': nrm((DEC_BATCH, DEC_SEQ, D)),
        'state_a_wkv': nrm((N_A, DEC_BATCH, A_HEADS, A_HEAD_DIM, A_HEAD_DIM), 0.3),
        'state_a_shift': nrm((N_A, DEC_BATCH, D)),
        'cache_b_kv_g0': nrm((N_B, DEC_BATCH, Lg[0], 2, B_HEADS, B_HEAD_DIM)),
        'cache_b_kv_g1': nrm((N_B, DEC_BATCH, Lg[1], 2, B_HEADS, B_HEAD_DIM)),
        'cache_b_kv_g2': nrm((N_B, DEC_BATCH, Lg[2], 2, B_HEADS, B_HEAD_DIM)),
        'state_c_C': nrm((N_C, DEC_BATCH, C_HEADS, C_V_DIM, C_QK_DIM), 0.3),
        'state_c_n': nrm((N_C, DEC_BATCH, C_HEADS, C_QK_DIM), 0.3),
        'state_c_m': nrm((N_C, DEC_BATCH, C_HEADS)),
        'rel_bias': nrm((N_BUCKETS, B_GROUPS * B_HEADS), 0.5),
        'norm_ffn1': gain((DEPTH, D)),
        'ffn1_w_in': nrm((DEPTH, D, 2 * D_FF), D ** -0.5),
        'ffn1_w_out': nrm((DEPTH, D_FF, D), D_FF ** -0.5),
        'norm_mix': gain((DEPTH, D)),
        'norm_ffn2': gain((DEPTH, D)),
        'ffn2_w_in': nrm((DEPTH, D, 2 * D_FF), D ** -0.5),
        'ffn2_w_out': nrm((DEPTH, D_FF, D), D_FF ** -0.5),
        'norm_final': gain((D,)),
        'a_mu': jax.random.uniform(next(ks), (N_A, 6, D), f32),
        'a_w_rkv': nrm((N_A, 3, D, D), D ** -0.5),
        'a_w0': jnp.linspace(-6.5, -1.0, D, dtype=f32) + nrm((N_A, D), 0.1),
        'a_w1': nrm((N_A, D, A_DECAY_LORA), D ** -0.5),
        'a_w2': nrm((N_A, A_DECAY_LORA, D), 0.1 * A_DECAY_LORA ** -0.5),
        'a_a0': nrm((N_A, D), 0.1),
        'a_a1': nrm((N_A, D, A_ICL_LORA), D ** -0.5),
        'a_a2': nrm((N_A, A_ICL_LORA, D), 0.3 * A_ICL_LORA ** -0.5),
        'a_g1': nrm((N_A, D, A_GATE_LORA), D ** -0.5),
        'a_g2': nrm((N_A, A_GATE_LORA, D), A_GATE_LORA ** -0.5),
        'a_k_k': 0.85 + nrm((N_A, D), 0.02),
        'a_k_a': 1.0 + nrm((N_A, D), 0.02),
        'a_r_k': nrm((N_A, A_HEADS, A_HEAD_DIM), 0.1),
        'a_gn_w': gain((N_A, D)),
        'a_gn_b': nrm((N_A, D), 0.02),
        'a_w_out': nrm((N_A, D, D), D ** -0.5),
        'a_v0': 1.0 + nrm((N_A - 1, D), 0.1),
        'a_v1': nrm((N_A - 1, D, A_VRES_LORA), D ** -0.5),
        'a_v2': nrm((N_A - 1, A_VRES_LORA, D), 0.1 * A_VRES_LORA ** -0.5),
        'b_w_qkv': nrm((N_B, D, 3 * B_GROUPS * B_INNER), D ** -0.5),
        'b_w_out': nrm((N_B, B_INNER, D), B_INNER ** -0.5),
        'c_w_in': nrm((N_C, D, C_IN), D ** -0.5),
        'c_b_gates': c_b_gates,
        'c_norm_w': gain((N_C, C_HEADS * C_V_DIM)),
        'c_w_out': nrm((N_C, C_HEADS * C_V_DIM, D), (C_HEADS * C_V_DIM) ** -0.5),
    }


def reference(x_prompt, x_sample, state_a_wkv, state_a_shift, cache_b_kv_g0, cache_b_kv_g1, cache_b_kv_g2,
              state_c_C, state_c_n, state_c_m, rel_bias, norm_ffn1, ffn1_w_in, ffn1_w_out, norm_mix,
              norm_ffn2, ffn2_w_in, ffn2_w_out, norm_final, a_mu, a_w_rkv, a_w0, a_w1, a_w2, a_a0, a_a1,
              a_a2, a_g1, a_g2, a_k_k, a_k_a, a_r_k, a_gn_w, a_gn_b, a_w_out, a_v0, a_v1, a_v2,
              b_w_qkv, b_w_out, c_w_in, c_b_gates, c_norm_w, c_w_out):
    P = dict(rel_bias=rel_bias, norm_ffn1=norm_ffn1, ffn1_w_in=ffn1_w_in, ffn1_w_out=ffn1_w_out,
             norm_mix=norm_mix, norm_ffn2=norm_ffn2, ffn2_w_in=ffn2_w_in, ffn2_w_out=ffn2_w_out,
             norm_final=norm_final, a_mu=a_mu, a_w_rkv=a_w_rkv, a_w0=a_w0, a_w1=a_w1, a_w2=a_w2,
             a_a0=a_a0, a_a1=a_a1, a_a2=a_a2, a_g1=a_g1, a_g2=a_g2, a_k_k=a_k_k, a_k_a=a_k_a,
             a_r_k=a_r_k, a_gn_w=a_gn_w, a_gn_b=a_gn_b, a_w_out=a_w_out, a_v0=a_v0, a_v1=a_v1,
             a_v2=a_v2, b_w_qkv=b_w_qkv, b_w_out=b_w_out, c_w_in=c_w_in, c_b_gates=c_b_gates,
             c_norm_w=c_norm_w, c_w_out=c_w_out)
    Bp, dt = x_prompt.shape[0], x_prompt.dtype
    z_wkv = jnp.zeros((N_A, Bp, A_HEADS, A_HEAD_DIM, A_HEAD_DIM), dt)
    z_shift = jnp.zeros((N_A, Bp, D_MODEL), dt)
    z_C = jnp.zeros((N_C, Bp, C_HEADS, C_V_DIM, C_QK_DIM), dt)
    z_n = jnp.zeros((N_C, Bp, C_HEADS, C_QK_DIM), dt)
    z_m = jnp.zeros((N_C, Bp, C_HEADS), dt)
    y_prompt, p_wkv, p_shift, p_kv, p_C, p_n, p_m = trunk(x_prompt, z_wkv, z_shift, None, z_C, z_n, z_m, P)
    y_sample, s_wkv, s_shift, s_kv, s_C, s_n, s_m = trunk(
        x_sample, state_a_wkv, state_a_shift, [cache_b_kv_g0, cache_b_kv_g1, cache_b_kv_g2],
        state_c_C, state_c_n, state_c_m, P)
    return (y_prompt, y_sample, p_wkv, s_wkv, p_shift, s_shift, p_kv[0], s_kv[0], p_kv[1], s_kv[1],
            p_kv[2], s_kv[2], p_C, s_C, p_n, s_n, p_m, s_m)
```

```python
import functools
import math

import jax
import jax.numpy as jnp
from jax import lax
from jax.experimental import pallas as pl
from jax.experimental.pallas import tpu as pltpu

F32 = jnp.float32
BF16 = jnp.bfloat16
HIGHEST = lax.Precision.HIGHEST

NORM_EPS = 1e-6
A_HEAD_DIM = 64
A_GN_EPS = 64e-5
B_WINDOWS = (128, 512, 2048)
B_DILATIONS = (1, 4, 16)
B_HEADS = 16
B_HEAD_DIM = 128
B_BLK = 128
N_BUCKETS = 32
BUCKET_MAX_DIST = 2048
C_HEADS = 8
C_QK_DIM = 128
C_V_DIM = 256
C_GATE_CAP = 15.0
CHUNK = 64
SAMPLE_PAD = 64
LANES = 128
VMEM_LIMIT = 56 * 1024 * 1024


def _params(sem):
    return pltpu.CompilerParams(dimension_semantics=sem, vmem_limit_bytes=VMEM_LIMIT)


def _dot(a, b, precision=None):
    return jnp.dot(a, b, preferred_element_type=F32, precision=precision)


def _dot_nt(a, b, precision=None):
    return lax.dot_general(a, b, (((1,), (1,)), ((), ())), preferred_element_type=F32, precision=precision)


def _dot_tn(a, b, precision=None):
    return lax.dot_general(a, b, (((0,), (0,)), ((), ())), preferred_element_type=F32, precision=precision)


def _sigmoid(x):
    return 1.0 / (1.0 + jnp.exp(-x))


def _rms(x, g):
    ms = jnp.mean(x * x, axis=-1, keepdims=True)
    return x * lax.rsqrt(ms + NORM_EPS) * g


def _rmsnorm_body(x_ref, g_ref, o_ref):
    o_ref[...] = _rms(x_ref[...], g_ref[...]).astype(o_ref.dtype)


def rmsnorm(x, g, tm):
    M, D = x.shape
    return pl.pallas_call(
        _rmsnorm_body, grid=(M // tm,),
        in_specs=[pl.BlockSpec((tm, D), lambda i: (i, 0)), pl.BlockSpec((1, D), lambda i: (0, 0))],
        out_specs=pl.BlockSpec((tm, D), lambda i: (i, 0)),
        out_shape=jax.ShapeDtypeStruct((M, D), F32),
        compiler_params=_params(("parallel",)), name="rmsnorm",
    )(x, g.reshape(1, D))


def _ffn_body(x_ref, g_ref, wg_ref, wu_ref, wo_ref, o_ref, xn_ref, acc_ref):
    j = pl.program_id(1)

    @pl.when(j == 0)
    def _():
        xn_ref[...] = _rms(x_ref[...], g_ref[...]).astype(BF16)
        acc_ref[...] = jnp.zeros_like(acc_ref)

    xn = xn_ref[...]
    gate = _dot(xn, wg_ref[...])
    up = _dot(xn, wu_ref[...])
    h = (gate * _sigmoid(gate) * up).astype(BF16)
    acc_ref[...] += _dot(h, wo_ref[...])

    @pl.when(j == pl.num_programs(1) - 1)
    def _():
        o_ref[...] = x_ref[...] + 0.5 * acc_ref[...]


def ffn(x, g, w_in, w_out, tm, tf):
    M, D = x.shape
    Fh = w_out.shape[0]
    nf = Fh // tf
    return pl.pallas_call(
        _ffn_body, grid=(M // tm, nf),
        in_specs=[pl.BlockSpec((tm, D), lambda i, j: (i, 0)),
                  pl.BlockSpec((1, D), lambda i, j: (0, 0)),
                  pl.BlockSpec((D, tf), lambda i, j: (0, j)),
                  pl.BlockSpec((D, tf), lambda i, j: (0, j + nf)),
                  pl.BlockSpec((tf, D), lambda i, j: (j, 0))],
        out_specs=pl.BlockSpec((tm, D), lambda i, j: (i, 0)),
        out_shape=jax.ShapeDtypeStruct((M, D), F32),
        scratch_shapes=[pltpu.VMEM((tm, D), BF16), pltpu.VMEM((tm, D), F32)],
        compiler_params=_params(("parallel", "arbitrary")), name="ffn",
    )(x, g.reshape(1, D), w_in, w_in, w_out)


def _mm_body(*refs, has_norm, has_res):
    it = iter(refs)
    x_ref = next(it)
    g_ref = next(it) if has_norm else None
    w_ref = next(it)
    res_ref = next(it) if has_res else None
    o_ref = next(it)
    xs_ref = next(it)

    @pl.when(pl.program_id(1) == 0)
    def _():
        x = x_ref[...].astype(F32)
        if has_norm:
            x = _rms(x, g_ref[...])
        xs_ref[...] = x.astype(BF16)

    acc = _dot(xs_ref[...], w_ref[...])
    if has_res:
        acc = res_ref[...] + acc
    o_ref[...] = acc.astype(o_ref.dtype)


def matmul(x, w, tm, tn, norm_g=None, residual=None, out_dtype=F32):
    M, K = x.shape
    N = w.shape[1]
    args = [x]
    specs = [pl.BlockSpec((tm, K), lambda i, j: (i, 0))]
    if norm_g is not None:
        args.append(norm_g.reshape(1, K))
        specs.append(pl.BlockSpec((1, K), lambda i, j: (0, 0)))
    args.append(w)
    specs.append(pl.BlockSpec((K, tn), lambda i, j: (0, j)))
    if residual is not None:
        args.append(residual)
        specs.append(pl.BlockSpec((tm, tn), lambda i, j: (i, j)))
    return pl.pallas_call(
        functools.partial(_mm_body, has_norm=norm_g is not None, has_res=residual is not None),
        grid=(M // tm, N // tn), in_specs=specs,
        out_specs=pl.BlockSpec((tm, tn), lambda i, j: (i, j)),
        out_shape=jax.ShapeDtypeStruct((M, N), out_dtype),
        scratch_shapes=[pltpu.VMEM((tm, K), BF16)],
        compiler_params=_params(("parallel", "arbitrary")), name="matmul",
    )(*args)


def _rkv_body(u_ref, up_ref, mu_ref, w_ref, o_ref, xs_ref):
    @pl.when(pl.program_id(2) == 0)
    def _():
        u = u_ref[...]
        xs_ref[...] = (u + (up_ref[...] - u) * mu_ref[...]).astype(BF16)

    o_ref[...] = _dot(xs_ref[...], w_ref[...])


def rwkv_rkv(u, u_prev, mu3, w3, tm, tn):
    M, D = u.shape
    return pl.pallas_call(
        _rkv_body, grid=(M // tm, 3, D // tn),
        in_specs=[pl.BlockSpec((tm, D), lambda i, k, j: (i, 0)),
                  pl.BlockSpec((tm, D), lambda i, k, j: (i, 0)),
                  pl.BlockSpec((None, 1, D), lambda i, k, j: (k, 0, 0)),
                  pl.BlockSpec((None, D, tn), lambda i, k, j: (k, 0, j))],
        out_specs=pl.BlockSpec((None, tm, tn), lambda i, k, j: (k, i, j)),
        out_shape=jax.ShapeDtypeStruct((3, M, D), F32),
        scratch_shapes=[pltpu.VMEM((tm, D), BF16)],
        compiler_params=_params(("parallel", "arbitrary", "arbitrary")), name="rwkv_rkv",
    )(u, u_prev, mu3, w3)


def _lora_body(*refs, has_vres):
    it = iter(refs)
    u_ref, up_ref, mu_ref = next(it), next(it), next(it)
    w1, w2, w0 = next(it), next(it), next(it)
    a1, a2, a0 = next(it), next(it), next(it)
    g1, g2 = next(it), next(it)
    if has_vres:
        v1, v2, v0 = next(it), next(it), next(it)
    lw_ref, a_ref, g_ref = next(it), next(it), next(it)
    nu_ref = next(it) if has_vres else None

    u = u_ref[...]
    du = up_ref[...] - u

    def mix(n):
        return (u + du * mu_ref[n:n + 1, :]).astype(BF16)

    hw = jnp.tanh(_dot(mix(0), w1[...])).astype(BF16)
    w_pre = w0[...] + _dot(hw, w2[...])
    softplus = jnp.maximum(-w_pre, 0.0) + jnp.log(1.0 + jnp.exp(-jnp.abs(w_pre)))
    lw_ref[...] = -jnp.exp(-softplus - 0.5)
    ha = _dot(mix(1), a1[...]).astype(BF16)
    a_ref[...] = _sigmoid(a0[...] + _dot(ha, a2[...]))
    hg = _sigmoid(_dot(mix(2), g1[...])).astype(BF16)
    g_ref[...] = _dot(hg, g2[...])
    if has_vres:
        hv = _dot(mix(3), v1[...]).astype(BF16)
        nu_ref[...] = _sigmoid(v0[...] + _dot(hv, v2[...]))


def rwkv_lora(u, u_prev, mu4, w, a, g, v, tm):
    M, D = u.shape
    has_vres = v is not None
    row = lambda i: (i, 0)
    full = lambda i: (0, 0)
    args = [u, u_prev, mu4]
    specs = [pl.BlockSpec((tm, D), row), pl.BlockSpec((tm, D), row), pl.BlockSpec(mu4.shape, full)]
    for t in (w, a, g) + ((v,) if has_vres else ()):
        for m in t:
            args.append(m)
            specs.append(pl.BlockSpec(m.shape, full))
    n_out = 4 if has_vres else 3
    return pl.pallas_call(
        functools.partial(_lora_body, has_vres=has_vres), grid=(M // tm,), in_specs=specs,
        out_specs=[pl.BlockSpec((tm, D), row)] * n_out,
        out_shape=[jax.ShapeDtypeStruct((M, D), F32)] * n_out,
        compiler_params=_params(("parallel",)), name="rwkv_lora",
    )(*args)


def _rwkv_rec_body(*refs, L, TB, hb, t_valid, has_vres, prec):
    N = A_HEAD_DIM
    it = iter(refs)
    r_ref, k_ref, v_ref, lw_ref, a_ref, g_ref = (next(it) for _ in range(6))
    if has_vres:
        vf_ref, nu_ref = next(it), next(it)
    kk_ref, ka_ref, rk_ref, gnw_ref, gnb_ref, s0_ref = (next(it) for _ in range(6))
    y_ref, sT_ref, S_scr = next(it), next(it), next(it)
    tb = pl.program_id(2)

    @pl.when(tb == 0)
    def _():
        S_scr[...] = s0_ref[...]

    row = lax.broadcasted_iota(jnp.int32, (L, L), 0)
    col = lax.broadcasted_iota(jnp.int32, (L, L), 1)
    strict = row > col
    incl = row >= col
    tril = incl.astype(F32)
    eye = (row == col).astype(F32)
    n_sq = int(math.log2(L)) - 1
    kk_p, ka_p, rk_p, gnw, gnb = kk_ref[...], ka_ref[...], rk_ref[...], gnw_ref[...], gnb_ref[...]

    def chunk(c, carry):
        sl = pl.ds(pl.multiple_of(c * L, L), L)
        r, k, v, lw, a, g = r_ref[sl, :], k_ref[sl, :], v_ref[sl, :], lw_ref[sl, :], a_ref[sl, :], g_ref[sl, :]
        if has_vres:
            v = v + (vf_ref[sl, :] - v) * nu_ref[sl, :]
        if t_valid is not None:
            t_idx = tb * TB + c * L + lax.broadcasted_iota(jnp.int32, (L, 1), 0)
            valid = t_idx < t_valid
            r, k, v, lw = (jnp.where(valid, t, 0.0) for t in (r, k, v, lw))
        cum = _dot(tril, lw, HIGHEST)
        cum_end = cum[L - 1:L, :]
        w_cur, w_prev, w_inv, w_rem, w_end = (jnp.exp(cum), jnp.exp(cum - lw), jnp.exp(-cum),
                                              jnp.exp(cum_end - cum), jnp.exp(cum_end))
        outs = []
        for hh in range(hb):
            cs = slice(hh * N, (hh + 1) * N)
            r_h, k_h, v_h, a_h = r[:, cs], k[:, cs], v[:, cs], a[:, cs]
            kkp = k_h * kk_p[:, cs]
            kk = kkp / jnp.maximum(jnp.sqrt(jnp.sum(kkp * kkp, axis=-1, keepdims=True)), 1e-12)
            b = kk * a_h
            k2 = k_h * (1.0 + (a_h - 1.0) * ka_p[:, cs])
            kkd, rd = kk * w_prev[:, cs], r_h * w_cur[:, cs]
            kd, bd = k2 * w_inv[:, cs], b * w_inv[:, cs]
            kend, bend = k2 * w_rem[:, cs], b * w_rem[:, cs]
            a_kb = jnp.where(strict, _dot_nt(kkd, bd, prec), 0.0)
            a_kk = jnp.where(strict, _dot_nt(kkd, kd, prec), 0.0)
            a_rk = jnp.where(incl, _dot_nt(rd, kd, prec), 0.0)
            a_rb = jnp.where(incl, _dot_nt(rd, bd, prec), 0.0)
            t_inv = eye - a_kb
            p = a_kb
            for _ in range(n_sq):
                p = _dot(p, p, prec)
                t_inv = t_inv + _dot(t_inv, p, prec)
            k_t = _dot(t_inv, kkd, prec)
            c_u = _dot(t_inv, _dot(a_kk, v_h, prec), prec)
            y_k = rd - _dot(a_rb, k_t, prec)
            y_c = _dot(a_rk, v_h, prec) - _dot(a_rb, c_u, prec)
            S = S_scr[hh]
            y = _dot_nt(y_k, S, prec) + y_c
            S_c = _dot_tn(v_h, kend, prec) - _dot_tn(c_u, bend, prec)
            S_scr[hh] = S * w_end[:, cs] - _dot(S, _dot_tn(k_t, bend, prec), prec) + S_c
            mean = jnp.mean(y, axis=-1, keepdims=True)
            var = jnp.mean(jnp.square(y - mean), axis=-1, keepdims=True)
            yn = (y - mean) * lax.rsqrt(var + A_GN_EPS) * gnw[:, cs] + gnb[:, cs]
            bonus = jnp.sum(r_h * k2 * rk_p[:, cs], axis=-1, keepdims=True) * v_h
            outs.append((yn + bonus) * g[:, cs])
        y_ref[sl, :] = jnp.concatenate(outs, axis=-1).astype(y_ref.dtype)
        return carry

    lax.fori_loop(0, TB // L, chunk, 0)

    @pl.when(tb == pl.num_programs(2) - 1)
    def _():
        sT_ref[...] = S_scr[...]


def rwkv_recurrence(rkv, lw, a, g, vres, params, s0, n_seq, T, row0, TB, t_valid, prec=HIGHEST):
    _, M, D = rkv.shape
    N = A_HEAD_DIM
    hb = LANES // N
    nb = T // TB
    rb0 = row0 // TB
    has_vres = vres is not None
    seq = lambda b, h, t: (rb0 + b * nb + t, h)
    args, specs = [], []
    for n in range(3):
        args.append(rkv)
        specs.append(pl.BlockSpec((None, TB, LANES), lambda b, h, t, n=n: (n, rb0 + b * nb + t, h)))
    for x in (lw, a, g) + (tuple(vres) if has_vres else ()):
        args.append(x)
        specs.append(pl.BlockSpec((TB, LANES), seq))
    for p in params:
        args.append(p)
        specs.append(pl.BlockSpec((1, LANES), lambda b, h, t: (0, h)))
    args.append(s0)
    specs.append(pl.BlockSpec((hb, N, N), lambda b, h, t: (b * (D // LANES) + h, 0, 0)))
    return pl.pallas_call(
        functools.partial(_rwkv_rec_body, L=CHUNK, TB=TB, hb=hb, t_valid=t_valid, has_vres=has_vres, prec=prec),
        grid=(n_seq, D // LANES, nb), in_specs=specs,
        out_specs=[pl.BlockSpec((TB, LANES), lambda b, h, t: (b * nb + t, h)),
                   pl.BlockSpec((hb, N, N), lambda b, h, t: (b * (D // LANES) + h, 0, 0))],
        out_shape=[jax.ShapeDtypeStruct((n_seq * T, D), BF16),
                   jax.ShapeDtypeStruct((n_seq * (D // N), N, N), F32)],
        scratch_shapes=[pltpu.VMEM((hb, N, N), F32)],
        compiler_params=_params(("parallel", "parallel", "arbitrary")), name="rwkv_recurrence",
    )(*args)


def _rel_bucket(dist):
    exact = N_BUCKETS // 2
    d = jnp.maximum(dist, 1).astype(F32)
    log_b = exact + (jnp.log(d / exact) / math.log(BUCKET_MAX_DIST / exact) * (N_BUCKETS - exact)).astype(jnp.int32)
    return jnp.where(dist < exact, dist, jnp.minimum(log_b, N_BUCKETS - 1))


def _attn_prompt_body(q_ref, k_ref, v_ref, bias_ref, o_ref, acc_ref, m_ref, l_ref, *, T):
    g = pl.program_id(2)
    blk = B_BLK
    scale = B_HEAD_DIM ** -0.5
    first_keys = lax.broadcasted_iota(jnp.int32, (blk, 2 * blk), 1) < blk

    for gi, dil in enumerate(B_DILATIONS):
        @pl.when(g == gi)
        def _(gi=gi, dil=dil):
            span = blk * dil
            bias = bias_ref[...]

            def block(idx, carry):
                n = idx // dil
                start = n * span + (idx - n * dil)
                prev = jnp.maximum(start - span, 0)
                cur_rows = pl.ds(start, blk, stride=dil)
                prev_rows = pl.ds(prev, blk, stride=dil)
                q = (q_ref[cur_rows, :] * scale).astype(BF16)
                kcat = jnp.concatenate([k_ref[prev_rows, :], k_ref[cur_rows, :]], axis=0).astype(BF16)
                vcat = jnp.concatenate([v_ref[prev_rows, :], v_ref[cur_rows, :]], axis=0).astype(BF16)
                logits = _dot_nt(q, kcat) + bias
                logits = jnp.where(first_keys & (n == 0), -jnp.inf, logits)
                mx = jnp.max(logits, axis=-1, keepdims=True)
                p = jnp.exp(logits - mx)
                den = jnp.sum(p, axis=-1, keepdims=True)
                pv = _dot(p.astype(BF16), vcat)
                if gi == 0:
                    acc_ref[cur_rows, :] = pv
                    m_ref[cur_rows, :] = mx
                    l_ref[cur_rows, :] = den
                else:
                    m_old = m_ref[cur_rows, :]
                    m_new = jnp.maximum(m_old, mx)
                    c_old, c_new = jnp.exp(m_old - m_new), jnp.exp(mx - m_new)
                    acc_ref[cur_rows, :] = acc_ref[cur_rows, :] * c_old + pv * c_new
                    l_ref[cur_rows, :] = l_ref[cur_rows, :] * c_old + den * c_new
                    m_ref[cur_rows, :] = m_new
                return carry

            lax.fori_loop(0, T // blk, block, 0)

    @pl.when(g == len(B_DILATIONS) - 1)
    def _():
        o_ref[...] = (acc_ref[...] / l_ref[...]).astype(o_ref.dtype)


def attn_prompt(qkv, bias, n_seq, T):
    H, Dh, G = B_HEADS, B_HEAD_DIM, len(B_DILATIONS)

    def col(which):
        return lambda b, h, g: (b, (g * 3 + which) * H + h)

    return pl.pallas_call(
        functools.partial(_attn_prompt_body, T=T), grid=(n_seq, H, G),
        in_specs=[pl.BlockSpec((T, Dh), col(0)), pl.BlockSpec((T, Dh), col(1)), pl.BlockSpec((T, Dh), col(2)),
                  pl.BlockSpec((None, None, B_BLK, 2 * B_BLK), lambda b, h, g: (g, h, 0, 0))],
        out_specs=pl.BlockSpec((T, Dh), lambda b, h, g: (b, h)),
        out_shape=jax.ShapeDtypeStruct((n_seq * T, H * Dh), BF16),
        scratch_shapes=[pltpu.VMEM((T, Dh), F32), pltpu.VMEM((T, 1), F32), pltpu.VMEM((T, 1), F32)],
        compiler_params=_params(("parallel", "parallel", "arbitrary")), name="attn_prompt",
    )(qkv, qkv, qkv, bias)


def _attn_sample_body(q_ref, k_ref, v_ref, c0_ref, c1_ref, c2_ref, bias_ref, o_ref, *, t_valid):
    blk = B_BLK
    scale = B_HEAD_DIM ** -0.5
    caches = (c0_ref, c1_ref, c2_ref)
    o_ref[...] = jnp.zeros_like(o_ref)
    for t in range(t_valid):
        m_run = l_run = acc = None
        for gi, dil in enumerate(B_DILATIONS):
            q = q_ref[t, gi] * scale
            c_ref = caches[gi]
            if dil == 1:
                kcat = jnp.concatenate([c_ref[t:, 0, 0], k_ref[:t + 1, gi]], axis=0)
                vcat = jnp.concatenate([c_ref[t:, 0, 1], v_ref[:t + 1, gi]], axis=0)
            else:
                kcat = jnp.concatenate([c_ref[:, t, 0], k_ref[t:t + 1, gi]], axis=0)
                vcat = jnp.concatenate([c_ref[:, t, 1], v_ref[t:t + 1, gi]], axis=0)
            logits = jnp.sum(q[None] * kcat, axis=-1, keepdims=True) + bias_ref[gi, :blk + 1]
            mx = jnp.max(logits, axis=0)
            p = jnp.exp(logits - mx[None])
            den = jnp.sum(p, axis=0)
            pv = jnp.sum(p * vcat, axis=0)
            if gi == 0:
                m_run, l_run, acc = mx, den, pv
            else:
                m_new = jnp.maximum(m_run, mx)
                c_old, c_new = jnp.exp(m_run - m_new), jnp.exp(mx - m_new)
                acc = acc * c_old + pv * c_new
                l_run = l_run * c_old + den * c_new
                m_run = m_new
        o_ref[t] = (acc / l_run).astype(o_ref.dtype)


def attn_sample(qkv, caches, bias, n_seq, T, row0, t_valid):
    H, Dh, G = B_HEADS, B_HEAD_DIM, len(B_DILATIONS)
    rb0 = row0 // T
    assert t_valid <= min(d for d in B_DILATIONS if d > 1)
    q5 = qkv.reshape(qkv.shape[0], G, 3, H, Dh)

    def spec(which):
        return pl.BlockSpec((T, G, None, H, Dh), lambda b: (b + rb0, 0, which, 0, 0))

    cache_specs = [pl.BlockSpec((None, B_BLK, min(d, t_valid), 2, H, Dh), lambda b: (b, 0, 0, 0, 0, 0))
                   for d in B_DILATIONS]
    return pl.pallas_call(
        functools.partial(_attn_sample_body, t_valid=t_valid), grid=(n_seq,),
        in_specs=[spec(0), spec(1), spec(2)] + cache_specs + [pl.BlockSpec(bias.shape, lambda b: (0, 0, 0, 0))],
        out_specs=pl.BlockSpec((T, H, Dh), lambda b: (b, 0, 0)),
        out_shape=jax.ShapeDtypeStruct((n_seq * T, H, Dh), BF16),
        compiler_params=_params(("parallel",)), name="attn_sample",
    )(q5, q5, q5, *caches, bias)


def _mlstm_body(q_ref, k_ref, v_ref, o_ref, gate_ref, gb_ref, nw_ref, c0_ref, n0_ref, m0_ref,
                y_ref, cT_ref, nT_ref, mT_ref, C_scr, n_scr, m_scr, *, L, TB, t_valid):
    H, E, V = C_HEADS, C_QK_DIM, C_V_DIM
    tb = pl.program_id(1)

    @pl.when(tb == 0)
    def _():
        C_scr[...] = c0_ref[...]
        n_scr[...] = n0_ref[...]
        m_scr[...] = m0_ref[...]

    row = lax.broadcasted_iota(jnp.int32, (L, L), 0)
    col = lax.broadcasted_iota(jnp.int32, (L, L), 1)
    causal = row >= col
    tril = causal.astype(F32)
    gb = gb_ref[...]
    nw = nw_ref[...]
    lane = lax.broadcasted_iota(jnp.int32, (L, LANES), 1)

    def chunk(c, carry):
        sl = pl.ds(pl.multiple_of(c * L, L), L)
        gact = C_GATE_CAP * jnp.tanh((gate_ref[sl, :] + gb) / C_GATE_CAP)
        lf = jnp.minimum(gact, 0.0) - jnp.log(1.0 + jnp.exp(-jnp.abs(gact)))
        ig = gact
        valid = None
        if t_valid is not None:
            t_idx = tb * TB + c * L + lax.broadcasted_iota(jnp.int32, (L, 1), 0)
            valid = t_idx < t_valid
            ig = jnp.where(valid, ig, -1e30)
            lf = jnp.where(valid, lf, 0.0)
        bcum = _dot(tril, lf, HIGHEST)
        ig_t = ig.T
        bcum_t = bcum.T
        outs = []
        for h in range(H):
            b_col, b_row = bcum[:, H + h:H + h + 1], bcum_t[H + h:H + h + 1, :]
            ig_col, ig_row = ig[:, h:h + 1], ig_t[h:h + 1, :]
            q = q_ref[sl, h * E:(h + 1) * E]
            k = k_ref[sl, h * E:(h + 1) * E] * (E ** -0.5)
            v = v_ref[sl, h * V:(h + 1) * V]
            if valid is not None:
                q, k, v = (jnp.where(valid, t, 0.0) for t in (q, k, v))
            m_prev = m_scr[h:h + 1, 0:1]
            dm = jnp.where(causal, b_col - b_row + ig_row, -jnp.inf)
            inter = b_col + m_prev
            mt = jnp.maximum(inter, jnp.max(dm, axis=-1, keepdims=True))
            w_d = jnp.exp(dm - mt)
            w_i = jnp.exp(inter - mt)
            qb, kb, vb = q.astype(BF16), k.astype(BF16), v.astype(BF16)
            sc = _dot_nt(qb, kb) * w_d
            C = C_scr[h]
            num = _dot(sc.astype(BF16), vb) + w_i * _dot_nt(qb, C.astype(BF16))
            den = jnp.sum(sc, axis=-1, keepdims=True) + w_i * jnp.sum(q * n_scr[h:h + 1, :], axis=-1, keepdims=True)
            hh = num / jnp.maximum(jnp.abs(den), jnp.exp(-mt))
            m_new = mt[L - 1:L, :]
            b_end = b_col[L - 1:L, :]
            w_s = jnp.exp(b_end - b_col + ig_col - m_new)
            dec = jnp.exp(b_end + m_prev - m_new)
            C_scr[h] = dec * C + _dot_tn((w_s * v).astype(BF16), kb)
            n_scr[h:h + 1, :] = dec * n_scr[h:h + 1, :] + jnp.sum(w_s * k, axis=0, keepdims=True)
            m_scr[h:h + 1, :] = jnp.broadcast_to(m_new, (1, LANES))
            hn = hh * lax.rsqrt(jnp.mean(hh * hh, axis=-1, keepdims=True) + NORM_EPS) * nw[:, h * V:(h + 1) * V]
            outs.append(hn * _sigmoid(o_ref[sl, h * V:(h + 1) * V]))
        y_ref[sl, :] = jnp.concatenate(outs, axis=-1).astype(y_ref.dtype)
        return carry

    lax.fori_loop(0, TB // L, chunk, 0)

    @pl.when(tb == pl.num_programs(1) - 1)
    def _():
        cT_ref[...] = C_scr[...]
        nT_ref[...] = n_scr[...]
        mT_ref[...] = m_scr[...]


def mlstm_recurrence(proj, gate_bias, norm_w, c0, n0, m0, n_seq, T, row0, TB, t_valid):
    H, E, V = C_HEADS, C_QK_DIM, C_V_DIM
    nb = T // TB
    rb0 = row0 // TB
    HE, HV = H * E, H * V

    def cols(cb):
        return lambda b, t: (rb0 + b * nb + t, cb)

    st4 = lambda b, t: (b, 0, 0, 0)
    st3 = lambda b, t: (b, 0, 0)
    return pl.pallas_call(
        functools.partial(_mlstm_body, L=CHUNK, TB=TB, t_valid=t_valid), grid=(n_seq, nb),
        in_specs=[pl.BlockSpec((TB, HE), cols(0)), pl.BlockSpec((TB, HE), cols(1)),
                  pl.BlockSpec((TB, HV), cols(2 * HE // HV)), pl.BlockSpec((TB, HV), cols(2 * HE // HV + 1)),
                  pl.BlockSpec((TB, LANES), cols((2 * HE + 2 * HV) // LANES)),
                  pl.BlockSpec((1, LANES), lambda b, t: (0, 0)), pl.BlockSpec((1, HV), lambda b, t: (0, 0)),
                  pl.BlockSpec((None, H, V, E), st4), pl.BlockSpec((None, H, E), st3),
                  pl.BlockSpec((None, H, LANES), st3)],
        out_specs=[pl.BlockSpec((TB, HV), lambda b, t: (b * nb + t, 0)),
                   pl.BlockSpec((None, H, V, E), st4), pl.BlockSpec((None, H, E), st3),
                   pl.BlockSpec((None, H, LANES), st3)],
        out_shape=[jax.ShapeDtypeStruct((n_seq * T, HV), BF16), jax.ShapeDtypeStruct((n_seq, H, V, E), F32),
                   jax.ShapeDtypeStruct((n_seq, H, E), F32), jax.ShapeDtypeStruct((n_seq, H, LANES), F32)],
        scratch_shapes=[pltpu.VMEM((H, V, E), F32), pltpu.VMEM((H, E), F32), pltpu.VMEM((H, LANES), F32)],
        compiler_params=_params(("parallel", "arbitrary")), name="mlstm_recurrence",
    )(proj, proj, proj, proj, proj, gate_bias, norm_w, c0, n0, m0)


def _tile(n, target):
    return max(t for t in range(LANES, min(n, target) + 1, LANES) if n % t == 0)


def _pad_cols(w, n):
    return jnp.pad(w, ((0, 0), (0, n - w.shape[1])))


def _pad_rows(w, n):
    return jnp.pad(w, ((0, n - w.shape[0]), (0, 0)))


def kernel(x_prompt, x_sample, state_a_wkv, state_a_shift, cache_b_kv_g0, cache_b_kv_g1, cache_b_kv_g2, state_c_C, state_c_n, state_c_m, rel_bias, norm_ffn1, ffn1_w_in, ffn1_w_out, norm_mix, norm_ffn2, ffn2_w_in, ffn2_w_out, norm_final, a_mu, a_w_rkv, a_w0, a_w1, a_w2, a_a0, a_a1, a_a2, a_g1, a_g2, a_k_k, a_k_a, a_r_k, a_gn_w, a_gn_b, a_w_out, a_v0, a_v1, a_v2, b_w_qkv, b_w_out, c_w_in, c_b_gates, c_norm_w, c_w_out):
    Bp, Tp, D = x_prompt.shape
    Bs, Ts, _ = x_sample.shape
    depth = norm_mix.shape[0]
    Tsp = SAMPLE_PAD
    Mp, Ms = Bp * Tp, Bs * Tsp
    M = Mp + Ms
    TM = 512
    TN = _tile(D, 512)
    TF = _tile(ffn1_w_out.shape[1], 512)
    H_a = D // A_HEAD_DIM
    G, H_b, Dh = len(B_DILATIONS), B_HEADS, B_HEAD_DIM
    bf = lambda w: w.astype(BF16)

    x = jnp.concatenate([x_prompt.reshape(Mp, D),
                         jnp.pad(x_sample, ((0, 0), (0, Tsp - Ts), (0, 0))).reshape(Ms, D)], axis=0)

    def last_rows(t):
        return t[:Mp].reshape(Bp, Tp, -1)[:, -1], t[Mp:].reshape(Bs, Tsp, -1)[:, Ts - 1]

    qi = jnp.arange(B_BLK)[:, None]
    kj = jnp.arange(2 * B_BLK)[None, :]
    step = qi + B_BLK - kj
    step_ok = (step >= 0) & (step <= B_BLK)
    m_desc = B_BLK - jnp.arange(B_BLK + 8)
    bias_p, bias_s = [], []
    for gi, dil in enumerate(B_DILATIONS):
        tab = rel_bias[:, gi * H_b:(gi + 1) * H_b].astype(F32)
        bp = jnp.transpose(tab[_rel_bucket(jnp.clip(step, 0, B_BLK) * dil)], (2, 0, 1))
        bias_p.append(jnp.where(step_ok[None], bp, -jnp.inf))
        bs = tab[_rel_bucket(jnp.maximum(m_desc, 0) * dil)]
        bias_s.append(jnp.broadcast_to(bs[:, :, None], (B_BLK + 8, H_b, Dh)))
    bias_p, bias_s = jnp.stack(bias_p), jnp.stack(bias_s)

    outs_a_wkv, outs_a_shift, outs_c = ([], []), ([], []), ([], [], [], [], [], [])
    outs_b = [([], []) for _ in range(G)]
    v_first = None
    for i in range(depth):
        x = ffn(x, norm_ffn1[i], bf(ffn1_w_in[i]), bf(ffn1_w_out[i]), TM, TF)
        kind, j = i % 3, i // 3
        if kind == 0:
            u = rmsnorm(x, norm_mix[i], TM)
            up, us = u[:Mp].reshape(Bp, Tp, D), u[Mp:].reshape(Bs, Tsp, D)
            u_prev = jnp.concatenate([
                jnp.concatenate([jnp.zeros((Bp, 1, D), F32), up[:, :-1]], axis=1).reshape(Mp, D),
                jnp.concatenate([state_a_shift[j][:, None, :], us[:, :-1]], axis=1).reshape(Ms, D)], axis=0)
            mu = a_mu[j]
            rkv = rwkv_rkv(u, u_prev, mu[jnp.array([0, 2, 3])][:, None, :], bf(a_w_rkv[j]), TM, TN)
            lr = LANES
            w_br = (bf(_pad_cols(a_w1[j], lr)), bf(_pad_rows(a_w2[j], lr)), a_w0[j].reshape(1, D))
            a_br = (bf(_pad_cols(a_a1[j], lr)), bf(_pad_rows(a_a2[j], lr)), a_a0[j].reshape(1, D))
            g_br = (bf(a_g1[j]), bf(a_g2[j]))
            v_br = None
            if j > 0:
                v_br = (bf(_pad_cols(a_v1[j - 1], lr)), bf(_pad_rows(a_v2[j - 1], lr)), a_v0[j - 1].reshape(1, D))
            lora = rwkv_lora(u, u_prev, mu[jnp.array([1, 4, 5, 3])], w_br, a_br, g_br, v_br, 256)
            lw, a_lr, gate = lora[:3]
            vres = None if j == 0 else (v_first, lora[3])
            if j == 0:
                v_first = rkv[2]
            par = tuple(p.reshape(1, D) for p in (a_k_k[j], a_k_a[j], a_r_k[j], a_gn_w[j], a_gn_b[j]))
            s0p = jnp.zeros((Bp * H_a, A_HEAD_DIM, A_HEAD_DIM), F32)
            s0s = state_a_wkv[j].reshape(Bs * H_a, A_HEAD_DIM, A_HEAD_DIM)
            yp, sp = rwkv_recurrence(rkv, lw, a_lr, gate, vres, par, s0p, Bp, Tp, 0, 512, None)
            ys, ss = rwkv_recurrence(rkv, lw, a_lr, gate, vres, par, s0s, Bs, Tsp, Mp, Tsp, Ts)
            x = matmul(jnp.concatenate([yp, ys], axis=0), bf(a_w_out[j]), TM, TN, residual=x)
            outs_a_wkv[0].append(sp.reshape(Bp, H_a, A_HEAD_DIM, A_HEAD_DIM))
            outs_a_wkv[1].append(ss.reshape(Bs, H_a, A_HEAD_DIM, A_HEAD_DIM))
            sh_p, sh_s = last_rows(u)
            outs_a_shift[0].append(sh_p)
            outs_a_shift[1].append(sh_s)
        elif kind == 1:
            qkv = matmul(x, bf(b_w_qkv[j]), TM, _tile(b_w_qkv.shape[2], 1024), norm_g=norm_mix[i])
            caches = [c[j].reshape(Bs, B_BLK, d, 2, H_b, Dh)
                      for c, d in zip((cache_b_kv_g0, cache_b_kv_g1, cache_b_kv_g2), B_DILATIONS)]
            op = attn_prompt(qkv, bias_p, Bp, Tp)
            os_ = attn_sample(qkv, caches, bias_s, Bs, Tsp, Mp, Ts)
            x = matmul(jnp.concatenate([op, os_.reshape(Ms, H_b * Dh)], axis=0), bf(b_w_out[j]), TM, TN, residual=x)
            q6 = qkv.reshape(M, G, 3, H_b, Dh)
            for gi in range(G):
                keep = min(B_WINDOWS[gi], Tp)
                kv = q6[:, gi, 1:3]
                outs_b[gi][0].append(kv[:Mp].reshape(Bp, Tp, 2, H_b, Dh)[:, Tp - keep:])
                outs_b[gi][1].append(kv[Mp:].reshape(Bs, Tsp, 2, H_b, Dh)[:, :Ts])
        else:
            H, E, V = C_HEADS, C_QK_DIM, C_V_DIM
            n_in = c_w_in.shape[2]
            n_pad = -(-n_in // LANES) * LANES
            proj = matmul(x, bf(_pad_cols(c_w_in[j], n_pad)), TM, _tile(n_pad, 1024), norm_g=norm_mix[i])
            gbias = _pad_cols(c_b_gates[j].reshape(1, 2 * H), LANES)
            nw = c_norm_w[j].reshape(1, H * V)
            zc = (jnp.zeros((Bp, H, V, E), F32), jnp.zeros((Bp, H, E), F32), jnp.zeros((Bp, H, LANES), F32))
            sc = (state_c_C[j], state_c_n[j], jnp.broadcast_to(state_c_m[j][:, :, None], (Bs, H, LANES)))
            hp, cp, np_, mp = mlstm_recurrence(proj, gbias, nw, *zc, Bp, Tp, 0, 256, None)
            hs, cs, ns, ms = mlstm_recurrence(proj, gbias, nw, *sc, Bs, Tsp, Mp, Tsp, Ts)
            x = matmul(jnp.concatenate([hp, hs], axis=0), bf(c_w_out[j]), TM, TN, residual=x)
            for lst, val in zip(outs_c, (cp, cs, np_, ns, mp[:, :, 0], ms[:, :, 0])):
                lst.append(val)
        x = ffn(x, norm_ffn2[i], bf(ffn2_w_in[i]), bf(ffn2_w_out[i]), TM, TF)

    y = rmsnorm(x, norm_final, TM)
    y_prompt = y[:Mp].reshape(Bp, Tp, D)
    y_sample = y[Mp:].reshape(Bs, Tsp, D)[:, :Ts]
    st = jnp.stack
    return (y_prompt, y_sample, st(outs_a_wkv[0]), st(outs_a_wkv[1]), st(outs_a_shift[0]), st(outs_a_shift[1]),
            st(outs_b[0][0]), st(outs_b[0][1]), st(outs_b[1][0]), st(outs_b[1][1]), st(outs_b[2][0]), st(outs_b[2][1]),
            st(outs_c[0]), st(outs_c[1]), st(outs_c[2]), st(outs_c[3]), st(outs_c[4]), st(outs_c[5]))
```

```python
import functools
import math

import jax
import jax.numpy as jnp
from jax import lax
from jax.experimental import pallas as pl
from jax.experimental.pallas import tpu as pltpu

F32 = jnp.float32
BF16 = jnp.bfloat16
HIGHEST = lax.Precision.HIGHEST

NORM_EPS = 1e-6
A_HEAD_DIM = 64
A_GN_EPS = 64e-5
A_REC_LANES = 512
A_REC_PRECISION = "bf16"
B_WINDOWS = (128, 512, 2048)
B_DILATIONS = (1, 4, 16)
B_HEADS = 16
B_HEAD_DIM = 128
B_BLK = 128
B_UNROLL = 4
N_BUCKETS = 32
BUCKET_MAX_DIST = 2048
C_HEADS = 8
C_QK_DIM = 128
C_V_DIM = 256
C_GATE_CAP = 15.0
CHUNK = 64
SAMPLE_PAD = 64
LANES = 128
VMEM_LIMIT = 56 * 1024 * 1024


def _params(sem):
    return pltpu.CompilerParams(dimension_semantics=sem, vmem_limit_bytes=VMEM_LIMIT)


def _dot(a, b, precision=None):
    return jnp.dot(a, b, preferred_element_type=F32, precision=precision)


def _dot_nt(a, b, precision=None):
    return lax.dot_general(a, b, (((1,), (1,)), ((), ())), preferred_element_type=F32, precision=precision)


def _dot_tn(a, b, precision=None):
    return lax.dot_general(a, b, (((0,), (0,)), ((), ())), preferred_element_type=F32, precision=precision)


_NN = (((1,), (0,)), ((), ()))
_NT = (((1,), (1,)), ((), ()))
_TN = (((0,), (0,)), ((), ()))


def _pmm(a, b, dims, mode):
    dg = functools.partial(lax.dot_general, dimension_numbers=dims, preferred_element_type=F32)
    if mode == "highest":
        return dg(a, b, precision=HIGHEST)
    a_hi, b_hi = a.astype(BF16), b.astype(BF16)
    if mode == "bf16":
        return dg(a_hi, b_hi)
    a_lo = (a - a_hi.astype(F32)).astype(BF16)
    b_lo = (b - b_hi.astype(F32)).astype(BF16)
    return dg(a_hi, b_hi) + (dg(a_hi, b_lo) + dg(a_lo, b_hi))


def _sigmoid(x):
    return 1.0 / (1.0 + jnp.exp(-x))


def _rms(x, g):
    ms = jnp.mean(x * x, axis=-1, keepdims=True)
    return x * lax.rsqrt(ms + NORM_EPS) * g


def _rmsnorm_body(x_ref, g_ref, o_ref):
    o_ref[...] = _rms(x_ref[...], g_ref[...]).astype(o_ref.dtype)


def rmsnorm(x, g, tm):
    M, D = x.shape
    return pl.pallas_call(
        _rmsnorm_body, grid=(M // tm,),
        in_specs=[pl.BlockSpec((tm, D), lambda i: (i, 0)), pl.BlockSpec((1, D), lambda i: (0, 0))],
        out_specs=pl.BlockSpec((tm, D), lambda i: (i, 0)),
        out_shape=jax.ShapeDtypeStruct((M, D), F32),
        compiler_params=_params(("parallel",)), name="rmsnorm",
    )(x, g.reshape(1, D))


def _ffn_body(x_ref, g_ref, wg_ref, wu_ref, wo_ref, o_ref, xn_ref, acc_ref):
    j = pl.program_id(1)

    @pl.when(j == 0)
    def _():
        xn_ref[...] = _rms(x_ref[...], g_ref[...]).astype(BF16)
        acc_ref[...] = jnp.zeros_like(acc_ref)

    xn = xn_ref[...]
    gate = _dot(xn, wg_ref[...])
    up = _dot(xn, wu_ref[...])
    h = (gate * _sigmoid(gate) * up).astype(BF16)
    acc_ref[...] += _dot(h, wo_ref[...])

    @pl.when(j == pl.num_programs(1) - 1)
    def _():
        o_ref[...] = x_ref[...] + 0.5 * acc_ref[...]


def ffn(x, g, w_in, w_out, layer, tm, tf):
    M, D = x.shape
    Fh = w_out.shape[1]
    nf = Fh // tf
    return pl.pallas_call(
        _ffn_body, grid=(M // tm, nf),
        in_specs=[pl.BlockSpec((tm, D), lambda i, j: (i, 0)),
                  pl.BlockSpec((1, D), lambda i, j: (0, 0)),
                  pl.BlockSpec((None, D, tf), lambda i, j: (layer, 0, j)),
                  pl.BlockSpec((None, D, tf), lambda i, j: (layer, 0, j + nf)),
                  pl.BlockSpec((None, tf, D), lambda i, j: (layer, j, 0))],
        out_specs=pl.BlockSpec((tm, D), lambda i, j: (i, 0)),
        out_shape=jax.ShapeDtypeStruct((M, D), F32),
        scratch_shapes=[pltpu.VMEM((tm, D), BF16), pltpu.VMEM((tm, D), F32)],
        compiler_params=_params(("parallel", "arbitrary")), name="ffn",
    )(x, g.reshape(1, D), w_in, w_in, w_out)


def _mm_body(*refs, has_norm, has_res):
    it = iter(refs)
    x_ref = next(it)
    g_ref = next(it) if has_norm else None
    w_ref = next(it)
    res_ref = next(it) if has_res else None
    o_ref = next(it)
    xs_ref = next(it)

    @pl.when(pl.program_id(1) == 0)
    def _():
        x = x_ref[...].astype(F32)
        if has_norm:
            x = _rms(x, g_ref[...])
        xs_ref[...] = x.astype(BF16)

    acc = _dot(xs_ref[...], w_ref[...])
    if has_res:
        acc = res_ref[...] + acc
    o_ref[...] = acc.astype(o_ref.dtype)


def matmul(x, w, tm, tn, norm_g=None, residual=None, out_dtype=F32):
    M, K = x.shape
    N = w.shape[1]
    args = [x]
    specs = [pl.BlockSpec((tm, K), lambda i, j: (i, 0))]
    if norm_g is not None:
        args.append(norm_g.reshape(1, K))
        specs.append(pl.BlockSpec((1, K), lambda i, j: (0, 0)))
    args.append(w)
    specs.append(pl.BlockSpec((K, tn), lambda i, j: (0, j)))
    if residual is not None:
        args.append(residual)
        specs.append(pl.BlockSpec((tm, tn), lambda i, j: (i, j)))
    return pl.pallas_call(
        functools.partial(_mm_body, has_norm=norm_g is not None, has_res=residual is not None),
        grid=(M // tm, N // tn), in_specs=specs,
        out_specs=pl.BlockSpec((tm, tn), lambda i, j: (i, j)),
        out_shape=jax.ShapeDtypeStruct((M, N), out_dtype),
        scratch_shapes=[pltpu.VMEM((tm, K), BF16)],
        compiler_params=_params(("parallel", "arbitrary")), name="matmul",
    )(*args)


def _rkv_body(u_ref, up_ref, mu_ref, w_ref, o_ref, xs_ref):
    @pl.when(pl.program_id(2) == 0)
    def _():
        u = u_ref[...]
        xs_ref[...] = (u + (up_ref[...] - u) * mu_ref[...]).astype(BF16)

    o_ref[...] = _dot(xs_ref[...], w_ref[...])


def rwkv_rkv(u, u_prev, mu3, w3, tm, tn):
    M, D = u.shape
    return pl.pallas_call(
        _rkv_body, grid=(M // tm, 3, D // tn),
        in_specs=[pl.BlockSpec((tm, D), lambda i, k, j: (i, 0)),
                  pl.BlockSpec((tm, D), lambda i, k, j: (i, 0)),
                  pl.BlockSpec((None, 1, D), lambda i, k, j: (k, 0, 0)),
                  pl.BlockSpec((None, D, tn), lambda i, k, j: (k, 0, j))],
        out_specs=pl.BlockSpec((None, tm, tn), lambda i, k, j: (k, i, j)),
        out_shape=jax.ShapeDtypeStruct((3, M, D), F32),
        scratch_shapes=[pltpu.VMEM((tm, D), BF16)],
        compiler_params=_params(("parallel", "arbitrary", "arbitrary")), name="rwkv_rkv",
    )(u, u_prev, mu3, w3)


def _lora_body(*refs, has_vres):
    it = iter(refs)
    u_ref, up_ref, mu_ref = next(it), next(it), next(it)
    w1, w2, w0 = next(it), next(it), next(it)
    a1, a2, a0 = next(it), next(it), next(it)
    g1, g2 = next(it), next(it)
    if has_vres:
        v1, v2, v0 = next(it), next(it), next(it)
    lw_ref, a_ref, g_ref = next(it), next(it), next(it)
    nu_ref = next(it) if has_vres else None

    u = u_ref[...]
    du = up_ref[...] - u

    def mix(n):
        return (u + du * mu_ref[n:n + 1, :]).astype(BF16)

    hw = jnp.tanh(_dot(mix(0), w1[...])).astype(BF16)
    w_pre = w0[...] + _dot(hw, w2[...])
    softplus = jnp.maximum(-w_pre, 0.0) + jnp.log(1.0 + jnp.exp(-jnp.abs(w_pre)))
    lw_ref[...] = -jnp.exp(-softplus - 0.5)
    ha = _dot(mix(1), a1[...]).astype(BF16)
    a_ref[...] = _sigmoid(a0[...] + _dot(ha, a2[...]))
    hg = _sigmoid(_dot(mix(2), g1[...])).astype(BF16)
    g_ref[...] = _dot(hg, g2[...])
    if has_vres:
        hv = _dot(mix(3), v1[...]).astype(BF16)
        nu_ref[...] = _sigmoid(v0[...] + _dot(hv, v2[...]))


def rwkv_lora(u, u_prev, mu4, w, a, g, v, tm):
    M, D = u.shape
    has_vres = v is not None
    row = lambda i: (i, 0)
    full = lambda i: (0, 0)
    args = [u, u_prev, mu4]
    specs = [pl.BlockSpec((tm, D), row), pl.BlockSpec((tm, D), row), pl.BlockSpec(mu4.shape, full)]
    for t in (w, a, g) + ((v,) if has_vres else ()):
        for m in t:
            args.append(m)
            specs.append(pl.BlockSpec(m.shape, full))
    n_out = 4 if has_vres else 3
    return pl.pallas_call(
        functools.partial(_lora_body, has_vres=has_vres), grid=(M // tm,), in_specs=specs,
        out_specs=[pl.BlockSpec((tm, D), row)] * n_out,
        out_shape=[jax.ShapeDtypeStruct((M, D), F32)] * n_out,
        compiler_params=_params(("parallel",)), name="rwkv_lora",
    )(*args)


def _rwkv_rec_body(*refs, L, TB, hb, t_valid, has_vres, prec):
    N = A_HEAD_DIM
    it = iter(refs)
    r_ref, k_ref, v_ref, lw_ref, a_ref, g_ref = (next(it) for _ in range(6))
    if has_vres:
        vf_ref, nu_ref = next(it), next(it)
    kk_ref, ka_ref, rk_ref, gnw_ref, gnb_ref, s0_ref = (next(it) for _ in range(6))
    y_ref, sT_ref, S_scr = next(it), next(it), next(it)
    tb = pl.program_id(2)

    @pl.when(tb == 0)
    def _():
        S_scr[...] = s0_ref[...]

    row = lax.broadcasted_iota(jnp.int32, (L, L), 0)
    col = lax.broadcasted_iota(jnp.int32, (L, L), 1)
    strict = row > col
    incl = row >= col
    tril = incl.astype(F32)
    eye = (row == col).astype(F32)
    row2 = lax.broadcasted_iota(jnp.int32, (2 * L, L), 0)
    col2 = lax.broadcasted_iota(jnp.int32, (2 * L, L), 1)
    mask2 = col2 < jnp.where(row2 < L, row2, row2 - L + 1)
    n_sq = int(math.log2(L)) - 1
    kk_p, ka_p, rk_p, gnw, gnb = kk_ref[...], ka_ref[...], rk_ref[...], gnw_ref[...], gnb_ref[...]

    def chunk(c, carry):
        sl = pl.ds(pl.multiple_of(c * L, L), L)
        r, k, v, lw, a, g = r_ref[sl, :], k_ref[sl, :], v_ref[sl, :], lw_ref[sl, :], a_ref[sl, :], g_ref[sl, :]
        if has_vres:
            v = v + (vf_ref[sl, :] - v) * nu_ref[sl, :]
        if t_valid is not None:
            t_idx = tb * TB + c * L + lax.broadcasted_iota(jnp.int32, (L, 1), 0)
            valid = t_idx < t_valid
            r, k, v, lw = (jnp.where(valid, t, 0.0) for t in (r, k, v, lw))
        cum = _dot(tril, lw, HIGHEST)
        cum_end = cum[L - 1:L, :]
        w_cur, w_prev, w_inv, w_rem, w_end = (jnp.exp(cum), jnp.exp(cum - lw), jnp.exp(-cum),
                                              jnp.exp(cum_end - cum), jnp.exp(cum_end))
        mm, mm_nt, mm_tn = (functools.partial(_pmm, dims=d, mode=prec) for d in (_NN, _NT, _TN))
        heads = range(hb)
        hs = [slice(h * N, (h + 1) * N) for h in heads]
        kkp = [k[:, s] * kk_p[:, s] for s in hs]
        kk = [x / jnp.maximum(jnp.sqrt(jnp.sum(x * x, axis=-1, keepdims=True)), 1e-12) for x in kkp]
        b = [kk[h] * a[:, hs[h]] for h in heads]
        k2 = [k[:, s] * (1.0 + (a[:, s] - 1.0) * ka_p[:, s]) for s in hs]
        lhs2 = [jnp.concatenate([kk[h] * w_prev[:, hs[h]], r[:, hs[h]] * w_cur[:, hs[h]]], axis=0) for h in heads]
        kd = [k2[h] * w_inv[:, hs[h]] for h in heads]
        bd = [b[h] * w_inv[:, hs[h]] for h in heads]
        kend = [k2[h] * w_rem[:, hs[h]] for h in heads]
        bend = [b[h] * w_rem[:, hs[h]] for h in heads]
        kkd = [x[:L] for x in lhs2]
        rd = [x[L:] for x in lhs2]
        a_k = [jnp.where(mask2, mm_nt(lhs2[h], kd[h]), 0.0) for h in heads]
        a_b = [jnp.where(mask2, mm_nt(lhs2[h], bd[h]), 0.0) for h in heads]
        a_kb = [x[:L] for x in a_b]
        a_rb = [x[L:] for x in a_b]
        a_v = [mm(a_k[h], v[:, hs[h]]) for h in heads]
        t_inv = [eye - x for x in a_kb]
        p = a_kb
        for _ in range(n_sq):
            p = [mm(x, x) for x in p]
            t_inv = [t_inv[h] + mm(t_inv[h], p[h]) for h in heads]
        k_t = [mm(t_inv[h], kkd[h]) for h in heads]
        c_u = [mm(t_inv[h], a_v[h][:L]) for h in heads]
        y_k = [rd[h] - mm(a_rb[h], k_t[h]) for h in heads]
        y_c = [a_v[h][L:] - mm(a_rb[h], c_u[h]) for h in heads]
        S = [S_scr[h] for h in heads]
        y = [mm_nt(y_k[h], S[h]) + y_c[h] for h in heads]
        ktb = [mm_tn(k_t[h], bend[h]) for h in heads]
        S_c = [mm_tn(v[:, hs[h]], kend[h]) - mm_tn(c_u[h], bend[h]) for h in heads]
        for h in heads:
            S_scr[h] = S[h] * w_end[:, hs[h]] - mm(S[h], ktb[h]) + S_c[h]
        outs = []
        for h in heads:
            mean = jnp.mean(y[h], axis=-1, keepdims=True)
            var = jnp.mean(jnp.square(y[h] - mean), axis=-1, keepdims=True)
            yn = (y[h] - mean) * lax.rsqrt(var + A_GN_EPS) * gnw[:, hs[h]] + gnb[:, hs[h]]
            bonus = jnp.sum(r[:, hs[h]] * k2[h] * rk_p[:, hs[h]], axis=-1, keepdims=True) * v[:, hs[h]]
            outs.append((yn + bonus) * g[:, hs[h]])
        y_ref[sl, :] = jnp.concatenate(outs, axis=-1).astype(y_ref.dtype)
        return carry

    lax.fori_loop(0, TB // L, chunk, 0)

    @pl.when(tb == pl.num_programs(2) - 1)
    def _():
        sT_ref[...] = S_scr[...]


def rwkv_recurrence(rkv, lw, a, g, vres, params, s0, n_seq, T, row0, TB, t_valid):
    _, M, D = rkv.shape
    N = A_HEAD_DIM
    LW = min(A_REC_LANES, D)
    hb = LW // N
    nb = T // TB
    rb0 = row0 // TB
    has_vres = vres is not None
    seq = lambda b, h, t: (rb0 + b * nb + t, h)
    args, specs = [], []
    for n in range(3):
        args.append(rkv)
        specs.append(pl.BlockSpec((None, TB, LW), lambda b, h, t, n=n: (n, rb0 + b * nb + t, h)))
    for x in (lw, a, g) + (tuple(vres) if has_vres else ()):
        args.append(x)
        specs.append(pl.BlockSpec((TB, LW), seq))
    for p in params:
        args.append(p)
        specs.append(pl.BlockSpec((1, LW), lambda b, h, t: (0, h)))
    args.append(s0)
    specs.append(pl.BlockSpec((hb, N, N), lambda b, h, t: (b * (D // LW) + h, 0, 0)))
    return pl.pallas_call(
        functools.partial(_rwkv_rec_body, L=CHUNK, TB=TB, hb=hb, t_valid=t_valid, has_vres=has_vres,
                          prec=A_REC_PRECISION),
        grid=(n_seq, D // LW, nb), in_specs=specs,
        out_specs=[pl.BlockSpec((TB, LW), lambda b, h, t: (b * nb + t, h)),
                   pl.BlockSpec((hb, N, N), lambda b, h, t: (b * (D // LW) + h, 0, 0))],
        out_shape=[jax.ShapeDtypeStruct((n_seq * T, D), BF16),
                   jax.ShapeDtypeStruct((n_seq * (D // N), N, N), F32)],
        scratch_shapes=[pltpu.VMEM((hb, N, N), F32)],
        compiler_params=_params(("parallel", "parallel", "arbitrary")), name="rwkv_recurrence",
    )(*args)


def _rel_bucket(dist):
    exact = N_BUCKETS // 2
    d = jnp.maximum(dist, 1).astype(F32)
    log_b = exact + (jnp.log(d / exact) / math.log(BUCKET_MAX_DIST / exact) * (N_BUCKETS - exact)).astype(jnp.int32)
    return jnp.where(dist < exact, dist, jnp.minimum(log_b, N_BUCKETS - 1))


def _attn_prompt_body(q_ref, k_ref, v_ref, bias_ref, o_ref, acc_ref, m_ref, l_ref, *, T):
    g = pl.program_id(2)
    blk = B_BLK
    scale = B_HEAD_DIM ** -0.5
    first_keys = lax.broadcasted_iota(jnp.int32, (blk, 2 * blk), 1) < blk

    for gi, dil in enumerate(B_DILATIONS):
        @pl.when(g == gi)
        def _(gi=gi, dil=dil):
            span = blk * dil
            bias = bias_ref[...]

            def blocks(it, carry):
                us = range(B_UNROLL)
                idx = [it * B_UNROLL + u for u in us]
                n = [i // dil for i in idx]
                start = [n[u] * span + (idx[u] - n[u] * dil) for u in us]
                cur = [pl.ds(s, blk, stride=dil) for s in start]
                prev = [pl.ds(jnp.maximum(s - span, 0), blk, stride=dil) for s in start]
                q = [(q_ref[cur[u], :] * scale).astype(BF16) for u in us]
                kcat = [jnp.concatenate([k_ref[prev[u], :], k_ref[cur[u], :]], axis=0).astype(BF16) for u in us]
                vcat = [jnp.concatenate([v_ref[prev[u], :], v_ref[cur[u], :]], axis=0).astype(BF16) for u in us]
                logits = [_dot_nt(q[u], kcat[u]) + bias for u in us]
                logits = [jnp.where(first_keys & (n[u] == 0), -jnp.inf, logits[u]) for u in us]
                mx = [jnp.max(x, axis=-1, keepdims=True) for x in logits]
                p = [jnp.exp(logits[u] - mx[u]) for u in us]
                den = [jnp.sum(x, axis=-1, keepdims=True) for x in p]
                pv = [_dot(p[u].astype(BF16), vcat[u]) for u in us]
                if gi > 0:
                    m_old = [m_ref[cur[u], :] for u in us]
                    l_old = [l_ref[cur[u], :] for u in us]
                    acc_old = [acc_ref[cur[u], :] for u in us]
                    m_new = [jnp.maximum(m_old[u], mx[u]) for u in us]
                    c_old = [jnp.exp(m_old[u] - m_new[u]) for u in us]
                    c_new = [jnp.exp(mx[u] - m_new[u]) for u in us]
                    pv = [acc_old[u] * c_old[u] + pv[u] * c_new[u] for u in us]
                    den = [l_old[u] * c_old[u] + den[u] * c_new[u] for u in us]
                    mx = m_new
                for u in us:
                    acc_ref[cur[u], :] = pv[u]
                    m_ref[cur[u], :] = mx[u]
                    l_ref[cur[u], :] = den[u]
                return carry

            lax.fori_loop(0, T // (blk * B_UNROLL), blocks, 0)

    @pl.when(g == len(B_DILATIONS) - 1)
    def _():
        o_ref[...] = (acc_ref[...] / l_ref[...]).astype(o_ref.dtype)


def attn_prompt(qkv, bias, n_seq, T):
    H, Dh, G = B_HEADS, B_HEAD_DIM, len(B_DILATIONS)

    def col(which):
        return lambda b, h, g: (b, (g * 3 + which) * H + h)

    return pl.pallas_call(
        functools.partial(_attn_prompt_body, T=T), grid=(n_seq, H, G),
        in_specs=[pl.BlockSpec((T, Dh), col(0)), pl.BlockSpec((T, Dh), col(1)), pl.BlockSpec((T, Dh), col(2)),
                  pl.BlockSpec((None, None, B_BLK, 2 * B_BLK), lambda b, h, g: (g, h, 0, 0))],
        out_specs=pl.BlockSpec((T, Dh), lambda b, h, g: (b, h)),
        out_shape=jax.ShapeDtypeStruct((n_seq * T, H * Dh), BF16),
        scratch_shapes=[pltpu.VMEM((T, Dh), F32), pltpu.VMEM((T, 1), F32), pltpu.VMEM((T, 1), F32)],
        compiler_params=_params(("parallel", "parallel", "arbitrary")), name="attn_prompt",
    )(qkv, qkv, qkv, bias)


def _attn_sample_body(q_ref, k_ref, v_ref, c0_ref, c1_ref, c2_ref, bias_ref, o_ref, *, t_valid):
    blk = B_BLK
    scale = B_HEAD_DIM ** -0.5
    caches = (c0_ref, c1_ref, c2_ref)
    o_ref[...] = jnp.zeros_like(o_ref)
    for t in range(t_valid):
        m_run = l_run = acc = None
        for gi, dil in enumerate(B_DILATIONS):
            q = q_ref[t, gi] * scale
            c_ref = caches[gi]
            if dil == 1:
                kcat = jnp.concatenate([c_ref[t:, 0, 0], k_ref[:t + 1, gi]], axis=0)
                vcat = jnp.concatenate([c_ref[t:, 0, 1], v_ref[:t + 1, gi]], axis=0)
            else:
                kcat = jnp.concatenate([c_ref[:, t, 0], k_ref[t:t + 1, gi]], axis=0)
                vcat = jnp.concatenate([c_ref[:, t, 1], v_ref[t:t + 1, gi]], axis=0)
            logits = jnp.sum(q[None] * kcat, axis=-1, keepdims=True) + bias_ref[gi, :blk + 1]
            mx = jnp.max(logits, axis=0)
            p = jnp.exp(logits - mx[None])
            den = jnp.sum(p, axis=0)
            pv = jnp.sum(p * vcat, axis=0)
            if gi == 0:
                m_run, l_run, acc = mx, den, pv
            else:
                m_new = jnp.maximum(m_run, mx)
                c_old, c_new = jnp.exp(m_run - m_new), jnp.exp(mx - m_new)
                acc = acc * c_old + pv * c_new
                l_run = l_run * c_old + den * c_new
                m_run = m_new
        o_ref[t] = (acc / l_run).astype(o_ref.dtype)


def attn_sample(qkv, caches, bias, n_seq, T, row0, t_valid):
    H, Dh, G = B_HEADS, B_HEAD_DIM, len(B_DILATIONS)
    rb0 = row0 // T
    assert t_valid <= min(d for d in B_DILATIONS if d > 1)
    q5 = qkv.reshape(qkv.shape[0], G, 3, H, Dh)

    def spec(which):
        return pl.BlockSpec((T, G, None, H, Dh), lambda b: (b + rb0, 0, which, 0, 0))

    cache_specs = [pl.BlockSpec((None, B_BLK, min(d, t_valid), 2, H, Dh), lambda b: (b, 0, 0, 0, 0, 0))
                   for d in B_DILATIONS]
    return pl.pallas_call(
        functools.partial(_attn_sample_body, t_valid=t_valid), grid=(n_seq,),
        in_specs=[spec(0), spec(1), spec(2)] + cache_specs + [pl.BlockSpec(bias.shape, lambda b: (0, 0, 0, 0))],
        out_specs=pl.BlockSpec((T, H, Dh), lambda b: (b, 0, 0)),
        out_shape=jax.ShapeDtypeStruct((n_seq * T, H, Dh), BF16),
        compiler_params=_params(("parallel",)), name="attn_sample",
    )(q5, q5, q5, *caches, bias)


def _mlstm_body(q_ref, k_ref, v_ref, o_ref, gate_ref, gb_ref, nw_ref, c0_ref, n0_ref, m0_ref,
                y_ref, cT_ref, nT_ref, mT_ref, C_scr, n_scr, m_scr, *, L, TB, t_valid):
    H, E, V = C_HEADS, C_QK_DIM, C_V_DIM
    tb = pl.program_id(1)

    @pl.when(tb == 0)
    def _():
        C_scr[...] = c0_ref[...]
        n_scr[...] = n0_ref[...]
        m_scr[...] = m0_ref[...]

    row = lax.broadcasted_iota(jnp.int32, (L, L), 0)
    col = lax.broadcasted_iota(jnp.int32, (L, L), 1)
    causal = row >= col
    tril = causal.astype(F32)
    gb = gb_ref[...]
    nw = nw_ref[...]
    lane = lax.broadcasted_iota(jnp.int32, (L, LANES), 1)

    def chunk(c, carry):
        sl = pl.ds(pl.multiple_of(c * L, L), L)
        gact = C_GATE_CAP * jnp.tanh((gate_ref[sl, :] + gb) / C_GATE_CAP)
        lf = jnp.minimum(gact, 0.0) - jnp.log(1.0 + jnp.exp(-jnp.abs(gact)))
        ig = gact
        valid = None
        if t_valid is not None:
            t_idx = tb * TB + c * L + lax.broadcasted_iota(jnp.int32, (L, 1), 0)
            valid = t_idx < t_valid
            ig = jnp.where(valid, ig, -1e30)
            lf = jnp.where(valid, lf, 0.0)
        bcum = _dot(tril, lf, HIGHEST)
        ig_t = ig.T
        bcum_t = bcum.T
        heads = range(H)
        es = [slice(h * E, (h + 1) * E) for h in heads]
        vs = [slice(h * V, (h + 1) * V) for h in heads]
        b_col = [bcum[:, H + h:H + h + 1] for h in heads]
        b_row = [bcum_t[H + h:H + h + 1, :] for h in heads]
        ig_col = [ig[:, h:h + 1] for h in heads]
        ig_row = [ig_t[h:h + 1, :] for h in heads]
        q = [q_ref[sl, s] for s in es]
        k = [k_ref[sl, s] * (E ** -0.5) for s in es]
        v = [v_ref[sl, s] for s in vs]
        if valid is not None:
            q, k, v = ([jnp.where(valid, t, 0.0) for t in ts] for ts in (q, k, v))
        m_prev = [m_scr[h:h + 1, 0:1] for h in heads]
        n_prev = [n_scr[h:h + 1, :] for h in heads]
        C = [C_scr[h] for h in heads]
        dm = [jnp.where(causal, b_col[h] - b_row[h] + ig_row[h], -jnp.inf) for h in heads]
        inter = [b_col[h] + m_prev[h] for h in heads]
        mt = [jnp.maximum(inter[h], jnp.max(dm[h], axis=-1, keepdims=True)) for h in heads]
        w_d = [jnp.exp(dm[h] - mt[h]) for h in heads]
        w_i = [jnp.exp(inter[h] - mt[h]) for h in heads]
        qb, kb, vb = ([t.astype(BF16) for t in ts] for ts in (q, k, v))
        sc = [_dot_nt(qb[h], kb[h]) * w_d[h] for h in heads]
        qc = [_dot_nt(qb[h], C[h].astype(BF16)) for h in heads]
        num = [_dot(sc[h].astype(BF16), vb[h]) + w_i[h] * qc[h] for h in heads]
        den = [jnp.sum(sc[h], axis=-1, keepdims=True) + w_i[h] * jnp.sum(q[h] * n_prev[h], axis=-1, keepdims=True)
               for h in heads]
        hh = [num[h] / jnp.maximum(jnp.abs(den[h]), jnp.exp(-mt[h])) for h in heads]
        m_new = [x[L - 1:L, :] for x in mt]
        b_end = [x[L - 1:L, :] for x in b_col]
        w_s = [jnp.exp(b_end[h] - b_col[h] + ig_col[h] - m_new[h]) for h in heads]
        dec = [jnp.exp(b_end[h] + m_prev[h] - m_new[h]) for h in heads]
        c_upd = [_dot_tn((w_s[h] * v[h]).astype(BF16), kb[h]) for h in heads]
        outs = []
        for h in heads:
            C_scr[h] = dec[h] * C[h] + c_upd[h]
            n_scr[h:h + 1, :] = dec[h] * n_prev[h] + jnp.sum(w_s[h] * k[h], axis=0, keepdims=True)
            m_scr[h:h + 1, :] = jnp.broadcast_to(m_new[h], (1, LANES))
            hn = hh[h] * lax.rsqrt(jnp.mean(hh[h] * hh[h], axis=-1, keepdims=True) + NORM_EPS) * nw[:, vs[h]]
            outs.append(hn * _sigmoid(o_ref[sl, vs[h]]))
        y_ref[sl, :] = jnp.concatenate(outs, axis=-1).astype(y_ref.dtype)
        return carry

    lax.fori_loop(0, TB // L, chunk, 0)

    @pl.when(tb == pl.num_programs(1) - 1)
    def _():
        cT_ref[...] = C_scr[...]
        nT_ref[...] = n_scr[...]
        mT_ref[...] = m_scr[...]


def mlstm_recurrence(proj, gate_bias, norm_w, c0, n0, m0, n_seq, T, row0, TB, t_valid):
    H, E, V = C_HEADS, C_QK_DIM, C_V_DIM
    nb = T // TB
    rb0 = row0 // TB
    HE, HV = H * E, H * V

    def cols(cb):
        return lambda b, t: (rb0 + b * nb + t, cb)

    st4 = lambda b, t: (b, 0, 0, 0)
    st3 = lambda b, t: (b, 0, 0)
    return pl.pallas_call(
        functools.partial(_mlstm_body, L=CHUNK, TB=TB, t_valid=t_valid), grid=(n_seq, nb),
        in_specs=[pl.BlockSpec((TB, HE), cols(0)), pl.BlockSpec((TB, HE), cols(1)),
                  pl.BlockSpec((TB, HV), cols(2 * HE // HV)), pl.BlockSpec((TB, HV), cols(2 * HE // HV + 1)),
                  pl.BlockSpec((TB, LANES), cols((2 * HE + 2 * HV) // LANES)),
                  pl.BlockSpec((1, LANES), lambda b, t: (0, 0)), pl.BlockSpec((1, HV), lambda b, t: (0, 0)),
                  pl.BlockSpec((None, H, V, E), st4), pl.BlockSpec((None, H, E), st3),
                  pl.BlockSpec((None, H, LANES), st3)],
        out_specs=[pl.BlockSpec((TB, HV), lambda b, t: (b * nb + t, 0)),
                   pl.BlockSpec((None, H, V, E), st4), pl.BlockSpec((None, H, E), st3),
                   pl.BlockSpec((None, H, LANES), st3)],
        out_shape=[jax.ShapeDtypeStruct((n_seq * T, HV), BF16), jax.ShapeDtypeStruct((n_seq, H, V, E), F32),
                   jax.ShapeDtypeStruct((n_seq, H, E), F32), jax.ShapeDtypeStruct((n_seq, H, LANES), F32)],
        scratch_shapes=[pltpu.VMEM((H, V, E), F32), pltpu.VMEM((H, E), F32), pltpu.VMEM((H, LANES), F32)],
        compiler_params=_params(("parallel", "arbitrary")), name="mlstm_recurrence",
    )(proj, proj, proj, proj, proj, gate_bias, norm_w, c0, n0, m0)


def _tile(n, target):
    return max(t for t in range(LANES, min(n, target) + 1, LANES) if n % t == 0)


def _pad_cols(w, n):
    return jnp.pad(w, ((0, 0), (0, n - w.shape[1])))


def _pad_rows(w, n):
    return jnp.pad(w, ((0, n - w.shape[0]), (0, 0)))


def kernel(x_prompt, x_sample, state_a_wkv, state_a_shift, cache_b_kv_g0, cache_b_kv_g1, cache_b_kv_g2, state_c_C, state_c_n, state_c_m, rel_bias, norm_ffn1, ffn1_w_in, ffn1_w_out, norm_mix, norm_ffn2, ffn2_w_in, ffn2_w_out, norm_final, a_mu, a_w_rkv, a_w0, a_w1, a_w2, a_a0, a_a1, a_a2, a_g1, a_g2, a_k_k, a_k_a, a_r_k, a_gn_w, a_gn_b, a_w_out, a_v0, a_v1, a_v2, b_w_qkv, b_w_out, c_w_in, c_b_gates, c_norm_w, c_w_out):
    Bp, Tp, D = x_prompt.shape
    Bs, Ts, _ = x_sample.shape
    depth = norm_mix.shape[0]
    Tsp = SAMPLE_PAD
    Mp, Ms = Bp * Tp, Bs * Tsp
    M = Mp + Ms
    TM = 512
    TN = _tile(D, 512)
    TF = _tile(ffn1_w_out.shape[1], 512)
    H_a = D // A_HEAD_DIM
    G, H_b, Dh = len(B_DILATIONS), B_HEADS, B_HEAD_DIM
    bf = lambda w: w.astype(BF16)

    x = jnp.concatenate([x_prompt.reshape(Mp, D),
                         jnp.pad(x_sample, ((0, 0), (0, Tsp - Ts), (0, 0))).reshape(Ms, D)], axis=0)

    def last_rows(t):
        return t[:Mp].reshape(Bp, Tp, -1)[:, -1], t[Mp:].reshape(Bs, Tsp, -1)[:, Ts - 1]

    qi = jnp.arange(B_BLK)[:, None]
    kj = jnp.arange(2 * B_BLK)[None, :]
    step = qi + B_BLK - kj
    step_ok = (step >= 0) & (step <= B_BLK)
    m_desc = B_BLK - jnp.arange(B_BLK + 8)
    bias_p, bias_s = [], []
    buckets = jnp.arange(N_BUCKETS)
    for gi, dil in enumerate(B_DILATIONS):
        tab = rel_bias[:, gi * H_b:(gi + 1) * H_b].astype(F32)
        hot = (_rel_bucket(jnp.clip(step, 0, B_BLK) * dil)[None] == buckets[:, None, None]).astype(F32)
        bp = jnp.einsum("nh,nqk->hqk", tab, hot, precision=HIGHEST)
        bias_p.append(jnp.where(step_ok[None], bp, -jnp.inf))
        bs = tab[_rel_bucket(jnp.maximum(m_desc, 0) * dil)]
        bias_s.append(jnp.broadcast_to(bs[:, :, None], (B_BLK + 8, H_b, Dh)))
    bias_p, bias_s = jnp.stack(bias_p), jnp.stack(bias_s)
    ffn1_in, ffn1_out, ffn2_in, ffn2_out = bf(ffn1_w_in), bf(ffn1_w_out), bf(ffn2_w_in), bf(ffn2_w_out)

    outs_a_wkv, outs_a_shift, outs_c = ([], []), ([], []), ([], [], [], [], [], [])
    outs_b = [([], []) for _ in range(G)]
    v_first = None
    for i in range(depth):
        x = ffn(x, norm_ffn1[i], ffn1_in, ffn1_out, i, TM, TF)
        kind, j = i % 3, i // 3
        if kind == 0:
            u = rmsnorm(x, norm_mix[i], TM)
            up, us = u[:Mp].reshape(Bp, Tp, D), u[Mp:].reshape(Bs, Tsp, D)
            u_prev = jnp.concatenate([
                jnp.concatenate([jnp.zeros((Bp, 1, D), F32), up[:, :-1]], axis=1).reshape(Mp, D),
                jnp.concatenate([state_a_shift[j][:, None, :], us[:, :-1]], axis=1).reshape(Ms, D)], axis=0)
            mu = a_mu[j]
            rkv = rwkv_rkv(u, u_prev, mu[jnp.array([0, 2, 3])][:, None, :], bf(a_w_rkv[j]), TM, TN)
            lr = LANES
            w_br = (bf(_pad_cols(a_w1[j], lr)), bf(_pad_rows(a_w2[j], lr)), a_w0[j].reshape(1, D))
            a_br = (bf(_pad_cols(a_a1[j], lr)), bf(_pad_rows(a_a2[j], lr)), a_a0[j].reshape(1, D))
            g_br = (bf(a_g1[j]), bf(a_g2[j]))
            v_br = None
            if j > 0:
                v_br = (bf(_pad_cols(a_v1[j - 1], lr)), bf(_pad_rows(a_v2[j - 1], lr)), a_v0[j - 1].reshape(1, D))
            lora = rwkv_lora(u, u_prev, mu[jnp.array([1, 4, 5, 3])], w_br, a_br, g_br, v_br, 256)
            lw, a_lr, gate = lora[:3]
            vres = None if j == 0 else (v_first, lora[3])
            if j == 0:
                v_first = rkv[2]
            par = tuple(p.reshape(1, D) for p in (a_k_k[j], a_k_a[j], a_r_k[j], a_gn_w[j], a_gn_b[j]))
            s0p = jnp.zeros((Bp * H_a, A_HEAD_DIM, A_HEAD_DIM), F32)
            s0s = state_a_wkv[j].reshape(Bs * H_a, A_HEAD_DIM, A_HEAD_DIM)
            yp, sp = rwkv_recurrence(rkv, lw, a_lr, gate, vres, par, s0p, Bp, Tp, 0, 512, None)
            ys, ss = rwkv_recurrence(rkv, lw, a_lr, gate, vres, par, s0s, Bs, Tsp, Mp, Tsp, Ts)
            x = matmul(jnp.concatenate([yp, ys], axis=0), bf(a_w_out[j]), TM, TN, residual=x)
            outs_a_wkv[0].append(sp.reshape(Bp, H_a, A_HEAD_DIM, A_HEAD_DIM))
            outs_a_wkv[1].append(ss.reshape(Bs, H_a, A_HEAD_DIM, A_HEAD_DIM))
            sh_p, sh_s = last_rows(u)
            outs_a_shift[0].append(sh_p)
            outs_a_shift[1].append(sh_s)
        elif kind == 1:
            qkv = matmul(x, bf(b_w_qkv[j]), TM, _tile(b_w_qkv.shape[2], 1024), norm_g=norm_mix[i])
            caches = [c[j].reshape(Bs, B_BLK, d, 2, H_b, Dh)
                      for c, d in zip((cache_b_kv_g0, cache_b_kv_g1, cache_b_kv_g2), B_DILATIONS)]
            op = attn_prompt(qkv, bias_p, Bp, Tp)
            os_ = attn_sample(qkv, caches, bias_s, Bs, Tsp, Mp, Ts)
            x = matmul(jnp.concatenate([op, os_.reshape(Ms, H_b * Dh)], axis=0), bf(b_w_out[j]), TM, TN, residual=x)
            qp = qkv[:Mp].reshape(Bp, Tp, G, 3, H_b, Dh)
            qs = qkv[Mp:].reshape(Bs, Tsp, G, 3, H_b, Dh)
            for gi in range(G):
                keep = min(B_WINDOWS[gi], Tp)
                outs_b[gi][0].append(qp[:, Tp - keep:, gi, 1:3])
                outs_b[gi][1].append(qs[:, :Ts, gi, 1:3])
        else:
            H, E, V = C_HEADS, C_QK_DIM, C_V_DIM
            n_in = c_w_in.shape[2]
            n_pad = -(-n_in // LANES) * LANES
            proj = matmul(x, bf(_pad_cols(c_w_in[j], n_pad)), TM, _tile(n_pad, 1024), norm_g=norm_mix[i])
            gbias = _pad_cols(c_b_gates[j].reshape(1, 2 * H), LANES)
            nw = c_norm_w[j].reshape(1, H * V)
            zc = (jnp.zeros((Bp, H, V, E), F32), jnp.zeros((Bp, H, E), F32), jnp.zeros((Bp, H, LANES), F32))
            sc = (state_c_C[j], state_c_n[j], jnp.broadcast_to(state_c_m[j][:, :, None], (Bs, H, LANES)))
            hp, cp, np_, mp = mlstm_recurrence(proj, gbias, nw, *zc, Bp, Tp, 0, 256, None)
            hs, cs, ns, ms = mlstm_recurrence(proj, gbias, nw, *sc, Bs, Tsp, Mp, Tsp, Ts)
            x = matmul(jnp.concatenate([hp, hs], axis=0), bf(c_w_out[j]), TM, TN, residual=x)
            for lst, val in zip(outs_c, (cp, cs, np_, ns, mp[:, :, 0], ms[:, :, 0])):
                lst.append(val)
        x = ffn(x, norm_ffn2[i], ffn2_in, ffn2_out, i, TM, TF)

    y = rmsnorm(x, norm_final, TM)
    y_prompt = y[:Mp].reshape(Bp, Tp, D)
    y_sample = y[Mp:].reshape(Bs, Tsp, D)[:, :Ts]
    st = jnp.stack
    return (y_prompt, y_sample, st(outs_a_wkv[0]), st(outs_a_wkv[1]), st(outs_a_shift[0]), st(outs_a_shift[1]),
            st(outs_b[0][0]), st(outs_b[0][1]), st(outs_b[1][0]), st(outs_b[1][1]), st(outs_b[2][0]), st(outs_b[2][1]),
            st(outs_c[0]), st(outs_c[1]), st(outs_c[2]), st(outs_c[3]), st(outs_c[4]), st(outs_c[5]))
```

```python
import functools
import math

import jax
import jax.numpy as jnp
from jax import lax
from jax.experimental import pallas as pl
from jax.experimental.pallas import tpu as pltpu

F32 = jnp.float32
BF16 = jnp.bfloat16
HIGHEST = lax.Precision.HIGHEST

NORM_EPS = 1e-6
A_HEAD_DIM = 64
A_GN_EPS = 64e-5
A_REC_LANES = 1024
A_REC_PRECISION = "bf16"
B_WINDOWS = (128, 512, 2048)
B_DILATIONS = (1, 4, 16)
B_HEADS = 16
B_HEAD_DIM = 128
B_BLK = 128
B_UNROLL = 4
N_BUCKETS = 32
BUCKET_MAX_DIST = 2048
C_HEADS = 8
C_QK_DIM = 128
C_V_DIM = 256
C_GATE_CAP = 15.0
CHUNK = 64
SAMPLE_PAD = 64
LANES = 128
VMEM_LIMIT = 56 * 1024 * 1024


def _params(sem):
    return pltpu.CompilerParams(dimension_semantics=sem, vmem_limit_bytes=VMEM_LIMIT)


def _dot(a, b, precision=None):
    return jnp.dot(a, b, preferred_element_type=F32, precision=precision)


def _dot_nt(a, b, precision=None):
    return lax.dot_general(a, b, (((1,), (1,)), ((), ())), preferred_element_type=F32, precision=precision)


def _dot_tn(a, b, precision=None):
    return lax.dot_general(a, b, (((0,), (0,)), ((), ())), preferred_element_type=F32, precision=precision)


_NN = (((1,), (0,)), ((), ()))
_NT = (((1,), (1,)), ((), ()))
_TN = (((0,), (0,)), ((), ()))


def _pmm(a, b, dims, mode):
    dg = functools.partial(lax.dot_general, dimension_numbers=dims, preferred_element_type=F32)
    if mode == "highest":
        return dg(a, b, precision=HIGHEST)
    a_hi, b_hi = a.astype(BF16), b.astype(BF16)
    if mode == "bf16":
        return dg(a_hi, b_hi)
    a_lo = (a - a_hi.astype(F32)).astype(BF16)
    b_lo = (b - b_hi.astype(F32)).astype(BF16)
    return dg(a_hi, b_hi) + (dg(a_hi, b_lo) + dg(a_lo, b_hi))


def _sigmoid(x):
    return 1.0 / (1.0 + jnp.exp(-x))


def _rms(x, g):
    ms = jnp.mean(x * x, axis=-1, keepdims=True)
    return x * lax.rsqrt(ms + NORM_EPS) * g


def _rmsnorm_body(x_ref, g_ref, o_ref):
    o_ref[...] = _rms(x_ref[...], g_ref[...]).astype(o_ref.dtype)


def rmsnorm(x, g, tm):
    M, D = x.shape
    return pl.pallas_call(
        _rmsnorm_body, grid=(M // tm,),
        in_specs=[pl.BlockSpec((tm, D), lambda i: (i, 0)), pl.BlockSpec((1, D), lambda i: (0, 0))],
        out_specs=pl.BlockSpec((tm, D), lambda i: (i, 0)),
        out_shape=jax.ShapeDtypeStruct((M, D), F32),
        compiler_params=_params(("parallel",)), name="rmsnorm",
    )(x, g.reshape(1, D))


def _ffn_body(x_ref, g_ref, wg_ref, wu_ref, wo_ref, o_ref, xn_ref, acc_ref):
    j = pl.program_id(1)

    @pl.when(j == 0)
    def _():
        xn_ref[...] = _rms(x_ref[...], g_ref[...]).astype(BF16)
        acc_ref[...] = jnp.zeros_like(acc_ref)

    xn = xn_ref[...]
    gate = _dot(xn, wg_ref[...])
    up = _dot(xn, wu_ref[...])
    h = (gate * _sigmoid(gate) * up).astype(BF16)
    acc_ref[...] += _dot(h, wo_ref[...])

    @pl.when(j == pl.num_programs(1) - 1)
    def _():
        o_ref[...] = x_ref[...] + 0.5 * acc_ref[...]


def ffn(x, g, w_in, w_out, layer, tm, tf):
    M, D = x.shape
    Fh = w_out.shape[1]
    nf = Fh // tf
    return pl.pallas_call(
        _ffn_body, grid=(M // tm, nf),
        in_specs=[pl.BlockSpec((tm, D), lambda i, j: (i, 0)),
                  pl.BlockSpec((1, D), lambda i, j: (0, 0)),
                  pl.BlockSpec((None, D, tf), lambda i, j: (layer, 0, j)),
                  pl.BlockSpec((None, D, tf), lambda i, j: (layer, 0, j + nf)),
                  pl.BlockSpec((None, tf, D), lambda i, j: (layer, j, 0))],
        out_specs=pl.BlockSpec((tm, D), lambda i, j: (i, 0)),
        out_shape=jax.ShapeDtypeStruct((M, D), F32),
        scratch_shapes=[pltpu.VMEM((tm, D), BF16), pltpu.VMEM((tm, D), F32)],
        compiler_params=_params(("parallel", "arbitrary")), name="ffn",
    )(x, g.reshape(1, D), w_in, w_in, w_out)


def _mm_body(*refs, has_norm, has_res):
    it = iter(refs)
    x_ref = next(it)
    g_ref = next(it) if has_norm else None
    w_ref = next(it)
    res_ref = next(it) if has_res else None
    o_ref = next(it)
    xs_ref = next(it)

    @pl.when(pl.program_id(1) == 0)
    def _():
        x = x_ref[...].astype(F32)
        if has_norm:
            x = _rms(x, g_ref[...])
        xs_ref[...] = x.astype(BF16)

    acc = _dot(xs_ref[...], w_ref[...])
    if has_res:
        acc = res_ref[...] + acc
    o_ref[...] = acc.astype(o_ref.dtype)


def matmul(x, w, tm, tn, norm_g=None, residual=None, out_dtype=F32):
    M, K = x.shape
    N = w.shape[1]
    args = [x]
    specs = [pl.BlockSpec((tm, K), lambda i, j: (i, 0))]
    if norm_g is not None:
        args.append(norm_g.reshape(1, K))
        specs.append(pl.BlockSpec((1, K), lambda i, j: (0, 0)))
    args.append(w)
    specs.append(pl.BlockSpec((K, tn), lambda i, j: (0, j)))
    if residual is not None:
        args.append(residual)
        specs.append(pl.BlockSpec((tm, tn), lambda i, j: (i, j)))
    return pl.pallas_call(
        functools.partial(_mm_body, has_norm=norm_g is not None, has_res=residual is not None),
        grid=(M // tm, N // tn), in_specs=specs,
        out_specs=pl.BlockSpec((tm, tn), lambda i, j: (i, j)),
        out_shape=jax.ShapeDtypeStruct((M, N), out_dtype),
        scratch_shapes=[pltpu.VMEM((tm, K), BF16)],
        compiler_params=_params(("parallel", "arbitrary")), name="matmul",
    )(*args)


def _rkv_body(u_ref, up_ref, mu_ref, w_ref, o_ref, xs_ref):
    @pl.when(pl.program_id(2) == 0)
    def _():
        u = u_ref[...]
        xs_ref[...] = (u + (up_ref[...] - u) * mu_ref[...]).astype(BF16)

    o_ref[...] = _dot(xs_ref[...], w_ref[...])


def rwkv_rkv(u, u_prev, mu3, w3, tm, tn):
    M, D = u.shape
    return pl.pallas_call(
        _rkv_body, grid=(M // tm, 3, D // tn),
        in_specs=[pl.BlockSpec((tm, D), lambda i, k, j: (i, 0)),
                  pl.BlockSpec((tm, D), lambda i, k, j: (i, 0)),
                  pl.BlockSpec((None, 1, D), lambda i, k, j: (k, 0, 0)),
                  pl.BlockSpec((None, D, tn), lambda i, k, j: (k, 0, j))],
        out_specs=pl.BlockSpec((None, tm, tn), lambda i, k, j: (k, i, j)),
        out_shape=jax.ShapeDtypeStruct((3, M, D), F32),
        scratch_shapes=[pltpu.VMEM((tm, D), BF16)],
        compiler_params=_params(("parallel", "arbitrary", "arbitrary")), name="rwkv_rkv",
    )(u, u_prev, mu3, w3)


def _lora_body(*refs, has_vres):
    it = iter(refs)
    u_ref, up_ref, mu_ref = next(it), next(it), next(it)
    w1, w2, w0 = next(it), next(it), next(it)
    a1, a2, a0 = next(it), next(it), next(it)
    g1, g2 = next(it), next(it)
    if has_vres:
        v1, v2, v0 = next(it), next(it), next(it)
    lw_ref, a_ref, g_ref = next(it), next(it), next(it)
    nu_ref = next(it) if has_vres else None

    u = u_ref[...]
    du = up_ref[...] - u

    def mix(n):
        return (u + du * mu_ref[n:n + 1, :]).astype(BF16)

    hw = jnp.tanh(_dot(mix(0), w1[...])).astype(BF16)
    w_pre = w0[...] + _dot(hw, w2[...])
    softplus = jnp.maximum(-w_pre, 0.0) + jnp.log(1.0 + jnp.exp(-jnp.abs(w_pre)))
    lw_ref[...] = -jnp.exp(-softplus - 0.5)
    ha = _dot(mix(1), a1[...]).astype(BF16)
    a_ref[...] = _sigmoid(a0[...] + _dot(ha, a2[...]))
    hg = _sigmoid(_dot(mix(2), g1[...])).astype(BF16)
    g_ref[...] = _dot(hg, g2[...])
    if has_vres:
        hv = _dot(mix(3), v1[...]).astype(BF16)
        nu_ref[...] = _sigmoid(v0[...] + _dot(hv, v2[...]))


def rwkv_lora(u, u_prev, mu4, w, a, g, v, tm):
    M, D = u.shape
    has_vres = v is not None
    row = lambda i: (i, 0)
    full = lambda i: (0, 0)
    args = [u, u_prev, mu4]
    specs = [pl.BlockSpec((tm, D), row), pl.BlockSpec((tm, D), row), pl.BlockSpec(mu4.shape, full)]
    for t in (w, a, g) + ((v,) if has_vres else ()):
        for m in t:
            args.append(m)
            specs.append(pl.BlockSpec(m.shape, full))
    n_out = 4 if has_vres else 3
    return pl.pallas_call(
        functools.partial(_lora_body, has_vres=has_vres), grid=(M // tm,), in_specs=specs,
        out_specs=[pl.BlockSpec((tm, D), row)] * n_out,
        out_shape=[jax.ShapeDtypeStruct((M, D), F32)] * n_out,
        compiler_params=_params(("parallel",)), name="rwkv_lora",
    )(*args)


def _rwkv_rec_body(*refs, L, TB, hb, t_valid, has_vres, prec):
    N = A_HEAD_DIM
    it = iter(refs)
    r_ref, k_ref, v_ref, lw_ref, a_ref, g_ref = (next(it) for _ in range(6))
    if has_vres:
        vf_ref, nu_ref = next(it), next(it)
    kk_ref, ka_ref, rk_ref, gnw_ref, gnb_ref, s0_ref = (next(it) for _ in range(6))
    y_ref, sT_ref, S_scr = next(it), next(it), next(it)
    tb = pl.program_id(2)

    assert L == N and 2 * N == LANES

    @pl.when(tb == 0)
    def _():
        for p in range(hb // 2):
            S_scr[p] = jnp.concatenate([s0_ref[2 * p], s0_ref[2 * p + 1]], axis=1)

    row = lax.broadcasted_iota(jnp.int32, (L, L), 0)
    col = lax.broadcasted_iota(jnp.int32, (L, L), 1)
    tril = (row >= col).astype(F32)
    row1 = lax.broadcasted_iota(jnp.int32, (L, 2 * N), 0)
    lane1 = lax.broadcasted_iota(jnp.int32, (L, 2 * N), 1)
    head0_lane = lane1 < N
    eye = ((lane1 & (N - 1)) == row1).astype(F32)
    row2 = lax.broadcasted_iota(jnp.int32, (2 * L, 2 * N), 0)
    lane2 = lax.broadcasted_iota(jnp.int32, (2 * L, 2 * N), 1)
    mask2 = (lane2 & (N - 1)) < jnp.where(row2 < L, row2, row2 - L + 1)
    bd_mask = (row2 // L) == (lane2 // N)
    ones_bd = bd_mask.astype(BF16)
    n_sq = int(math.log2(L)) - 1
    kk_p, ka_p, rk_p, gnw, gnb = kk_ref[...], ka_ref[...], rk_ref[...], gnw_ref[...], gnb_ref[...]

    def chunk(c, carry):
        sl = pl.ds(pl.multiple_of(c * L, L), L)
        r, k, v, lw, a, g = r_ref[sl, :], k_ref[sl, :], v_ref[sl, :], lw_ref[sl, :], a_ref[sl, :], g_ref[sl, :]
        if has_vres:
            v = v + (vf_ref[sl, :] - v) * nu_ref[sl, :]
        if t_valid is not None:
            t_idx = tb * TB + c * L + lax.broadcasted_iota(jnp.int32, (L, 1), 0)
            valid = t_idx < t_valid
            r, k, v, lw = (jnp.where(valid, t, 0.0) for t in (r, k, v, lw))
        cum = _dot(tril, lw, HIGHEST)
        cum_end = cum[L - 1:L, :]
        w_cur, w_prev, w_inv, w_rem, w_end = (jnp.exp(cum), jnp.exp(cum - lw), jnp.exp(-cum),
                                              jnp.exp(cum_end - cum), jnp.exp(cum_end))
        mm, mm_nt, mm_tn = (functools.partial(_pmm, dims=d, mode=prec) for d in (_NN, _NT, _TN))
        pairs = range(hb // 2)
        ps = [slice(p * LANES, (p + 1) * LANES) for p in pairs]

        def head_sum(x):
            x_hi = x.astype(BF16)
            x_lo = (x - x_hi.astype(F32)).astype(BF16)
            return _dot(x_hi, ones_bd) + _dot(x_lo, ones_bd)

        def bdiag(x):
            return jnp.where(bd_mask, jnp.concatenate([x, x], axis=0), 0.0)

        kkp = [k[:, s] * kk_p[:, s] for s in ps]
        kk = [x / jnp.maximum(jnp.sqrt(head_sum(x * x)), 1e-12) for x in kkp]
        b = [kk[p] * a[:, ps[p]] for p in pairs]
        k2 = [k[:, s] * (1.0 + (a[:, s] - 1.0) * ka_p[:, s]) for s in ps]
        lhs2 = [jnp.concatenate([kk[p] * w_prev[:, ps[p]], r[:, ps[p]] * w_cur[:, ps[p]]], axis=0) for p in pairs]
        kd = [k2[p] * w_inv[:, ps[p]] for p in pairs]
        bd = [b[p] * w_inv[:, ps[p]] for p in pairs]
        kend = [k2[p] * w_rem[:, ps[p]] for p in pairs]
        bend = [b[p] * w_rem[:, ps[p]] for p in pairs]
        kkd = [x[:L] for x in lhs2]
        rd = [x[L:] for x in lhs2]
        a_k = [jnp.where(mask2, mm_nt(lhs2[p], bdiag(kd[p])), 0.0) for p in pairs]
        a_b = [jnp.where(mask2, mm_nt(lhs2[p], bdiag(bd[p])), 0.0) for p in pairs]
        a_kb = [x[:L] for x in a_b]
        a_rb = [x[L:] for x in a_b]
        a_v = [mm(a_k[p], bdiag(v[:, ps[p]])) for p in pairs]
        t_inv = [eye - x for x in a_kb]
        pw = a_kb
        for _ in range(n_sq):
            pw = [mm(x, bdiag(x)) for x in pw]
            t_inv = [t_inv[p] + mm(t_inv[p], bdiag(pw[p])) for p in pairs]
        k_t = [mm(t_inv[p], bdiag(kkd[p])) for p in pairs]
        c_u = [mm(t_inv[p], bdiag(a_v[p][:L])) for p in pairs]
        y_k = [rd[p] - mm(a_rb[p], bdiag(k_t[p])) for p in pairs]
        y_c = [a_v[p][L:] - mm(a_rb[p], bdiag(c_u[p])) for p in pairs]
        S = [S_scr[p] for p in pairs]
        y = [mm_nt(y_k[p], bdiag(S[p])) + y_c[p] for p in pairs]
        ktb = [jnp.where(bd_mask, mm_tn(k_t[p], bend[p]), 0.0) for p in pairs]
        s_full = [mm_tn(v[:, ps[p]], kend[p]) - mm_tn(c_u[p], bend[p]) for p in pairs]
        S_c = [jnp.where(head0_lane, x[:N], x[N:]) for x in s_full]
        for p in pairs:
            S_scr[p] = S[p] * w_end[:, ps[p]] - mm(S[p], ktb[p]) + S_c[p]
        mean = [head_sum(y[p]) * (1.0 / N) for p in pairs]
        var = [head_sum(jnp.square(y[p] - mean[p])) * (1.0 / N) for p in pairs]
        bonus = [head_sum(r[:, ps[p]] * k2[p] * rk_p[:, ps[p]]) * v[:, ps[p]] for p in pairs]
        outs = [((y[p] - mean[p]) * lax.rsqrt(var[p] + A_GN_EPS) * gnw[:, ps[p]] + gnb[:, ps[p]] + bonus[p])
                * g[:, ps[p]] for p in pairs]
        y_ref[sl, :] = jnp.concatenate(outs, axis=-1).astype(y_ref.dtype)
        return carry

    lax.fori_loop(0, TB // L, chunk, 0)

    @pl.when(tb == pl.num_programs(2) - 1)
    def _():
        for p in range(hb // 2):
            S = S_scr[p]
            sT_ref[2 * p] = S[:, :N]
            sT_ref[2 * p + 1] = S[:, N:]


def rwkv_recurrence(rkv, lw, a, g, vres, params, s0, n_seq, T, row0, TB, t_valid):
    _, M, D = rkv.shape
    N = A_HEAD_DIM
    LW = min(A_REC_LANES, D)
    hb = LW // N
    nb = T // TB
    rb0 = row0 // TB
    has_vres = vres is not None
    seq = lambda b, h, t: (rb0 + b * nb + t, h)
    args, specs = [], []
    for n in range(3):
        args.append(rkv)
        specs.append(pl.BlockSpec((None, TB, LW), lambda b, h, t, n=n: (n, rb0 + b * nb + t, h)))
    for x in (lw, a, g) + (tuple(vres) if has_vres else ()):
        args.append(x)
        specs.append(pl.BlockSpec((TB, LW), seq))
    for p in params:
        args.append(p)
        specs.append(pl.BlockSpec((1, LW), lambda b, h, t: (0, h)))
    args.append(s0)
    specs.append(pl.BlockSpec((hb, N, N), lambda b, h, t: (b * (D // LW) + h, 0, 0)))
    return pl.pallas_call(
        functools.partial(_rwkv_rec_body, L=CHUNK, TB=TB, hb=hb, t_valid=t_valid, has_vres=has_vres,
                          prec=A_REC_PRECISION),
        grid=(n_seq, D // LW, nb), in_specs=specs,
        out_specs=[pl.BlockSpec((TB, LW), lambda b, h, t: (b * nb + t, h)),
                   pl.BlockSpec((hb, N, N), lambda b, h, t: (b * (D // LW) + h, 0, 0))],
        out_shape=[jax.ShapeDtypeStruct((n_seq * T, D), BF16),
                   jax.ShapeDtypeStruct((n_seq * (D // N), N, N), F32)],
        scratch_shapes=[pltpu.VMEM((hb // 2, N, 2 * N), F32)],
        compiler_params=_params(("parallel", "parallel", "arbitrary")), name="rwkv_recurrence",
    )(*args)


def _rel_bucket(dist):
    exact = N_BUCKETS // 2
    d = jnp.maximum(dist, 1).astype(F32)
    log_b = exact + (jnp.log(d / exact) / math.log(BUCKET_MAX_DIST / exact) * (N_BUCKETS - exact)).astype(jnp.int32)
    return jnp.where(dist < exact, dist, jnp.minimum(log_b, N_BUCKETS - 1))


def _attn_prompt_body(q_ref, k_ref, v_ref, bias_ref, o_ref, acc_ref, m_ref, l_ref, *, T):
    g = pl.program_id(2)
    blk = B_BLK
    scale = B_HEAD_DIM ** -0.5
    first_keys = lax.broadcasted_iota(jnp.int32, (blk, 2 * blk), 1) < blk

    for gi, dil in enumerate(B_DILATIONS):
        @pl.when(g == gi)
        def _(gi=gi, dil=dil):
            span = blk * dil
            bias = bias_ref[...]

            def blocks(it, carry):
                us = range(B_UNROLL)
                idx = [it * B_UNROLL + u for u in us]
                n = [i // dil for i in idx]
                start = [n[u] * span + (idx[u] - n[u] * dil) for u in us]
                cur = [pl.ds(s, blk, stride=dil) for s in start]
                prev = [pl.ds(jnp.maximum(s - span, 0), blk, stride=dil) for s in start]
                q = [(q_ref[cur[u], :] * scale).astype(BF16) for u in us]
                kcat = [jnp.concatenate([k_ref[prev[u], :], k_ref[cur[u], :]], axis=0).astype(BF16) for u in us]
                vcat = [jnp.concatenate([v_ref[prev[u], :], v_ref[cur[u], :]], axis=0).astype(BF16) for u in us]
                logits = [_dot_nt(q[u], kcat[u]) + bias for u in us]
                logits = [jnp.where(first_keys & (n[u] == 0), -jnp.inf, logits[u]) for u in us]
                mx = [jnp.max(x, axis=-1, keepdims=True) for x in logits]
                p = [jnp.exp(logits[u] - mx[u]) for u in us]
                den = [jnp.sum(x, axis=-1, keepdims=True) for x in p]
                pv = [_dot(p[u].astype(BF16), vcat[u]) for u in us]
                if gi > 0:
                    m_old = [m_ref[cur[u], :] for u in us]
                    l_old = [l_ref[cur[u], :] for u in us]
                    acc_old = [acc_ref[cur[u], :] for u in us]
                    m_new = [jnp.maximum(m_old[u], mx[u]) for u in us]
                    c_old = [jnp.exp(m_old[u] - m_new[u]) for u in us]
                    c_new = [jnp.exp(mx[u] - m_new[u]) for u in us]
                    pv = [acc_old[u] * c_old[u] + pv[u] * c_new[u] for u in us]
                    den = [l_old[u] * c_old[u] + den[u] * c_new[u] for u in us]
                    mx = m_new
                for u in us:
                    acc_ref[cur[u], :] = pv[u]
                    m_ref[cur[u], :] = mx[u]
                    l_ref[cur[u], :] = den[u]
                return carry

            lax.fori_loop(0, T // (blk * B_UNROLL), blocks, 0)

    @pl.when(g == len(B_DILATIONS) - 1)
    def _():
        o_ref[...] = (acc_ref[...] / l_ref[...]).astype(o_ref.dtype)


def attn_prompt(qkv, bias, n_seq, T):
    H, Dh, G = B_HEADS, B_HEAD_DIM, len(B_DILATIONS)

    def col(which):
        return lambda b, h, g: (b, (g * 3 + which) * H + h)

    return pl.pallas_call(
        functools.partial(_attn_prompt_body, T=T), grid=(n_seq, H, G),
        in_specs=[pl.BlockSpec((T, Dh), col(0)), pl.BlockSpec((T, Dh), col(1)), pl.BlockSpec((T, Dh), col(2)),
                  pl.BlockSpec((None, None, B_BLK, 2 * B_BLK), lambda b, h, g: (g, h, 0, 0))],
        out_specs=pl.BlockSpec((T, Dh), lambda b, h, g: (b, h)),
        out_shape=jax.ShapeDtypeStruct((n_seq * T, H * Dh), BF16),
        scratch_shapes=[pltpu.VMEM((T, Dh), F32), pltpu.VMEM((T, 1), F32), pltpu.VMEM((T, 1), F32)],
        compiler_params=_params(("parallel", "parallel", "arbitrary")), name="attn_prompt",
    )(qkv, qkv, qkv, bias)


def _attn_sample_body(q_ref, k_ref, v_ref, c0_ref, c1_ref, c2_ref, bias_ref, o_ref, *, t_valid):
    blk = B_BLK
    scale = B_HEAD_DIM ** -0.5
    caches = (c0_ref, c1_ref, c2_ref)
    o_ref[...] = jnp.zeros_like(o_ref)
    for t in range(t_valid):
        m_run = l_run = acc = None
        for gi, dil in enumerate(B_DILATIONS):
            q = q_ref[t, gi] * scale
            c_ref = caches[gi]
            if dil == 1:
                kcat = jnp.concatenate([c_ref[t:, 0, 0], k_ref[:t + 1, gi]], axis=0)
                vcat = jnp.concatenate([c_ref[t:, 0, 1], v_ref[:t + 1, gi]], axis=0)
            else:
                kcat = jnp.concatenate([c_ref[:, t, 0], k_ref[t:t + 1, gi]], axis=0)
                vcat = jnp.concatenate([c_ref[:, t, 1], v_ref[t:t + 1, gi]], axis=0)
            logits = jnp.sum(q[None] * kcat, axis=-1, keepdims=True) + bias_ref[gi, :blk + 1]
            mx = jnp.max(logits, axis=0)
            p = jnp.exp(logits - mx[None])
            den = jnp.sum(p, axis=0)
            pv = jnp.sum(p * vcat, axis=0)
            if gi == 0:
                m_run, l_run, acc = mx, den, pv
            else:
                m_new = jnp.maximum(m_run, mx)
                c_old, c_new = jnp.exp(m_run - m_new), jnp.exp(mx - m_new)
                acc = acc * c_old + pv * c_new
                l_run = l_run * c_old + den * c_new
                m_run = m_new
        o_ref[t] = (acc / l_run).astype(o_ref.dtype)


def attn_sample(qkv, caches, bias, n_seq, T, row0, t_valid):
    H, Dh, G = B_HEADS, B_HEAD_DIM, len(B_DILATIONS)
    rb0 = row0 // T
    assert t_valid <= min(d for d in B_DILATIONS if d > 1)
    q5 = qkv.reshape(qkv.shape[0], G, 3, H, Dh)

    def spec(which):
        return pl.BlockSpec((T, G, None, H, Dh), lambda b: (b + rb0, 0, which, 0, 0))

    cache_specs = [pl.BlockSpec((None, B_BLK, min(d, t_valid), 2, H, Dh), lambda b: (b, 0, 0, 0, 0, 0))
                   for d in B_DILATIONS]
    return pl.pallas_call(
        functools.partial(_attn_sample_body, t_valid=t_valid), grid=(n_seq,),
        in_specs=[spec(0), spec(1), spec(2)] + cache_specs + [pl.BlockSpec(bias.shape, lambda b: (0, 0, 0, 0))],
        out_specs=pl.BlockSpec((T, H, Dh), lambda b: (b, 0, 0)),
        out_shape=jax.ShapeDtypeStruct((n_seq * T, H, Dh), BF16),
        compiler_params=_params(("parallel",)), name="attn_sample",
    )(q5, q5, q5, *caches, bias)


def _mlstm_body(q_ref, k_ref, v_ref, o_ref, gate_ref, gb_ref, nw_ref, c0_ref, n0_ref, m0_ref,
                y_ref, cT_ref, nT_ref, mT_ref, C_scr, n_scr, m_scr, *, L, TB, t_valid):
    H, E, V = C_HEADS, C_QK_DIM, C_V_DIM
    tb = pl.program_id(1)

    @pl.when(tb == 0)
    def _():
        C_scr[...] = c0_ref[...]
        n_scr[...] = n0_ref[...]
        m_scr[...] = m0_ref[...]

    row = lax.broadcasted_iota(jnp.int32, (L, L), 0)
    col = lax.broadcasted_iota(jnp.int32, (L, L), 1)
    causal = row >= col
    tril = causal.astype(F32)
    gb = gb_ref[...]
    nw = nw_ref[...]
    lane = lax.broadcasted_iota(jnp.int32, (L, LANES), 1)

    def chunk(c, carry):
        sl = pl.ds(pl.multiple_of(c * L, L), L)
        gact = C_GATE_CAP * jnp.tanh((gate_ref[sl, :] + gb) / C_GATE_CAP)
        lf = jnp.minimum(gact, 0.0) - jnp.log(1.0 + jnp.exp(-jnp.abs(gact)))
        ig = gact
        valid = None
        if t_valid is not None:
            t_idx = tb * TB + c * L + lax.broadcasted_iota(jnp.int32, (L, 1), 0)
            valid = t_idx < t_valid
            ig = jnp.where(valid, ig, -1e30)
            lf = jnp.where(valid, lf, 0.0)
        bcum = _dot(tril, lf, HIGHEST)
        ig_t = ig.T
        bcum_t = bcum.T
        heads = range(H)
        es = [slice(h * E, (h + 1) * E) for h in heads]
        vs = [slice(h * V, (h + 1) * V) for h in heads]
        b_col = [bcum[:, H + h:H + h + 1] for h in heads]
        b_row = [bcum_t[H + h:H + h + 1, :] for h in heads]
        ig_col = [ig[:, h:h + 1] for h in heads]
        ig_row = [ig_t[h:h + 1, :] for h in heads]
        q = [q_ref[sl, s] for s in es]
        k = [k_ref[sl, s] * (E ** -0.5) for s in es]
        v = [v_ref[sl, s] for s in vs]
        if valid is not None:
            q, k, v = ([jnp.where(valid, t, 0.0) for t in ts] for ts in (q, k, v))
        m_prev = [m_scr[h:h + 1, 0:1] for h in heads]
        n_prev = [n_scr[h:h + 1, :] for h in heads]
        C = [C_scr[h] for h in heads]
        dm = [jnp.where(causal, b_col[h] - b_row[h] + ig_row[h], -jnp.inf) for h in heads]
        inter = [b_col[h] + m_prev[h] for h in heads]
        mt = [jnp.maximum(inter[h], jnp.max(dm[h], axis=-1, keepdims=True)) for h in heads]
        w_d = [jnp.exp(dm[h] - mt[h]) for h in heads]
        w_i = [jnp.exp(inter[h] - mt[h]) for h in heads]
        qb, kb, vb = ([t.astype(BF16) for t in ts] for ts in (q, k, v))
        sc = [_dot_nt(qb[h], kb[h]) * w_d[h] for h in heads]
        qc = [_dot_nt(qb[h], C[h].astype(BF16)) for h in heads]
        num = [_dot(sc[h].astype(BF16), vb[h]) + w_i[h] * qc[h] for h in heads]
        den = [jnp.sum(sc[h], axis=-1, keepdims=True) + w_i[h] * jnp.sum(q[h] * n_prev[h], axis=-1, keepdims=True)
               for h in heads]
        hh = [num[h] / jnp.maximum(jnp.abs(den[h]), jnp.exp(-mt[h])) for h in heads]
        m_new = [x[L - 1:L, :] for x in mt]
        b_end = [x[L - 1:L, :] for x in b_col]
        w_s = [jnp.exp(b_end[h] - b_col[h] + ig_col[h] - m_new[h]) for h in heads]
        dec = [jnp.exp(b_end[h] + m_prev[h] - m_new[h]) for h in heads]
        c_upd = [_dot_tn((w_s[h] * v[h]).astype(BF16), kb[h]) for h in heads]
        outs = []
        for h in heads:
            C_scr[h] = dec[h] * C[h] + c_upd[h]
            n_scr[h:h + 1, :] = dec[h] * n_prev[h] + jnp.sum(w_s[h] * k[h], axis=0, keepdims=True)
            m_scr[h:h + 1, :] = jnp.broadcast_to(m_new[h], (1, LANES))
            hn = hh[h] * lax.rsqrt(jnp.mean(hh[h] * hh[h], axis=-1, keepdims=True) + NORM_EPS) * nw[:, vs[h]]
            outs.append(hn * _sigmoid(o_ref[sl, vs[h]]))
        y_ref[sl, :] = jnp.concatenate(outs, axis=-1).astype(y_ref.dtype)
        return carry

    lax.fori_loop(0, TB // L, chunk, 0)

    @pl.when(tb == pl.num_programs(1) - 1)
    def _():
        cT_ref[...] = C_scr[...]
        nT_ref[...] = n_scr[...]
        mT_ref[...] = m_scr[...]


def mlstm_recurrence(proj, gate_bias, norm_w, c0, n0, m0, n_seq, T, row0, TB, t_valid):
    H, E, V = C_HEADS, C_QK_DIM, C_V_DIM
    nb = T // TB
    rb0 = row0 // TB
    HE, HV = H * E, H * V

    def cols(cb):
        return lambda b, t: (rb0 + b * nb + t, cb)

    st4 = lambda b, t: (b, 0, 0, 0)
    st3 = lambda b, t: (b, 0, 0)
    return pl.pallas_call(
        functools.partial(_mlstm_body, L=CHUNK, TB=TB, t_valid=t_valid), grid=(n_seq, nb),
        in_specs=[pl.BlockSpec((TB, HE), cols(0)), pl.BlockSpec((TB, HE), cols(1)),
                  pl.BlockSpec((TB, HV), cols(2 * HE // HV)), pl.BlockSpec((TB, HV), cols(2 * HE // HV + 1)),
                  pl.BlockSpec((TB, LANES), cols((2 * HE + 2 * HV) // LANES)),
                  pl.BlockSpec((1, LANES), lambda b, t: (0, 0)), pl.BlockSpec((1, HV), lambda b, t: (0, 0)),
                  pl.BlockSpec((None, H, V, E), st4), pl.BlockSpec((None, H, E), st3),
                  pl.BlockSpec((None, H, LANES), st3)],
        out_specs=[pl.BlockSpec((TB, HV), lambda b, t: (b * nb + t, 0)),
                   pl.BlockSpec((None, H, V, E), st4), pl.BlockSpec((None, H, E), st3),
                   pl.BlockSpec((None, H, LANES), st3)],
        out_shape=[jax.ShapeDtypeStruct((n_seq * T, HV), BF16), jax.ShapeDtypeStruct((n_seq, H, V, E), F32),
                   jax.ShapeDtypeStruct((n_seq, H, E), F32), jax.ShapeDtypeStruct((n_seq, H, LANES), F32)],
        scratch_shapes=[pltpu.VMEM((H, V, E), F32), pltpu.VMEM((H, E), F32), pltpu.VMEM((H, LANES), F32)],
        compiler_params=_params(("parallel", "arbitrary")), name="mlstm_recurrence",
    )(proj, proj, proj, proj, proj, gate_bias, norm_w, c0, n0, m0)


def _tile(n, target):
    return max(t for t in range(LANES, min(n, target) + 1, LANES) if n % t == 0)


def _pad_cols(w, n):
    return jnp.pad(w, ((0, 0), (0, n - w.shape[1])))


def _pad_rows(w, n):
    return jnp.pad(w, ((0, n - w.shape[0]), (0, 0)))


def kernel(x_prompt, x_sample, state_a_wkv, state_a_shift, cache_b_kv_g0, cache_b_kv_g1, cache_b_kv_g2, state_c_C, state_c_n, state_c_m, rel_bias, norm_ffn1, ffn1_w_in, ffn1_w_out, norm_mix, norm_ffn2, ffn2_w_in, ffn2_w_out, norm_final, a_mu, a_w_rkv, a_w0, a_w1, a_w2, a_a0, a_a1, a_a2, a_g1, a_g2, a_k_k, a_k_a, a_r_k, a_gn_w, a_gn_b, a_w_out, a_v0, a_v1, a_v2, b_w_qkv, b_w_out, c_w_in, c_b_gates, c_norm_w, c_w_out):
    Bp, Tp, D = x_prompt.shape
    Bs, Ts, _ = x_sample.shape
    depth = norm_mix.shape[0]
    Tsp = SAMPLE_PAD
    Mp, Ms = Bp * Tp, Bs * Tsp
    M = Mp + Ms
    TM = 512
    TN = _tile(D, 2048)
    TF = _tile(ffn1_w_out.shape[1], 512)
    H_a = D // A_HEAD_DIM
    G, H_b, Dh = len(B_DILATIONS), B_HEADS, B_HEAD_DIM
    bf = lambda w: w.astype(BF16)

    x = jnp.concatenate([x_prompt.reshape(Mp, D),
                         jnp.pad(x_sample, ((0, 0), (0, Tsp - Ts), (0, 0))).reshape(Ms, D)], axis=0)

    def last_rows(t):
        return t[:Mp].reshape(Bp, Tp, -1)[:, -1], t[Mp:].reshape(Bs, Tsp, -1)[:, Ts - 1]

    qi = jnp.arange(B_BLK)[:, None]
    kj = jnp.arange(2 * B_BLK)[None, :]
    step = qi + B_BLK - kj
    step_ok = (step >= 0) & (step <= B_BLK)
    m_desc = B_BLK - jnp.arange(B_BLK + 8)
    bias_p, bias_s = [], []
    buckets = jnp.arange(N_BUCKETS)
    for gi, dil in enumerate(B_DILATIONS):
        tab = rel_bias[:, gi * H_b:(gi + 1) * H_b].astype(F32)
        hot = (_rel_bucket(jnp.clip(step, 0, B_BLK) * dil)[None] == buckets[:, None, None]).astype(F32)
        bp = jnp.einsum("nh,nqk->hqk", tab, hot, precision=HIGHEST)
        bias_p.append(jnp.where(step_ok[None], bp, -jnp.inf))
        bs = tab[_rel_bucket(jnp.maximum(m_desc, 0) * dil)]
        bias_s.append(jnp.broadcast_to(bs[:, :, None], (B_BLK + 8, H_b, Dh)))
    bias_p, bias_s = jnp.stack(bias_p), jnp.stack(bias_s)
    ffn1_in, ffn1_out, ffn2_in, ffn2_out = bf(ffn1_w_in), bf(ffn1_w_out), bf(ffn2_w_in), bf(ffn2_w_out)

    outs_a_wkv, outs_a_shift, outs_c = ([], []), ([], []), ([], [], [], [], [], [])
    outs_b = [([], []) for _ in range(G)]
    v_first = None
    for i in range(depth):
        x = ffn(x, norm_ffn1[i], ffn1_in, ffn1_out, i, TM, TF)
        kind, j = i % 3, i // 3
        if kind == 0:
            u = rmsnorm(x, norm_mix[i], TM)
            up, us = u[:Mp].reshape(Bp, Tp, D), u[Mp:].reshape(Bs, Tsp, D)
            u_prev = jnp.concatenate([
                jnp.concatenate([jnp.zeros((Bp, 1, D), F32), up[:, :-1]], axis=1).reshape(Mp, D),
                jnp.concatenate([state_a_shift[j][:, None, :], us[:, :-1]], axis=1).reshape(Ms, D)], axis=0)
            mu = a_mu[j]
            rkv = rwkv_rkv(u, u_prev, mu[jnp.array([0, 2, 3])][:, None, :], bf(a_w_rkv[j]), TM, TN)
            lr = LANES
            w_br = (bf(_pad_cols(a_w1[j], lr)), bf(_pad_rows(a_w2[j], lr)), a_w0[j].reshape(1, D))
            a_br = (bf(_pad_cols(a_a1[j], lr)), bf(_pad_rows(a_a2[j], lr)), a_a0[j].reshape(1, D))
            g_br = (bf(a_g1[j]), bf(a_g2[j]))
            v_br = None
            if j > 0:
                v_br = (bf(_pad_cols(a_v1[j - 1], lr)), bf(_pad_rows(a_v2[j - 1], lr)), a_v0[j - 1].reshape(1, D))
            lora = rwkv_lora(u, u_prev, mu[jnp.array([1, 4, 5, 3])], w_br, a_br, g_br, v_br, 256)
            lw, a_lr, gate = lora[:3]
            vres = None if j == 0 else (v_first, lora[3])
            if j == 0:
                v_first = rkv[2]
            par = tuple(p.reshape(1, D) for p in (a_k_k[j], a_k_a[j], a_r_k[j], a_gn_w[j], a_gn_b[j]))
            s0p = jnp.zeros((Bp * H_a, A_HEAD_DIM, A_HEAD_DIM), F32)
            s0s = state_a_wkv[j].reshape(Bs * H_a, A_HEAD_DIM, A_HEAD_DIM)
            yp, sp = rwkv_recurrence(rkv, lw, a_lr, gate, vres, par, s0p, Bp, Tp, 0, 256, None)
            ys, ss = rwkv_recurrence(rkv, lw, a_lr, gate, vres, par, s0s, Bs, Tsp, Mp, Tsp, Ts)
            x = matmul(jnp.concatenate([yp, ys], axis=0), bf(a_w_out[j]), TM, TN, residual=x)
            outs_a_wkv[0].append(sp.reshape(Bp, H_a, A_HEAD_DIM, A_HEAD_DIM))
            outs_a_wkv[1].append(ss.reshape(Bs, H_a, A_HEAD_DIM, A_HEAD_DIM))
            sh_p, sh_s = last_rows(u)
            outs_a_shift[0].append(sh_p)
            outs_a_shift[1].append(sh_s)
        elif kind == 1:
            qkv = matmul(x, bf(b_w_qkv[j]), TM, _tile(b_w_qkv.shape[2], 1024), norm_g=norm_mix[i])
            caches = [c[j].reshape(Bs, B_BLK, d, 2, H_b, Dh)
                      for c, d in zip((cache_b_kv_g0, cache_b_kv_g1, cache_b_kv_g2), B_DILATIONS)]
            op = attn_prompt(qkv, bias_p, Bp, Tp)
            os_ = attn_sample(qkv[Mp:], caches, bias_s, Bs, Tsp, 0, Ts)
            x = matmul(jnp.concatenate([op, os_.reshape(Ms, H_b * Dh)], axis=0), bf(b_w_out[j]), TM, TN, residual=x)
            qp = qkv[:Mp].reshape(Bp, Tp, -1)
            qs = qkv[Mp:].reshape(Bs, Tsp, -1)
            for gi in range(G):
                keep = min(B_WINDOWS[gi], Tp)
                c0, c1 = (gi * 3 + 1) * H_b * Dh, (gi * 3 + 3) * H_b * Dh
                outs_b[gi][0].append(qp[:, Tp - keep:, c0:c1].reshape(Bp, keep, 2, H_b, Dh))
                outs_b[gi][1].append(qs[:, :Ts, c0:c1].reshape(Bs, Ts, 2, H_b, Dh))
        else:
            H, E, V = C_HEADS, C_QK_DIM, C_V_DIM
            n_in = c_w_in.shape[2]
            n_pad = -(-n_in // LANES) * LANES
            proj = matmul(x, bf(_pad_cols(c_w_in[j], n_pad)), TM, _tile(n_pad, 1024), norm_g=norm_mix[i])
            gbias = _pad_cols(c_b_gates[j].reshape(1, 2 * H), LANES)
            nw = c_norm_w[j].reshape(1, H * V)
            zc = (jnp.zeros((Bp, H, V, E), F32), jnp.zeros((Bp, H, E), F32), jnp.zeros((Bp, H, LANES), F32))
            sc = (state_c_C[j], state_c_n[j], jnp.broadcast_to(state_c_m[j][:, :, None], (Bs, H, LANES)))
            hp, cp, np_, mp = mlstm_recurrence(proj, gbias, nw, *zc, Bp, Tp, 0, 256, None)
            hs, cs, ns, ms = mlstm_recurrence(proj, gbias, nw, *sc, Bs, Tsp, Mp, Tsp, Ts)
            x = matmul(jnp.concatenate([hp, hs], axis=0), bf(c_w_out[j]), TM, TN, residual=x)
            for lst, val in zip(outs_c, (cp, cs, np_, ns, mp[:, :, 0], ms[:, :, 0])):
                lst.append(val)
        x = ffn(x, norm_ffn2[i], ffn2_in, ffn2_out, i, TM, TF)

    y = rmsnorm(x, norm_final, TM)
    y_prompt = y[:Mp].reshape(Bp, Tp, D)
    y_sample = y[Mp:].reshape(Bs, Tsp, D)[:, :Ts]
    st = jnp.stack
    return (y_prompt, y_sample, st(outs_a_wkv[0]), st(outs_a_wkv[1]), st(outs_a_shift[0]), st(outs_a_shift[1]),
            st(outs_b[0][0]), st(outs_b[0][1]), st(outs_b[1][0]), st(outs_b[1][1]), st(outs_b[2][0]), st(outs_b[2][1]),
            st(outs_c[0]), st(outs_c[1]), st(outs_c[2]), st(outs_c[3]), st(outs_c[4]), st(outs_c[5]))
```

```python
import functools
import math

import jax
import jax.numpy as jnp
from jax import lax
from jax.experimental import pallas as pl
from jax.experimental.pallas import tpu as pltpu

F32 = jnp.float32
BF16 = jnp.bfloat16
HIGHEST = lax.Precision.HIGHEST

NORM_EPS = 1e-6
A_HEAD_DIM = 64
A_GN_EPS = 64e-5
A_REC_LANES = 1024
A_REC_PRECISION = "bf16"
B_WINDOWS = (128, 512, 2048)
B_DILATIONS = (1, 4, 16)
B_HEADS = 16
B_HEAD_DIM = 128
B_BLK = 128
B_UNROLL = 4
N_BUCKETS = 32
BUCKET_MAX_DIST = 2048
C_HEADS = 8
C_QK_DIM = 128
C_V_DIM = 256
C_GATE_CAP = 15.0
CHUNK = 64
SAMPLE_PAD = 64
LANES = 128
VMEM_LIMIT = 56 * 1024 * 1024


def _params(sem):
    return pltpu.CompilerParams(dimension_semantics=sem, vmem_limit_bytes=VMEM_LIMIT)


def _dot(a, b, precision=None):
    return jnp.dot(a, b, preferred_element_type=F32, precision=precision)


def _dot_nt(a, b, precision=None):
    return lax.dot_general(a, b, (((1,), (1,)), ((), ())), preferred_element_type=F32, precision=precision)


def _dot_tn(a, b, precision=None):
    return lax.dot_general(a, b, (((0,), (0,)), ((), ())), preferred_element_type=F32, precision=precision)


_NN = (((1,), (0,)), ((), ()))
_NT = (((1,), (1,)), ((), ()))
_TN = (((0,), (0,)), ((), ()))


def _pmm(a, b, dims, mode):
    dg = functools.partial(lax.dot_general, dimension_numbers=dims, preferred_element_type=F32)
    if mode == "highest":
        return dg(a, b, precision=HIGHEST)
    a_hi, b_hi = a.astype(BF16), b.astype(BF16)
    if mode == "bf16":
        return dg(a_hi, b_hi)
    a_lo = (a - a_hi.astype(F32)).astype(BF16)
    b_lo = (b - b_hi.astype(F32)).astype(BF16)
    return dg(a_hi, b_hi) + (dg(a_hi, b_lo) + dg(a_lo, b_hi))


def _sigmoid(x):
    return 1.0 / (1.0 + jnp.exp(-x))


def _rms(x, g):
    ms = jnp.mean(x * x, axis=-1, keepdims=True)
    return x * lax.rsqrt(ms + NORM_EPS) * g


def _rmsnorm_body(x_ref, g_ref, o_ref):
    o_ref[...] = _rms(x_ref[...], g_ref[...]).astype(o_ref.dtype)


def rmsnorm(x, g, tm):
    M, D = x.shape
    return pl.pallas_call(
        _rmsnorm_body, grid=(M // tm,),
        in_specs=[pl.BlockSpec((tm, D), lambda i: (i, 0)), pl.BlockSpec((1, D), lambda i: (0, 0))],
        out_specs=pl.BlockSpec((tm, D), lambda i: (i, 0)),
        out_shape=jax.ShapeDtypeStruct((M, D), F32),
        compiler_params=_params(("parallel",)), name="rmsnorm",
    )(x, g.reshape(1, D))


def _rmsnorm_shift_body(x_ref, xp_ref, g_ref, st_ref, u_ref, up_ref, *, tm, grp, seq_len, seq_rows):
    i = pl.program_id(0)
    g = g_ref[...]
    u = _rms(x_ref[...], g)
    u_ref[...] = u
    up_ref[...] = pltpu.roll(u, 1, axis=0)
    tail = _rms(xp_ref[...], g)[-1:, :]
    for k in range(tm // grp):
        row0 = i * tm + k * grp
        is_start = (row0 >= seq_rows) | (lax.rem(row0, seq_len) == 0)
        before = tail if k == 0 else u[k * grp - 1:k * grp, :]
        up_ref[k * grp:k * grp + 1, :] = jnp.where(is_start, st_ref[k:k + 1, :], before)


def rmsnorm_shift(x, g, starts, seq_len, seq_rows, grp, tm):
    M, D = x.shape
    sub = 8
    return pl.pallas_call(
        functools.partial(_rmsnorm_shift_body, tm=tm, grp=grp, seq_len=seq_len, seq_rows=seq_rows), grid=(M // tm,),
        in_specs=[pl.BlockSpec((tm, D), lambda i: (i, 0)),
                  pl.BlockSpec((sub, D), lambda i: (jnp.maximum(i * (tm // sub) - 1, 0), 0)),
                  pl.BlockSpec((1, D), lambda i: (0, 0)),
                  pl.BlockSpec((tm // grp, D), lambda i: (i, 0))],
        out_specs=[pl.BlockSpec((tm, D), lambda i: (i, 0))] * 2,
        out_shape=[jax.ShapeDtypeStruct((M, D), F32)] * 2,
        compiler_params=_params(("parallel",)), name="rmsnorm_shift",
    )(x, x, g.reshape(1, D), starts)


def _ffn_body(x_ref, g_ref, wg_ref, wu_ref, wo_ref, o_ref, xn_ref, acc_ref):
    j = pl.program_id(1)

    @pl.when(j == 0)
    def _():
        xn_ref[...] = _rms(x_ref[...], g_ref[...]).astype(BF16)
        acc_ref[...] = jnp.zeros_like(acc_ref)

    xn = xn_ref[...]
    gate = _dot(xn, wg_ref[...])
    up = _dot(xn, wu_ref[...])
    h = (gate * _sigmoid(gate) * up).astype(BF16)
    acc_ref[...] += _dot(h, wo_ref[...])

    @pl.when(j == pl.num_programs(1) - 1)
    def _():
        o_ref[...] = x_ref[...] + 0.5 * acc_ref[...]


def ffn(x, g, w_in, w_out, layer, tm, tf):
    M, D = x.shape
    Fh = w_out.shape[1]
    nf = Fh // tf
    return pl.pallas_call(
        _ffn_body, grid=(M // tm, nf),
        in_specs=[pl.BlockSpec((tm, D), lambda i, j: (i, 0)),
                  pl.BlockSpec((1, D), lambda i, j: (0, 0)),
                  pl.BlockSpec((None, D, tf), lambda i, j: (layer, 0, j)),
                  pl.BlockSpec((None, D, tf), lambda i, j: (layer, 0, j + nf)),
                  pl.BlockSpec((None, tf, D), lambda i, j: (layer, j, 0))],
        out_specs=pl.BlockSpec((tm, D), lambda i, j: (i, 0)),
        out_shape=jax.ShapeDtypeStruct((M, D), F32),
        scratch_shapes=[pltpu.VMEM((tm, D), BF16), pltpu.VMEM((tm, D), F32)],
        compiler_params=_params(("parallel", "arbitrary")), name="ffn",
    )(x, g.reshape(1, D), w_in, w_in, w_out)


def _mm_body(*refs, nb0, has_norm, has_res):
    it = iter(refs)
    x_ref = next(it)
    x1_ref = next(it) if nb0 is not None else None
    g_ref = next(it) if has_norm else None
    w_ref = next(it)
    res_ref = next(it) if has_res else None
    o_ref = next(it)
    xs_ref = next(it)

    def stage(ref):
        x = ref[...].astype(F32)
        if has_norm:
            x = _rms(x, g_ref[...])
        xs_ref[...] = x.astype(BF16)

    @pl.when(pl.program_id(1) == 0)
    def _():
        if nb0 is None:
            stage(x_ref)
        else:
            pl.when(pl.program_id(0) < nb0)(lambda: stage(x_ref))
            pl.when(pl.program_id(0) >= nb0)(lambda: stage(x1_ref))

    acc = _dot(xs_ref[...], w_ref[...])
    if has_res:
        acc = res_ref[...] + acc
    o_ref[...] = acc.astype(o_ref.dtype)


def matmul(x, w, tm, tn, norm_g=None, residual=None, out_dtype=F32):
    nb0 = None
    if isinstance(x, tuple):
        x0, x1 = x
        nb0 = x0.shape[0] // tm
        M, K = x0.shape[0] + x1.shape[0], x0.shape[1]
        args = [x0, x1]
        specs = [pl.BlockSpec((tm, K), lambda i, j: (jnp.minimum(i, nb0 - 1), 0)),
                 pl.BlockSpec((tm, K), lambda i, j: (jnp.maximum(i - nb0, 0), 0))]
    else:
        M, K = x.shape
        args = [x]
        specs = [pl.BlockSpec((tm, K), lambda i, j: (i, 0))]
    N = w.shape[1]
    if norm_g is not None:
        args.append(norm_g.reshape(1, K))
        specs.append(pl.BlockSpec((1, K), lambda i, j: (0, 0)))
    args.append(w)
    specs.append(pl.BlockSpec((K, tn), lambda i, j: (0, j)))
    if residual is not None:
        args.append(residual)
        specs.append(pl.BlockSpec((tm, tn), lambda i, j: (i, j)))
    return pl.pallas_call(
        functools.partial(_mm_body, nb0=nb0, has_norm=norm_g is not None, has_res=residual is not None),
        grid=(M // tm, N // tn), in_specs=specs,
        out_specs=pl.BlockSpec((tm, tn), lambda i, j: (i, j)),
        out_shape=jax.ShapeDtypeStruct((M, N), out_dtype),
        scratch_shapes=[pltpu.VMEM((tm, K), BF16)],
        compiler_params=_params(("parallel", "arbitrary")), name="matmul",
    )(*args)


def _rkv_body(u_ref, up_ref, mu_ref, w_ref, o_ref, xs_ref):
    @pl.when(pl.program_id(2) == 0)
    def _():
        u = u_ref[...]
        xs_ref[...] = (u + (up_ref[...] - u) * mu_ref[...]).astype(BF16)

    o_ref[...] = _dot(xs_ref[...], w_ref[...])


def rwkv_rkv(u, u_prev, mu3, w3, tm, tn):
    M, D = u.shape
    return pl.pallas_call(
        _rkv_body, grid=(M // tm, 3, D // tn),
        in_specs=[pl.BlockSpec((tm, D), lambda i, k, j: (i, 0)),
                  pl.BlockSpec((tm, D), lambda i, k, j: (i, 0)),
                  pl.BlockSpec((None, 1, D), lambda i, k, j: (k, 0, 0)),
                  pl.BlockSpec((None, D, tn), lambda i, k, j: (k, 0, j))],
        out_specs=pl.BlockSpec((None, tm, tn), lambda i, k, j: (k, i, j)),
        out_shape=jax.ShapeDtypeStruct((3, M, D), F32),
        scratch_shapes=[pltpu.VMEM((tm, D), BF16)],
        compiler_params=_params(("parallel", "arbitrary", "arbitrary")), name="rwkv_rkv",
    )(u, u_prev, mu3, w3)


def _lora_body(*refs, has_vres):
    it = iter(refs)
    u_ref, up_ref, mu_ref = next(it), next(it), next(it)
    w1, w2, w0 = next(it), next(it), next(it)
    a1, a2, a0 = next(it), next(it), next(it)
    g1, g2 = next(it), next(it)
    if has_vres:
        v1, v2, v0 = next(it), next(it), next(it)
    lw_ref, a_ref, g_ref = next(it), next(it), next(it)
    nu_ref = next(it) if has_vres else None

    u = u_ref[...]
    du = up_ref[...] - u

    def mix(n):
        return (u + du * mu_ref[n:n + 1, :]).astype(BF16)

    hw = jnp.tanh(_dot(mix(0), w1[...])).astype(BF16)
    w_pre = w0[...] + _dot(hw, w2[...])
    softplus = jnp.maximum(-w_pre, 0.0) + jnp.log(1.0 + jnp.exp(-jnp.abs(w_pre)))
    lw_ref[...] = -jnp.exp(-softplus - 0.5)
    ha = _dot(mix(1), a1[...]).astype(BF16)
    a_ref[...] = _sigmoid(a0[...] + _dot(ha, a2[...]))
    hg = _sigmoid(_dot(mix(2), g1[...])).astype(BF16)
    g_ref[...] = _dot(hg, g2[...])
    if has_vres:
        hv = _dot(mix(3), v1[...]).astype(BF16)
        nu_ref[...] = _sigmoid(v0[...] + _dot(hv, v2[...]))


def rwkv_lora(u, u_prev, mu4, w, a, g, v, tm):
    M, D = u.shape
    has_vres = v is not None
    row = lambda i: (i, 0)
    full = lambda i: (0, 0)
    args = [u, u_prev, mu4]
    specs = [pl.BlockSpec((tm, D), row), pl.BlockSpec((tm, D), row), pl.BlockSpec(mu4.shape, full)]
    for t in (w, a, g) + ((v,) if has_vres else ()):
        for m in t:
            args.append(m)
            specs.append(pl.BlockSpec(m.shape, full))
    n_out = 4 if has_vres else 3
    return pl.pallas_call(
        functools.partial(_lora_body, has_vres=has_vres), grid=(M // tm,), in_specs=specs,
        out_specs=[pl.BlockSpec((tm, D), row)] * n_out,
        out_shape=[jax.ShapeDtypeStruct((M, D), F32)] * n_out,
        compiler_params=_params(("parallel",)), name="rwkv_lora",
    )(*args)


def _rwkv_rec_body(*refs, L, TB, hb, t_valid, has_vres, prec):
    N = A_HEAD_DIM
    it = iter(refs)
    r_ref, k_ref, v_ref, lw_ref, a_ref, g_ref = (next(it) for _ in range(6))
    if has_vres:
        vf_ref, nu_ref = next(it), next(it)
    kk_ref, ka_ref, rk_ref, gnw_ref, gnb_ref, s0_ref = (next(it) for _ in range(6))
    y_ref, sT_ref, S_scr = next(it), next(it), next(it)
    tb = pl.program_id(2)

    assert L == N and 2 * N == LANES

    @pl.when(tb == 0)
    def _():
        for p in range(hb // 2):
            S_scr[p] = jnp.concatenate([s0_ref[2 * p], s0_ref[2 * p + 1]], axis=1)

    row = lax.broadcasted_iota(jnp.int32, (L, L), 0)
    col = lax.broadcasted_iota(jnp.int32, (L, L), 1)
    tril = (row >= col).astype(F32)
    row1 = lax.broadcasted_iota(jnp.int32, (L, 2 * N), 0)
    lane1 = lax.broadcasted_iota(jnp.int32, (L, 2 * N), 1)
    head0_lane = lane1 < N
    eye = ((lane1 & (N - 1)) == row1).astype(F32)
    row2 = lax.broadcasted_iota(jnp.int32, (2 * L, 2 * N), 0)
    lane2 = lax.broadcasted_iota(jnp.int32, (2 * L, 2 * N), 1)
    mask2 = (lane2 & (N - 1)) < jnp.where(row2 < L, row2, row2 - L + 1)
    bd_mask = (row2 // L) == (lane2 // N)
    ones_bd = bd_mask.astype(BF16)
    n_sq = int(math.log2(L)) - 1
    kk_p, ka_p, rk_p, gnw, gnb = kk_ref[...], ka_ref[...], rk_ref[...], gnw_ref[...], gnb_ref[...]

    def chunk(c, carry):
        sl = pl.ds(pl.multiple_of(c * L, L), L)
        r, k, v, lw, a, g = r_ref[sl, :], k_ref[sl, :], v_ref[sl, :], lw_ref[sl, :], a_ref[sl, :], g_ref[sl, :]
        if has_vres:
            v = v + (vf_ref[sl, :] - v) * nu_ref[sl, :]
        if t_valid is not None:
            t_idx = tb * TB + c * L + lax.broadcasted_iota(jnp.int32, (L, 1), 0)
            valid = t_idx < t_valid
            r, k, v, lw = (jnp.where(valid, t, 0.0) for t in (r, k, v, lw))
        cum = _dot(tril, lw, HIGHEST)
        cum_end = cum[L - 1:L, :]
        w_cur, w_prev, w_inv, w_rem, w_end = (jnp.exp(cum), jnp.exp(cum - lw), jnp.exp(-cum),
                                              jnp.exp(cum_end - cum), jnp.exp(cum_end))
        mm, mm_nt, mm_tn = (functools.partial(_pmm, dims=d, mode=prec) for d in (_NN, _NT, _TN))
        pairs = range(hb // 2)
        ps = [slice(p * LANES, (p + 1) * LANES) for p in pairs]

        def head_sum(x):
            x_hi = x.astype(BF16)
            x_lo = (x - x_hi.astype(F32)).astype(BF16)
            return _dot(x_hi, ones_bd) + _dot(x_lo, ones_bd)

        def bdiag(x):
            return jnp.where(bd_mask, jnp.concatenate([x, x], axis=0), 0.0)

        kkp = [k[:, s] * kk_p[:, s] for s in ps]
        kk = [x / jnp.maximum(jnp.sqrt(head_sum(x * x)), 1e-12) for x in kkp]
        b = [kk[p] * a[:, ps[p]] for p in pairs]
        k2 = [k[:, s] * (1.0 + (a[:, s] - 1.0) * ka_p[:, s]) for s in ps]
        lhs2 = [jnp.concatenate([kk[p] * w_prev[:, ps[p]], r[:, ps[p]] * w_cur[:, ps[p]]], axis=0) for p in pairs]
        kd = [k2[p] * w_inv[:, ps[p]] for p in pairs]
        bd = [b[p] * w_inv[:, ps[p]] for p in pairs]
        kend = [k2[p] * w_rem[:, ps[p]] for p in pairs]
        bend = [b[p] * w_rem[:, ps[p]] for p in pairs]
        kkd = [x[:L] for x in lhs2]
        rd = [x[L:] for x in lhs2]
        a_k = [jnp.where(mask2, mm_nt(lhs2[p], bdiag(kd[p])), 0.0) for p in pairs]
        a_b = [jnp.where(mask2, mm_nt(lhs2[p], bdiag(bd[p])), 0.0) for p in pairs]
        a_kb = [x[:L] for x in a_b]
        a_rb = [x[L:] for x in a_b]
        a_v = [mm(a_k[p], bdiag(v[:, ps[p]])) for p in pairs]
        t_inv = [eye - x for x in a_kb]
        pw = a_kb
        for _ in range(n_sq):
            pw = [mm(x, bdiag(x)) for x in pw]
            t_inv = [t_inv[p] + mm(t_inv[p], bdiag(pw[p])) for p in pairs]
        k_t = [mm(t_inv[p], bdiag(kkd[p])) for p in pairs]
        c_u = [mm(t_inv[p], bdiag(a_v[p][:L])) for p in pairs]
        y_k = [rd[p] - mm(a_rb[p], bdiag(k_t[p])) for p in pairs]
        y_c = [a_v[p][L:] - mm(a_rb[p], bdiag(c_u[p])) for p in pairs]
        S = [S_scr[p] for p in pairs]
        y = [mm_nt(y_k[p], bdiag(S[p])) + y_c[p] for p in pairs]
        ktb = [jnp.where(bd_mask, mm_tn(k_t[p], bend[p]), 0.0) for p in pairs]
        s_full = [mm_tn(v[:, ps[p]], kend[p]) - mm_tn(c_u[p], bend[p]) for p in pairs]
        S_c = [jnp.where(head0_lane, x[:N], x[N:]) for x in s_full]
        for p in pairs:
            S_scr[p] = S[p] * w_end[:, ps[p]] - mm(S[p], ktb[p]) + S_c[p]
        mean = [head_sum(y[p]) * (1.0 / N) for p in pairs]
        var = [head_sum(jnp.square(y[p] - mean[p])) * (1.0 / N) for p in pairs]
        bonus = [head_sum(r[:, ps[p]] * k2[p] * rk_p[:, ps[p]]) * v[:, ps[p]] for p in pairs]
        outs = [((y[p] - mean[p]) * lax.rsqrt(var[p] + A_GN_EPS) * gnw[:, ps[p]] + gnb[:, ps[p]] + bonus[p])
                * g[:, ps[p]] for p in pairs]
        y_ref[sl, :] = jnp.concatenate(outs, axis=-1).astype(y_ref.dtype)
        return carry

    lax.fori_loop(0, TB // L, chunk, 0)

    @pl.when(tb == pl.num_programs(2) - 1)
    def _():
        for p in range(hb // 2):
            S = S_scr[p]
            sT_ref[2 * p] = S[:, :N]
            sT_ref[2 * p + 1] = S[:, N:]


def rwkv_recurrence(rkv, lw, a, g, vres, params, s0, n_seq, T, row0, TB, t_valid):
    _, M, D = rkv.shape
    N = A_HEAD_DIM
    LW = min(A_REC_LANES, D)
    hb = LW // N
    nb = T // TB
    rb0 = row0 // TB
    has_vres = vres is not None
    seq = lambda b, h, t: (rb0 + b * nb + t, h)
    args, specs = [], []
    for n in range(3):
        args.append(rkv)
        specs.append(pl.BlockSpec((None, TB, LW), lambda b, h, t, n=n: (n, rb0 + b * nb + t, h)))
    for x in (lw, a, g):
        args.append(x)
        specs.append(pl.BlockSpec((TB, LW), seq))
    if has_vres:
        args += list(vres)
        specs += [pl.BlockSpec((None, TB, LW), lambda b, h, t: (2, rb0 + b * nb + t, h)), pl.BlockSpec((TB, LW), seq)]
    for p in params:
        args.append(p)
        specs.append(pl.BlockSpec((1, LW), lambda b, h, t: (0, h)))
    args.append(s0)
    specs.append(pl.BlockSpec((hb, N, N), lambda b, h, t: (b * (D // LW) + h, 0, 0)))
    return pl.pallas_call(
        functools.partial(_rwkv_rec_body, L=CHUNK, TB=TB, hb=hb, t_valid=t_valid, has_vres=has_vres,
                          prec=A_REC_PRECISION),
        grid=(n_seq, D // LW, nb), in_specs=specs,
        out_specs=[pl.BlockSpec((TB, LW), lambda b, h, t: (b * nb + t, h)),
                   pl.BlockSpec((hb, N, N), lambda b, h, t: (b * (D // LW) + h, 0, 0))],
        out_shape=[jax.ShapeDtypeStruct((n_seq * T, D), BF16),
                   jax.ShapeDtypeStruct((n_seq * (D // N), N, N), F32)],
        scratch_shapes=[pltpu.VMEM((hb // 2, N, 2 * N), F32)],
        compiler_params=_params(("parallel", "parallel", "arbitrary")), name="rwkv_recurrence",
    )(*args)


def _rel_bucket(dist):
    exact = N_BUCKETS // 2
    d = jnp.maximum(dist, 1).astype(F32)
    log_b = exact + (jnp.log(d / exact) / math.log(BUCKET_MAX_DIST / exact) * (N_BUCKETS - exact)).astype(jnp.int32)
    return jnp.where(dist < exact, dist, jnp.minimum(log_b, N_BUCKETS - 1))


def _attn_prompt_body(q_ref, k_ref, v_ref, bias_ref, o_ref, acc_ref, m_ref, l_ref, *, T):
    g = pl.program_id(2)
    blk = B_BLK
    scale = B_HEAD_DIM ** -0.5
    first_keys = lax.broadcasted_iota(jnp.int32, (blk, 2 * blk), 1) < blk
    ones_cols = jnp.ones((2 * blk, B_HEAD_DIM), BF16)

    for gi, dil in enumerate(B_DILATIONS):
        @pl.when(g == gi)
        def _(gi=gi, dil=dil):
            span = blk * dil
            bias = bias_ref[...]

            def blocks(it, carry):
                us = range(B_UNROLL)
                idx = [it * B_UNROLL + u for u in us]
                n = [i // dil for i in idx]
                start = [n[u] * span + (idx[u] - n[u] * dil) for u in us]
                cur = [pl.ds(s, blk, stride=dil) for s in start]
                prev = [pl.ds(jnp.maximum(s - span, 0), blk, stride=dil) for s in start]
                q = [(q_ref[cur[u], :] * scale).astype(BF16) for u in us]
                kcat = [jnp.concatenate([k_ref[prev[u], :], k_ref[cur[u], :]], axis=0).astype(BF16) for u in us]
                vcat = [jnp.concatenate([v_ref[prev[u], :], v_ref[cur[u], :]], axis=0).astype(BF16) for u in us]
                logits = [_dot_nt(q[u], kcat[u]) + bias for u in us]
                logits = [jnp.where(first_keys & (n[u] == 0), -jnp.inf, logits[u]) for u in us]
                mx = [jnp.max(x, axis=-1, keepdims=True) for x in logits]
                p = [jnp.exp(logits[u] - mx[u]) for u in us]
                pvd = [_dot(p[u].astype(BF16), jnp.concatenate([vcat[u], ones_cols], axis=1)) for u in us]
                pv = [x[:, :B_HEAD_DIM] for x in pvd]
                den = [x[:, B_HEAD_DIM:] for x in pvd]
                if gi > 0:
                    m_old = [m_ref[cur[u], :] for u in us]
                    l_old = [l_ref[cur[u], :] for u in us]
                    acc_old = [acc_ref[cur[u], :] for u in us]
                    m_new = [jnp.maximum(m_old[u], mx[u]) for u in us]
                    c_old = [jnp.exp(m_old[u] - m_new[u]) for u in us]
                    c_new = [jnp.exp(mx[u] - m_new[u]) for u in us]
                    pv = [acc_old[u] * c_old[u] + pv[u] * c_new[u] for u in us]
                    den = [l_old[u] * c_old[u] + den[u] * c_new[u] for u in us]
                    mx = m_new
                for u in us:
                    acc_ref[cur[u], :] = pv[u]
                    m_ref[cur[u], :] = mx[u]
                    l_ref[cur[u], :] = den[u]
                return carry

            lax.fori_loop(0, T // (blk * B_UNROLL), blocks, 0)

    @pl.when(g == len(B_DILATIONS) - 1)
    def _():
        o_ref[...] = (acc_ref[...] / l_ref[...]).astype(o_ref.dtype)


def attn_prompt(qkv, bias, n_seq, T):
    H, Dh, G = B_HEADS, B_HEAD_DIM, len(B_DILATIONS)

    def col(which):
        return lambda b, h, g: (b, (g * 3 + which) * H + h)

    return pl.pallas_call(
        functools.partial(_attn_prompt_body, T=T), grid=(n_seq, H, G),
        in_specs=[pl.BlockSpec((T, Dh), col(0)), pl.BlockSpec((T, Dh), col(1)), pl.BlockSpec((T, Dh), col(2)),
                  pl.BlockSpec((None, None, B_BLK, 2 * B_BLK), lambda b, h, g: (g, h, 0, 0))],
        out_specs=pl.BlockSpec((T, Dh), lambda b, h, g: (b, h)),
        out_shape=jax.ShapeDtypeStruct((n_seq * T, H * Dh), BF16),
        scratch_shapes=[pltpu.VMEM((T, Dh), F32), pltpu.VMEM((T, 1), F32), pltpu.VMEM((T, Dh), F32)],
        compiler_params=_params(("parallel", "parallel", "arbitrary")), name="attn_prompt",
    )(qkv, qkv, qkv, bias)


def _attn_sample_body(q_ref, k_ref, v_ref, c0_ref, c1_ref, c2_ref, bias_ref, o_ref, *, t_valid):
    blk = B_BLK
    scale = B_HEAD_DIM ** -0.5
    caches = (c0_ref, c1_ref, c2_ref)
    o_ref[...] = jnp.zeros_like(o_ref)
    for t in range(t_valid):
        m_run = l_run = acc = None
        for gi, dil in enumerate(B_DILATIONS):
            q = q_ref[t, gi] * scale
            c_ref = caches[gi]
            if dil == 1:
                kcat = jnp.concatenate([c_ref[t:, 0, 0], k_ref[:t + 1, gi]], axis=0)
                vcat = jnp.concatenate([c_ref[t:, 0, 1], v_ref[:t + 1, gi]], axis=0)
            else:
                kcat = jnp.concatenate([c_ref[:, t, 0], k_ref[t:t + 1, gi]], axis=0)
                vcat = jnp.concatenate([c_ref[:, t, 1], v_ref[t:t + 1, gi]], axis=0)
            logits = jnp.sum(q[None] * kcat, axis=-1, keepdims=True) + bias_ref[gi, :blk + 1]
            mx = jnp.max(logits, axis=0)
            p = jnp.exp(logits - mx[None])
            den = jnp.sum(p, axis=0)
            pv = jnp.sum(p * vcat, axis=0)
            if gi == 0:
                m_run, l_run, acc = mx, den, pv
            else:
                m_new = jnp.maximum(m_run, mx)
                c_old, c_new = jnp.exp(m_run - m_new), jnp.exp(mx - m_new)
                acc = acc * c_old + pv * c_new
                l_run = l_run * c_old + den * c_new
                m_run = m_new
        o_ref[t] = (acc / l_run).astype(o_ref.dtype)


def attn_sample(qkv, caches, bias, n_seq, T, row0, t_valid):
    H, Dh, G = B_HEADS, B_HEAD_DIM, len(B_DILATIONS)
    rb0 = row0 // T
    assert t_valid <= min(d for d in B_DILATIONS if d > 1)
    q5 = qkv.reshape(qkv.shape[0], G, 3, H, Dh)

    def spec(which):
        return pl.BlockSpec((T, G, None, H, Dh), lambda b: (b + rb0, 0, which, 0, 0))

    cache_specs = [pl.BlockSpec((None, B_BLK, min(d, t_valid), 2, H, Dh), lambda b: (b, 0, 0, 0, 0, 0))
                   for d in B_DILATIONS]
    return pl.pallas_call(
        functools.partial(_attn_sample_body, t_valid=t_valid), grid=(n_seq,),
        in_specs=[spec(0), spec(1), spec(2)] + cache_specs + [pl.BlockSpec(bias.shape, lambda b: (0, 0, 0, 0))],
        out_specs=pl.BlockSpec((T, H, Dh), lambda b: (b, 0, 0)),
        out_shape=jax.ShapeDtypeStruct((n_seq * T, H, Dh), BF16),
        compiler_params=_params(("parallel",)), name="attn_sample",
    )(q5, q5, q5, *caches, bias)


def _mlstm_body(q_ref, k_ref, v_ref, o_ref, gate_ref, gb_ref, nw_ref, c0_ref, n0_ref, m0_ref,
                y_ref, cT_ref, nT_ref, mT_ref, C_scr, n_scr, m_scr, *, L, TB, t_valid):
    H, E, V = C_HEADS, C_QK_DIM, C_V_DIM
    tb = pl.program_id(1)

    @pl.when(tb == 0)
    def _():
        C_scr[...] = c0_ref[...]
        n_scr[...] = n0_ref[...]
        m_scr[...] = m0_ref[...]

    row = lax.broadcasted_iota(jnp.int32, (L, L), 0)
    col = lax.broadcasted_iota(jnp.int32, (L, L), 1)
    causal = row >= col
    tril = causal.astype(F32)
    gb = gb_ref[...]
    nw = nw_ref[...]
    lane = lax.broadcasted_iota(jnp.int32, (L, LANES), 1)

    def chunk(c, carry):
        sl = pl.ds(pl.multiple_of(c * L, L), L)
        gact = C_GATE_CAP * jnp.tanh((gate_ref[sl, :] + gb) / C_GATE_CAP)
        lf = jnp.minimum(gact, 0.0) - jnp.log(1.0 + jnp.exp(-jnp.abs(gact)))
        ig = gact
        valid = None
        if t_valid is not None:
            t_idx = tb * TB + c * L + lax.broadcasted_iota(jnp.int32, (L, 1), 0)
            valid = t_idx < t_valid
            ig = jnp.where(valid, ig, -1e30)
            lf = jnp.where(valid, lf, 0.0)
        bcum = _dot(tril, lf, HIGHEST)
        ig_t = ig.T
        bcum_t = bcum.T
        heads = range(H)
        es = [slice(h * E, (h + 1) * E) for h in heads]
        vs = [slice(h * V, (h + 1) * V) for h in heads]
        b_col = [bcum[:, H + h:H + h + 1] for h in heads]
        b_row = [bcum_t[H + h:H + h + 1, :] for h in heads]
        ig_col = [ig[:, h:h + 1] for h in heads]
        ig_row = [ig_t[h:h + 1, :] for h in heads]
        q = [q_ref[sl, s] for s in es]
        k = [k_ref[sl, s] * (E ** -0.5) for s in es]
        v = [v_ref[sl, s] for s in vs]
        if valid is not None:
            q, k, v = ([jnp.where(valid, t, 0.0) for t in ts] for ts in (q, k, v))
        m_prev = [m_scr[h:h + 1, 0:1] for h in heads]
        n_prev = [n_scr[h:h + 1, :] for h in heads]
        C = [C_scr[h] for h in heads]
        dm = [jnp.where(causal, b_col[h] - b_row[h] + ig_row[h], -jnp.inf) for h in heads]
        inter = [b_col[h] + m_prev[h] for h in heads]
        mt = [jnp.maximum(inter[h], jnp.max(dm[h], axis=-1, keepdims=True)) for h in heads]
        w_d = [jnp.exp(dm[h] - mt[h]) for h in heads]
        w_i = [jnp.exp(inter[h] - mt[h]) for h in heads]
        qb, kb, vb = ([t.astype(BF16) for t in ts] for ts in (q, k, v))
        sc = [_dot_nt(qb[h], kb[h]) * w_d[h] for h in heads]
        qc = [_dot_nt(qb[h], C[h].astype(BF16)) for h in heads]
        num = [_dot(sc[h].astype(BF16), vb[h]) + w_i[h] * qc[h] for h in heads]
        den = [jnp.sum(sc[h], axis=-1, keepdims=True) + w_i[h] * jnp.sum(q[h] * n_prev[h], axis=-1, keepdims=True)
               for h in heads]
        hh = [num[h] / jnp.maximum(jnp.abs(den[h]), jnp.exp(-mt[h])) for h in heads]
        m_new = [x[L - 1:L, :] for x in mt]
        b_end = [x[L - 1:L, :] for x in b_col]
        w_s = [jnp.exp(b_end[h] - b_col[h] + ig_col[h] - m_new[h]) for h in heads]
        dec = [jnp.exp(b_end[h] + m_prev[h] - m_new[h]) for h in heads]
        c_upd = [_dot_tn((w_s[h] * v[h]).astype(BF16), kb[h]) for h in heads]
        outs = []
        for h in heads:
            C_scr[h] = dec[h] * C[h] + c_upd[h]
            n_scr[h:h + 1, :] = dec[h] * n_prev[h] + jnp.sum(w_s[h] * k[h], axis=0, keepdims=True)
            m_scr[h:h + 1, :] = jnp.broadcast_to(m_new[h], (1, LANES))
            hn = hh[h] * lax.rsqrt(jnp.mean(hh[h] * hh[h], axis=-1, keepdims=True) + NORM_EPS) * nw[:, vs[h]]
            outs.append(hn * _sigmoid(o_ref[sl, vs[h]]))
        y_ref[sl, :] = jnp.concatenate(outs, axis=-1).astype(y_ref.dtype)
        return carry

    lax.fori_loop(0, TB // L, chunk, 0)

    @pl.when(tb == pl.num_programs(1) - 1)
    def _():
        cT_ref[...] = C_scr[...]
        nT_ref[...] = n_scr[...]
        mT_ref[...] = m_scr[...]


def mlstm_recurrence(proj, gate_bias, norm_w, c0, n0, m0, n_seq, T, row0, TB, t_valid):
    H, E, V = C_HEADS, C_QK_DIM, C_V_DIM
    nb = T // TB
    rb0 = row0 // TB
    HE, HV = H * E, H * V

    def cols(cb):
        return lambda b, t: (rb0 + b * nb + t, cb)

    st4 = lambda b, t: (b, 0, 0, 0)
    st3 = lambda b, t: (b, 0, 0)
    return pl.pallas_call(
        functools.partial(_mlstm_body, L=CHUNK, TB=TB, t_valid=t_valid), grid=(n_seq, nb),
        in_specs=[pl.BlockSpec((TB, HE), cols(0)), pl.BlockSpec((TB, HE), cols(1)),
                  pl.BlockSpec((TB, HV), cols(2 * HE // HV)), pl.BlockSpec((TB, HV), cols(2 * HE // HV + 1)),
                  pl.BlockSpec((TB, LANES), cols((2 * HE + 2 * HV) // LANES)),
                  pl.BlockSpec((1, LANES), lambda b, t: (0, 0)), pl.BlockSpec((1, HV), lambda b, t: (0, 0)),
                  pl.BlockSpec((None, H, V, E), st4), pl.BlockSpec((None, H, E), st3),
                  pl.BlockSpec((None, H, LANES), st3)],
        out_specs=[pl.BlockSpec((TB, HV), lambda b, t: (b * nb + t, 0)),
                   pl.BlockSpec((None, H, V, E), st4), pl.BlockSpec((None, H, E), st3),
                   pl.BlockSpec((None, H, LANES), st3)],
        out_shape=[jax.ShapeDtypeStruct((n_seq * T, HV), BF16), jax.ShapeDtypeStruct((n_seq, H, V, E), F32),
                   jax.ShapeDtypeStruct((n_seq, H, E), F32), jax.ShapeDtypeStruct((n_seq, H, LANES), F32)],
        scratch_shapes=[pltpu.VMEM((H, V, E), F32), pltpu.VMEM((H, E), F32), pltpu.VMEM((H, LANES), F32)],
        compiler_params=_params(("parallel", "arbitrary")), name="mlstm_recurrence",
    )(proj, proj, proj, proj, proj, gate_bias, norm_w, c0, n0, m0)


def _tile(n, target):
    return max(t for t in range(LANES, min(n, target) + 1, LANES) if n % t == 0)


def _pad_cols(w, n):
    return jnp.pad(w, ((0, 0), (0, n - w.shape[1])))


def _pad_rows(w, n):
    return jnp.pad(w, ((0, n - w.shape[0]), (0, 0)))


def kernel(x_prompt, x_sample, state_a_wkv, state_a_shift, cache_b_kv_g0, cache_b_kv_g1, cache_b_kv_g2, state_c_C, state_c_n, state_c_m, rel_bias, norm_ffn1, ffn1_w_in, ffn1_w_out, norm_mix, norm_ffn2, ffn2_w_in, ffn2_w_out, norm_final, a_mu, a_w_rkv, a_w0, a_w1, a_w2, a_a0, a_a1, a_a2, a_g1, a_g2, a_k_k, a_k_a, a_r_k, a_gn_w, a_gn_b, a_w_out, a_v0, a_v1, a_v2, b_w_qkv, b_w_out, c_w_in, c_b_gates, c_norm_w, c_w_out):
    Bp, Tp, D = x_prompt.shape
    Bs, Ts, _ = x_sample.shape
    depth = norm_mix.shape[0]
    Tsp = SAMPLE_PAD
    Mp, Ms = Bp * Tp, Bs * Tsp
    M = Mp + Ms
    TM = 512
    TN = _tile(D, 2048)
    TF = _tile(ffn1_w_out.shape[1], 512)
    H_a = D // A_HEAD_DIM
    G, H_b, Dh = len(B_DILATIONS), B_HEADS, B_HEAD_DIM
    bf = lambda w: w.astype(BF16)

    x = jnp.concatenate([x_prompt.reshape(Mp, D),
                         jnp.pad(x_sample, ((0, 0), (0, Tsp - Ts), (0, 0))).reshape(Ms, D)], axis=0)

    def last_rows(t):
        return t[:Mp].reshape(Bp, Tp, -1)[:, -1], t[Mp:].reshape(Bs, Tsp, -1)[:, Ts - 1]

    qi = jnp.arange(B_BLK)[:, None]
    kj = jnp.arange(2 * B_BLK)[None, :]
    step = qi + B_BLK - kj
    step_ok = (step >= 0) & (step <= B_BLK)
    m_desc = B_BLK - jnp.arange(B_BLK + 8)
    bias_p, bias_s = [], []
    buckets = jnp.arange(N_BUCKETS)
    for gi, dil in enumerate(B_DILATIONS):
        tab = rel_bias[:, gi * H_b:(gi + 1) * H_b].astype(F32)
        hot = (_rel_bucket(jnp.clip(step, 0, B_BLK) * dil)[None] == buckets[:, None, None]).astype(F32)
        bp = jnp.einsum("nh,nqk->hqk", tab, hot, precision=HIGHEST)
        bias_p.append(jnp.where(step_ok[None], bp, -jnp.inf))
        bs = tab[_rel_bucket(jnp.maximum(m_desc, 0) * dil)]
        bias_s.append(jnp.broadcast_to(bs[:, :, None], (B_BLK + 8, H_b, Dh)))
    bias_p, bias_s = jnp.stack(bias_p), jnp.stack(bias_s)
    ffn1_in, ffn1_out, ffn2_in, ffn2_out = bf(ffn1_w_in), bf(ffn1_w_out), bf(ffn2_w_in), bf(ffn2_w_out)

    outs_a_wkv, outs_a_shift, outs_c = ([], []), ([], []), ([], [], [], [], [], [])
    outs_b = [([], []) for _ in range(G)]
    v_first = None
    for i in range(depth):
        x = ffn(x, norm_ffn1[i], ffn1_in, ffn1_out, i, TM, TF)
        kind, j = i % 3, i // 3
        if kind == 0:
            starts = jnp.concatenate([jnp.zeros((Mp // Tsp, D), F32), state_a_shift[j]], axis=0)
            u, u_prev = rmsnorm_shift(x, norm_mix[i], starts, Tp, Mp, Tsp, TM)
            mu = a_mu[j]
            rkv = rwkv_rkv(u, u_prev, mu[jnp.array([0, 2, 3])][:, None, :], bf(a_w_rkv[j]), TM, TN)
            lr = LANES
            w_br = (bf(_pad_cols(a_w1[j], lr)), bf(_pad_rows(a_w2[j], lr)), a_w0[j].reshape(1, D))
            a_br = (bf(_pad_cols(a_a1[j], lr)), bf(_pad_rows(a_a2[j], lr)), a_a0[j].reshape(1, D))
            g_br = (bf(a_g1[j]), bf(a_g2[j]))
            v_br = None
            if j > 0:
                v_br = (bf(_pad_cols(a_v1[j - 1], lr)), bf(_pad_rows(a_v2[j - 1], lr)), a_v0[j - 1].reshape(1, D))
            lora = rwkv_lora(u, u_prev, mu[jnp.array([1, 4, 5, 3])], w_br, a_br, g_br, v_br, 256)
            lw, a_lr, gate = lora[:3]
            vres = None if j == 0 else (v_first, lora[3])
            if j == 0:
                v_first = rkv
            par = tuple(p.reshape(1, D) for p in (a_k_k[j], a_k_a[j], a_r_k[j], a_gn_w[j], a_gn_b[j]))
            s0p = jnp.zeros((Bp * H_a, A_HEAD_DIM, A_HEAD_DIM), F32)
            s0s = state_a_wkv[j].reshape(Bs * H_a, A_HEAD_DIM, A_HEAD_DIM)
            yp, sp = rwkv_recurrence(rkv, lw, a_lr, gate, vres, par, s0p, Bp, Tp, 0, 256, None)
            ys, ss = rwkv_recurrence(rkv, lw, a_lr, gate, vres, par, s0s, Bs, Tsp, Mp, Tsp, Ts)
            x = matmul((yp, ys), bf(a_w_out[j]), TM, TN, residual=x)
            outs_a_wkv[0].append(sp.reshape(Bp, H_a, A_HEAD_DIM, A_HEAD_DIM))
            outs_a_wkv[1].append(ss.reshape(Bs, H_a, A_HEAD_DIM, A_HEAD_DIM))
            sh_p, sh_s = last_rows(u)
            outs_a_shift[0].append(sh_p)
            outs_a_shift[1].append(sh_s)
        elif kind == 1:
            qkv = matmul(x, bf(b_w_qkv[j]), TM, _tile(b_w_qkv.shape[2], 2048), norm_g=norm_mix[i])
            caches = [c[j].reshape(Bs, B_BLK, d, 2, H_b, Dh)
                      for c, d in zip((cache_b_kv_g0, cache_b_kv_g1, cache_b_kv_g2), B_DILATIONS)]
            op = attn_prompt(qkv, bias_p, Bp, Tp)
            os_ = attn_sample(qkv[Mp:], caches, bias_s, Bs, Tsp, 0, Ts)
            x = matmul((op, os_.reshape(Ms, H_b * Dh)), bf(b_w_out[j]), TM, TN, residual=x)
            for gi in range(G):
                keep = min(B_WINDOWS[gi], Tp)
                c0, c1 = (gi * 3 + 1) * H_b * Dh, (gi * 3 + 3) * H_b * Dh
                kv_p = jnp.stack([lax.slice(qkv, ((b + 1) * Tp - keep, c0), ((b + 1) * Tp, c1)) for b in range(Bp)])
                kv_s = lax.slice(qkv, (Mp, c0), (M, c1)).reshape(Bs, Tsp, c1 - c0)[:, :Ts]
                outs_b[gi][0].append(kv_p.reshape(Bp, keep, 2, H_b, Dh))
                outs_b[gi][1].append(kv_s.reshape(Bs, Ts, 2, H_b, Dh))
        else:
            H, E, V = C_HEADS, C_QK_DIM, C_V_DIM
            n_in = c_w_in.shape[2]
            n_pad = -(-n_in // LANES) * LANES
            proj = matmul(x, bf(_pad_cols(c_w_in[j], n_pad)), TM, _tile(n_pad, 1024), norm_g=norm_mix[i])
            gbias = _pad_cols(c_b_gates[j].reshape(1, 2 * H), LANES)
            nw = c_norm_w[j].reshape(1, H * V)
            zc = (jnp.zeros((Bp, H, V, E), F32), jnp.zeros((Bp, H, E), F32), jnp.zeros((Bp, H, LANES), F32))
            sc = (state_c_C[j], state_c_n[j], jnp.broadcast_to(state_c_m[j][:, :, None], (Bs, H, LANES)))
            hp, cp, np_, mp = mlstm_recurrence(proj, gbias, nw, *zc, Bp, Tp, 0, 256, None)
            hs, cs, ns, ms = mlstm_recurrence(proj, gbias, nw, *sc, Bs, Tsp, Mp, Tsp, Ts)
            x = matmul((hp, hs), bf(c_w_out[j]), TM, TN, residual=x)
            for lst, val in zip(outs_c, (cp, cs, np_, ns, mp[:, :, 0], ms[:, :, 0])):
                lst.append(val)
        x = ffn(x, norm_ffn2[i], ffn2_in, ffn2_out, i, TM, TF)

    y = rmsnorm(x, norm_final, TM)
    y_prompt = y[:Mp].reshape(Bp, Tp, D)
    y_sample = y[Mp:].reshape(Bs, Tsp, D)[:, :Ts]
    st = jnp.stack
    return (y_prompt, y_sample, st(outs_a_wkv[0]), st(outs_a_wkv[1]), st(outs_a_shift[0]), st(outs_a_shift[1]),
            st(outs_b[0][0]), st(outs_b[0][1]), st(outs_b[1][0]), st(outs_b[1][1]), st(outs_b[2][0]), st(outs_b[2][1]),
            st(outs_c[0]), st(outs_c[1]), st(outs_c[2]), st(outs_c[3]), st(outs_c[4]), st(outs_c[5]))
```

```python
import functools
import math

import jax
import jax.numpy as jnp
from jax import lax
from jax.experimental import pallas as pl
from jax.experimental.pallas import tpu as pltpu

F32 = jnp.float32
BF16 = jnp.bfloat16
HIGHEST = lax.Precision.HIGHEST

NORM_EPS = 1e-6
A_HEAD_DIM = 64
A_GN_EPS = 64e-5
A_REC_LANES = 1024
A_REC_PRECISION = "bf16"
B_WINDOWS = (128, 512, 2048)
B_DILATIONS = (1, 4, 16)
B_HEADS = 16
B_HEAD_DIM = 128
B_BLK = 128
B_UNROLL = 4
N_BUCKETS = 32
BUCKET_MAX_DIST = 2048
C_HEADS = 8
C_QK_DIM = 128
C_V_DIM = 256
C_GATE_CAP = 15.0
CHUNK = 64
SAMPLE_PAD = 64
LANES = 128
VMEM_LIMIT = 56 * 1024 * 1024


def _params(sem):
    return pltpu.CompilerParams(dimension_semantics=sem, vmem_limit_bytes=VMEM_LIMIT)


def _dot(a, b, precision=None):
    return jnp.dot(a, b, preferred_element_type=F32, precision=precision)


def _dot_nt(a, b, precision=None):
    return lax.dot_general(a, b, (((1,), (1,)), ((), ())), preferred_element_type=F32, precision=precision)


def _dot_tn(a, b, precision=None):
    return lax.dot_general(a, b, (((0,), (0,)), ((), ())), preferred_element_type=F32, precision=precision)


_NN = (((1,), (0,)), ((), ()))
_NT = (((1,), (1,)), ((), ()))
_TN = (((0,), (0,)), ((), ()))


def _pmm(a, b, dims, mode):
    dg = functools.partial(lax.dot_general, dimension_numbers=dims, preferred_element_type=F32)
    if mode == "highest":
        return dg(a, b, precision=HIGHEST)
    a_hi, b_hi = a.astype(BF16), b.astype(BF16)
    if mode == "bf16":
        return dg(a_hi, b_hi)
    a_lo = (a - a_hi.astype(F32)).astype(BF16)
    b_lo = (b - b_hi.astype(F32)).astype(BF16)
    return dg(a_hi, b_hi) + (dg(a_hi, b_lo) + dg(a_lo, b_hi))


def _sigmoid(x):
    return 1.0 / (1.0 + jnp.exp(-x))


def _rms(x, g):
    ms = jnp.mean(x * x, axis=-1, keepdims=True)
    return x * lax.rsqrt(ms + NORM_EPS) * g


def _rmsnorm_body(x_ref, g_ref, o_ref):
    o_ref[...] = _rms(x_ref[...], g_ref[...]).astype(o_ref.dtype)


def rmsnorm(x, g, tm):
    M, D = x.shape
    return pl.pallas_call(
        _rmsnorm_body, grid=(M // tm,),
        in_specs=[pl.BlockSpec((tm, D), lambda i: (i, 0)), pl.BlockSpec((1, D), lambda i: (0, 0))],
        out_specs=pl.BlockSpec((tm, D), lambda i: (i, 0)),
        out_shape=jax.ShapeDtypeStruct((M, D), F32),
        compiler_params=_params(("parallel",)), name="rmsnorm",
    )(x, g.reshape(1, D))


def _rmsnorm_shift_body(x_ref, xp_ref, g_ref, st_ref, u_ref, up_ref, *, tm, grp, seq_len, seq_rows):
    i = pl.program_id(0)
    g = g_ref[...]
    u = _rms(x_ref[...], g)
    u_ref[...] = u
    up_ref[...] = pltpu.roll(u, 1, axis=0)
    tail = _rms(xp_ref[...], g)[-1:, :]
    for k in range(tm // grp):
        row0 = i * tm + k * grp
        is_start = (row0 >= seq_rows) | (lax.rem(row0, seq_len) == 0)
        before = tail if k == 0 else u[k * grp - 1:k * grp, :]
        up_ref[k * grp:k * grp + 1, :] = jnp.where(is_start, st_ref[k:k + 1, :], before)


def rmsnorm_shift(x, g, starts, seq_len, seq_rows, grp, tm):
    M, D = x.shape
    sub = 8
    return pl.pallas_call(
        functools.partial(_rmsnorm_shift_body, tm=tm, grp=grp, seq_len=seq_len, seq_rows=seq_rows), grid=(M // tm,),
        in_specs=[pl.BlockSpec((tm, D), lambda i: (i, 0)),
                  pl.BlockSpec((sub, D), lambda i: (jnp.maximum(i * (tm // sub) - 1, 0), 0)),
                  pl.BlockSpec((1, D), lambda i: (0, 0)),
                  pl.BlockSpec((tm // grp, D), lambda i: (i, 0))],
        out_specs=[pl.BlockSpec((tm, D), lambda i: (i, 0))] * 2,
        out_shape=[jax.ShapeDtypeStruct((M, D), F32)] * 2,
        compiler_params=_params(("parallel",)), name="rmsnorm_shift",
    )(x, x, g.reshape(1, D), starts)


def _ffn_body(x_ref, g_ref, wg_ref, wu_ref, wo_ref, o_ref, xn_ref, acc_ref):
    j = pl.program_id(1)

    @pl.when(j == 0)
    def _():
        xn_ref[...] = _rms(x_ref[...], g_ref[...]).astype(BF16)
        acc_ref[...] = jnp.zeros_like(acc_ref)

    xn = xn_ref[...]
    gate = _dot(xn, wg_ref[...])
    up = _dot(xn, wu_ref[...])
    h = (gate * _sigmoid(gate) * up).astype(BF16)
    acc_ref[...] += _dot(h, wo_ref[...])

    @pl.when(j == pl.num_programs(1) - 1)
    def _():
        o_ref[...] = x_ref[...] + 0.5 * acc_ref[...]


def ffn(x, g, w_in, w_out, layer, tm, tf):
    M, D = x.shape
    Fh = w_out.shape[1]
    nf = Fh // tf
    return pl.pallas_call(
        _ffn_body, grid=(M // tm, nf),
        in_specs=[pl.BlockSpec((tm, D), lambda i, j: (i, 0)),
                  pl.BlockSpec((1, D), lambda i, j: (0, 0)),
                  pl.BlockSpec((None, D, tf), lambda i, j: (layer, 0, j)),
                  pl.BlockSpec((None, D, tf), lambda i, j: (layer, 0, j + nf)),
                  pl.BlockSpec((None, tf, D), lambda i, j: (layer, j, 0))],
        out_specs=pl.BlockSpec((tm, D), lambda i, j: (i, 0)),
        out_shape=jax.ShapeDtypeStruct((M, D), F32),
        scratch_shapes=[pltpu.VMEM((tm, D), BF16), pltpu.VMEM((tm, D), F32)],
        compiler_params=_params(("parallel", "arbitrary")), name="ffn",
    )(x, g.reshape(1, D), w_in, w_in, w_out)


def _mm_body(*refs, nb0, has_norm, has_res):
    it = iter(refs)
    x_ref = next(it)
    x1_ref = next(it) if nb0 is not None else None
    g_ref = next(it) if has_norm else None
    w_ref = next(it)
    res_ref = next(it) if has_res else None
    o_ref = next(it)
    xs_ref = next(it)

    def stage(ref):
        x = ref[...].astype(F32)
        if has_norm:
            x = _rms(x, g_ref[...])
        xs_ref[...] = x.astype(BF16)

    @pl.when(pl.program_id(1) == 0)
    def _():
        if nb0 is None:
            stage(x_ref)
        else:
            pl.when(pl.program_id(0) < nb0)(lambda: stage(x_ref))
            pl.when(pl.program_id(0) >= nb0)(lambda: stage(x1_ref))

    acc = _dot(xs_ref[...], w_ref[...])
    if has_res:
        acc = res_ref[...] + acc
    o_ref[...] = acc.astype(o_ref.dtype)


def matmul(x, w, tm, tn, norm_g=None, residual=None, out_dtype=F32):
    nb0 = None
    if isinstance(x, tuple):
        x0, x1 = x
        nb0 = x0.shape[0] // tm
        M, K = x0.shape[0] + x1.shape[0], x0.shape[1]
        args = [x0, x1]
        specs = [pl.BlockSpec((tm, K), lambda i, j: (jnp.minimum(i, nb0 - 1), 0)),
                 pl.BlockSpec((tm, K), lambda i, j: (jnp.maximum(i - nb0, 0), 0))]
    else:
        M, K = x.shape
        args = [x]
        specs = [pl.BlockSpec((tm, K), lambda i, j: (i, 0))]
    N = w.shape[1]
    if norm_g is not None:
        args.append(norm_g.reshape(1, K))
        specs.append(pl.BlockSpec((1, K), lambda i, j: (0, 0)))
    args.append(w)
    specs.append(pl.BlockSpec((K, tn), lambda i, j: (0, j)))
    if residual is not None:
        args.append(residual)
        specs.append(pl.BlockSpec((tm, tn), lambda i, j: (i, j)))
    return pl.pallas_call(
        functools.partial(_mm_body, nb0=nb0, has_norm=norm_g is not None, has_res=residual is not None),
        grid=(M // tm, N // tn), in_specs=specs,
        out_specs=pl.BlockSpec((tm, tn), lambda i, j: (i, j)),
        out_shape=jax.ShapeDtypeStruct((M, N), out_dtype),
        scratch_shapes=[pltpu.VMEM((tm, K), BF16)],
        compiler_params=_params(("parallel", "arbitrary")), name="matmul",
    )(*args)


def _rkv_body(u_ref, up_ref, mu_ref, w_ref, o_ref, xs_ref):
    @pl.when(pl.program_id(2) == 0)
    def _():
        u = u_ref[...]
        xs_ref[...] = (u + (up_ref[...] - u) * mu_ref[...]).astype(BF16)

    o_ref[...] = _dot(xs_ref[...], w_ref[...])


def rwkv_rkv(u, u_prev, mu3, w3, tm, tn):
    M, D = u.shape
    return pl.pallas_call(
        _rkv_body, grid=(M // tm, 3, D // tn),
        in_specs=[pl.BlockSpec((tm, D), lambda i, k, j: (i, 0)),
                  pl.BlockSpec((tm, D), lambda i, k, j: (i, 0)),
                  pl.BlockSpec((None, 1, D), lambda i, k, j: (k, 0, 0)),
                  pl.BlockSpec((None, D, tn), lambda i, k, j: (k, 0, j))],
        out_specs=pl.BlockSpec((None, tm, tn), lambda i, k, j: (k, i, j)),
        out_shape=jax.ShapeDtypeStruct((3, M, D), F32),
        scratch_shapes=[pltpu.VMEM((tm, D), BF16)],
        compiler_params=_params(("parallel", "arbitrary", "arbitrary")), name="rwkv_rkv",
    )(u, u_prev, mu3, w3)


def _lora_body(*refs, has_vres):
    it = iter(refs)
    u_ref, up_ref, mu_ref = next(it), next(it), next(it)
    w1, w2, w0 = next(it), next(it), next(it)
    a1, a2, a0 = next(it), next(it), next(it)
    g1, g2 = next(it), next(it)
    if has_vres:
        v1, v2, v0 = next(it), next(it), next(it)
    lw_ref, a_ref, g_ref = next(it), next(it), next(it)
    nu_ref = next(it) if has_vres else None

    u = u_ref[...]
    du = up_ref[...] - u

    def mix(n):
        return (u + du * mu_ref[n:n + 1, :]).astype(BF16)

    hw = jnp.tanh(_dot(mix(0), w1[...])).astype(BF16)
    w_pre = w0[...] + _dot(hw, w2[...])
    softplus = jnp.maximum(-w_pre, 0.0) + jnp.log(1.0 + jnp.exp(-jnp.abs(w_pre)))
    lw_ref[...] = -jnp.exp(-softplus - 0.5)
    ha = _dot(mix(1), a1[...]).astype(BF16)
    a_ref[...] = _sigmoid(a0[...] + _dot(ha, a2[...]))
    hg = _sigmoid(_dot(mix(2), g1[...])).astype(BF16)
    g_ref[...] = _dot(hg, g2[...])
    if has_vres:
        hv = _dot(mix(3), v1[...]).astype(BF16)
        nu_ref[...] = _sigmoid(v0[...] + _dot(hv, v2[...]))


def rwkv_lora(u, u_prev, mu4, w, a, g, v, tm):
    M, D = u.shape
    has_vres = v is not None
    row = lambda i: (i, 0)
    full = lambda i: (0, 0)
    args = [u, u_prev, mu4]
    specs = [pl.BlockSpec((tm, D), row), pl.BlockSpec((tm, D), row), pl.BlockSpec(mu4.shape, full)]
    for t in (w, a, g) + ((v,) if has_vres else ()):
        for m in t:
            args.append(m)
            specs.append(pl.BlockSpec(m.shape, full))
    n_out = 4 if has_vres else 3
    return pl.pallas_call(
        functools.partial(_lora_body, has_vres=has_vres), grid=(M // tm,), in_specs=specs,
        out_specs=[pl.BlockSpec((tm, D), row)] * n_out,
        out_shape=[jax.ShapeDtypeStruct((M, D), F32)] * n_out,
        compiler_params=_params(("parallel",)), name="rwkv_lora",
    )(*args)


def _rwkv_rec_body(*refs, L, TB, hb, t_valid, has_vres, prec):
    N = A_HEAD_DIM
    it = iter(refs)
    r_ref, k_ref, v_ref, lw_ref, a_ref, g_ref = (next(it) for _ in range(6))
    if has_vres:
        vf_ref, nu_ref = next(it), next(it)
    kk_ref, ka_ref, rk_ref, gnw_ref, gnb_ref, s0_ref = (next(it) for _ in range(6))
    y_ref, sT_ref, S_scr = next(it), next(it), next(it)
    tb = pl.program_id(2)

    assert L == N and 2 * N == LANES

    @pl.when(tb == 0)
    def _():
        for p in range(hb // 2):
            S_scr[p] = jnp.concatenate([s0_ref[2 * p], s0_ref[2 * p + 1]], axis=1)

    row = lax.broadcasted_iota(jnp.int32, (L, L), 0)
    col = lax.broadcasted_iota(jnp.int32, (L, L), 1)
    tril = (row >= col).astype(F32)
    row1 = lax.broadcasted_iota(jnp.int32, (L, 2 * N), 0)
    lane1 = lax.broadcasted_iota(jnp.int32, (L, 2 * N), 1)
    head0_lane = lane1 < N
    eye = ((lane1 & (N - 1)) == row1).astype(F32)
    row2 = lax.broadcasted_iota(jnp.int32, (2 * L, 2 * N), 0)
    lane2 = lax.broadcasted_iota(jnp.int32, (2 * L, 2 * N), 1)
    mask2 = (lane2 & (N - 1)) < jnp.where(row2 < L, row2, row2 - L + 1)
    bd_mask = (row2 // L) == (lane2 // N)
    ones_bd = bd_mask.astype(BF16)
    n_sq = int(math.log2(L)) - 1
    kk_p, ka_p, rk_p, gnw, gnb = kk_ref[...], ka_ref[...], rk_ref[...], gnw_ref[...], gnb_ref[...]

    def chunk(c, carry):
        sl = pl.ds(pl.multiple_of(c * L, L), L)
        r, k, v, lw, a, g = r_ref[sl, :], k_ref[sl, :], v_ref[sl, :], lw_ref[sl, :], a_ref[sl, :], g_ref[sl, :]
        if has_vres:
            v = v + (vf_ref[sl, :] - v) * nu_ref[sl, :]
        if t_valid is not None:
            t_idx = tb * TB + c * L + lax.broadcasted_iota(jnp.int32, (L, 1), 0)
            valid = t_idx < t_valid
            r, k, v, lw = (jnp.where(valid, t, 0.0) for t in (r, k, v, lw))
        cum = _dot(tril, lw, HIGHEST)
        cum_end = cum[L - 1:L, :]
        w_cur, w_prev, w_inv, w_rem, w_end = (jnp.exp(cum), jnp.exp(cum - lw), jnp.exp(-cum),
                                              jnp.exp(cum_end - cum), jnp.exp(cum_end))
        mm, mm_nt, mm_tn = (functools.partial(_pmm, dims=d, mode=prec) for d in (_NN, _NT, _TN))
        pairs = range(hb // 2)
        ps = [slice(p * LANES, (p + 1) * LANES) for p in pairs]

        def head_sum(x):
            x_hi = x.astype(BF16)
            x_lo = (x - x_hi.astype(F32)).astype(BF16)
            return _dot(x_hi, ones_bd) + _dot(x_lo, ones_bd)

        def bdiag(x):
            return jnp.where(bd_mask, jnp.concatenate([x, x], axis=0), 0.0)

        kkp = [k[:, s] * kk_p[:, s] for s in ps]
        kk = [x / jnp.maximum(jnp.sqrt(head_sum(x * x)), 1e-12) for x in kkp]
        b = [kk[p] * a[:, ps[p]] for p in pairs]
        k2 = [k[:, s] * (1.0 + (a[:, s] - 1.0) * ka_p[:, s]) for s in ps]
        lhs2 = [jnp.concatenate([kk[p] * w_prev[:, ps[p]], r[:, ps[p]] * w_cur[:, ps[p]]], axis=0) for p in pairs]
        kd = [k2[p] * w_inv[:, ps[p]] for p in pairs]
        bd = [b[p] * w_inv[:, ps[p]] for p in pairs]
        kend = [k2[p] * w_rem[:, ps[p]] for p in pairs]
        bend = [b[p] * w_rem[:, ps[p]] for p in pairs]
        kkd = [x[:L] for x in lhs2]
        rd = [x[L:] for x in lhs2]
        a_kb2 = [mm_nt(lhs2[p], jnp.concatenate([bdiag(kd[p]), bdiag(bd[p])], axis=0)) for p in pairs]
        a_k = [jnp.where(mask2, x[:, :LANES], 0.0) for x in a_kb2]
        a_b = [jnp.where(mask2, x[:, LANES:], 0.0) for x in a_kb2]
        a_kb = [x[:L] for x in a_b]
        a_rb = [x[L:] for x in a_b]
        a_v = [mm(a_k[p], bdiag(v[:, ps[p]])) for p in pairs]
        t_inv = [eye - x for x in a_kb]
        pw = [mm(x, bdiag(x)) for x in a_kb]
        for _ in range(n_sq - 1):
            both = [mm(jnp.concatenate([pw[p], t_inv[p]], axis=0), bdiag(pw[p])) for p in pairs]
            t_inv = [t_inv[p] + both[p][L:] for p in pairs]
            pw = [x[:L] for x in both]
        t_inv = [t_inv[p] + mm(t_inv[p], bdiag(pw[p])) for p in pairs]
        ktcu = [mm(t_inv[p], jnp.concatenate([bdiag(kkd[p]), bdiag(a_v[p][:L])], axis=1)) for p in pairs]
        k_t = [x[:, :LANES] for x in ktcu]
        c_u = [x[:, LANES:] for x in ktcu]
        ykc = [mm(a_rb[p], jnp.concatenate([bdiag(k_t[p]), bdiag(c_u[p])], axis=1)) for p in pairs]
        y_k = [rd[p] - ykc[p][:, :LANES] for p in pairs]
        y_c = [a_v[p][L:] - ykc[p][:, LANES:] for p in pairs]
        S = [S_scr[p] for p in pairs]
        y = [mm_nt(y_k[p], bdiag(S[p])) + y_c[p] for p in pairs]
        upd = [mm_tn(jnp.concatenate([jnp.concatenate([v[:, ps[p]], jnp.zeros_like(k_t[p])], axis=1),
                                      jnp.concatenate([-c_u[p], k_t[p]], axis=1)], axis=0),
                     jnp.concatenate([kend[p], bend[p]], axis=0)) for p in pairs]
        S_c = [jnp.where(head0_lane, x[:N], x[N:LANES]) for x in upd]
        ktb = [jnp.where(bd_mask, x[LANES:], 0.0) for x in upd]
        for p in pairs:
            S_scr[p] = S[p] * w_end[:, ps[p]] - mm(S[p], ktb[p]) + S_c[p]
        mean = [head_sum(y[p]) * (1.0 / N) for p in pairs]
        var = [head_sum(jnp.square(y[p] - mean[p])) * (1.0 / N) for p in pairs]
        bonus = [head_sum(r[:, ps[p]] * k2[p] * rk_p[:, ps[p]]) * v[:, ps[p]] for p in pairs]
        outs = [((y[p] - mean[p]) * lax.rsqrt(var[p] + A_GN_EPS) * gnw[:, ps[p]] + gnb[:, ps[p]] + bonus[p])
                * g[:, ps[p]] for p in pairs]
        y_ref[sl, :] = jnp.concatenate(outs, axis=-1).astype(y_ref.dtype)
        return carry

    lax.fori_loop(0, TB // L, chunk, 0)

    @pl.when(tb == pl.num_programs(2) - 1)
    def _():
        for p in range(hb // 2):
            S = S_scr[p]
            sT_ref[2 * p] = S[:, :N]
            sT_ref[2 * p + 1] = S[:, N:]


def rwkv_recurrence(rkv, lw, a, g, vres, params, s0, n_seq, T, row0, TB, t_valid):
    _, M, D = rkv.shape
    N = A_HEAD_DIM
    LW = min(A_REC_LANES, D)
    hb = LW // N
    nb = T // TB
    rb0 = row0 // TB
    has_vres = vres is not None
    seq = lambda b, h, t: (rb0 + b * nb + t, h)
    args, specs = [], []
    for n in range(3):
        args.append(rkv)
        specs.append(pl.BlockSpec((None, TB, LW), lambda b, h, t, n=n: (n, rb0 + b * nb + t, h)))
    for x in (lw, a, g):
        args.append(x)
        specs.append(pl.BlockSpec((TB, LW), seq))
    if has_vres:
        args += list(vres)
        specs += [pl.BlockSpec((None, TB, LW), lambda b, h, t: (2, rb0 + b * nb + t, h)), pl.BlockSpec((TB, LW), seq)]
    for p in params:
        args.append(p)
        specs.append(pl.BlockSpec((1, LW), lambda b, h, t: (0, h)))
    args.append(s0)
    specs.append(pl.BlockSpec((hb, N, N), lambda b, h, t: (b * (D // LW) + h, 0, 0)))
    return pl.pallas_call(
        functools.partial(_rwkv_rec_body, L=CHUNK, TB=TB, hb=hb, t_valid=t_valid, has_vres=has_vres,
                          prec=A_REC_PRECISION),
        grid=(n_seq, D // LW, nb), in_specs=specs,
        out_specs=[pl.BlockSpec((TB, LW), lambda b, h, t: (b * nb + t, h)),
                   pl.BlockSpec((hb, N, N), lambda b, h, t: (b * (D // LW) + h, 0, 0))],
        out_shape=[jax.ShapeDtypeStruct((n_seq * T, D), BF16),
                   jax.ShapeDtypeStruct((n_seq * (D // N), N, N), F32)],
        scratch_shapes=[pltpu.VMEM((hb // 2, N, 2 * N), F32)],
        compiler_params=_params(("parallel", "parallel", "arbitrary")), name="rwkv_recurrence",
    )(*args)


def _rel_bucket(dist):
    exact = N_BUCKETS // 2
    d = jnp.maximum(dist, 1).astype(F32)
    log_b = exact + (jnp.log(d / exact) / math.log(BUCKET_MAX_DIST / exact) * (N_BUCKETS - exact)).astype(jnp.int32)
    return jnp.where(dist < exact, dist, jnp.minimum(log_b, N_BUCKETS - 1))


def _attn_prompt_body(q_ref, k_ref, v_ref, bias_ref, o_ref, acc_ref, m_ref, l_ref, kd_ref, vd_ref, *, T):
    step = pl.program_id(2)
    blk = B_BLK
    G = len(B_DILATIONS)
    scale = B_HEAD_DIM ** -0.5
    first_keys = lax.broadcasted_iota(jnp.int32, (blk, 2 * blk), 1) < blk
    ones_cols = jnp.ones((2 * blk, B_HEAD_DIM), BF16)

    for si, dil in enumerate(reversed(B_DILATIONS)):
        @pl.when(step == si)
        def _(gi=si, dil=dil):
            span = blk * dil
            res_rows = T // dil + blk
            bias = bias_ref[...]

            def where(idx):
                n = idx // dil
                r = idx - n * dil
                return n, n * span + r, pl.multiple_of(r * res_rows + n * blk, blk)

            for r in range(dil):
                kd_ref[r * res_rows:r * res_rows + blk, :] = jnp.zeros((blk, B_HEAD_DIM), BF16)
                vd_ref[r * res_rows:r * res_rows + blk, :] = jnp.zeros((blk, B_HEAD_DIM), BF16)

            def stage(it, carry):
                for u in range(B_UNROLL):
                    _, start, dst = where(it * B_UNROLL + u)
                    rows = pl.ds(start, blk, stride=dil)
                    kd_ref[pl.ds(dst + blk, blk), :] = k_ref[rows, :].astype(BF16)
                    vd_ref[pl.ds(dst + blk, blk), :] = v_ref[rows, :].astype(BF16)
                return carry

            lax.fori_loop(0, T // (blk * B_UNROLL), stage, 0)

            def blocks(it, carry):
                us = range(B_UNROLL)
                pos = [where(it * B_UNROLL + u) for u in us]
                n = [x[0] for x in pos]
                cur = [pl.ds(x[1], blk, stride=dil) for x in pos]
                q = [(q_ref[cur[u], :] * scale).astype(BF16) for u in us]
                kcat = [kd_ref[pl.ds(x[2], 2 * blk), :] for x in pos]
                vcat = [vd_ref[pl.ds(x[2], 2 * blk), :] for x in pos]
                logits = [_dot_nt(q[u], kcat[u]) + bias for u in us]
                logits = [jnp.where(first_keys & (n[u] == 0), -jnp.inf, logits[u]) for u in us]
                mx = [jnp.max(x, axis=-1, keepdims=True) for x in logits]
                p = [jnp.exp(logits[u] - mx[u]) for u in us]
                pvd = [_dot(p[u].astype(BF16), jnp.concatenate([vcat[u], ones_cols], axis=1)) for u in us]
                pv = [x[:, :B_HEAD_DIM] for x in pvd]
                den = [x[:, B_HEAD_DIM:] for x in pvd]
                if gi > 0:
                    m_old = [m_ref[cur[u], :] for u in us]
                    l_old = [l_ref[cur[u], :] for u in us]
                    acc_old = [acc_ref[cur[u], :] for u in us]
                    m_new = [jnp.maximum(m_old[u], mx[u]) for u in us]
                    c_old = [jnp.exp(m_old[u] - m_new[u]) for u in us]
                    c_new = [jnp.exp(mx[u] - m_new[u]) for u in us]
                    pv = [acc_old[u] * c_old[u] + pv[u] * c_new[u] for u in us]
                    den = [l_old[u] * c_old[u] + den[u] * c_new[u] for u in us]
                    mx = m_new
                for u in us:
                    acc_ref[cur[u], :] = pv[u]
                    m_ref[cur[u], :] = mx[u]
                    l_ref[cur[u], :] = den[u]
                return carry

            lax.fori_loop(0, T // (blk * B_UNROLL), blocks, 0)

    @pl.when(step == G - 1)
    def _():
        o_ref[...] = (acc_ref[...] / l_ref[...]).astype(o_ref.dtype)


def attn_prompt(qkv, bias, n_seq, T):
    H, Dh, G = B_HEADS, B_HEAD_DIM, len(B_DILATIONS)

    def col(which):
        return lambda b, h, s: (b, ((G - 1 - s) * 3 + which) * H + h)

    staged_rows = T + B_BLK * max(B_DILATIONS)
    return pl.pallas_call(
        functools.partial(_attn_prompt_body, T=T), grid=(n_seq, H, G),
        in_specs=[pl.BlockSpec((T, Dh), col(0)), pl.BlockSpec((T, Dh), col(1)), pl.BlockSpec((T, Dh), col(2)),
                  pl.BlockSpec((None, None, B_BLK, 2 * B_BLK), lambda b, h, s: (G - 1 - s, h, 0, 0))],
        out_specs=pl.BlockSpec((T, Dh), lambda b, h, s: (b, h)),
        out_shape=jax.ShapeDtypeStruct((n_seq * T, H * Dh), BF16),
        scratch_shapes=[pltpu.VMEM((T, Dh), F32), pltpu.VMEM((T, 1), F32), pltpu.VMEM((T, Dh), F32),
                        pltpu.VMEM((staged_rows, Dh), BF16), pltpu.VMEM((staged_rows, Dh), BF16)],
        compiler_params=_params(("parallel", "parallel", "arbitrary")), name="attn_prompt",
    )(qkv, qkv, qkv, bias)


def _attn_sample_body(q_ref, k_ref, v_ref, c0_ref, c1_ref, c2_ref, bias_ref, o_ref, *, t_valid):
    blk = B_BLK
    scale = B_HEAD_DIM ** -0.5
    caches = (c0_ref, c1_ref, c2_ref)
    o_ref[...] = jnp.zeros_like(o_ref)
    for t in range(t_valid):
        m_run = l_run = acc = None
        for gi, dil in enumerate(B_DILATIONS):
            q = q_ref[t, gi] * scale
            c_ref = caches[gi]
            if dil == 1:
                kcat = jnp.concatenate([c_ref[t:, 0, 0], k_ref[:t + 1, gi]], axis=0)
                vcat = jnp.concatenate([c_ref[t:, 0, 1], v_ref[:t + 1, gi]], axis=0)
            else:
                kcat = jnp.concatenate([c_ref[:, t, 0], k_ref[t:t + 1, gi]], axis=0)
                vcat = jnp.concatenate([c_ref[:, t, 1], v_ref[t:t + 1, gi]], axis=0)
            logits = jnp.sum(q[None] * kcat, axis=-1, keepdims=True) + bias_ref[gi, :blk + 1]
            mx = jnp.max(logits, axis=0)
            p = jnp.exp(logits - mx[None])
            den = jnp.sum(p, axis=0)
            pv = jnp.sum(p * vcat, axis=0)
            if gi == 0:
                m_run, l_run, acc = mx, den, pv
            else:
                m_new = jnp.maximum(m_run, mx)
                c_old, c_new = jnp.exp(m_run - m_new), jnp.exp(mx - m_new)
                acc = acc * c_old + pv * c_new
                l_run = l_run * c_old + den * c_new
                m_run = m_new
        o_ref[t] = (acc / l_run).astype(o_ref.dtype)


def attn_sample(qkv, caches, bias, n_seq, T, row0, t_valid):
    H, Dh, G = B_HEADS, B_HEAD_DIM, len(B_DILATIONS)
    rb0 = row0 // T
    assert t_valid <= min(d for d in B_DILATIONS if d > 1)
    q5 = qkv.reshape(qkv.shape[0], G, 3, H, Dh)

    def spec(which):
        return pl.BlockSpec((T, G, None, H, Dh), lambda b: (b + rb0, 0, which, 0, 0))

    cache_specs = [pl.BlockSpec((None, B_BLK, min(d, t_valid), 2, H, Dh), lambda b: (b, 0, 0, 0, 0, 0))
                   for d in B_DILATIONS]
    return pl.pallas_call(
        functools.partial(_attn_sample_body, t_valid=t_valid), grid=(n_seq,),
        in_specs=[spec(0), spec(1), spec(2)] + cache_specs + [pl.BlockSpec(bias.shape, lambda b: (0, 0, 0, 0))],
        out_specs=pl.BlockSpec((T, H, Dh), lambda b: (b, 0, 0)),
        out_shape=jax.ShapeDtypeStruct((n_seq * T, H, Dh), BF16),
        compiler_params=_params(("parallel",)), name="attn_sample",
    )(q5, q5, q5, *caches, bias)


def _mlstm_body(q_ref, k_ref, v_ref, o_ref, gate_ref, gb_ref, nw_ref, c0_ref, n0_ref, m0_ref,
                y_ref, cT_ref, nT_ref, mT_ref, C_scr, n_scr, m_scr, *, L, TB, t_valid):
    H, E, V = C_HEADS, C_QK_DIM, C_V_DIM
    tb = pl.program_id(1)

    @pl.when(tb == 0)
    def _():
        C_scr[...] = c0_ref[...]
        n_scr[...] = n0_ref[...]
        m_scr[...] = m0_ref[...]

    row = lax.broadcasted_iota(jnp.int32, (L, L), 0)
    col = lax.broadcasted_iota(jnp.int32, (L, L), 1)
    causal = row >= col
    tril = causal.astype(F32)
    gb = gb_ref[...]
    nw = nw_ref[...]
    lane = lax.broadcasted_iota(jnp.int32, (L, LANES), 1)

    def chunk(c, carry):
        sl = pl.ds(pl.multiple_of(c * L, L), L)
        gact = C_GATE_CAP * jnp.tanh((gate_ref[sl, :] + gb) / C_GATE_CAP)
        lf = jnp.minimum(gact, 0.0) - jnp.log(1.0 + jnp.exp(-jnp.abs(gact)))
        ig = gact
        valid = None
        if t_valid is not None:
            t_idx = tb * TB + c * L + lax.broadcasted_iota(jnp.int32, (L, 1), 0)
            valid = t_idx < t_valid
            ig = jnp.where(valid, ig, -1e30)
            lf = jnp.where(valid, lf, 0.0)
        bcum = _dot(tril, lf, HIGHEST)
        ig_t = ig.T
        bcum_t = bcum.T
        heads = range(H)
        es = [slice(h * E, (h + 1) * E) for h in heads]
        vs = [slice(h * V, (h + 1) * V) for h in heads]
        b_col = [bcum[:, H + h:H + h + 1] for h in heads]
        b_row = [bcum_t[H + h:H + h + 1, :] for h in heads]
        ig_col = [ig[:, h:h + 1] for h in heads]
        ig_row = [ig_t[h:h + 1, :] for h in heads]
        q = [q_ref[sl, s] for s in es]
        k = [k_ref[sl, s] * (E ** -0.5) for s in es]
        v = [v_ref[sl, s] for s in vs]
        if valid is not None:
            q, k, v = ([jnp.where(valid, t, 0.0) for t in ts] for ts in (q, k, v))
        m_prev = [m_scr[h:h + 1, 0:1] for h in heads]
        n_prev = [n_scr[h:h + 1, :] for h in heads]
        C = [C_scr[h] for h in heads]
        dm = [jnp.where(causal, b_col[h] - b_row[h] + ig_row[h], -jnp.inf) for h in heads]
        inter = [b_col[h] + m_prev[h] for h in heads]
        mt = [jnp.maximum(inter[h], jnp.max(dm[h], axis=-1, keepdims=True)) for h in heads]
        w_d = [jnp.exp(dm[h] - mt[h]) for h in heads]
        w_i = [jnp.exp(inter[h] - mt[h]) for h in heads]
        qb, kb, vb = ([t.astype(BF16) for t in ts] for ts in (q, k, v))
        sc = [_dot_nt(qb[h], kb[h]) * w_d[h] for h in heads]
        qc = [_dot_nt(qb[h], C[h].astype(BF16)) for h in heads]
        num = [_dot(sc[h].astype(BF16), vb[h]) + w_i[h] * qc[h] for h in heads]
        den = [jnp.sum(sc[h], axis=-1, keepdims=True) + w_i[h] * jnp.sum(q[h] * n_prev[h], axis=-1, keepdims=True)
               for h in heads]
        hh = [num[h] / jnp.maximum(jnp.abs(den[h]), jnp.exp(-mt[h])) for h in heads]
        m_new = [x[L - 1:L, :] for x in mt]
        b_end = [x[L - 1:L, :] for x in b_col]
        w_s = [jnp.exp(b_end[h] - b_col[h] + ig_col[h] - m_new[h]) for h in heads]
        dec = [jnp.exp(b_end[h] + m_prev[h] - m_new[h]) for h in heads]
        c_upd = [_dot_tn((w_s[h] * v[h]).astype(BF16), kb[h]) for h in heads]
        outs = []
        for h in heads:
            C_scr[h] = dec[h] * C[h] + c_upd[h]
            n_scr[h:h + 1, :] = dec[h] * n_prev[h] + jnp.sum(w_s[h] * k[h], axis=0, keepdims=True)
            m_scr[h:h + 1, :] = jnp.broadcast_to(m_new[h], (1, LANES))
            hn = hh[h] * lax.rsqrt(jnp.mean(hh[h] * hh[h], axis=-1, keepdims=True) + NORM_EPS) * nw[:, vs[h]]
            outs.append(hn * _sigmoid(o_ref[sl, vs[h]]))
        y_ref[sl, :] = jnp.concatenate(outs, axis=-1).astype(y_ref.dtype)
        return carry

    lax.fori_loop(0, TB // L, chunk, 0)

    @pl.when(tb == pl.num_programs(1) - 1)
    def _():
        cT_ref[...] = C_scr[...]
        nT_ref[...] = n_scr[...]
        mT_ref[...] = m_scr[...]


def mlstm_recurrence(proj, gate_bias, norm_w, c0, n0, m0, n_seq, T, row0, TB, t_valid):
    H, E, V = C_HEADS, C_QK_DIM, C_V_DIM
    nb = T // TB
    rb0 = row0 // TB
    HE, HV = H * E, H * V

    def cols(cb):
        return lambda b, t: (rb0 + b * nb + t, cb)

    st4 = lambda b, t: (b, 0, 0, 0)
    st3 = lambda b, t: (b, 0, 0)
    return pl.pallas_call(
        functools.partial(_mlstm_body, L=CHUNK, TB=TB, t_valid=t_valid), grid=(n_seq, nb),
        in_specs=[pl.BlockSpec((TB, HE), cols(0)), pl.BlockSpec((TB, HE), cols(1)),
                  pl.BlockSpec((TB, HV), cols(2 * HE // HV)), pl.BlockSpec((TB, HV), cols(2 * HE // HV + 1)),
                  pl.BlockSpec((TB, LANES), cols((2 * HE + 2 * HV) // LANES)),
                  pl.BlockSpec((1, LANES), lambda b, t: (0, 0)), pl.BlockSpec((1, HV), lambda b, t: (0, 0)),
                  pl.BlockSpec((None, H, V, E), st4), pl.BlockSpec((None, H, E), st3),
                  pl.BlockSpec((None, H, LANES), st3)],
        out_specs=[pl.BlockSpec((TB, HV), lambda b, t: (b * nb + t, 0)),
                   pl.BlockSpec((None, H, V, E), st4), pl.BlockSpec((None, H, E), st3),
                   pl.BlockSpec((None, H, LANES), st3)],
        out_shape=[jax.ShapeDtypeStruct((n_seq * T, HV), BF16), jax.ShapeDtypeStruct((n_seq, H, V, E), F32),
                   jax.ShapeDtypeStruct((n_seq, H, E), F32), jax.ShapeDtypeStruct((n_seq, H, LANES), F32)],
        scratch_shapes=[pltpu.VMEM((H, V, E), F32), pltpu.VMEM((H, E), F32), pltpu.VMEM((H, LANES), F32)],
        compiler_params=_params(("parallel", "arbitrary")), name="mlstm_recurrence",
    )(proj, proj, proj, proj, proj, gate_bias, norm_w, c0, n0, m0)


def _tile(n, target):
    return max(t for t in range(LANES, min(n, target) + 1, LANES) if n % t == 0)


def _pad_cols(w, n):
    return jnp.pad(w, ((0, 0), (0, n - w.shape[1])))


def _pad_rows(w, n):
    return jnp.pad(w, ((0, n - w.shape[0]), (0, 0)))


def kernel(x_prompt, x_sample, state_a_wkv, state_a_shift, cache_b_kv_g0, cache_b_kv_g1, cache_b_kv_g2, state_c_C, state_c_n, state_c_m, rel_bias, norm_ffn1, ffn1_w_in, ffn1_w_out, norm_mix, norm_ffn2, ffn2_w_in, ffn2_w_out, norm_final, a_mu, a_w_rkv, a_w0, a_w1, a_w2, a_a0, a_a1, a_a2, a_g1, a_g2, a_k_k, a_k_a, a_r_k, a_gn_w, a_gn_b, a_w_out, a_v0, a_v1, a_v2, b_w_qkv, b_w_out, c_w_in, c_b_gates, c_norm_w, c_w_out):
    Bp, Tp, D = x_prompt.shape
    Bs, Ts, _ = x_sample.shape
    depth = norm_mix.shape[0]
    Tsp = SAMPLE_PAD
    Mp, Ms = Bp * Tp, Bs * Tsp
    M = Mp + Ms
    TM = 512
    TN = _tile(D, 2048)
    TF = _tile(ffn1_w_out.shape[1], 512)
    H_a = D // A_HEAD_DIM
    G, H_b, Dh = len(B_DILATIONS), B_HEADS, B_HEAD_DIM
    bf = lambda w: w.astype(BF16)

    x = jnp.concatenate([x_prompt.reshape(Mp, D),
                         jnp.pad(x_sample, ((0, 0), (0, Tsp - Ts), (0, 0))).reshape(Ms, D)], axis=0)

    def last_rows(t):
        return t[:Mp].reshape(Bp, Tp, -1)[:, -1], t[Mp:].reshape(Bs, Tsp, -1)[:, Ts - 1]

    qi = jnp.arange(B_BLK)[:, None]
    kj = jnp.arange(2 * B_BLK)[None, :]
    step = qi + B_BLK - kj
    step_ok = (step >= 0) & (step <= B_BLK)
    m_desc = B_BLK - jnp.arange(B_BLK + 8)
    bias_p, bias_s = [], []
    buckets = jnp.arange(N_BUCKETS)
    for gi, dil in enumerate(B_DILATIONS):
        tab = rel_bias[:, gi * H_b:(gi + 1) * H_b].astype(F32)
        hot = (_rel_bucket(jnp.clip(step, 0, B_BLK) * dil)[None] == buckets[:, None, None]).astype(F32)
        bp = jnp.einsum("nh,nqk->hqk", tab, hot, precision=HIGHEST)
        bias_p.append(jnp.where(step_ok[None], bp, -jnp.inf))
        bs = tab[_rel_bucket(jnp.maximum(m_desc, 0) * dil)]
        bias_s.append(jnp.broadcast_to(bs[:, :, None], (B_BLK + 8, H_b, Dh)))
    bias_p, bias_s = jnp.stack(bias_p), jnp.stack(bias_s)
    ffn1_in, ffn1_out, ffn2_in, ffn2_out = bf(ffn1_w_in), bf(ffn1_w_out), bf(ffn2_w_in), bf(ffn2_w_out)

    outs_a_wkv, outs_a_shift, outs_c = ([], []), ([], []), ([], [], [], [], [], [])
    outs_b = [([], []) for _ in range(G)]
    v_first = None
    for i in range(depth):
        x = ffn(x, norm_ffn1[i], ffn1_in, ffn1_out, i, TM, TF)
        kind, j = i % 3, i // 3
        if kind == 0:
            starts = jnp.concatenate([jnp.zeros((Mp // Tsp, D), F32), state_a_shift[j]], axis=0)
            u, u_prev = rmsnorm_shift(x, norm_mix[i], starts, Tp, Mp, Tsp, TM)
            mu = a_mu[j]
            rkv = rwkv_rkv(u, u_prev, mu[jnp.array([0, 2, 3])][:, None, :], bf(a_w_rkv[j]), TM, TN)
            lr = LANES
            w_br = (bf(_pad_cols(a_w1[j], lr)), bf(_pad_rows(a_w2[j], lr)), a_w0[j].reshape(1, D))
            a_br = (bf(_pad_cols(a_a1[j], lr)), bf(_pad_rows(a_a2[j], lr)), a_a0[j].reshape(1, D))
            g_br = (bf(a_g1[j]), bf(a_g2[j]))
            v_br = None
            if j > 0:
                v_br = (bf(_pad_cols(a_v1[j - 1], lr)), bf(_pad_rows(a_v2[j - 1], lr)), a_v0[j - 1].reshape(1, D))
            lora = rwkv_lora(u, u_prev, mu[jnp.array([1, 4, 5, 3])], w_br, a_br, g_br, v_br, 256)
            lw, a_lr, gate = lora[:3]
            vres = None if j == 0 else (v_first, lora[3])
            if j == 0:
                v_first = rkv
            par = tuple(p.reshape(1, D) for p in (a_k_k[j], a_k_a[j], a_r_k[j], a_gn_w[j], a_gn_b[j]))
            s0p = jnp.zeros((Bp * H_a, A_HEAD_DIM, A_HEAD_DIM), F32)
            s0s = state_a_wkv[j].reshape(Bs * H_a, A_HEAD_DIM, A_HEAD_DIM)
            yp, sp = rwkv_recurrence(rkv, lw, a_lr, gate, vres, par, s0p, Bp, Tp, 0, 256, None)
            ys, ss = rwkv_recurrence(rkv, lw, a_lr, gate, vres, par, s0s, Bs, Tsp, Mp, Tsp, Ts)
            x = matmul((yp, ys), bf(a_w_out[j]), TM, TN, residual=x)
            outs_a_wkv[0].append(sp.reshape(Bp, H_a, A_HEAD_DIM, A_HEAD_DIM))
            outs_a_wkv[1].append(ss.reshape(Bs, H_a, A_HEAD_DIM, A_HEAD_DIM))
            sh_p, sh_s = last_rows(u)
            outs_a_shift[0].append(sh_p)
            outs_a_shift[1].append(sh_s)
        elif kind == 1:
            qkv = matmul(x, bf(b_w_qkv[j]), TM, _tile(b_w_qkv.shape[2], 2048), norm_g=norm_mix[i])
            caches = [c[j].reshape(Bs, B_BLK, d, 2, H_b, Dh)
                      for c, d in zip((cache_b_kv_g0, cache_b_kv_g1, cache_b_kv_g2), B_DILATIONS)]
            op = attn_prompt(qkv, bias_p, Bp, Tp)
            os_ = attn_sample(qkv[Mp:], caches, bias_s, Bs, Tsp, 0, Ts)
            x = matmul((op, os_.reshape(Ms, H_b * Dh)), bf(b_w_out[j]), TM, TN, residual=x)
            for gi in range(G):
                keep = min(B_WINDOWS[gi], Tp)
                c0, c1 = (gi * 3 + 1) * H_b * Dh, (gi * 3 + 3) * H_b * Dh
                kv_p = jnp.stack([lax.slice(qkv, ((b + 1) * Tp - keep, c0), ((b + 1) * Tp, c1)) for b in range(Bp)])
                kv_s = lax.slice(qkv, (Mp, c0), (M, c1)).reshape(Bs, Tsp, c1 - c0)[:, :Ts]
                outs_b[gi][0].append(kv_p.reshape(Bp, keep, 2, H_b, Dh))
                outs_b[gi][1].append(kv_s.reshape(Bs, Ts, 2, H_b, Dh))
        else:
            H, E, V = C_HEADS, C_QK_DIM, C_V_DIM
            n_in = c_w_in.shape[2]
            n_pad = -(-n_in // LANES) * LANES
            proj = matmul(x, bf(_pad_cols(c_w_in[j], n_pad)), TM, _tile(n_pad, 1024), norm_g=norm_mix[i])
            gbias = _pad_cols(c_b_gates[j].reshape(1, 2 * H), LANES)
            nw = c_norm_w[j].reshape(1, H * V)
            zc = (jnp.zeros((Bp, H, V, E), F32), jnp.zeros((Bp, H, E), F32), jnp.zeros((Bp, H, LANES), F32))
            sc = (state_c_C[j], state_c_n[j], jnp.broadcast_to(state_c_m[j][:, :, None], (Bs, H, LANES)))
            hp, cp, np_, mp = mlstm_recurrence(proj, gbias, nw, *zc, Bp, Tp, 0, 256, None)
            hs, cs, ns, ms = mlstm_recurrence(proj, gbias, nw, *sc, Bs, Tsp, Mp, Tsp, Ts)
            x = matmul((hp, hs), bf(c_w_out[j]), TM, TN, residual=x)
            for lst, val in zip(outs_c, (cp, cs, np_, ns, mp[:, :, 0], ms[:, :, 0])):
                lst.append(val)
        x = ffn(x, norm_ffn2[i], ffn2_in, ffn2_out, i, TM, TF)

    y = rmsnorm(x, norm_final, TM)
    y_prompt = y[:Mp].reshape(Bp, Tp, D)
    y_sample = y[Mp:].reshape(Bs, Tsp, D)[:, :Ts]
    st = jnp.stack
    return (y_prompt, y_sample, st(outs_a_wkv[0]), st(outs_a_wkv[1]), st(outs_a_shift[0]), st(outs_a_shift[1]),
            st(outs_b[0][0]), st(outs_b[0][1]), st(outs_b[1][0]), st(outs_b[1][1]), st(outs_b[2][0]), st(outs_b[2][1]),
            st(outs_c[0]), st(outs_c[1]), st(outs_c[2]), st(outs_c[3]), st(outs_c[4]), st(outs_c[5]))
```

```python
import functools
import math

import jax
import jax.numpy as jnp
from jax import lax
from jax.experimental import pallas as pl
from jax.experimental.pallas import tpu as pltpu

F32 = jnp.float32
BF16 = jnp.bfloat16
HIGHEST = lax.Precision.HIGHEST

NORM_EPS = 1e-6
A_HEAD_DIM = 64
A_GN_EPS = 64e-5
A_REC_LANES = 1024
A_REC_PRECISION = "bf16"
B_WINDOWS = (128, 512, 2048)
B_DILATIONS = (1, 4, 16)
B_HEADS = 16
B_HEAD_DIM = 128
B_BLK = 128
B_UNROLL = 4
N_BUCKETS = 32
BUCKET_MAX_DIST = 2048
C_HEADS = 8
C_QK_DIM = 128
C_V_DIM = 256
C_GATE_CAP = 15.0
CHUNK = 64
SAMPLE_PAD = 8
LANES = 128
VMEM_LIMIT = 56 * 1024 * 1024


def _row_dtype(rows):
    return BF16 if rows % 16 == 0 else F32


def _params(sem):
    return pltpu.CompilerParams(dimension_semantics=sem, vmem_limit_bytes=VMEM_LIMIT)


def _dot(a, b, precision=None):
    return jnp.dot(a, b, preferred_element_type=F32, precision=precision)


def _dot_nt(a, b, precision=None):
    return lax.dot_general(a, b, (((1,), (1,)), ((), ())), preferred_element_type=F32, precision=precision)


def _dot_tn(a, b, precision=None):
    return lax.dot_general(a, b, (((0,), (0,)), ((), ())), preferred_element_type=F32, precision=precision)


_NN = (((1,), (0,)), ((), ()))
_NT = (((1,), (1,)), ((), ()))
_TN = (((0,), (0,)), ((), ()))


def _pmm(a, b, dims, mode):
    dg = functools.partial(lax.dot_general, dimension_numbers=dims, preferred_element_type=F32)
    if mode == "highest":
        return dg(a, b, precision=HIGHEST)
    a_hi, b_hi = a.astype(BF16), b.astype(BF16)
    if mode == "bf16":
        return dg(a_hi, b_hi)
    a_lo = (a - a_hi.astype(F32)).astype(BF16)
    b_lo = (b - b_hi.astype(F32)).astype(BF16)
    return dg(a_hi, b_hi) + (dg(a_hi, b_lo) + dg(a_lo, b_hi))


def _sigmoid(x):
    return 1.0 / (1.0 + jnp.exp(-x))


def _rms(x, g):
    ms = jnp.mean(x * x, axis=-1, keepdims=True)
    return x * lax.rsqrt(ms + NORM_EPS) * g


def _rmsnorm_body(x_ref, g_ref, o_ref):
    o_ref[...] = _rms(x_ref[...], g_ref[...]).astype(o_ref.dtype)


def rmsnorm(x, g, tm):
    M, D = x.shape
    return pl.pallas_call(
        _rmsnorm_body, grid=(M // tm,),
        in_specs=[pl.BlockSpec((tm, D), lambda i: (i, 0)), pl.BlockSpec((1, D), lambda i: (0, 0))],
        out_specs=pl.BlockSpec((tm, D), lambda i: (i, 0)),
        out_shape=jax.ShapeDtypeStruct((M, D), F32),
        compiler_params=_params(("parallel",)), name="rmsnorm",
    )(x, g.reshape(1, D))


def _rmsnorm_shift_body(x_ref, xp_ref, g_ref, st_ref, u_ref, up_ref, *, tm, grp, seq_len, seq_rows):
    i = pl.program_id(0)
    g = g_ref[...]
    u = _rms(x_ref[...], g)
    u_ref[...] = u
    up_ref[...] = pltpu.roll(u, 1, axis=0)
    tail = _rms(xp_ref[...], g)[-1:, :]
    n_late = st_ref.shape[0]
    for k in range(tm // grp):
        row0 = i * tm + k * grp
        late = row0 >= seq_rows
        is_start = late | (lax.rem(row0, seq_len) == 0)
        state = st_ref[pl.ds(jnp.clip((row0 - seq_rows) // grp, 0, n_late - 1), 1), :]
        before = tail if k == 0 else u[k * grp - 1:k * grp, :]
        up_ref[k * grp:k * grp + 1, :] = jnp.where(is_start, jnp.where(late, state, 0.0), before)


def rmsnorm_shift(x, g, late_states, seq_len, seq_rows, grp, tm):
    M, D = x.shape
    sub = 8
    return pl.pallas_call(
        functools.partial(_rmsnorm_shift_body, tm=tm, grp=grp, seq_len=seq_len, seq_rows=seq_rows), grid=(M // tm,),
        in_specs=[pl.BlockSpec((tm, D), lambda i: (i, 0)),
                  pl.BlockSpec((sub, D), lambda i: (jnp.maximum(i * (tm // sub) - 1, 0), 0)),
                  pl.BlockSpec((1, D), lambda i: (0, 0)),
                  pl.BlockSpec(late_states.shape, lambda i: (0, 0))],
        out_specs=[pl.BlockSpec((tm, D), lambda i: (i, 0))] * 2,
        out_shape=[jax.ShapeDtypeStruct((M, D), F32)] * 2,
        compiler_params=_params(("parallel",)), name="rmsnorm_shift",
    )(x, x, g.reshape(1, D), late_states)


def _ffn_body(x_ref, g_ref, wg_ref, wu_ref, wo_ref, o_ref, xn_ref):
    j = pl.program_id(1)

    @pl.when(j == 0)
    def _():
        xn_ref[...] = _rms(x_ref[...], g_ref[...]).astype(BF16)
        o_ref[...] = jnp.zeros_like(o_ref)

    xn = xn_ref[...]
    gate = _dot(xn, wg_ref[...].astype(BF16))
    up = _dot(xn, wu_ref[...].astype(BF16))
    h = (gate * _sigmoid(gate) * up).astype(BF16)
    o_ref[...] += _dot(h, wo_ref[...].astype(BF16))

    @pl.when(j == pl.num_programs(1) - 1)
    def _():
        o_ref[...] = x_ref[...] + 0.5 * o_ref[...]


def ffn(x, g, w_in, w_out, layer, tm, tf):
    M, D = x.shape
    Fh = w_out.shape[1]
    nf = Fh // tf
    return pl.pallas_call(
        _ffn_body, grid=(M // tm, nf),
        in_specs=[pl.BlockSpec((tm, D), lambda i, j: (i, 0)),
                  pl.BlockSpec((1, D), lambda i, j: (0, 0)),
                  pl.BlockSpec((None, D, tf), lambda i, j: (layer, 0, j)),
                  pl.BlockSpec((None, D, tf), lambda i, j: (layer, 0, j + nf)),
                  pl.BlockSpec((None, tf, D), lambda i, j: (layer, j, 0))],
        out_specs=pl.BlockSpec((tm, D), lambda i, j: (i, 0)),
        out_shape=jax.ShapeDtypeStruct((M, D), F32),
        scratch_shapes=[pltpu.VMEM((tm, D), BF16)],
        compiler_params=_params(("parallel", "arbitrary")), name="ffn",
    )(x, g.reshape(1, D), w_in, w_in, w_out)


def _mm_body(*refs, nb0, has_norm, has_res):
    it = iter(refs)
    x_ref = next(it)
    x1_ref = next(it) if nb0 is not None else None
    g_ref = next(it) if has_norm else None
    w_ref = next(it)
    res_ref = next(it) if has_res else None
    o_ref = next(it)
    xs_ref = next(it)

    def stage(ref, rows=slice(None), dst=slice(None)):
        x = ref[rows, :].astype(F32)
        if has_norm:
            x = _rms(x, g_ref[...])
        xs_ref[dst, :] = x.astype(BF16)

    @pl.when(pl.program_id(1) == 0)
    def _():
        if nb0 is None:
            stage(x_ref)
        else:
            rem = xs_ref.shape[0] - x1_ref.shape[0]
            pl.when(pl.program_id(0) < nb0)(lambda: stage(x_ref))

            @pl.when(pl.program_id(0) == nb0)
            def _():
                if rem > 0:
                    stage(x_ref, slice(0, rem), slice(0, rem))
                stage(x1_ref, slice(None), slice(rem, None))

    acc = _dot(xs_ref[...], w_ref[...])
    if has_res:
        acc = res_ref[...] + acc
    o_ref[...] = acc.astype(o_ref.dtype)


def matmul(x, w, tm, tn, norm_g=None, residual=None, out_dtype=F32):
    nb0 = None
    if isinstance(x, tuple):
        x0, x1 = x
        nb0 = x0.shape[0] // tm
        M, K = x0.shape[0] + x1.shape[0], x0.shape[1]
        assert M == (nb0 + 1) * tm and x1.shape[0] <= tm and x1.shape[0] % 16 == 0
        args = [x0, x1]
        last0 = -(-x0.shape[0] // tm) - 1
        specs = [pl.BlockSpec((tm, K), lambda i, j: (jnp.minimum(i, last0), 0)),
                 pl.BlockSpec(x1.shape, lambda i, j: (0, 0))]
    else:
        M, K = x.shape
        args = [x]
        specs = [pl.BlockSpec((tm, K), lambda i, j: (i, 0))]
    N = w.shape[1]
    if norm_g is not None:
        args.append(norm_g.reshape(1, K))
        specs.append(pl.BlockSpec((1, K), lambda i, j: (0, 0)))
    args.append(w)
    specs.append(pl.BlockSpec((K, tn), lambda i, j: (0, j)))
    if residual is not None:
        args.append(residual)
        specs.append(pl.BlockSpec((tm, tn), lambda i, j: (i, j)))
    return pl.pallas_call(
        functools.partial(_mm_body, nb0=nb0, has_norm=norm_g is not None, has_res=residual is not None),
        grid=(M // tm, N // tn), in_specs=specs,
        out_specs=pl.BlockSpec((tm, tn), lambda i, j: (i, j)),
        out_shape=jax.ShapeDtypeStruct((M, N), out_dtype),
        scratch_shapes=[pltpu.VMEM((tm, K), BF16)],
        compiler_params=_params(("parallel", "arbitrary")), name="matmul",
    )(*args)


def _rkv_body(u_ref, up_ref, mu_ref, w_ref, o_ref, xs_ref):
    @pl.when(pl.program_id(2) == 0)
    def _():
        u = u_ref[...]
        xs_ref[...] = (u + (up_ref[...] - u) * mu_ref[...]).astype(BF16)

    o_ref[...] = _dot(xs_ref[...], w_ref[...])


def rwkv_rkv(u, u_prev, mu3, w3, tm, tn):
    M, D = u.shape
    return pl.pallas_call(
        _rkv_body, grid=(M // tm, 3, D // tn),
        in_specs=[pl.BlockSpec((tm, D), lambda i, k, j: (i, 0)),
                  pl.BlockSpec((tm, D), lambda i, k, j: (i, 0)),
                  pl.BlockSpec((None, 1, D), lambda i, k, j: (k, 0, 0)),
                  pl.BlockSpec((None, D, tn), lambda i, k, j: (k, 0, j))],
        out_specs=pl.BlockSpec((None, tm, tn), lambda i, k, j: (k, i, j)),
        out_shape=jax.ShapeDtypeStruct((3, M, D), F32),
        scratch_shapes=[pltpu.VMEM((tm, D), BF16)],
        compiler_params=_params(("parallel", "arbitrary", "arbitrary")), name="rwkv_rkv",
    )(u, u_prev, mu3, w3)


def _lora_body(*refs, has_vres):
    it = iter(refs)
    u_ref, up_ref, mu_ref = next(it), next(it), next(it)
    w1, w2, w0 = next(it), next(it), next(it)
    a1, a2, a0 = next(it), next(it), next(it)
    g1, g2 = next(it), next(it)
    if has_vres:
        v1, v2, v0 = next(it), next(it), next(it)
    lw_ref, a_ref, g_ref = next(it), next(it), next(it)
    nu_ref = next(it) if has_vres else None

    u = u_ref[...]
    du = up_ref[...] - u

    def mix(n):
        return (u + du * mu_ref[n:n + 1, :]).astype(BF16)

    hw = jnp.tanh(_dot(mix(0), w1[...])).astype(BF16)
    w_pre = w0[...] + _dot(hw, w2[...])
    softplus = jnp.maximum(-w_pre, 0.0) + jnp.log(1.0 + jnp.exp(-jnp.abs(w_pre)))
    lw_ref[...] = -jnp.exp(-softplus - 0.5)
    ha = _dot(mix(1), a1[...]).astype(BF16)
    a_ref[...] = _sigmoid(a0[...] + _dot(ha, a2[...]))
    hg = _sigmoid(_dot(mix(2), g1[...])).astype(BF16)
    g_ref[...] = _dot(hg, g2[...])
    if has_vres:
        hv = _dot(mix(3), v1[...]).astype(BF16)
        nu_ref[...] = _sigmoid(v0[...] + _dot(hv, v2[...]))


def rwkv_lora(u, u_prev, mu4, w, a, g, v, tm):
    M, D = u.shape
    has_vres = v is not None
    row = lambda i: (i, 0)
    full = lambda i: (0, 0)
    args = [u, u_prev, mu4]
    specs = [pl.BlockSpec((tm, D), row), pl.BlockSpec((tm, D), row), pl.BlockSpec(mu4.shape, full)]
    for t in (w, a, g) + ((v,) if has_vres else ()):
        for m in t:
            args.append(m)
            specs.append(pl.BlockSpec(m.shape, full))
    n_out = 4 if has_vres else 3
    return pl.pallas_call(
        functools.partial(_lora_body, has_vres=has_vres), grid=(M // tm,), in_specs=specs,
        out_specs=[pl.BlockSpec((tm, D), row)] * n_out,
        out_shape=[jax.ShapeDtypeStruct((M, D), F32)] * n_out,
        compiler_params=_params(("parallel",)), name="rwkv_lora",
    )(*args)


def _rwkv_rec_body(*refs, L, TB, hb, t_valid, has_vres, prec):
    N = A_HEAD_DIM
    it = iter(refs)
    r_ref, k_ref, v_ref, lw_ref, a_ref, g_ref = (next(it) for _ in range(6))
    if has_vres:
        vf_ref, nu_ref = next(it), next(it)
    kk_ref, ka_ref, rk_ref, gnw_ref, gnb_ref, s0_ref = (next(it) for _ in range(6))
    y_ref, sT_ref, S_scr = next(it), next(it), next(it)
    tb = pl.program_id(2)

    assert L == N and 2 * N == LANES

    @pl.when(tb == 0)
    def _():
        for p in range(hb // 2):
            S_scr[p] = jnp.concatenate([s0_ref[2 * p], s0_ref[2 * p + 1]], axis=1)

    row = lax.broadcasted_iota(jnp.int32, (L, L), 0)
    col = lax.broadcasted_iota(jnp.int32, (L, L), 1)
    tril = (row >= col).astype(F32)
    row1 = lax.broadcasted_iota(jnp.int32, (L, 2 * N), 0)
    lane1 = lax.broadcasted_iota(jnp.int32, (L, 2 * N), 1)
    head0_lane = lane1 < N
    eye = ((lane1 & (N - 1)) == row1).astype(F32)
    row2 = lax.broadcasted_iota(jnp.int32, (2 * L, 2 * N), 0)
    lane2 = lax.broadcasted_iota(jnp.int32, (2 * L, 2 * N), 1)
    mask2 = (lane2 & (N - 1)) < jnp.where(row2 < L, row2, row2 - L + 1)
    bd_mask = (row2 // L) == (lane2 // N)
    ones_bd = bd_mask.astype(BF16)
    n_sq = int(math.log2(L)) - 1
    kk_p, ka_p, rk_p, gnw, gnb = kk_ref[...], ka_ref[...], rk_ref[...], gnw_ref[...], gnb_ref[...]

    def chunk(c, carry):
        sl = pl.ds(pl.multiple_of(c * L, L), L)

        def load(ref):
            if TB >= L:
                return ref[sl, :]
            return jnp.concatenate([ref[...], jnp.zeros((L - TB, ref.shape[1]), F32)], axis=0)

        r, k, v, lw, a, g = (load(ref) for ref in (r_ref, k_ref, v_ref, lw_ref, a_ref, g_ref))
        if has_vres:
            v = v + (load(vf_ref) - v) * load(nu_ref)
        if t_valid is not None:
            t_idx = tb * TB + c * L + lax.broadcasted_iota(jnp.int32, (L, 1), 0)
            valid = t_idx < t_valid
            r, k, v, lw = (jnp.where(valid, t, 0.0) for t in (r, k, v, lw))
        cum = _dot(tril, lw, HIGHEST)
        cum_end = cum[L - 1:L, :]
        w_cur, w_prev, w_inv, w_rem, w_end = (jnp.exp(cum), jnp.exp(cum - lw), jnp.exp(-cum),
                                              jnp.exp(cum_end - cum), jnp.exp(cum_end))
        mm, mm_nt, mm_tn = (functools.partial(_pmm, dims=d, mode=prec) for d in (_NN, _NT, _TN))
        pairs = range(hb // 2)
        ps = [slice(p * LANES, (p + 1) * LANES) for p in pairs]

        def head_sum(x):
            x_hi = x.astype(BF16)
            x_lo = (x - x_hi.astype(F32)).astype(BF16)
            return _dot(x_hi, ones_bd) + _dot(x_lo, ones_bd)

        def bdiag(x):
            return jnp.where(bd_mask, jnp.concatenate([x, x], axis=0), 0.0)

        kkp = [k[:, s] * kk_p[:, s] for s in ps]
        kk = [x / jnp.maximum(jnp.sqrt(head_sum(x * x)), 1e-12) for x in kkp]
        b = [kk[p] * a[:, ps[p]] for p in pairs]
        k2 = [k[:, s] * (1.0 + (a[:, s] - 1.0) * ka_p[:, s]) for s in ps]
        lhs2 = [jnp.concatenate([kk[p] * w_prev[:, ps[p]], r[:, ps[p]] * w_cur[:, ps[p]]], axis=0) for p in pairs]
        kd = [k2[p] * w_inv[:, ps[p]] for p in pairs]
        bd = [b[p] * w_inv[:, ps[p]] for p in pairs]
        kend = [k2[p] * w_rem[:, ps[p]] for p in pairs]
        bend = [b[p] * w_rem[:, ps[p]] for p in pairs]
        kkd = [x[:L] for x in lhs2]
        rd = [x[L:] for x in lhs2]
        a_kb2 = [mm_nt(lhs2[p], jnp.concatenate([bdiag(kd[p]), bdiag(bd[p])], axis=0)) for p in pairs]
        a_k = [jnp.where(mask2, x[:, :LANES], 0.0) for x in a_kb2]
        a_b = [jnp.where(mask2, x[:, LANES:], 0.0) for x in a_kb2]
        a_kb = [x[:L] for x in a_b]
        a_rb = [x[L:] for x in a_b]
        a_v = [mm(a_k[p], bdiag(v[:, ps[p]])) for p in pairs]
        t_inv = [eye - x for x in a_kb]
        pw = [mm(x, bdiag(x)) for x in a_kb]
        for _ in range(n_sq - 1):
            both = [mm(jnp.concatenate([pw[p], t_inv[p]], axis=0), bdiag(pw[p])) for p in pairs]
            t_inv = [t_inv[p] + both[p][L:] for p in pairs]
            pw = [x[:L] for x in both]
        t_inv = [t_inv[p] + mm(t_inv[p], bdiag(pw[p])) for p in pairs]
        ktcu = [mm(t_inv[p], jnp.concatenate([bdiag(kkd[p]), bdiag(a_v[p][:L])], axis=1)) for p in pairs]
        k_t = [x[:, :LANES] for x in ktcu]
        c_u = [x[:, LANES:] for x in ktcu]
        ykc = [mm(a_rb[p], jnp.concatenate([bdiag(k_t[p]), bdiag(c_u[p])], axis=1)) for p in pairs]
        y_k = [rd[p] - ykc[p][:, :LANES] for p in pairs]
        y_c = [a_v[p][L:] - ykc[p][:, LANES:] for p in pairs]
        S = [S_scr[p] for p in pairs]
        y = [mm_nt(y_k[p], bdiag(S[p])) + y_c[p] for p in pairs]
        upd = [mm_tn(jnp.concatenate([jnp.concatenate([v[:, ps[p]], jnp.zeros_like(k_t[p])], axis=1),
                                      jnp.concatenate([-c_u[p], k_t[p]], axis=1)], axis=0),
                     jnp.concatenate([kend[p], bend[p]], axis=0)) for p in pairs]
        S_c = [jnp.where(head0_lane, x[:N], x[N:LANES]) for x in upd]
        ktb = [jnp.where(bd_mask, x[LANES:], 0.0) for x in upd]
        for p in pairs:
            S_scr[p] = S[p] * w_end[:, ps[p]] - mm(S[p], ktb[p]) + S_c[p]
        mean = [head_sum(y[p]) * (1.0 / N) for p in pairs]
        var = [head_sum(jnp.square(y[p] - mean[p])) * (1.0 / N) for p in pairs]
        bonus = [head_sum(r[:, ps[p]] * k2[p] * rk_p[:, ps[p]]) * v[:, ps[p]] for p in pairs]
        outs = [((y[p] - mean[p]) * lax.rsqrt(var[p] + A_GN_EPS) * gnw[:, ps[p]] + gnb[:, ps[p]] + bonus[p])
                * g[:, ps[p]] for p in pairs]
        y_out = jnp.concatenate(outs, axis=-1).astype(y_ref.dtype)
        if TB >= L:
            y_ref[sl, :] = y_out
        else:
            y_ref[...] = y_out[:TB]
        return carry

    lax.fori_loop(0, max(TB // L, 1), chunk, 0)

    @pl.when(tb == pl.num_programs(2) - 1)
    def _():
        for p in range(hb // 2):
            S = S_scr[p]
            sT_ref[2 * p] = S[:, :N]
            sT_ref[2 * p + 1] = S[:, N:]


def rwkv_recurrence(rkv, lw, a, g, vres, params, s0, n_seq, T, row0, TB, t_valid):
    _, M, D = rkv.shape
    N = A_HEAD_DIM
    LW = min(A_REC_LANES, D)
    hb = LW // N
    nb = T // TB
    rb0 = row0 // TB
    has_vres = vres is not None
    seq = lambda b, h, t: (rb0 + b * nb + t, h)
    args, specs = [], []
    for n in range(3):
        args.append(rkv)
        specs.append(pl.BlockSpec((None, TB, LW), lambda b, h, t, n=n: (n, rb0 + b * nb + t, h)))
    for x in (lw, a, g):
        args.append(x)
        specs.append(pl.BlockSpec((TB, LW), seq))
    if has_vres:
        args += list(vres)
        specs += [pl.BlockSpec((None, TB, LW), lambda b, h, t: (2, rb0 + b * nb + t, h)), pl.BlockSpec((TB, LW), seq)]
    for p in params:
        args.append(p)
        specs.append(pl.BlockSpec((1, LW), lambda b, h, t: (0, h)))
    args.append(s0)
    specs.append(pl.BlockSpec((hb, N, N), lambda b, h, t: (b * (D // LW) + h, 0, 0)))
    return pl.pallas_call(
        functools.partial(_rwkv_rec_body, L=CHUNK, TB=TB, hb=hb, t_valid=t_valid, has_vres=has_vres,
                          prec=A_REC_PRECISION),
        grid=(n_seq, D // LW, nb), in_specs=specs,
        out_specs=[pl.BlockSpec((TB, LW), lambda b, h, t: (b * nb + t, h)),
                   pl.BlockSpec((hb, N, N), lambda b, h, t: (b * (D // LW) + h, 0, 0))],
        out_shape=[jax.ShapeDtypeStruct((n_seq * T, D), _row_dtype(TB)),
                   jax.ShapeDtypeStruct((n_seq * (D // N), N, N), F32)],
        scratch_shapes=[pltpu.VMEM((hb // 2, N, 2 * N), F32)],
        compiler_params=_params(("parallel", "parallel", "arbitrary")), name="rwkv_recurrence",
    )(*args)


def _rel_bucket(dist):
    exact = N_BUCKETS // 2
    d = jnp.maximum(dist, 1).astype(F32)
    log_b = exact + (jnp.log(d / exact) / math.log(BUCKET_MAX_DIST / exact) * (N_BUCKETS - exact)).astype(jnp.int32)
    return jnp.where(dist < exact, dist, jnp.minimum(log_b, N_BUCKETS - 1))


def _attn_prompt_body(q_ref, k_ref, v_ref, bias_ref, o_ref, acc_ref, m_ref, l_ref, kd_ref, vd_ref, *, T):
    step = pl.program_id(2)
    blk = B_BLK
    G = len(B_DILATIONS)
    scale = B_HEAD_DIM ** -0.5
    first_keys = lax.broadcasted_iota(jnp.int32, (blk, 2 * blk), 1) < blk
    ones_cols = jnp.ones((2 * blk, B_HEAD_DIM), BF16)

    for si, dil in enumerate(reversed(B_DILATIONS)):
        @pl.when(step == si)
        def _(gi=si, dil=dil):
            span = blk * dil
            res_rows = T // dil + blk
            bias = bias_ref[...]

            def where(idx):
                n = idx // dil
                r = idx - n * dil
                return n, n * span + r, pl.multiple_of(r * res_rows + n * blk, blk)

            for r in range(dil):
                kd_ref[r * res_rows:r * res_rows + blk, :] = jnp.zeros((blk, B_HEAD_DIM), BF16)
                vd_ref[r * res_rows:r * res_rows + blk, :] = jnp.zeros((blk, B_HEAD_DIM), BF16)

            def stage(it, carry):
                for u in range(B_UNROLL):
                    _, start, dst = where(it * B_UNROLL + u)
                    rows = pl.ds(start, blk, stride=dil)
                    kd_ref[pl.ds(dst + blk, blk), :] = k_ref[rows, :].astype(BF16)
                    vd_ref[pl.ds(dst + blk, blk), :] = v_ref[rows, :].astype(BF16)
                return carry

            lax.fori_loop(0, T // (blk * B_UNROLL), stage, 0)

            def blocks(it, carry):
                us = range(B_UNROLL)
                pos = [where(it * B_UNROLL + u) for u in us]
                n = [x[0] for x in pos]
                cur = [pl.ds(x[1], blk, stride=dil) for x in pos]
                q = [(q_ref[cur[u], :] * scale).astype(BF16) for u in us]
                kcat = [kd_ref[pl.ds(x[2], 2 * blk), :] for x in pos]
                vcat = [vd_ref[pl.ds(x[2], 2 * blk), :] for x in pos]
                logits = [_dot_nt(q[u], kcat[u]) + bias for u in us]
                logits = [jnp.where(first_keys & (n[u] == 0), -jnp.inf, logits[u]) for u in us]
                mx = [jnp.max(x, axis=-1, keepdims=True) for x in logits]
                p = [jnp.exp(logits[u] - mx[u]) for u in us]
                pvd = [_dot(p[u].astype(BF16), jnp.concatenate([vcat[u], ones_cols], axis=1)) for u in us]
                pv = [x[:, :B_HEAD_DIM] for x in pvd]
                den = [x[:, B_HEAD_DIM:] for x in pvd]
                if gi > 0:
                    m_old = [m_ref[cur[u], :] for u in us]
                    l_old = [l_ref[cur[u], :] for u in us]
                    acc_old = [acc_ref[cur[u], :] for u in us]
                    m_new = [jnp.maximum(m_old[u], mx[u]) for u in us]
                    c_old = [jnp.exp(m_old[u] - m_new[u]) for u in us]
                    c_new = [jnp.exp(mx[u] - m_new[u]) for u in us]
                    pv = [acc_old[u] * c_old[u] + pv[u] * c_new[u] for u in us]
                    den = [l_old[u] * c_old[u] + den[u] * c_new[u] for u in us]
                    mx = m_new
                for u in us:
                    acc_ref[cur[u], :] = pv[u]
                    m_ref[cur[u], :] = mx[u]
                    l_ref[cur[u], :] = den[u]
                return carry

            lax.fori_loop(0, T // (blk * B_UNROLL), blocks, 0)

    @pl.when(step == G - 1)
    def _():
        o_ref[...] = (acc_ref[...] / l_ref[...]).astype(o_ref.dtype)


def attn_prompt(qkv, bias, n_seq, T):
    H, Dh, G = B_HEADS, B_HEAD_DIM, len(B_DILATIONS)

    def col(which):
        return lambda b, h, s: (b, ((G - 1 - s) * 3 + which) * H + h)

    staged_rows = T + B_BLK * max(B_DILATIONS)
    return pl.pallas_call(
        functools.partial(_attn_prompt_body, T=T), grid=(n_seq, H, G),
        in_specs=[pl.BlockSpec((T, Dh), col(0)), pl.BlockSpec((T, Dh), col(1)), pl.BlockSpec((T, Dh), col(2)),
                  pl.BlockSpec((None, None, B_BLK, 2 * B_BLK), lambda b, h, s: (G - 1 - s, h, 0, 0))],
        out_specs=pl.BlockSpec((T, Dh), lambda b, h, s: (b, h)),
        out_shape=jax.ShapeDtypeStruct((n_seq * T, H * Dh), BF16),
        scratch_shapes=[pltpu.VMEM((T, Dh), F32), pltpu.VMEM((T, 1), F32), pltpu.VMEM((T, Dh), F32),
                        pltpu.VMEM((staged_rows, Dh), BF16), pltpu.VMEM((staged_rows, Dh), BF16)],
        compiler_params=_params(("parallel", "parallel", "arbitrary")), name="attn_prompt",
    )(qkv, qkv, qkv, bias)


def _attn_sample_body(q_ref, k_ref, v_ref, c0_ref, c1_ref, c2_ref, bias_ref, o_ref, *, t_valid):
    blk = B_BLK
    scale = B_HEAD_DIM ** -0.5
    caches = (c0_ref, c1_ref, c2_ref)
    o_ref[...] = jnp.zeros_like(o_ref)
    for t in range(t_valid):
        m_run = l_run = acc = None
        for gi, dil in enumerate(B_DILATIONS):
            q = q_ref[t, gi] * scale
            c_ref = caches[gi]
            if dil == 1:
                kcat = jnp.concatenate([c_ref[t:, 0, 0], k_ref[:t + 1, gi]], axis=0)
                vcat = jnp.concatenate([c_ref[t:, 0, 1], v_ref[:t + 1, gi]], axis=0)
            else:
                kcat = jnp.concatenate([c_ref[:, t, 0], k_ref[t:t + 1, gi]], axis=0)
                vcat = jnp.concatenate([c_ref[:, t, 1], v_ref[t:t + 1, gi]], axis=0)
            logits = jnp.sum(q[None] * kcat, axis=-1, keepdims=True) + bias_ref[gi, :blk + 1]
            mx = jnp.max(logits, axis=0)
            p = jnp.exp(logits - mx[None])
            den = jnp.sum(p, axis=0)
            pv = jnp.sum(p * vcat, axis=0)
            if gi == 0:
                m_run, l_run, acc = mx, den, pv
            else:
                m_new = jnp.maximum(m_run, mx)
                c_old, c_new = jnp.exp(m_run - m_new), jnp.exp(mx - m_new)
                acc = acc * c_old + pv * c_new
                l_run = l_run * c_old + den * c_new
                m_run = m_new
        o_ref[t] = (acc / l_run).astype(o_ref.dtype)


def attn_sample(qkv, caches, bias, n_seq, T, row0, t_valid):
    H, Dh, G = B_HEADS, B_HEAD_DIM, len(B_DILATIONS)
    rb0 = row0 // T
    assert t_valid <= min(d for d in B_DILATIONS if d > 1)
    q5 = qkv.reshape(qkv.shape[0], G, 3, H, Dh)

    def spec(which):
        return pl.BlockSpec((T, G, None, H, Dh), lambda b: (b + rb0, 0, which, 0, 0))

    cache_specs = [pl.BlockSpec((None, B_BLK, min(d, t_valid), 2, H, Dh), lambda b: (b, 0, 0, 0, 0, 0))
                   for d in B_DILATIONS]
    return pl.pallas_call(
        functools.partial(_attn_sample_body, t_valid=t_valid), grid=(n_seq,),
        in_specs=[spec(0), spec(1), spec(2)] + cache_specs + [pl.BlockSpec(bias.shape, lambda b: (0, 0, 0, 0))],
        out_specs=pl.BlockSpec((T, H, Dh), lambda b: (b, 0, 0)),
        out_shape=jax.ShapeDtypeStruct((n_seq * T, H, Dh), BF16),
        compiler_params=_params(("parallel",)), name="attn_sample",
    )(q5, q5, q5, *caches, bias)


def _mlstm_body(q_ref, k_ref, v_ref, o_ref, gate_ref, gb_ref, nw_ref, c0_ref, n0_ref, m0_ref,
                y_ref, cT_ref, nT_ref, mT_ref, C_scr, n_scr, m_scr, *, L, TB, t_valid):
    H, E, V = C_HEADS, C_QK_DIM, C_V_DIM
    tb = pl.program_id(1)

    @pl.when(tb == 0)
    def _():
        C_scr[...] = c0_ref[...]
        n_scr[...] = n0_ref[...]
        m_scr[...] = m0_ref[...]

    row = lax.broadcasted_iota(jnp.int32, (L, L), 0)
    col = lax.broadcasted_iota(jnp.int32, (L, L), 1)
    causal = row >= col
    tril = causal.astype(F32)
    gb = gb_ref[...]
    nw = nw_ref[...]
    lane = lax.broadcasted_iota(jnp.int32, (L, LANES), 1)

    def chunk(c, carry):
        sl = pl.ds(pl.multiple_of(c * L, L), L)

        def load(ref, cols=slice(None)):
            if TB >= L:
                return ref[sl, cols]
            x = ref[:, cols]
            return jnp.concatenate([x, jnp.zeros((L - TB, x.shape[1]), F32)], axis=0)

        gact = C_GATE_CAP * jnp.tanh((load(gate_ref) + gb) / C_GATE_CAP)
        lf = jnp.minimum(gact, 0.0) - jnp.log(1.0 + jnp.exp(-jnp.abs(gact)))
        ig = gact
        valid = None
        if t_valid is not None:
            t_idx = tb * TB + c * L + lax.broadcasted_iota(jnp.int32, (L, 1), 0)
            valid = t_idx < t_valid
            ig = jnp.where(valid, ig, -1e30)
            lf = jnp.where(valid, lf, 0.0)
        bcum = _dot(tril, lf, HIGHEST)
        ig_t = ig.T
        bcum_t = bcum.T
        heads = range(H)
        es = [slice(h * E, (h + 1) * E) for h in heads]
        vs = [slice(h * V, (h + 1) * V) for h in heads]
        b_col = [bcum[:, H + h:H + h + 1] for h in heads]
        b_row = [bcum_t[H + h:H + h + 1, :] for h in heads]
        ig_col = [ig[:, h:h + 1] for h in heads]
        ig_row = [ig_t[h:h + 1, :] for h in heads]
        q = [load(q_ref, s) for s in es]
        k = [load(k_ref, s) * (E ** -0.5) for s in es]
        v = [load(v_ref, s) for s in vs]
        if valid is not None:
            q, k, v = ([jnp.where(valid, t, 0.0) for t in ts] for ts in (q, k, v))
        m_prev = [m_scr[h:h + 1, 0:1] for h in heads]
        n_prev = [n_scr[h:h + 1, :] for h in heads]
        C = [C_scr[h] for h in heads]
        dm = [jnp.where(causal, b_col[h] - b_row[h] + ig_row[h], -jnp.inf) for h in heads]
        inter = [b_col[h] + m_prev[h] for h in heads]
        mt = [jnp.maximum(inter[h], jnp.max(dm[h], axis=-1, keepdims=True)) for h in heads]
        w_d = [jnp.exp(dm[h] - mt[h]) for h in heads]
        w_i = [jnp.exp(inter[h] - mt[h]) for h in heads]
        qb, kb, vb = ([t.astype(BF16) for t in ts] for ts in (q, k, v))
        sc = [_dot_nt(qb[h], kb[h]) * w_d[h] for h in heads]
        qc = [_dot_nt(qb[h], C[h].astype(BF16)) for h in heads]
        num = [_dot(sc[h].astype(BF16), vb[h]) + w_i[h] * qc[h] for h in heads]
        den = [jnp.sum(sc[h], axis=-1, keepdims=True) + w_i[h] * jnp.sum(q[h] * n_prev[h], axis=-1, keepdims=True)
               for h in heads]
        hh = [num[h] / jnp.maximum(jnp.abs(den[h]), jnp.exp(-mt[h])) for h in heads]
        m_new = [x[L - 1:L, :] for x in mt]
        b_end = [x[L - 1:L, :] for x in b_col]
        w_s = [jnp.exp(b_end[h] - b_col[h] + ig_col[h] - m_new[h]) for h in heads]
        dec = [jnp.exp(b_end[h] + m_prev[h] - m_new[h]) for h in heads]
        c_upd = [_dot_tn((w_s[h] * v[h]).astype(BF16), kb[h]) for h in heads]
        outs = []
        for h in heads:
            C_scr[h] = dec[h] * C[h] + c_upd[h]
            n_scr[h:h + 1, :] = dec[h] * n_prev[h] + jnp.sum(w_s[h] * k[h], axis=0, keepdims=True)
            m_scr[h:h + 1, :] = jnp.broadcast_to(m_new[h], (1, LANES))
            hn = hh[h] * lax.rsqrt(jnp.mean(hh[h] * hh[h], axis=-1, keepdims=True) + NORM_EPS) * nw[:, vs[h]]
            outs.append(hn * _sigmoid(load(o_ref, vs[h])))
        y_out = jnp.concatenate(outs, axis=-1).astype(y_ref.dtype)
        if TB >= L:
            y_ref[sl, :] = y_out
        else:
            y_ref[...] = y_out[:TB]
        return carry

    lax.fori_loop(0, max(TB // L, 1), chunk, 0)

    @pl.when(tb == pl.num_programs(1) - 1)
    def _():
        cT_ref[...] = C_scr[...]
        nT_ref[...] = n_scr[...]
        mT_ref[...] = m_scr[...]


def mlstm_recurrence(proj, gate_bias, norm_w, c0, n0, m0, n_seq, T, row0, TB, t_valid):
    H, E, V = C_HEADS, C_QK_DIM, C_V_DIM
    nb = T // TB
    rb0 = row0 // TB
    HE, HV = H * E, H * V

    def cols(cb):
        return lambda b, t: (rb0 + b * nb + t, cb)

    st4 = lambda b, t: (b, 0, 0, 0)
    st3 = lambda b, t: (b, 0, 0)
    return pl.pallas_call(
        functools.partial(_mlstm_body, L=CHUNK, TB=TB, t_valid=t_valid), grid=(n_seq, nb),
        in_specs=[pl.BlockSpec((TB, HE), cols(0)), pl.BlockSpec((TB, HE), cols(1)),
                  pl.BlockSpec((TB, HV), cols(2 * HE // HV)), pl.BlockSpec((TB, HV), cols(2 * HE // HV + 1)),
                  pl.BlockSpec((TB, LANES), cols((2 * HE + 2 * HV) // LANES)),
                  pl.BlockSpec((1, LANES), lambda b, t: (0, 0)), pl.BlockSpec((1, HV), lambda b, t: (0, 0)),
                  pl.BlockSpec((None, H, V, E), st4), pl.BlockSpec((None, H, E), st3),
                  pl.BlockSpec((None, H, LANES), st3)],
        out_specs=[pl.BlockSpec((TB, HV), lambda b, t: (b * nb + t, 0)),
                   pl.BlockSpec((None, H, V, E), st4), pl.BlockSpec((None, H, E), st3),
                   pl.BlockSpec((None, H, LANES), st3)],
        out_shape=[jax.ShapeDtypeStruct((n_seq * T, HV), _row_dtype(TB)), jax.ShapeDtypeStruct((n_seq, H, V, E), F32),
                   jax.ShapeDtypeStruct((n_seq, H, E), F32), jax.ShapeDtypeStruct((n_seq, H, LANES), F32)],
        scratch_shapes=[pltpu.VMEM((H, V, E), F32), pltpu.VMEM((H, E), F32), pltpu.VMEM((H, LANES), F32)],
        compiler_params=_params(("parallel", "arbitrary")), name="mlstm_recurrence",
    )(proj, proj, proj, proj, proj, gate_bias, norm_w, c0, n0, m0)


def _row_tile(m, target, mult):
    return max(t for t in range(mult, min(m, target) + 1, mult) if m % t == 0)


def _tile(n, target):
    return max(t for t in range(LANES, min(n, target) + 1, LANES) if n % t == 0)


def _pad_cols(w, n):
    return jnp.pad(w, ((0, 0), (0, n - w.shape[1])))


def _pad_rows(w, n):
    return jnp.pad(w, ((0, n - w.shape[0]), (0, 0)))


def kernel(x_prompt, x_sample, state_a_wkv, state_a_shift, cache_b_kv_g0, cache_b_kv_g1, cache_b_kv_g2, state_c_C, state_c_n, state_c_m, rel_bias, norm_ffn1, ffn1_w_in, ffn1_w_out, norm_mix, norm_ffn2, ffn2_w_in, ffn2_w_out, norm_final, a_mu, a_w_rkv, a_w0, a_w1, a_w2, a_a0, a_a1, a_a2, a_g1, a_g2, a_k_k, a_k_a, a_r_k, a_gn_w, a_gn_b, a_w_out, a_v0, a_v1, a_v2, b_w_qkv, b_w_out, c_w_in, c_b_gates, c_norm_w, c_w_out):
    Bp, Tp, D = x_prompt.shape
    Bs, Ts, _ = x_sample.shape
    depth = norm_mix.shape[0]
    Tsp = SAMPLE_PAD
    Mp, Ms = Bp * Tp, Bs * Tsp
    M = Mp + Ms
    TM = _row_tile(M, 768, 16)
    TM_LORA = _row_tile(M, 384, 8)
    TN = _tile(D, 2048)
    TF = _tile(ffn1_w_out.shape[1], 512)
    H_a = D // A_HEAD_DIM
    G, H_b, Dh = len(B_DILATIONS), B_HEADS, B_HEAD_DIM
    bf = lambda w: w.astype(BF16)

    x = jnp.concatenate([x_prompt.reshape(Mp, D),
                         jnp.pad(x_sample, ((0, 0), (0, Tsp - Ts), (0, 0))).reshape(Ms, D)], axis=0)

    def last_rows(t):
        return t[:Mp].reshape(Bp, Tp, -1)[:, -1], t[Mp:].reshape(Bs, Tsp, -1)[:, Ts - 1]

    qi = jnp.arange(B_BLK)[:, None]
    kj = jnp.arange(2 * B_BLK)[None, :]
    step = qi + B_BLK - kj
    step_ok = (step >= 0) & (step <= B_BLK)
    m_desc = B_BLK - jnp.arange(B_BLK + 8)
    bias_p, bias_s = [], []
    buckets = jnp.arange(N_BUCKETS)
    for gi, dil in enumerate(B_DILATIONS):
        tab = rel_bias[:, gi * H_b:(gi + 1) * H_b].astype(F32)
        hot = (_rel_bucket(jnp.clip(step, 0, B_BLK) * dil)[None] == buckets[:, None, None]).astype(F32)
        bp = jnp.einsum("nh,nqk->hqk", tab, hot, precision=HIGHEST)
        bias_p.append(jnp.where(step_ok[None], bp, -jnp.inf))
        bs = tab[_rel_bucket(jnp.maximum(m_desc, 0) * dil)]
        bias_s.append(jnp.broadcast_to(bs[:, :, None], (B_BLK + 8, H_b, Dh)))
    bias_p, bias_s = jnp.stack(bias_p), jnp.stack(bias_s)

    outs_a_wkv, outs_a_shift, outs_c = ([], []), ([], []), ([], [], [], [], [], [])
    outs_b = [([], []) for _ in range(G)]
    v_first = None
    for i in range(depth):
        x = ffn(x, norm_ffn1[i], ffn1_w_in, ffn1_w_out, i, TM, TF)
        kind, j = i % 3, i // 3
        if kind == 0:
            u, u_prev = rmsnorm_shift(x, norm_mix[i], state_a_shift[j], Tp, Mp, Tsp, TM)
            mu = a_mu[j]
            rkv = rwkv_rkv(u, u_prev, mu[jnp.array([0, 2, 3])][:, None, :], bf(a_w_rkv[j]), TM, TN)
            lr = LANES
            w_br = (bf(_pad_cols(a_w1[j], lr)), bf(_pad_rows(a_w2[j], lr)), a_w0[j].reshape(1, D))
            a_br = (bf(_pad_cols(a_a1[j], lr)), bf(_pad_rows(a_a2[j], lr)), a_a0[j].reshape(1, D))
            g_br = (bf(a_g1[j]), bf(a_g2[j]))
            v_br = None
            if j > 0:
                v_br = (bf(_pad_cols(a_v1[j - 1], lr)), bf(_pad_rows(a_v2[j - 1], lr)), a_v0[j - 1].reshape(1, D))
            lora = rwkv_lora(u, u_prev, mu[jnp.array([1, 4, 5, 3])], w_br, a_br, g_br, v_br, TM_LORA)
            lw, a_lr, gate = lora[:3]
            vres = None if j == 0 else (v_first, lora[3])
            if j == 0:
                v_first = rkv
            par = tuple(p.reshape(1, D) for p in (a_k_k[j], a_k_a[j], a_r_k[j], a_gn_w[j], a_gn_b[j]))
            s0p = jnp.zeros((Bp * H_a, A_HEAD_DIM, A_HEAD_DIM), F32)
            s0s = state_a_wkv[j].reshape(Bs * H_a, A_HEAD_DIM, A_HEAD_DIM)
            yp, sp = rwkv_recurrence(rkv, lw, a_lr, gate, vres, par, s0p, Bp, Tp, 0, 256, None)
            ys, ss = rwkv_recurrence(rkv, lw, a_lr, gate, vres, par, s0s, Bs, Tsp, Mp, Tsp, Ts)
            x = matmul((yp, ys), bf(a_w_out[j]), TM, TN, residual=x)
            outs_a_wkv[0].append(sp.reshape(Bp, H_a, A_HEAD_DIM, A_HEAD_DIM))
            outs_a_wkv[1].append(ss.reshape(Bs, H_a, A_HEAD_DIM, A_HEAD_DIM))
            sh_p, sh_s = last_rows(u)
            outs_a_shift[0].append(sh_p)
            outs_a_shift[1].append(sh_s)
        elif kind == 1:
            qkv = matmul(x, bf(b_w_qkv[j]), TM, _tile(b_w_qkv.shape[2], 2048), norm_g=norm_mix[i])
            caches = [c[j].reshape(Bs, B_BLK, d, 2, H_b, Dh)
                      for c, d in zip((cache_b_kv_g0, cache_b_kv_g1, cache_b_kv_g2), B_DILATIONS)]
            op = attn_prompt(qkv, bias_p, Bp, Tp)
            os_ = attn_sample(qkv[Mp:], caches, bias_s, Bs, Tsp, 0, Ts)
            x = matmul((op, os_.reshape(Ms, H_b * Dh)), bf(b_w_out[j]), TM, TN, residual=x)
            for gi in range(G):
                keep = min(B_WINDOWS[gi], Tp)
                c0, c1 = (gi * 3 + 1) * H_b * Dh, (gi * 3 + 3) * H_b * Dh
                kv_p = jnp.stack([lax.slice(qkv, ((b + 1) * Tp - keep, c0), ((b + 1) * Tp, c1)) for b in range(Bp)])
                kv_s = lax.slice(qkv, (Mp, c0), (M, c1)).reshape(Bs, Tsp, c1 - c0)[:, :Ts]
                outs_b[gi][0].append(kv_p.reshape(Bp, keep, 2, H_b, Dh))
                outs_b[gi][1].append(kv_s.reshape(Bs, Ts, 2, H_b, Dh))
        else:
            H, E, V = C_HEADS, C_QK_DIM, C_V_DIM
            n_in = c_w_in.shape[2]
            n_pad = -(-n_in // LANES) * LANES
            proj = matmul(x, bf(_pad_cols(c_w_in[j], n_pad)), TM, _tile(n_pad, 1024), norm_g=norm_mix[i])
            gbias = _pad_cols(c_b_gates[j].reshape(1, 2 * H), LANES)
            nw = c_norm_w[j].reshape(1, H * V)
            zc = (jnp.zeros((Bp, H, V, E), F32), jnp.zeros((Bp, H, E), F32), jnp.zeros((Bp, H, LANES), F32))
            sc = (state_c_C[j], state_c_n[j], jnp.broadcast_to(state_c_m[j][:, :, None], (Bs, H, LANES)))
            hp, cp, np_, mp = mlstm_recurrence(proj, gbias, nw, *zc, Bp, Tp, 0, 256, None)
            hs, cs, ns, ms = mlstm_recurrence(proj, gbias, nw, *sc, Bs, Tsp, Mp, Tsp, Ts)
            x = matmul((hp, hs), bf(c_w_out[j]), TM, TN, residual=x)
            for lst, val in zip(outs_c, (cp, cs, np_, ns, mp[:, :, 0], ms[:, :, 0])):
                lst.append(val)
        x = ffn(x, norm_ffn2[i], ffn2_w_in, ffn2_w_out, i, TM, TF)

    y = rmsnorm(x, norm_final, TM)
    y_prompt = y[:Mp].reshape(Bp, Tp, D)
    y_sample = y[Mp:].reshape(Bs, Tsp, D)[:, :Ts]
    st = jnp.stack
    return (y_prompt, y_sample, st(outs_a_wkv[0]), st(outs_a_wkv[1]), st(outs_a_shift[0]), st(outs_a_shift[1]),
            st(outs_b[0][0]), st(outs_b[0][1]), st(outs_b[1][0]), st(outs_b[1][1]), st(outs_b[2][0]), st(outs_b[2][1]),
            st(outs_c[0]), st(outs_c[1]), st(outs_c[2]), st(outs_c[3]), st(outs_c[4]), st(outs_c[5]))
```

```python
import functools
import math

import jax
import jax.numpy as jnp
from jax import lax
from jax.experimental import pallas as pl
from jax.experimental.pallas import tpu as pltpu

F32 = jnp.float32
BF16 = jnp.bfloat16
HIGHEST = lax.Precision.HIGHEST

NORM_EPS = 1e-6
A_HEAD_DIM = 64
A_GN_EPS = 64e-5
A_REC_LANES = 2048
A_REC_PRECISION = "bf16"
B_WINDOWS = (128, 512, 2048)
B_DILATIONS = (1, 4, 16)
B_HEADS = 16
B_HEAD_DIM = 128
B_BLK = 128
B_UNROLL = 4
N_BUCKETS = 32
BUCKET_MAX_DIST = 2048
C_HEADS = 8
C_QK_DIM = 128
C_V_DIM = 256
C_GATE_CAP = 15.0
CHUNK = 64
SAMPLE_PAD = 8
LANES = 128
VMEM_LIMIT = 56 * 1024 * 1024


def _row_dtype(rows):
    return BF16 if rows % 16 == 0 else F32


def _params(sem):
    return pltpu.CompilerParams(dimension_semantics=sem, vmem_limit_bytes=VMEM_LIMIT)


def _dot(a, b, precision=None):
    return jnp.dot(a, b, preferred_element_type=F32, precision=precision)


def _dot_nt(a, b, precision=None):
    return lax.dot_general(a, b, (((1,), (1,)), ((), ())), preferred_element_type=F32, precision=precision)


def _dot_tn(a, b, precision=None):
    return lax.dot_general(a, b, (((0,), (0,)), ((), ())), preferred_element_type=F32, precision=precision)


_NN = (((1,), (0,)), ((), ()))
_NT = (((1,), (1,)), ((), ()))
_TN = (((0,), (0,)), ((), ()))


def _pmm(a, b, dims, mode):
    dg = functools.partial(lax.dot_general, dimension_numbers=dims, preferred_element_type=F32)
    if mode == "highest":
        return dg(a, b, precision=HIGHEST)
    a_hi, b_hi = a.astype(BF16), b.astype(BF16)
    if mode == "bf16":
        return dg(a_hi, b_hi)
    a_lo = (a - a_hi.astype(F32)).astype(BF16)
    b_lo = (b - b_hi.astype(F32)).astype(BF16)
    return dg(a_hi, b_hi) + (dg(a_hi, b_lo) + dg(a_lo, b_hi))


def _sigmoid(x):
    return 1.0 / (1.0 + jnp.exp(-x))


def _rms(x, g):
    ms = jnp.mean(x * x, axis=-1, keepdims=True)
    return x * lax.rsqrt(ms + NORM_EPS) * g


def _rmsnorm_body(x_ref, g_ref, o_ref):
    o_ref[...] = _rms(x_ref[...], g_ref[...]).astype(o_ref.dtype)


def rmsnorm(x, g, tm, row0, rows):
    D = x.shape[1]
    rb0 = row0 // tm
    return pl.pallas_call(
        _rmsnorm_body, grid=(rows // tm,),
        in_specs=[pl.BlockSpec((tm, D), lambda i: (rb0 + i, 0)), pl.BlockSpec((1, D), lambda i: (0, 0))],
        out_specs=pl.BlockSpec((tm, D), lambda i: (i, 0)),
        out_shape=jax.ShapeDtypeStruct((rows, D), F32),
        compiler_params=_params(("parallel",)), name="rmsnorm",
    )(x, g.reshape(1, D))


def _rmsnorm_shift_body(x_ref, xp_ref, g_ref, st_ref, u_ref, up_ref, *, tm, grp, seq_len, seq_rows):
    i = pl.program_id(0)
    g = g_ref[...]
    u = _rms(x_ref[...], g)
    u_ref[...] = u
    up_ref[...] = pltpu.roll(u, 1, axis=0)
    tail = _rms(xp_ref[...], g)[-1:, :]
    n_late = st_ref.shape[0]
    for k in range(tm // grp):
        row0 = i * tm + k * grp
        late = row0 >= seq_rows
        is_start = late | (lax.rem(row0, seq_len) == 0)
        state = st_ref[pl.ds(jnp.clip((row0 - seq_rows) // grp, 0, n_late - 1), 1), :]
        before = tail if k == 0 else u[k * grp - 1:k * grp, :]
        up_ref[k * grp:k * grp + 1, :] = jnp.where(is_start, jnp.where(late, state, 0.0), before)


def rmsnorm_shift(x, g, late_states, seq_len, seq_rows, grp, tm):
    M, D = x.shape
    sub = 8
    return pl.pallas_call(
        functools.partial(_rmsnorm_shift_body, tm=tm, grp=grp, seq_len=seq_len, seq_rows=seq_rows), grid=(M // tm,),
        in_specs=[pl.BlockSpec((tm, D), lambda i: (i, 0)),
                  pl.BlockSpec((sub, D), lambda i: (jnp.maximum(i * (tm // sub) - 1, 0), 0)),
                  pl.BlockSpec((1, D), lambda i: (0, 0)),
                  pl.BlockSpec(late_states.shape, lambda i: (0, 0))],
        out_specs=[pl.BlockSpec((tm, D), lambda i: (i, 0))] * 2,
        out_shape=[jax.ShapeDtypeStruct((M, D), F32)] * 2,
        compiler_params=_params(("parallel",)), name="rmsnorm_shift",
    )(x, x, g.reshape(1, D), late_states)


def _ffn_body(x_ref, g_ref, wg_ref, wu_ref, wo_ref, o_ref, xn_ref):
    j = pl.program_id(1)

    @pl.when(j == 0)
    def _():
        xn_ref[...] = _rms(x_ref[...], g_ref[...]).astype(BF16)
        o_ref[...] = jnp.zeros_like(o_ref)

    xn = xn_ref[...]
    gate = _dot(xn, wg_ref[...].astype(BF16))
    up = _dot(xn, wu_ref[...].astype(BF16))
    h = (gate * _sigmoid(gate) * up).astype(BF16)
    o_ref[...] += _dot(h, wo_ref[...].astype(BF16))

    @pl.when(j == pl.num_programs(1) - 1)
    def _():
        o_ref[...] = x_ref[...] + 0.5 * o_ref[...]


def ffn(x, g, w_in, w_out, layer, tm, tf):
    M, D = x.shape
    Fh = w_out.shape[1]
    nf = Fh // tf
    return pl.pallas_call(
        _ffn_body, grid=(M // tm, nf),
        in_specs=[pl.BlockSpec((tm, D), lambda i, j: (i, 0), pipeline_mode=pl.Buffered(1)),
                  pl.BlockSpec((1, D), lambda i, j: (0, 0)),
                  pl.BlockSpec((None, D, tf), lambda i, j: (layer, 0, j)),
                  pl.BlockSpec((None, D, tf), lambda i, j: (layer, 0, j + nf)),
                  pl.BlockSpec((None, tf, D), lambda i, j: (layer, j, 0))],
        out_specs=pl.BlockSpec((tm, D), lambda i, j: (i, 0)),
        out_shape=jax.ShapeDtypeStruct((M, D), F32),
        scratch_shapes=[pltpu.VMEM((tm, D), BF16)],
        compiler_params=_params(("parallel", "arbitrary")), name="ffn",
    )(x, g.reshape(1, D), w_in, w_in, w_out)


def _mm_body(*refs, nb0, has_norm, has_res):
    it = iter(refs)
    x_ref = next(it)
    x1_ref = next(it) if nb0 is not None else None
    g_ref = next(it) if has_norm else None
    w_ref = next(it)
    res_ref = next(it) if has_res else None
    o_ref = next(it)
    xs_ref = next(it)

    def stage(ref, rows=slice(None), dst=slice(None)):
        x = ref[rows, :].astype(F32)
        if has_norm:
            x = _rms(x, g_ref[...])
        xs_ref[dst, :] = x.astype(BF16)

    @pl.when(pl.program_id(1) == 0)
    def _():
        if nb0 is None:
            stage(x_ref)
        else:
            rem = xs_ref.shape[0] - x1_ref.shape[0]
            pl.when(pl.program_id(0) < nb0)(lambda: stage(x_ref))

            @pl.when(pl.program_id(0) == nb0)
            def _():
                if rem > 0:
                    stage(x_ref, slice(0, rem), slice(0, rem))
                stage(x1_ref, slice(None), slice(rem, None))

    acc = _dot(xs_ref[...], w_ref[...])
    if has_res:
        acc = res_ref[...] + acc
    o_ref[...] = acc.astype(o_ref.dtype)


def matmul(x, w, tm, tn, norm_g=None, residual=None, out_dtype=F32):
    nb0 = None
    if isinstance(x, tuple):
        x0, x1 = x
        nb0 = x0.shape[0] // tm
        M, K = x0.shape[0] + x1.shape[0], x0.shape[1]
        assert M == (nb0 + 1) * tm and x1.shape[0] <= tm and x1.shape[0] % 16 == 0
        args = [x0, x1]
        last0 = -(-x0.shape[0] // tm) - 1
        specs = [pl.BlockSpec((tm, K), lambda i, j: (jnp.minimum(i, last0), 0)),
                 pl.BlockSpec(x1.shape, lambda i, j: (0, 0))]
    else:
        M, K = x.shape
        args = [x]
        specs = [pl.BlockSpec((tm, K), lambda i, j: (i, 0))]
    N = w.shape[1]
    if norm_g is not None:
        args.append(norm_g.reshape(1, K))
        specs.append(pl.BlockSpec((1, K), lambda i, j: (0, 0)))
    args.append(w)
    specs.append(pl.BlockSpec((K, tn), lambda i, j: (0, j)))
    if residual is not None:
        args.append(residual)
        specs.append(pl.BlockSpec((tm, tn), lambda i, j: (i, j)))
    return pl.pallas_call(
        functools.partial(_mm_body, nb0=nb0, has_norm=norm_g is not None, has_res=residual is not None),
        grid=(M // tm, N // tn), in_specs=specs,
        out_specs=pl.BlockSpec((tm, tn), lambda i, j: (i, j)),
        out_shape=jax.ShapeDtypeStruct((M, N), out_dtype),
        scratch_shapes=[pltpu.VMEM((tm, K), BF16)],
        compiler_params=_params(("parallel", "arbitrary")), name="matmul",
    )(*args)


def _rkv_body(u_ref, up_ref, mu_ref, w_ref, o_ref, xs_ref):
    @pl.when(pl.program_id(2) == 0)
    def _():
        u = u_ref[...]
        xs_ref[...] = (u + (up_ref[...] - u) * mu_ref[...]).astype(BF16)

    o_ref[...] = _dot(xs_ref[...], w_ref[...])


def rwkv_rkv(u, u_prev, mu3, w3, tm, tn):
    M, D = u.shape
    return pl.pallas_call(
        _rkv_body, grid=(M // tm, 3, D // tn),
        in_specs=[pl.BlockSpec((tm, D), lambda i, k, j: (i, 0)),
                  pl.BlockSpec((tm, D), lambda i, k, j: (i, 0)),
                  pl.BlockSpec((None, 1, D), lambda i, k, j: (k, 0, 0)),
                  pl.BlockSpec((None, D, tn), lambda i, k, j: (k, 0, j))],
        out_specs=pl.BlockSpec((None, tm, tn), lambda i, k, j: (k, i, j)),
        out_shape=jax.ShapeDtypeStruct((3, M, D), F32),
        scratch_shapes=[pltpu.VMEM((tm, D), BF16)],
        compiler_params=_params(("parallel", "arbitrary", "arbitrary")), name="rwkv_rkv",
    )(u, u_prev, mu3, w3)


def _lora_body(*refs, has_vres):
    it = iter(refs)
    u_ref, up_ref, mu_ref = next(it), next(it), next(it)
    w1, w2, w0 = next(it), next(it), next(it)
    a1, a2, a0 = next(it), next(it), next(it)
    g1, g2 = next(it), next(it)
    if has_vres:
        v1, v2, v0 = next(it), next(it), next(it)
    lw_ref, a_ref, g_ref = next(it), next(it), next(it)
    nu_ref = next(it) if has_vres else None

    u = u_ref[...]
    du = up_ref[...] - u

    def mix(n):
        return (u + du * mu_ref[n:n + 1, :]).astype(BF16)

    hw = jnp.tanh(_dot(mix(0), w1[...])).astype(BF16)
    w_pre = w0[...] + _dot(hw, w2[...])
    softplus = jnp.maximum(-w_pre, 0.0) + jnp.log(1.0 + jnp.exp(-jnp.abs(w_pre)))
    lw_ref[...] = -jnp.exp(-softplus - 0.5)
    ha = _dot(mix(1), a1[...]).astype(BF16)
    a_ref[...] = _sigmoid(a0[...] + _dot(ha, a2[...]))
    hg = _sigmoid(_dot(mix(2), g1[...])).astype(BF16)
    g_ref[...] = _dot(hg, g2[...])
    if has_vres:
        hv = _dot(mix(3), v1[...]).astype(BF16)
        nu_ref[...] = _sigmoid(v0[...] + _dot(hv, v2[...]))


def rwkv_lora(u, u_prev, mu4, w, a, g, v, tm):
    M, D = u.shape
    has_vres = v is not None
    row = lambda i: (i, 0)
    full = lambda i: (0, 0)
    args = [u, u_prev, mu4]
    specs = [pl.BlockSpec((tm, D), row), pl.BlockSpec((tm, D), row), pl.BlockSpec(mu4.shape, full)]
    for t in (w, a, g) + ((v,) if has_vres else ()):
        for m in t:
            args.append(m)
            specs.append(pl.BlockSpec(m.shape, full))
    n_out = 4 if has_vres else 3
    return pl.pallas_call(
        functools.partial(_lora_body, has_vres=has_vres), grid=(M // tm,), in_specs=specs,
        out_specs=[pl.BlockSpec((tm, D), row)] * n_out,
        out_shape=[jax.ShapeDtypeStruct((M, D), F32)] * n_out,
        compiler_params=_params(("parallel",)), name="rwkv_lora",
    )(*args)


def _rwkv_rec_body(*refs, L, TB, hb, t_valid, has_vres, prec):
    N = A_HEAD_DIM
    it = iter(refs)
    r_ref, k_ref, v_ref, lw_ref, a_ref, g_ref = (next(it) for _ in range(6))
    if has_vres:
        vf_ref, nu_ref = next(it), next(it)
    kk_ref, ka_ref, rk_ref, gnw_ref, gnb_ref, s0_ref = (next(it) for _ in range(6))
    y_ref, sT_ref, S_scr = next(it), next(it), next(it)
    tb = pl.program_id(2)

    assert L == N and 2 * N == LANES

    @pl.when(tb == 0)
    def _():
        for p in range(hb // 2):
            S_scr[p] = jnp.concatenate([s0_ref[2 * p], s0_ref[2 * p + 1]], axis=1)

    row = lax.broadcasted_iota(jnp.int32, (L, L), 0)
    col = lax.broadcasted_iota(jnp.int32, (L, L), 1)
    tril = (row >= col).astype(F32)
    row1 = lax.broadcasted_iota(jnp.int32, (L, 2 * N), 0)
    lane1 = lax.broadcasted_iota(jnp.int32, (L, 2 * N), 1)
    head0_lane = lane1 < N
    eye = ((lane1 & (N - 1)) == row1).astype(F32)
    row2 = lax.broadcasted_iota(jnp.int32, (2 * L, 2 * N), 0)
    lane2 = lax.broadcasted_iota(jnp.int32, (2 * L, 2 * N), 1)
    mask2 = (lane2 & (N - 1)) < jnp.where(row2 < L, row2, row2 - L + 1)
    bd_mask = (row2 // L) == (lane2 // N)
    ones_bd = bd_mask.astype(BF16)
    n_sq = int(math.log2(L)) - 1
    kk_p, ka_p, rk_p, gnw, gnb = kk_ref[...], ka_ref[...], rk_ref[...], gnw_ref[...], gnb_ref[...]

    def chunk(c, carry):
        sl = pl.ds(pl.multiple_of(c * L, L), L)

        def load(ref):
            if TB >= L:
                return ref[sl, :]
            return jnp.concatenate([ref[...], jnp.zeros((L - TB, ref.shape[1]), F32)], axis=0)

        r, k, v, lw, a, g = (load(ref) for ref in (r_ref, k_ref, v_ref, lw_ref, a_ref, g_ref))
        if has_vres:
            v = v + (load(vf_ref) - v) * load(nu_ref)
        if t_valid is not None:
            t_idx = tb * TB + c * L + lax.broadcasted_iota(jnp.int32, (L, 1), 0)
            valid = t_idx < t_valid
            r, k, v, lw = (jnp.where(valid, t, 0.0) for t in (r, k, v, lw))
        cum = _dot(tril, lw, HIGHEST)
        cum_end = cum[L - 1:L, :]
        w_cur, w_prev, w_inv, w_rem, w_end = (jnp.exp(cum), jnp.exp(cum - lw), jnp.exp(-cum),
                                              jnp.exp(cum_end - cum), jnp.exp(cum_end))
        mm, mm_nt, mm_tn = (functools.partial(_pmm, dims=d, mode=prec) for d in (_NN, _NT, _TN))
        pairs = range(hb // 2)
        ps = [slice(p * LANES, (p + 1) * LANES) for p in pairs]

        def head_sum(x):
            x_hi = x.astype(BF16)
            x_lo = (x - x_hi.astype(F32)).astype(BF16)
            return _dot(x_hi, ones_bd) + _dot(x_lo, ones_bd)

        def bdiag(x):
            return jnp.where(bd_mask, jnp.concatenate([x, x], axis=0), 0.0)

        kkp = [k[:, s] * kk_p[:, s] for s in ps]
        kk = [x / jnp.maximum(jnp.sqrt(head_sum(x * x)), 1e-12) for x in kkp]
        b = [kk[p] * a[:, ps[p]] for p in pairs]
        k2 = [k[:, s] * (1.0 + (a[:, s] - 1.0) * ka_p[:, s]) for s in ps]
        lhs2 = [jnp.concatenate([kk[p] * w_prev[:, ps[p]], r[:, ps[p]] * w_cur[:, ps[p]]], axis=0) for p in pairs]
        kd = [k2[p] * w_inv[:, ps[p]] for p in pairs]
        bd = [b[p] * w_inv[:, ps[p]] for p in pairs]
        kend = [k2[p] * w_rem[:, ps[p]] for p in pairs]
        bend = [b[p] * w_rem[:, ps[p]] for p in pairs]
        kkd = [x[:L] for x in lhs2]
        rd = [x[L:] for x in lhs2]
        a_kb2 = [mm_nt(lhs2[p], jnp.concatenate([bdiag(kd[p]), bdiag(bd[p])], axis=0)) for p in pairs]
        a_k = [jnp.where(mask2, x[:, :LANES], 0.0) for x in a_kb2]
        a_b = [jnp.where(mask2, x[:, LANES:], 0.0) for x in a_kb2]
        a_kb = [x[:L] for x in a_b]
        a_rb = [x[L:] for x in a_b]
        a_v = [mm(a_k[p], bdiag(v[:, ps[p]])) for p in pairs]
        t_inv = [eye - x for x in a_kb]
        pw = [mm(x, bdiag(x)) for x in a_kb]
        for _ in range(n_sq - 1):
            both = [mm(jnp.concatenate([pw[p], t_inv[p]], axis=0), bdiag(pw[p])) for p in pairs]
            t_inv = [t_inv[p] + both[p][L:] for p in pairs]
            pw = [x[:L] for x in both]
        t_inv = [t_inv[p] + mm(t_inv[p], bdiag(pw[p])) for p in pairs]
        ktcu = [mm(t_inv[p], jnp.concatenate([bdiag(kkd[p]), bdiag(a_v[p][:L])], axis=1)) for p in pairs]
        k_t = [x[:, :LANES] for x in ktcu]
        c_u = [x[:, LANES:] for x in ktcu]
        ykc = [mm(a_rb[p], jnp.concatenate([bdiag(k_t[p]), bdiag(c_u[p])], axis=1)) for p in pairs]
        y_k = [rd[p] - ykc[p][:, :LANES] for p in pairs]
        y_c = [a_v[p][L:] - ykc[p][:, LANES:] for p in pairs]
        S = [S_scr[p] for p in pairs]
        y = [mm_nt(y_k[p], bdiag(S[p])) + y_c[p] for p in pairs]
        upd = [mm_tn(jnp.concatenate([jnp.concatenate([v[:, ps[p]], jnp.zeros_like(k_t[p])], axis=1),
                                      jnp.concatenate([-c_u[p], k_t[p]], axis=1)], axis=0),
                     jnp.concatenate([kend[p], bend[p]], axis=0)) for p in pairs]
        S_c = [jnp.where(head0_lane, x[:N], x[N:LANES]) for x in upd]
        ktb = [jnp.where(bd_mask, x[LANES:], 0.0) for x in upd]
        for p in pairs:
            S_scr[p] = S[p] * w_end[:, ps[p]] - mm(S[p], ktb[p]) + S_c[p]
        mean = [head_sum(y[p]) * (1.0 / N) for p in pairs]
        var = [head_sum(jnp.square(y[p] - mean[p])) * (1.0 / N) for p in pairs]
        bonus = [head_sum(r[:, ps[p]] * k2[p] * rk_p[:, ps[p]]) * v[:, ps[p]] for p in pairs]
        outs = [((y[p] - mean[p]) * lax.rsqrt(var[p] + A_GN_EPS) * gnw[:, ps[p]] + gnb[:, ps[p]] + bonus[p])
                * g[:, ps[p]] for p in pairs]
        y_out = jnp.concatenate(outs, axis=-1).astype(y_ref.dtype)
        if TB >= L:
            y_ref[sl, :] = y_out
        else:
            y_ref[...] = y_out[:TB]
        return carry

    lax.fori_loop(0, max(TB // L, 1), chunk, 0)

    @pl.when(tb == pl.num_programs(2) - 1)
    def _():
        for p in range(hb // 2):
            S = S_scr[p]
            sT_ref[2 * p] = S[:, :N]
            sT_ref[2 * p + 1] = S[:, N:]


def rwkv_recurrence(rkv, lw, a, g, vres, params, s0, n_seq, T, row0, TB, t_valid):
    _, M, D = rkv.shape
    N = A_HEAD_DIM
    LW = min(A_REC_LANES, D)
    hb = LW // N
    nb = T // TB
    rb0 = row0 // TB
    has_vres = vres is not None
    seq = lambda b, h, t: (rb0 + b * nb + t, h)
    args, specs = [], []
    for n in range(3):
        args.append(rkv)
        specs.append(pl.BlockSpec((None, TB, LW), lambda b, h, t, n=n: (n, rb0 + b * nb + t, h)))
    for x in (lw, a, g):
        args.append(x)
        specs.append(pl.BlockSpec((TB, LW), seq))
    if has_vres:
        args += list(vres)
        specs += [pl.BlockSpec((None, TB, LW), lambda b, h, t: (2, rb0 + b * nb + t, h)), pl.BlockSpec((TB, LW), seq)]
    for p in params:
        args.append(p)
        specs.append(pl.BlockSpec((1, LW), lambda b, h, t: (0, h)))
    args.append(s0)
    specs.append(pl.BlockSpec((hb, N, N), lambda b, h, t: (b * (D // LW) + h, 0, 0)))
    return pl.pallas_call(
        functools.partial(_rwkv_rec_body, L=CHUNK, TB=TB, hb=hb, t_valid=t_valid, has_vres=has_vres,
                          prec=A_REC_PRECISION),
        grid=(n_seq, D // LW, nb), in_specs=specs,
        out_specs=[pl.BlockSpec((TB, LW), lambda b, h, t: (b * nb + t, h)),
                   pl.BlockSpec((hb, N, N), lambda b, h, t: (b * (D // LW) + h, 0, 0))],
        out_shape=[jax.ShapeDtypeStruct((n_seq * T, D), _row_dtype(TB)),
                   jax.ShapeDtypeStruct((n_seq * (D // N), N, N), F32)],
        scratch_shapes=[pltpu.VMEM((hb // 2, N, 2 * N), F32)],
        compiler_params=_params(("parallel", "parallel", "arbitrary")), name="rwkv_recurrence",
    )(*args)


def _rel_bucket(dist):
    exact = N_BUCKETS // 2
    d = jnp.maximum(dist, 1).astype(F32)
    log_b = exact + (jnp.log(d / exact) / math.log(BUCKET_MAX_DIST / exact) * (N_BUCKETS - exact)).astype(jnp.int32)
    return jnp.where(dist < exact, dist, jnp.minimum(log_b, N_BUCKETS - 1))


def _attn_prompt_body(q_ref, k_ref, v_ref, bias_ref, o_ref, acc_ref, m_ref, l_ref, kd_ref, vd_ref, *, T):
    step = pl.program_id(2)
    blk = B_BLK
    G = len(B_DILATIONS)
    scale = B_HEAD_DIM ** -0.5
    first_keys = lax.broadcasted_iota(jnp.int32, (blk, 2 * blk), 1) < blk
    ones_cols = jnp.ones((2 * blk, B_HEAD_DIM), BF16)

    for si, dil in enumerate(reversed(B_DILATIONS)):
        @pl.when(step == si)
        def _(gi=si, dil=dil):
            span = blk * dil
            res_rows = T // dil + blk
            bias = bias_ref[...]

            def where(idx):
                n = idx // dil
                r = idx - n * dil
                return n, n * span + r, pl.multiple_of(r * res_rows + n * blk, blk)

            for r in range(dil):
                kd_ref[r * res_rows:r * res_rows + blk, :] = jnp.zeros((blk, B_HEAD_DIM), BF16)
                vd_ref[r * res_rows:r * res_rows + blk, :] = jnp.zeros((blk, B_HEAD_DIM), BF16)

            def stage(it, carry):
                for u in range(B_UNROLL):
                    _, start, dst = where(it * B_UNROLL + u)
                    rows = pl.ds(start, blk, stride=dil)
                    kd_ref[pl.ds(dst + blk, blk), :] = k_ref[rows, :].astype(BF16)
                    vd_ref[pl.ds(dst + blk, blk), :] = v_ref[rows, :].astype(BF16)
                return carry

            lax.fori_loop(0, T // (blk * B_UNROLL), stage, 0)

            def blocks(it, carry):
                us = range(B_UNROLL)
                pos = [where(it * B_UNROLL + u) for u in us]
                n = [x[0] for x in pos]
                cur = [pl.ds(x[1], blk, stride=dil) for x in pos]
                q = [(q_ref[cur[u], :] * scale).astype(BF16) for u in us]
                kcat = [kd_ref[pl.ds(x[2], 2 * blk), :] for x in pos]
                vcat = [vd_ref[pl.ds(x[2], 2 * blk), :] for x in pos]
                logits = [_dot_nt(q[u], kcat[u]) + bias for u in us]
                logits = [jnp.where(first_keys & (n[u] == 0), -jnp.inf, logits[u]) for u in us]
                mx = [jnp.max(x, axis=-1, keepdims=True) for x in logits]
                p = [jnp.exp(logits[u] - mx[u]) for u in us]
                pvd = [_dot(p[u].astype(BF16), jnp.concatenate([vcat[u], ones_cols], axis=1)) for u in us]
                pv = [x[:, :B_HEAD_DIM] for x in pvd]
                den = [x[:, B_HEAD_DIM:] for x in pvd]
                if gi > 0:
                    m_old = [m_ref[cur[u], :] for u in us]
                    l_old = [l_ref[cur[u], :] for u in us]
                    acc_old = [acc_ref[cur[u], :] for u in us]
                    m_new = [jnp.maximum(m_old[u], mx[u]) for u in us]
                    c_old = [jnp.exp(m_old[u] - m_new[u]) for u in us]
                    c_new = [jnp.exp(mx[u] - m_new[u]) for u in us]
                    pv = [acc_old[u] * c_old[u] + pv[u] * c_new[u] for u in us]
                    den = [l_old[u] * c_old[u] + den[u] * c_new[u] for u in us]
                    mx = m_new
                for u in us:
                    acc_ref[cur[u], :] = pv[u]
                    m_ref[cur[u], :] = mx[u]
                    l_ref[cur[u], :] = den[u]
                return carry

            lax.fori_loop(0, T // (blk * B_UNROLL), blocks, 0)

    @pl.when(step == G - 1)
    def _():
        o_ref[...] = (acc_ref[...] / l_ref[...]).astype(o_ref.dtype)


def attn_prompt(qkv, bias, n_seq, T):
    H, Dh, G = B_HEADS, B_HEAD_DIM, len(B_DILATIONS)

    def col(which):
        return lambda b, h, s: (b, ((G - 1 - s) * 3 + which) * H + h)

    staged_rows = T + B_BLK * max(B_DILATIONS)
    return pl.pallas_call(
        functools.partial(_attn_prompt_body, T=T), grid=(n_seq, H, G),
        in_specs=[pl.BlockSpec((T, Dh), col(0)), pl.BlockSpec((T, Dh), col(1)), pl.BlockSpec((T, Dh), col(2)),
                  pl.BlockSpec((None, None, B_BLK, 2 * B_BLK), lambda b, h, s: (G - 1 - s, h, 0, 0))],
        out_specs=pl.BlockSpec((T, Dh), lambda b, h, s: (b, h)),
        out_shape=jax.ShapeDtypeStruct((n_seq * T, H * Dh), BF16),
        scratch_shapes=[pltpu.VMEM((T, Dh), F32), pltpu.VMEM((T, 1), F32), pltpu.VMEM((T, Dh), F32),
                        pltpu.VMEM((staged_rows, Dh), BF16), pltpu.VMEM((staged_rows, Dh), BF16)],
        compiler_params=_params(("parallel", "parallel", "arbitrary")), name="attn_prompt",
    )(qkv, qkv, qkv, bias)


def _attn_sample_body(q_ref, k_ref, v_ref, c0_ref, c1_ref, c2_ref, bias_ref, o_ref, *, t_valid):
    blk = B_BLK
    scale = B_HEAD_DIM ** -0.5
    caches = (c0_ref, c1_ref, c2_ref)
    o_ref[...] = jnp.zeros_like(o_ref)
    for t in range(t_valid):
        m_run = l_run = acc = None
        for gi, dil in enumerate(B_DILATIONS):
            q = q_ref[t, gi] * scale
            c_ref = caches[gi]
            if dil == 1:
                kcat = jnp.concatenate([c_ref[t:, 0, 0], k_ref[:t + 1, gi]], axis=0)
                vcat = jnp.concatenate([c_ref[t:, 0, 1], v_ref[:t + 1, gi]], axis=0)
            else:
                kcat = jnp.concatenate([c_ref[:, t, 0], k_ref[t:t + 1, gi]], axis=0)
                vcat = jnp.concatenate([c_ref[:, t, 1], v_ref[t:t + 1, gi]], axis=0)
            logits = jnp.sum(q[None] * kcat, axis=-1, keepdims=True) + bias_ref[gi, :blk + 1]
            mx = jnp.max(logits, axis=0)
            p = jnp.exp(logits - mx[None])
            den = jnp.sum(p, axis=0)
            pv = jnp.sum(p * vcat, axis=0)
            if gi == 0:
                m_run, l_run, acc = mx, den, pv
            else:
                m_new = jnp.maximum(m_run, mx)
                c_old, c_new = jnp.exp(m_run - m_new), jnp.exp(mx - m_new)
                acc = acc * c_old + pv * c_new
                l_run = l_run * c_old + den * c_new
                m_run = m_new
        o_ref[t] = (acc / l_run).astype(o_ref.dtype)


def attn_sample(qkv, caches, bias, n_seq, T, row0, t_valid):
    H, Dh, G = B_HEADS, B_HEAD_DIM, len(B_DILATIONS)
    rb0 = row0 // T
    assert t_valid <= min(d for d in B_DILATIONS if d > 1)
    q5 = qkv.reshape(qkv.shape[0], G, 3, H, Dh)

    def spec(which):
        return pl.BlockSpec((T, G, None, H, Dh), lambda b: (b + rb0, 0, which, 0, 0))

    cache_specs = [pl.BlockSpec((None, B_BLK, min(d, t_valid), 2, H, Dh), lambda b: (b, 0, 0, 0, 0, 0))
                   for d in B_DILATIONS]
    return pl.pallas_call(
        functools.partial(_attn_sample_body, t_valid=t_valid), grid=(n_seq,),
        in_specs=[spec(0), spec(1), spec(2)] + cache_specs + [pl.BlockSpec(bias.shape, lambda b: (0, 0, 0, 0))],
        out_specs=pl.BlockSpec((T, H, Dh), lambda b: (b, 0, 0)),
        out_shape=jax.ShapeDtypeStruct((n_seq * T, H, Dh), BF16),
        compiler_params=_params(("parallel",)), name="attn_sample",
    )(q5, q5, q5, *caches, bias)


def _mlstm_body(q_ref, k_ref, v_ref, o_ref, gate_ref, gb_ref, nw_ref, c0_ref, n0_ref, m0_ref,
                y_ref, cT_ref, nT_ref, mT_ref, C_scr, n_scr, m_scr, *, L, TB, t_valid):
    H, E, V = C_HEADS, C_QK_DIM, C_V_DIM
    tb = pl.program_id(1)

    @pl.when(tb == 0)
    def _():
        C_scr[...] = c0_ref[...]
        n_scr[...] = n0_ref[...]
        m_scr[...] = m0_ref[...]

    row = lax.broadcasted_iota(jnp.int32, (L, L), 0)
    col = lax.broadcasted_iota(jnp.int32, (L, L), 1)
    causal = row >= col
    tril = causal.astype(F32)
    gb = gb_ref[...]
    nw = nw_ref[...]
    lane = lax.broadcasted_iota(jnp.int32, (L, LANES), 1)

    def chunk(c, carry):
        sl = pl.ds(pl.multiple_of(c * L, L), L)

        def load(ref, cols=slice(None)):
            if TB >= L:
                return ref[sl, cols]
            x = ref[:, cols]
            return jnp.concatenate([x, jnp.zeros((L - TB, x.shape[1]), F32)], axis=0)

        gact = C_GATE_CAP * jnp.tanh((load(gate_ref) + gb) / C_GATE_CAP)
        lf = jnp.minimum(gact, 0.0) - jnp.log(1.0 + jnp.exp(-jnp.abs(gact)))
        ig = gact
        valid = None
        if t_valid is not None:
            t_idx = tb * TB + c * L + lax.broadcasted_iota(jnp.int32, (L, 1), 0)
            valid = t_idx < t_valid
            ig = jnp.where(valid, ig, -1e30)
            lf = jnp.where(valid, lf, 0.0)
        bcum = _dot(tril, lf, HIGHEST)
        ig_t = ig.T
        bcum_t = bcum.T
        heads = range(H)
        es = [slice(h * E, (h + 1) * E) for h in heads]
        vs = [slice(h * V, (h + 1) * V) for h in heads]
        b_col = [bcum[:, H + h:H + h + 1] for h in heads]
        b_row = [bcum_t[H + h:H + h + 1, :] for h in heads]
        ig_col = [ig[:, h:h + 1] for h in heads]
        ig_row = [ig_t[h:h + 1, :] for h in heads]
        q = [load(q_ref, s) for s in es]
        k = [load(k_ref, s) * (E ** -0.5) for s in es]
        v = [load(v_ref, s) for s in vs]
        if valid is not None:
            q, k, v = ([jnp.where(valid, t, 0.0) for t in ts] for ts in (q, k, v))
        m_prev = [m_scr[h:h + 1, 0:1] for h in heads]
        n_prev = [n_scr[h:h + 1, :] for h in heads]
        C = [C_scr[h] for h in heads]
        dm = [jnp.where(causal, b_col[h] - b_row[h] + ig_row[h], -jnp.inf) for h in heads]
        inter = [b_col[h] + m_prev[h] for h in heads]
        mt = [jnp.maximum(inter[h], jnp.max(dm[h], axis=-1, keepdims=True)) for h in heads]
        w_d = [jnp.exp(dm[h] - mt[h]) for h in heads]
        w_i = [jnp.exp(inter[h] - mt[h]) for h in heads]
        qb, kb, vb = ([t.astype(BF16) for t in ts] for ts in (q, k, v))
        sc = [_dot_nt(qb[h], kb[h]) * w_d[h] for h in heads]
        qc = [_dot_nt(qb[h], C[h].astype(BF16)) for h in heads]
        num = [_dot(sc[h].astype(BF16), vb[h]) + w_i[h] * qc[h] for h in heads]
        den = [jnp.sum(sc[h], axis=-1, keepdims=True) + w_i[h] * jnp.sum(q[h] * n_prev[h], axis=-1, keepdims=True)
               for h in heads]
        hh = [num[h] / jnp.maximum(jnp.abs(den[h]), jnp.exp(-mt[h])) for h in heads]
        m_new = [x[L - 1:L, :] for x in mt]
        b_end = [x[L - 1:L, :] for x in b_col]
        w_s = [jnp.exp(b_end[h] - b_col[h] + ig_col[h] - m_new[h]) for h in heads]
        dec = [jnp.exp(b_end[h] + m_prev[h] - m_new[h]) for h in heads]
        c_upd = [_dot_tn((w_s[h] * v[h]).astype(BF16), kb[h]) for h in heads]
        outs = []
        for h in heads:
            C_scr[h] = dec[h] * C[h] + c_upd[h]
            n_scr[h:h + 1, :] = dec[h] * n_prev[h] + jnp.sum(w_s[h] * k[h], axis=0, keepdims=True)
            m_scr[h:h + 1, :] = jnp.broadcast_to(m_new[h], (1, LANES))
            hn = hh[h] * lax.rsqrt(jnp.mean(hh[h] * hh[h], axis=-1, keepdims=True) + NORM_EPS) * nw[:, vs[h]]
            outs.append(hn * _sigmoid(load(o_ref, vs[h])))
        y_out = jnp.concatenate(outs, axis=-1).astype(y_ref.dtype)
        if TB >= L:
            y_ref[sl, :] = y_out
        else:
            y_ref[...] = y_out[:TB]
        return carry

    lax.fori_loop(0, max(TB // L, 1), chunk, 0)

    @pl.when(tb == pl.num_programs(1) - 1)
    def _():
        cT_ref[...] = C_scr[...]
        nT_ref[...] = n_scr[...]
        mT_ref[...] = m_scr[...]


def mlstm_recurrence(proj, gate_bias, norm_w, c0, n0, m0, n_seq, T, row0, TB, t_valid):
    H, E, V = C_HEADS, C_QK_DIM, C_V_DIM
    nb = T // TB
    rb0 = row0 // TB
    HE, HV = H * E, H * V

    def cols(cb):
        return lambda b, t: (rb0 + b * nb + t, cb)

    st4 = lambda b, t: (b, 0, 0, 0)
    st3 = lambda b, t: (b, 0, 0)
    return pl.pallas_call(
        functools.partial(_mlstm_body, L=CHUNK, TB=TB, t_valid=t_valid), grid=(n_seq, nb),
        in_specs=[pl.BlockSpec((TB, HE), cols(0)), pl.BlockSpec((TB, HE), cols(1)),
                  pl.BlockSpec((TB, HV), cols(2 * HE // HV)), pl.BlockSpec((TB, HV), cols(2 * HE // HV + 1)),
                  pl.BlockSpec((TB, LANES), cols((2 * HE + 2 * HV) // LANES)),
                  pl.BlockSpec((1, LANES), lambda b, t: (0, 0)), pl.BlockSpec((1, HV), lambda b, t: (0, 0)),
                  pl.BlockSpec((None, H, V, E), st4), pl.BlockSpec((None, H, E), st3),
                  pl.BlockSpec((None, H, LANES), st3)],
        out_specs=[pl.BlockSpec((TB, HV), lambda b, t: (b * nb + t, 0)),
                   pl.BlockSpec((None, H, V, E), st4), pl.BlockSpec((None, H, E), st3),
                   pl.BlockSpec((None, H, LANES), st3)],
        out_shape=[jax.ShapeDtypeStruct((n_seq * T, HV), _row_dtype(TB)), jax.ShapeDtypeStruct((n_seq, H, V, E), F32),
                   jax.ShapeDtypeStruct((n_seq, H, E), F32), jax.ShapeDtypeStruct((n_seq, H, LANES), F32)],
        scratch_shapes=[pltpu.VMEM((H, V, E), F32), pltpu.VMEM((H, E), F32), pltpu.VMEM((H, LANES), F32)],
        compiler_params=_params(("parallel", "arbitrary")), name="mlstm_recurrence",
    )(proj, proj, proj, proj, proj, gate_bias, norm_w, c0, n0, m0)


def _row_tile(m, target, mult):
    return max(t for t in range(mult, min(m, target) + 1, mult) if m % t == 0)


def _tile(n, target):
    return max(t for t in range(LANES, min(n, target) + 1, LANES) if n % t == 0)


def _pad_cols(w, n):
    return jnp.pad(w, ((0, 0), (0, n - w.shape[1])))


def _pad_rows(w, n):
    return jnp.pad(w, ((0, n - w.shape[0]), (0, 0)))


def kernel(x_prompt, x_sample, state_a_wkv, state_a_shift, cache_b_kv_g0, cache_b_kv_g1, cache_b_kv_g2, state_c_C, state_c_n, state_c_m, rel_bias, norm_ffn1, ffn1_w_in, ffn1_w_out, norm_mix, norm_ffn2, ffn2_w_in, ffn2_w_out, norm_final, a_mu, a_w_rkv, a_w0, a_w1, a_w2, a_a0, a_a1, a_a2, a_g1, a_g2, a_k_k, a_k_a, a_r_k, a_gn_w, a_gn_b, a_w_out, a_v0, a_v1, a_v2, b_w_qkv, b_w_out, c_w_in, c_b_gates, c_norm_w, c_w_out):
    Bp, Tp, D = x_prompt.shape
    Bs, Ts, _ = x_sample.shape
    depth = norm_mix.shape[0]
    Tsp = SAMPLE_PAD
    Mp, Ms = Bp * Tp, Bs * Tsp
    M = Mp + Ms
    TM = _row_tile(M, 768, 16)
    TM_LORA = _row_tile(M, 384, 8)
    TN = _tile(D, 2048)
    TM_FFN = _row_tile(M, 1536, 16)
    TF = _tile(ffn1_w_out.shape[1], 256)
    H_a = D // A_HEAD_DIM
    G, H_b, Dh = len(B_DILATIONS), B_HEADS, B_HEAD_DIM
    bf = lambda w: w.astype(BF16)

    x = jnp.concatenate([x_prompt.reshape(Mp, D),
                         jnp.pad(x_sample, ((0, 0), (0, Tsp - Ts), (0, 0))).reshape(Ms, D)], axis=0)

    def last_rows(t):
        return (jnp.stack([t[(b + 1) * Tp - 1] for b in range(Bp)]),
                jnp.stack([t[Mp + b * Tsp + Ts - 1] for b in range(Bs)]))

    qi = jnp.arange(B_BLK)[:, None]
    kj = jnp.arange(2 * B_BLK)[None, :]
    step = qi + B_BLK - kj
    step_ok = (step >= 0) & (step <= B_BLK)
    m_desc = B_BLK - jnp.arange(B_BLK + 8)
    bias_p, bias_s = [], []
    buckets = jnp.arange(N_BUCKETS)
    for gi, dil in enumerate(B_DILATIONS):
        tab = rel_bias[:, gi * H_b:(gi + 1) * H_b].astype(F32)
        hot = (_rel_bucket(jnp.clip(step, 0, B_BLK) * dil)[None] == buckets[:, None, None]).astype(F32)
        bp = jnp.einsum("nh,nqk->hqk", tab, hot, precision=HIGHEST)
        bias_p.append(jnp.where(step_ok[None], bp, -jnp.inf))
        bs = tab[_rel_bucket(jnp.maximum(m_desc, 0) * dil)]
        bias_s.append(jnp.broadcast_to(bs[:, :, None], (B_BLK + 8, H_b, Dh)))
    bias_p, bias_s = jnp.stack(bias_p), jnp.stack(bias_s)

    outs_a_wkv, outs_a_shift, outs_c = ([], []), ([], []), ([], [], [], [], [], [])
    outs_b = [([], []) for _ in range(G)]
    v_first = None
    for i in range(depth):
        x = ffn(x, norm_ffn1[i], ffn1_w_in, ffn1_w_out, i, TM_FFN, TF)
        kind, j = i % 3, i // 3
        if kind == 0:
            u, u_prev = rmsnorm_shift(x, norm_mix[i], state_a_shift[j], Tp, Mp, Tsp, TM)
            mu = a_mu[j]
            rkv = rwkv_rkv(u, u_prev, mu[jnp.array([0, 2, 3])][:, None, :], bf(a_w_rkv[j]), TM, TN)
            lr = LANES
            w_br = (bf(_pad_cols(a_w1[j], lr)), bf(_pad_rows(a_w2[j], lr)), a_w0[j].reshape(1, D))
            a_br = (bf(_pad_cols(a_a1[j], lr)), bf(_pad_rows(a_a2[j], lr)), a_a0[j].reshape(1, D))
            g_br = (bf(a_g1[j]), bf(a_g2[j]))
            v_br = None
            if j > 0:
                v_br = (bf(_pad_cols(a_v1[j - 1], lr)), bf(_pad_rows(a_v2[j - 1], lr)), a_v0[j - 1].reshape(1, D))
            lora = rwkv_lora(u, u_prev, mu[jnp.array([1, 4, 5, 3])], w_br, a_br, g_br, v_br, TM_LORA)
            lw, a_lr, gate = lora[:3]
            vres = None if j == 0 else (v_first, lora[3])
            if j == 0:
                v_first = rkv
            par = tuple(p.reshape(1, D) for p in (a_k_k[j], a_k_a[j], a_r_k[j], a_gn_w[j], a_gn_b[j]))
            s0p = jnp.zeros((Bp * H_a, A_HEAD_DIM, A_HEAD_DIM), F32)
            s0s = state_a_wkv[j].reshape(Bs * H_a, A_HEAD_DIM, A_HEAD_DIM)
            yp, sp = rwkv_recurrence(rkv, lw, a_lr, gate, vres, par, s0p, Bp, Tp, 0, 128, None)
            ys, ss = rwkv_recurrence(rkv, lw, a_lr, gate, vres, par, s0s, Bs, Tsp, Mp, Tsp, Ts)
            x = matmul((yp, ys), bf(a_w_out[j]), TM, TN, residual=x)
            outs_a_wkv[0].append(sp.reshape(Bp, H_a, A_HEAD_DIM, A_HEAD_DIM))
            outs_a_wkv[1].append(ss.reshape(Bs, H_a, A_HEAD_DIM, A_HEAD_DIM))
            sh_p, sh_s = last_rows(u)
            outs_a_shift[0].append(sh_p)
            outs_a_shift[1].append(sh_s)
        elif kind == 1:
            qkv = matmul(x, bf(b_w_qkv[j]), TM, _tile(b_w_qkv.shape[2], 2048), norm_g=norm_mix[i])
            caches = [c[j].reshape(Bs, B_BLK, d, 2, H_b, Dh)
                      for c, d in zip((cache_b_kv_g0, cache_b_kv_g1, cache_b_kv_g2), B_DILATIONS)]
            op = attn_prompt(qkv, bias_p, Bp, Tp)
            os_ = attn_sample(qkv[Mp:], caches, bias_s, Bs, Tsp, 0, Ts)
            x = matmul((op, os_.reshape(Ms, H_b * Dh)), bf(b_w_out[j]), TM, TN, residual=x)
            for gi in range(G):
                keep = min(B_WINDOWS[gi], Tp)
                c0, c1 = (gi * 3 + 1) * H_b * Dh, (gi * 3 + 3) * H_b * Dh
                kv_p = jnp.stack([lax.slice(qkv, ((b + 1) * Tp - keep, c0), ((b + 1) * Tp, c1)) for b in range(Bp)])
                kv_s = lax.slice(qkv, (Mp, c0), (M, c1)).reshape(Bs, Tsp, c1 - c0)[:, :Ts]
                outs_b[gi][0].append(kv_p.reshape(Bp, keep, 2, H_b, Dh))
                outs_b[gi][1].append(kv_s.reshape(Bs, Ts, 2, H_b, Dh))
        else:
            H, E, V = C_HEADS, C_QK_DIM, C_V_DIM
            n_in = c_w_in.shape[2]
            n_pad = -(-n_in // LANES) * LANES
            proj = matmul(x, bf(_pad_cols(c_w_in[j], n_pad)), TM, _tile(n_pad, 1024), norm_g=norm_mix[i])
            gbias = _pad_cols(c_b_gates[j].reshape(1, 2 * H), LANES)
            nw = c_norm_w[j].reshape(1, H * V)
            zc = (jnp.zeros((Bp, H, V, E), F32), jnp.zeros((Bp, H, E), F32), jnp.zeros((Bp, H, LANES), F32))
            sc = (state_c_C[j], state_c_n[j], jnp.broadcast_to(state_c_m[j][:, :, None], (Bs, H, LANES)))
            hp, cp, np_, mp = mlstm_recurrence(proj, gbias, nw, *zc, Bp, Tp, 0, 256, None)
            hs, cs, ns, ms = mlstm_recurrence(proj, gbias, nw, *sc, Bs, Tsp, Mp, Tsp, Ts)
            x = matmul((hp, hs), bf(c_w_out[j]), TM, TN, residual=x)
            for lst, val in zip(outs_c, (cp, cs, np_, ns, mp[:, :, 0], ms[:, :, 0])):
                lst.append(val)
        x = ffn(x, norm_ffn2[i], ffn2_w_in, ffn2_w_out, i, TM_FFN, TF)

    y_prompt = rmsnorm(x, norm_final, _row_tile(Mp, 512, 8), 0, Mp).reshape(Bp, Tp, D)
    y_sample = rmsnorm(x, norm_final, Ms, Mp, Ms).reshape(Bs, Tsp, D)[:, :Ts]
    st = jnp.stack
    return (y_prompt, y_sample, st(outs_a_wkv[0]), st(outs_a_wkv[1]), st(outs_a_shift[0]), st(outs_a_shift[1]),
            st(outs_b[0][0]), st(outs_b[0][1]), st(outs_b[1][0]), st(outs_b[1][1]), st(outs_b[2][0]), st(outs_b[2][1]),
            st(outs_c[0]), st(outs_c[1]), st(outs_c[2]), st(outs_c[3]), st(outs_c[4]), st(outs_c[5]))
```

```python
import functools
import math

import jax
import jax.numpy as jnp
from jax import lax
from jax.experimental import pallas as pl
from jax.experimental.pallas import tpu as pltpu

F32 = jnp.float32
BF16 = jnp.bfloat16
HIGHEST = lax.Precision.HIGHEST

NORM_EPS = 1e-6
A_HEAD_DIM = 64
A_GN_EPS = 64e-5
A_REC_LANES = 2048
A_REC_PRECISION = "bf16"
B_WINDOWS = (128, 512, 2048)
B_DILATIONS = (1, 4, 16)
B_HEADS = 16
B_HEAD_DIM = 128
B_BLK = 128
B_UNROLL = 4
N_BUCKETS = 32
BUCKET_MAX_DIST = 2048
C_HEADS = 8
C_QK_DIM = 128
C_V_DIM = 256
C_GATE_CAP = 15.0
CHUNK = 64
SAMPLE_PAD = 8
LANES = 128
VMEM_LIMIT = 56 * 1024 * 1024


def _row_dtype(rows):
    return BF16 if rows % 16 == 0 else F32


def _params(sem):
    return pltpu.CompilerParams(dimension_semantics=sem, vmem_limit_bytes=VMEM_LIMIT)


def _dot(a, b, precision=None):
    return jnp.dot(a, b, preferred_element_type=F32, precision=precision)


def _dot_nt(a, b, precision=None):
    return lax.dot_general(a, b, (((1,), (1,)), ((), ())), preferred_element_type=F32, precision=precision)


def _dot_tn(a, b, precision=None):
    return lax.dot_general(a, b, (((0,), (0,)), ((), ())), preferred_element_type=F32, precision=precision)


_NN = (((1,), (0,)), ((), ()))
_NT = (((1,), (1,)), ((), ()))
_TN = (((0,), (0,)), ((), ()))


def _pmm(a, b, dims, mode):
    dg = functools.partial(lax.dot_general, dimension_numbers=dims, preferred_element_type=F32)
    if mode == "highest":
        return dg(a, b, precision=HIGHEST)
    a_hi, b_hi = a.astype(BF16), b.astype(BF16)
    if mode == "bf16":
        return dg(a_hi, b_hi)
    a_lo = (a - a_hi.astype(F32)).astype(BF16)
    b_lo = (b - b_hi.astype(F32)).astype(BF16)
    return dg(a_hi, b_hi) + (dg(a_hi, b_lo) + dg(a_lo, b_hi))


def _sigmoid(x):
    return 1.0 / (1.0 + jnp.exp(-x))


def _rms(x, g):
    ms = jnp.mean(x * x, axis=-1, keepdims=True)
    return x * lax.rsqrt(ms + NORM_EPS) * g


def _rmsnorm_body(x_ref, g_ref, o_ref):
    o_ref[...] = _rms(x_ref[...], g_ref[...]).astype(o_ref.dtype)


def rmsnorm(x, g, tm, row0, rows):
    D = x.shape[1]
    rb0 = row0 // tm
    return pl.pallas_call(
        _rmsnorm_body, grid=(rows // tm,),
        in_specs=[pl.BlockSpec((tm, D), lambda i: (rb0 + i, 0)), pl.BlockSpec((1, D), lambda i: (0, 0))],
        out_specs=pl.BlockSpec((tm, D), lambda i: (i, 0)),
        out_shape=jax.ShapeDtypeStruct((rows, D), F32),
        compiler_params=_params(("parallel",)), name="rmsnorm",
    )(x, g.reshape(1, D))


def _rmsnorm_shift_body(x_ref, xp_ref, g_ref, st_ref, u_ref, up_ref, *, tm, grp, seq_len, seq_rows):
    i = pl.program_id(0)
    g = g_ref[...]
    u = _rms(x_ref[...], g)
    u_ref[...] = u
    up_ref[...] = pltpu.roll(u, 1, axis=0)
    tail = _rms(xp_ref[...], g)[-1:, :]
    n_late = st_ref.shape[0]
    for k in range(tm // grp):
        row0 = i * tm + k * grp
        late = row0 >= seq_rows
        is_start = late | (lax.rem(row0, seq_len) == 0)
        state = st_ref[pl.ds(jnp.clip((row0 - seq_rows) // grp, 0, n_late - 1), 1), :]
        before = tail if k == 0 else u[k * grp - 1:k * grp, :]
        up_ref[k * grp:k * grp + 1, :] = jnp.where(is_start, jnp.where(late, state, 0.0), before)


def rmsnorm_shift(x, g, late_states, seq_len, seq_rows, grp, tm):
    M, D = x.shape
    sub = 8
    return pl.pallas_call(
        functools.partial(_rmsnorm_shift_body, tm=tm, grp=grp, seq_len=seq_len, seq_rows=seq_rows), grid=(M // tm,),
        in_specs=[pl.BlockSpec((tm, D), lambda i: (i, 0)),
                  pl.BlockSpec((sub, D), lambda i: (jnp.maximum(i * (tm // sub) - 1, 0), 0)),
                  pl.BlockSpec((1, D), lambda i: (0, 0)),
                  pl.BlockSpec(late_states.shape, lambda i: (0, 0))],
        out_specs=[pl.BlockSpec((tm, D), lambda i: (i, 0))] * 2,
        out_shape=[jax.ShapeDtypeStruct((M, D), F32)] * 2,
        compiler_params=_params(("parallel",)), name="rmsnorm_shift",
    )(x, x, g.reshape(1, D), late_states)


def _ffn_body(x_ref, g_ref, wg_ref, wu_ref, wo_ref, o_ref, xn_ref):
    j = pl.program_id(1)

    @pl.when(j == 0)
    def _():
        xn_ref[...] = _rms(x_ref[...], g_ref[...]).astype(BF16)
        o_ref[...] = jnp.zeros_like(o_ref)

    xn = xn_ref[...]
    gate = _dot(xn, wg_ref[...].astype(BF16))
    up = _dot(xn, wu_ref[...].astype(BF16))
    h = (gate * _sigmoid(gate) * up).astype(BF16)
    o_ref[...] += _dot(h, wo_ref[...].astype(BF16))

    @pl.when(j == pl.num_programs(1) - 1)
    def _():
        o_ref[...] = x_ref[...] + 0.5 * o_ref[...]


def ffn(x, g, w_in, w_out, layer, tm, tf):
    M, D = x.shape
    Fh = w_out.shape[1]
    nf = Fh // tf
    return pl.pallas_call(
        _ffn_body, grid=(M // tm, nf),
        in_specs=[pl.BlockSpec((tm, D), lambda i, j: (i, 0)),
                  pl.BlockSpec((1, D), lambda i, j: (0, 0)),
                  pl.BlockSpec((None, D, tf), lambda i, j: (layer, 0, j)),
                  pl.BlockSpec((None, D, tf), lambda i, j: (layer, 0, j + nf)),
                  pl.BlockSpec((None, tf, D), lambda i, j: (layer, j, 0))],
        out_specs=pl.BlockSpec((tm, D), lambda i, j: (i, 0)),
        out_shape=jax.ShapeDtypeStruct((M, D), F32),
        scratch_shapes=[pltpu.VMEM((tm, D), BF16)],
        compiler_params=_params(("parallel", "arbitrary")), name="ffn",
    )(x, g.reshape(1, D), w_in, w_in, w_out)


def _mm_body(*refs, nb0, has_norm, has_res):
    it = iter(refs)
    x_ref = next(it)
    x1_ref = next(it) if nb0 is not None else None
    g_ref = next(it) if has_norm else None
    w_ref = next(it)
    res_ref = next(it) if has_res else None
    o_ref = next(it)
    xs_ref = next(it)

    def stage(ref, rows=slice(None), dst=slice(None)):
        x = ref[rows, :].astype(F32)
        if has_norm:
            x = _rms(x, g_ref[...])
        xs_ref[dst, :] = x.astype(BF16)

    @pl.when(pl.program_id(1) == 0)
    def _():
        if nb0 is None:
            stage(x_ref)
        else:
            rem = xs_ref.shape[0] - x1_ref.shape[0]
            pl.when(pl.program_id(0) < nb0)(lambda: stage(x_ref))

            @pl.when(pl.program_id(0) == nb0)
            def _():
                if rem > 0:
                    stage(x_ref, slice(0, rem), slice(0, rem))
                stage(x1_ref, slice(None), slice(rem, None))

    acc = _dot(xs_ref[...], w_ref[...])
    if has_res:
        acc = res_ref[...] + acc
    o_ref[...] = acc.astype(o_ref.dtype)


def matmul(x, w, tm, tn, norm_g=None, residual=None, out_dtype=F32):
    nb0 = None
    if isinstance(x, tuple):
        x0, x1 = x
        nb0 = x0.shape[0] // tm
        M, K = x0.shape[0] + x1.shape[0], x0.shape[1]
        assert M == (nb0 + 1) * tm and x1.shape[0] <= tm and x1.shape[0] % 16 == 0
        args = [x0, x1]
        last0 = -(-x0.shape[0] // tm) - 1
        specs = [pl.BlockSpec((tm, K), lambda i, j: (jnp.minimum(i, last0), 0)),
                 pl.BlockSpec(x1.shape, lambda i, j: (0, 0))]
    else:
        M, K = x.shape
        args = [x]
        specs = [pl.BlockSpec((tm, K), lambda i, j: (i, 0))]
    N = w.shape[1]
    if norm_g is not None:
        args.append(norm_g.reshape(1, K))
        specs.append(pl.BlockSpec((1, K), lambda i, j: (0, 0)))
    args.append(w)
    specs.append(pl.BlockSpec((K, tn), lambda i, j: (0, j)))
    if residual is not None:
        args.append(residual)
        specs.append(pl.BlockSpec((tm, tn), lambda i, j: (i, j)))
    return pl.pallas_call(
        functools.partial(_mm_body, nb0=nb0, has_norm=norm_g is not None, has_res=residual is not None),
        grid=(M // tm, N // tn), in_specs=specs,
        out_specs=pl.BlockSpec((tm, tn), lambda i, j: (i, j)),
        out_shape=jax.ShapeDtypeStruct((M, N), out_dtype),
        scratch_shapes=[pltpu.VMEM((tm, K), BF16)],
        compiler_params=_params(("parallel", "arbitrary")), name="matmul",
    )(*args)


def _rkv_body(u_ref, up_ref, mu_ref, w_ref, o_ref, xs_ref):
    @pl.when(pl.program_id(2) == 0)
    def _():
        u = u_ref[...]
        xs_ref[...] = (u + (up_ref[...] - u) * mu_ref[...]).astype(BF16)

    o_ref[...] = _dot(xs_ref[...], w_ref[...])


def rwkv_rkv(u, u_prev, mu3, w, layer, tm, tn):
    M, D = u.shape
    return pl.pallas_call(
        _rkv_body, grid=(M // tm, 3, D // tn),
        in_specs=[pl.BlockSpec((tm, D), lambda i, k, j: (i, 0)),
                  pl.BlockSpec((tm, D), lambda i, k, j: (i, 0)),
                  pl.BlockSpec((None, 1, D), lambda i, k, j: (k, 0, 0)),
                  pl.BlockSpec((None, None, D, tn), lambda i, k, j: (layer, k, 0, j))],
        out_specs=pl.BlockSpec((None, tm, tn), lambda i, k, j: (k, i, j)),
        out_shape=jax.ShapeDtypeStruct((3, M, D), F32),
        scratch_shapes=[pltpu.VMEM((tm, D), BF16)],
        compiler_params=_params(("parallel", "arbitrary", "arbitrary")), name="rwkv_rkv",
    )(u, u_prev, mu3, w)


def _lora_body(*refs, has_vres):
    it = iter(refs)
    u_ref, up_ref, mu_ref = next(it), next(it), next(it)
    w1, w2, w0 = next(it), next(it), next(it)
    a1, a2, a0 = next(it), next(it), next(it)
    g1, g2 = next(it), next(it)
    if has_vres:
        v1, v2, v0 = next(it), next(it), next(it)
    lw_ref, a_ref, g_ref = next(it), next(it), next(it)
    nu_ref = next(it) if has_vres else None

    u = u_ref[...]
    du = up_ref[...] - u

    def mix(n):
        return (u + du * mu_ref[n:n + 1, :]).astype(BF16)

    hw = jnp.tanh(_dot(mix(0), w1[...])).astype(BF16)
    w_pre = w0[...] + _dot(hw, w2[...])
    softplus = jnp.maximum(-w_pre, 0.0) + jnp.log(1.0 + jnp.exp(-jnp.abs(w_pre)))
    lw_ref[...] = -jnp.exp(-softplus - 0.5)
    ha = _dot(mix(1), a1[...]).astype(BF16)
    a_ref[...] = _sigmoid(a0[...] + _dot(ha, a2[...]))
    hg = _sigmoid(_dot(mix(2), g1[...])).astype(BF16)
    g_ref[...] = _dot(hg, g2[...])
    if has_vres:
        hv = _dot(mix(3), v1[...]).astype(BF16)
        nu_ref[...] = _sigmoid(v0[...] + _dot(hv, v2[...]))


def rwkv_lora(u, u_prev, mu4, w, a, g, v, tm):
    M, D = u.shape
    has_vres = v is not None
    row = lambda i: (i, 0)
    full = lambda i: (0, 0)
    args = [u, u_prev, mu4]
    specs = [pl.BlockSpec((tm, D), row), pl.BlockSpec((tm, D), row), pl.BlockSpec(mu4.shape, full)]
    for t in (w, a, g) + ((v,) if has_vres else ()):
        for m in t:
            args.append(m)
            specs.append(pl.BlockSpec(m.shape, full))
    n_out = 4 if has_vres else 3
    return pl.pallas_call(
        functools.partial(_lora_body, has_vres=has_vres), grid=(M // tm,), in_specs=specs,
        out_specs=[pl.BlockSpec((tm, D), row)] * n_out,
        out_shape=[jax.ShapeDtypeStruct((M, D), F32)] * n_out,
        compiler_params=_params(("parallel",)), name="rwkv_lora",
    )(*args)


def _rwkv_rec_body(*refs, L, TB, hb, t_valid, has_vres, prec):
    N = A_HEAD_DIM
    it = iter(refs)
    r_ref, k_ref, v_ref, lw_ref, a_ref, g_ref = (next(it) for _ in range(6))
    if has_vres:
        vf_ref, nu_ref = next(it), next(it)
    kk_ref, ka_ref, rk_ref, gnw_ref, gnb_ref, s0_ref = (next(it) for _ in range(6))
    y_ref, sT_ref, S_scr = next(it), next(it), next(it)
    tb = pl.program_id(2)

    assert L == N and 2 * N == LANES

    @pl.when(tb == 0)
    def _():
        for p in range(hb // 2):
            S_scr[p] = jnp.concatenate([s0_ref[2 * p], s0_ref[2 * p + 1]], axis=1)

    row = lax.broadcasted_iota(jnp.int32, (L, L), 0)
    col = lax.broadcasted_iota(jnp.int32, (L, L), 1)
    tril = (row >= col).astype(F32)
    row1 = lax.broadcasted_iota(jnp.int32, (L, 2 * N), 0)
    lane1 = lax.broadcasted_iota(jnp.int32, (L, 2 * N), 1)
    head0_lane = lane1 < N
    eye = ((lane1 & (N - 1)) == row1).astype(F32)
    row2 = lax.broadcasted_iota(jnp.int32, (2 * L, 2 * N), 0)
    lane2 = lax.broadcasted_iota(jnp.int32, (2 * L, 2 * N), 1)
    mask2 = (lane2 & (N - 1)) < jnp.where(row2 < L, row2, row2 - L + 1)
    bd_mask = (row2 // L) == (lane2 // N)
    ones_bd = bd_mask.astype(BF16)
    n_sq = int(math.log2(L)) - 1
    kk_p, ka_p, rk_p, gnw, gnb = kk_ref[...], ka_ref[...], rk_ref[...], gnw_ref[...], gnb_ref[...]

    def chunk(c, carry):
        sl = pl.ds(pl.multiple_of(c * L, L), L)

        def load(ref):
            if TB >= L:
                return ref[sl, :]
            return jnp.concatenate([ref[...], jnp.zeros((L - TB, ref.shape[1]), F32)], axis=0)

        r, k, v, lw, a, g = (load(ref) for ref in (r_ref, k_ref, v_ref, lw_ref, a_ref, g_ref))
        if has_vres:
            v = v + (load(vf_ref) - v) * load(nu_ref)
        if t_valid is not None:
            t_idx = tb * TB + c * L + lax.broadcasted_iota(jnp.int32, (L, 1), 0)
            valid = t_idx < t_valid
            r, k, v, lw = (jnp.where(valid, t, 0.0) for t in (r, k, v, lw))
        cum = _dot(tril, lw, HIGHEST)
        cum_end = cum[L - 1:L, :]
        w_cur, w_prev, w_inv, w_rem, w_end = (jnp.exp(cum), jnp.exp(cum - lw), jnp.exp(-cum),
                                              jnp.exp(cum_end - cum), jnp.exp(cum_end))
        mm, mm_nt, mm_tn = (functools.partial(_pmm, dims=d, mode=prec) for d in (_NN, _NT, _TN))
        pairs = range(hb // 2)
        ps = [slice(p * LANES, (p + 1) * LANES) for p in pairs]

        def head_sum(x):
            x_hi = x.astype(BF16)
            x_lo = (x - x_hi.astype(F32)).astype(BF16)
            return _dot(x_hi, ones_bd) + _dot(x_lo, ones_bd)

        def bdiag(x):
            return jnp.where(bd_mask, jnp.concatenate([x, x], axis=0), 0.0)

        kkp = [k[:, s] * kk_p[:, s] for s in ps]
        kk = [x / jnp.maximum(jnp.sqrt(head_sum(x * x)), 1e-12) for x in kkp]
        b = [kk[p] * a[:, ps[p]] for p in pairs]
        k2 = [k[:, s] * (1.0 + (a[:, s] - 1.0) * ka_p[:, s]) for s in ps]
        lhs2 = [jnp.concatenate([kk[p] * w_prev[:, ps[p]], r[:, ps[p]] * w_cur[:, ps[p]]], axis=0) for p in pairs]
        kd = [k2[p] * w_inv[:, ps[p]] for p in pairs]
        bd = [b[p] * w_inv[:, ps[p]] for p in pairs]
        kend = [k2[p] * w_rem[:, ps[p]] for p in pairs]
        bend = [b[p] * w_rem[:, ps[p]] for p in pairs]
        kkd = [x[:L] for x in lhs2]
        rd = [x[L:] for x in lhs2]
        a_kb2 = [mm_nt(lhs2[p], jnp.concatenate([bdiag(kd[p]), bdiag(bd[p])], axis=0)) for p in pairs]
        a_k = [jnp.where(mask2, x[:, :LANES], 0.0) for x in a_kb2]
        a_b = [jnp.where(mask2, x[:, LANES:], 0.0) for x in a_kb2]
        a_kb = [x[:L] for x in a_b]
        a_rb = [x[L:] for x in a_b]
        a_v = [mm(a_k[p], bdiag(v[:, ps[p]])) for p in pairs]
        t_inv = [eye - x for x in a_kb]
        pw = [mm(x, bdiag(x)) for x in a_kb]
        for _ in range(n_sq - 1):
            both = [mm(jnp.concatenate([pw[p], t_inv[p]], axis=0), bdiag(pw[p])) for p in pairs]
            t_inv = [t_inv[p] + both[p][L:] for p in pairs]
            pw = [x[:L] for x in both]
        t_inv = [t_inv[p] + mm(t_inv[p], bdiag(pw[p])) for p in pairs]
        ktcu = [mm(t_inv[p], jnp.concatenate([bdiag(kkd[p]), bdiag(a_v[p][:L])], axis=1)) for p in pairs]
        k_t = [x[:, :LANES] for x in ktcu]
        c_u = [x[:, LANES:] for x in ktcu]
        ykc = [mm(a_rb[p], jnp.concatenate([bdiag(k_t[p]), bdiag(c_u[p])], axis=1)) for p in pairs]
        y_k = [rd[p] - ykc[p][:, :LANES] for p in pairs]
        y_c = [a_v[p][L:] - ykc[p][:, LANES:] for p in pairs]
        S = [S_scr[p] for p in pairs]
        y = [mm_nt(y_k[p], bdiag(S[p])) + y_c[p] for p in pairs]
        upd = [mm_tn(jnp.concatenate([jnp.concatenate([v[:, ps[p]], jnp.zeros_like(k_t[p])], axis=1),
                                      jnp.concatenate([-c_u[p], k_t[p]], axis=1)], axis=0),
                     jnp.concatenate([kend[p], bend[p]], axis=0)) for p in pairs]
        S_c = [jnp.where(head0_lane, x[:N], x[N:LANES]) for x in upd]
        ktb = [jnp.where(bd_mask, x[LANES:], 0.0) for x in upd]
        for p in pairs:
            S_scr[p] = S[p] * w_end[:, ps[p]] - mm(S[p], ktb[p]) + S_c[p]
        mean = [head_sum(y[p]) * (1.0 / N) for p in pairs]
        var = [head_sum(jnp.square(y[p] - mean[p])) * (1.0 / N) for p in pairs]
        bonus = [head_sum(r[:, ps[p]] * k2[p] * rk_p[:, ps[p]]) * v[:, ps[p]] for p in pairs]
        outs = [((y[p] - mean[p]) * lax.rsqrt(var[p] + A_GN_EPS) * gnw[:, ps[p]] + gnb[:, ps[p]] + bonus[p])
                * g[:, ps[p]] for p in pairs]
        y_out = jnp.concatenate(outs, axis=-1).astype(y_ref.dtype)
        if TB >= L:
            y_ref[sl, :] = y_out
        else:
            y_ref[...] = y_out[:TB]
        return carry

    lax.fori_loop(0, max(TB // L, 1), chunk, 0)

    @pl.when(tb == pl.num_programs(2) - 1)
    def _():
        for p in range(hb // 2):
            S = S_scr[p]
            sT_ref[2 * p] = S[:, :N]
            sT_ref[2 * p + 1] = S[:, N:]


def rwkv_recurrence(rkv, lw, a, g, vres, params, s0, n_seq, T, row0, TB, t_valid):
    _, M, D = rkv.shape
    N = A_HEAD_DIM
    LW = min(A_REC_LANES, D)
    hb = LW // N
    nb = T // TB
    rb0 = row0 // TB
    has_vres = vres is not None
    seq = lambda b, h, t: (rb0 + b * nb + t, h)
    args, specs = [], []
    for n in range(3):
        args.append(rkv)
        specs.append(pl.BlockSpec((None, TB, LW), lambda b, h, t, n=n: (n, rb0 + b * nb + t, h)))
    for x in (lw, a, g):
        args.append(x)
        specs.append(pl.BlockSpec((TB, LW), seq))
    if has_vres:
        args += list(vres)
        specs += [pl.BlockSpec((None, TB, LW), lambda b, h, t: (2, rb0 + b * nb + t, h)), pl.BlockSpec((TB, LW), seq)]
    for p in params:
        args.append(p)
        specs.append(pl.BlockSpec((1, LW), lambda b, h, t: (0, h)))
    args.append(s0)
    specs.append(pl.BlockSpec((hb, N, N), lambda b, h, t: (b * (D // LW) + h, 0, 0)))
    return pl.pallas_call(
        functools.partial(_rwkv_rec_body, L=CHUNK, TB=TB, hb=hb, t_valid=t_valid, has_vres=has_vres,
                          prec=A_REC_PRECISION),
        grid=(n_seq, D // LW, nb), in_specs=specs,
        out_specs=[pl.BlockSpec((TB, LW), lambda b, h, t: (b * nb + t, h)),
                   pl.BlockSpec((hb, N, N), lambda b, h, t: (b * (D // LW) + h, 0, 0))],
        out_shape=[jax.ShapeDtypeStruct((n_seq * T, D), _row_dtype(TB)),
                   jax.ShapeDtypeStruct((n_seq * (D // N), N, N), F32)],
        scratch_shapes=[pltpu.VMEM((hb // 2, N, 2 * N), F32)],
        compiler_params=_params(("parallel", "parallel", "arbitrary")), name="rwkv_recurrence",
    )(*args)


def _rel_bucket(dist):
    exact = N_BUCKETS // 2
    d = jnp.maximum(dist, 1).astype(F32)
    log_b = exact + (jnp.log(d / exact) / math.log(BUCKET_MAX_DIST / exact) * (N_BUCKETS - exact)).astype(jnp.int32)
    return jnp.where(dist < exact, dist, jnp.minimum(log_b, N_BUCKETS - 1))


def _attn_prompt_body(q_ref, k_ref, v_ref, bias_ref, o_ref, acc_ref, m_ref, l_ref, kd_ref, vd_ref, *, T):
    step = pl.program_id(2)
    blk = B_BLK
    G = len(B_DILATIONS)
    scale = B_HEAD_DIM ** -0.5
    first_keys = lax.broadcasted_iota(jnp.int32, (blk, 2 * blk), 1) < blk
    ones_cols = jnp.ones((2 * blk, B_HEAD_DIM), BF16)

    for si, dil in enumerate(reversed(B_DILATIONS)):
        @pl.when(step == si)
        def _(gi=si, dil=dil):
            span = blk * dil
            res_rows = T // dil + blk
            bias = bias_ref[...]

            def where(idx):
                n = idx // dil
                r = idx - n * dil
                return n, n * span + r, pl.multiple_of(r * res_rows + n * blk, blk)

            for r in range(dil):
                kd_ref[r * res_rows:r * res_rows + blk, :] = jnp.zeros((blk, B_HEAD_DIM), BF16)
                vd_ref[r * res_rows:r * res_rows + blk, :] = jnp.zeros((blk, B_HEAD_DIM), BF16)

            def stage(it, carry):
                for u in range(B_UNROLL):
                    _, start, dst = where(it * B_UNROLL + u)
                    rows = pl.ds(start, blk, stride=dil)
                    kd_ref[pl.ds(dst + blk, blk), :] = k_ref[rows, :].astype(BF16)
                    vd_ref[pl.ds(dst + blk, blk), :] = v_ref[rows, :].astype(BF16)
                return carry

            lax.fori_loop(0, T // (blk * B_UNROLL), stage, 0)

            def blocks(it, carry):
                us = range(B_UNROLL)
                pos = [where(it * B_UNROLL + u) for u in us]
                n = [x[0] for x in pos]
                cur = [pl.ds(x[1], blk, stride=dil) for x in pos]
                q = [(q_ref[cur[u], :] * scale).astype(BF16) for u in us]
                kcat = [kd_ref[pl.ds(x[2], 2 * blk), :] for x in pos]
                vcat = [vd_ref[pl.ds(x[2], 2 * blk), :] for x in pos]
                logits = [_dot_nt(q[u], kcat[u]) + bias for u in us]
                logits = [jnp.where(first_keys & (n[u] == 0), -jnp.inf, logits[u]) for u in us]
                mx = [jnp.max(x, axis=-1, keepdims=True) for x in logits]
                p = [jnp.exp(logits[u] - mx[u]) for u in us]
                pvd = [_dot(p[u].astype(BF16), jnp.concatenate([vcat[u], ones_cols], axis=1)) for u in us]
                pv = [x[:, :B_HEAD_DIM] for x in pvd]
                den = [x[:, B_HEAD_DIM:] for x in pvd]
                if gi > 0:
                    m_old = [m_ref[cur[u], :] for u in us]
                    l_old = [l_ref[cur[u], :] for u in us]
                    acc_old = [acc_ref[cur[u], :] for u in us]
                    m_new = [jnp.maximum(m_old[u], mx[u]) for u in us]
                    c_old = [jnp.exp(m_old[u] - m_new[u]) for u in us]
                    c_new = [jnp.exp(mx[u] - m_new[u]) for u in us]
                    pv = [acc_old[u] * c_old[u] + pv[u] * c_new[u] for u in us]
                    den = [l_old[u] * c_old[u] + den[u] * c_new[u] for u in us]
                    mx = m_new
                for u in us:
                    acc_ref[cur[u], :] = pv[u]
                    m_ref[cur[u], :] = mx[u]
                    l_ref[cur[u], :] = den[u]
                return carry

            lax.fori_loop(0, T // (blk * B_UNROLL), blocks, 0)

    @pl.when(step == G - 1)
    def _():
        o_ref[...] = (acc_ref[...] / l_ref[...]).astype(o_ref.dtype)


def attn_prompt(qkv, bias, n_seq, T):
    H, Dh, G = B_HEADS, B_HEAD_DIM, len(B_DILATIONS)

    def col(which):
        return lambda b, h, s: (b, ((G - 1 - s) * 3 + which) * H + h)

    staged_rows = T + B_BLK * max(B_DILATIONS)
    return pl.pallas_call(
        functools.partial(_attn_prompt_body, T=T), grid=(n_seq, H, G),
        in_specs=[pl.BlockSpec((T, Dh), col(0)), pl.BlockSpec((T, Dh), col(1)), pl.BlockSpec((T, Dh), col(2)),
                  pl.BlockSpec((None, None, B_BLK, 2 * B_BLK), lambda b, h, s: (G - 1 - s, h, 0, 0))],
        out_specs=pl.BlockSpec((T, Dh), lambda b, h, s: (b, h)),
        out_shape=jax.ShapeDtypeStruct((n_seq * T, H * Dh), BF16),
        scratch_shapes=[pltpu.VMEM((T, Dh), F32), pltpu.VMEM((T, 1), F32), pltpu.VMEM((T, Dh), F32),
                        pltpu.VMEM((staged_rows, Dh), BF16), pltpu.VMEM((staged_rows, Dh), BF16)],
        compiler_params=_params(("parallel", "parallel", "arbitrary")), name="attn_prompt",
    )(qkv, qkv, qkv, bias)


def _attn_sample_body(q_ref, k_ref, v_ref, c0_ref, c1_ref, c2_ref, bias_ref, o_ref, *, t_valid):
    blk = B_BLK
    scale = B_HEAD_DIM ** -0.5
    caches = (c0_ref, c1_ref, c2_ref)
    o_ref[...] = jnp.zeros_like(o_ref)
    for t in range(t_valid):
        m_run = l_run = acc = None
        for gi, dil in enumerate(B_DILATIONS):
            q = q_ref[t, gi] * scale
            c_ref = caches[gi]
            if dil == 1:
                kcat = jnp.concatenate([c_ref[t:, 0, 0], k_ref[:t + 1, gi]], axis=0)
                vcat = jnp.concatenate([c_ref[t:, 0, 1], v_ref[:t + 1, gi]], axis=0)
            else:
                kcat = jnp.concatenate([c_ref[:, t, 0], k_ref[t:t + 1, gi]], axis=0)
                vcat = jnp.concatenate([c_ref[:, t, 1], v_ref[t:t + 1, gi]], axis=0)
            logits = jnp.sum(q[None] * kcat, axis=-1, keepdims=True) + bias_ref[gi, :blk + 1]
            mx = jnp.max(logits, axis=0)
            p = jnp.exp(logits - mx[None])
            den = jnp.sum(p, axis=0)
            pv = jnp.sum(p * vcat, axis=0)
            if gi == 0:
                m_run, l_run, acc = mx, den, pv
            else:
                m_new = jnp.maximum(m_run, mx)
                c_old, c_new = jnp.exp(m_run - m_new), jnp.exp(mx - m_new)
                acc = acc * c_old + pv * c_new
                l_run = l_run * c_old + den * c_new
                m_run = m_new
        o_ref[t] = (acc / l_run).astype(o_ref.dtype)


def attn_sample(qkv, caches, bias, n_seq, T, row0, t_valid):
    H, Dh, G = B_HEADS, B_HEAD_DIM, len(B_DILATIONS)
    rb0 = row0 // T
    assert t_valid <= min(d for d in B_DILATIONS if d > 1)
    q5 = qkv.reshape(qkv.shape[0], G, 3, H, Dh)

    def spec(which):
        return pl.BlockSpec((T, G, None, H, Dh), lambda b: (b + rb0, 0, which, 0, 0))

    cache_specs = [pl.BlockSpec((None, B_BLK, min(d, t_valid), 2, H, Dh), lambda b: (b, 0, 0, 0, 0, 0))
                   for d in B_DILATIONS]
    return pl.pallas_call(
        functools.partial(_attn_sample_body, t_valid=t_valid), grid=(n_seq,),
        in_specs=[spec(0), spec(1), spec(2)] + cache_specs + [pl.BlockSpec(bias.shape, lambda b: (0, 0, 0, 0))],
        out_specs=pl.BlockSpec((T, H, Dh), lambda b: (b, 0, 0)),
        out_shape=jax.ShapeDtypeStruct((n_seq * T, H, Dh), BF16),
        compiler_params=_params(("parallel",)), name="attn_sample",
    )(q5, q5, q5, *caches, bias)


def _mlstm_body(q_ref, k_ref, v_ref, o_ref, gate_ref, gb_ref, nw_ref, c0_ref, n0_ref, m0_ref,
                y_ref, cT_ref, nT_ref, mT_ref, C_scr, n_scr, m_scr, *, L, TB, t_valid):
    H, E, V = C_HEADS, C_QK_DIM, C_V_DIM
    tb = pl.program_id(1)

    @pl.when(tb == 0)
    def _():
        C_scr[...] = c0_ref[...]
        n_scr[...] = n0_ref[...]
        m_scr[...] = m0_ref[...]

    row = lax.broadcasted_iota(jnp.int32, (L, L), 0)
    col = lax.broadcasted_iota(jnp.int32, (L, L), 1)
    causal = row >= col
    tril = causal.astype(F32)
    gb = gb_ref[...]
    nw = nw_ref[...]
    lane = lax.broadcasted_iota(jnp.int32, (L, LANES), 1)

    def chunk(c, carry):
        sl = pl.ds(pl.multiple_of(c * L, L), L)

        def load(ref, cols=slice(None)):
            if TB >= L:
                return ref[sl, cols]
            x = ref[:, cols]
            return jnp.concatenate([x, jnp.zeros((L - TB, x.shape[1]), F32)], axis=0)

        gact = C_GATE_CAP * jnp.tanh((load(gate_ref) + gb) / C_GATE_CAP)
        lf = jnp.minimum(gact, 0.0) - jnp.log(1.0 + jnp.exp(-jnp.abs(gact)))
        ig = gact
        valid = None
        if t_valid is not None:
            t_idx = tb * TB + c * L + lax.broadcasted_iota(jnp.int32, (L, 1), 0)
            valid = t_idx < t_valid
            ig = jnp.where(valid, ig, -1e30)
            lf = jnp.where(valid, lf, 0.0)
        bcum = _dot(tril, lf, HIGHEST)
        ig_t = ig.T
        bcum_t = bcum.T
        heads = range(H)
        es = [slice(h * E, (h + 1) * E) for h in heads]
        vs = [slice(h * V, (h + 1) * V) for h in heads]
        b_col = [bcum[:, H + h:H + h + 1] for h in heads]
        b_row = [bcum_t[H + h:H + h + 1, :] for h in heads]
        ig_col = [ig[:, h:h + 1] for h in heads]
        ig_row = [ig_t[h:h + 1, :] for h in heads]
        q = [load(q_ref, s) for s in es]
        k = [load(k_ref, s) * (E ** -0.5) for s in es]
        v = [load(v_ref, s) for s in vs]
        if valid is not None:
            q, k, v = ([jnp.where(valid, t, 0.0) for t in ts] for ts in (q, k, v))
        m_prev = [m_scr[h:h + 1, 0:1] for h in heads]
        n_prev = [n_scr[h:h + 1, :] for h in heads]
        C = [C_scr[h] for h in heads]
        dm = [jnp.where(causal, b_col[h] - b_row[h] + ig_row[h], -jnp.inf) for h in heads]
        inter = [b_col[h] + m_prev[h] for h in heads]
        mt = [jnp.maximum(inter[h], jnp.max(dm[h], axis=-1, keepdims=True)) for h in heads]
        w_d = [jnp.exp(dm[h] - mt[h]) for h in heads]
        w_i = [jnp.exp(inter[h] - mt[h]) for h in heads]
        qb, kb, vb = ([t.astype(BF16) for t in ts] for ts in (q, k, v))
        sc = [_dot_nt(qb[h], kb[h]) * w_d[h] for h in heads]
        qc = [_dot_nt(qb[h], C[h].astype(BF16)) for h in heads]
        num = [_dot(sc[h].astype(BF16), vb[h]) + w_i[h] * qc[h] for h in heads]
        den = [jnp.sum(sc[h], axis=-1, keepdims=True) + w_i[h] * jnp.sum(q[h] * n_prev[h], axis=-1, keepdims=True)
               for h in heads]
        hh = [num[h] / jnp.maximum(jnp.abs(den[h]), jnp.exp(-mt[h])) for h in heads]
        m_new = [x[L - 1:L, :] for x in mt]
        b_end = [x[L - 1:L, :] for x in b_col]
        w_s = [jnp.exp(b_end[h] - b_col[h] + ig_col[h] - m_new[h]) for h in heads]
        dec = [jnp.exp(b_end[h] + m_prev[h] - m_new[h]) for h in heads]
        c_upd = [_dot_tn((w_s[h] * v[h]).astype(BF16), kb[h]) for h in heads]
        outs = []
        for h in heads:
            C_scr[h] = dec[h] * C[h] + c_upd[h]
            n_scr[h:h + 1, :] = dec[h] * n_prev[h] + jnp.sum(w_s[h] * k[h], axis=0, keepdims=True)
            m_scr[h:h + 1, :] = jnp.broadcast_to(m_new[h], (1, LANES))
            hn = hh[h] * lax.rsqrt(jnp.mean(hh[h] * hh[h], axis=-1, keepdims=True) + NORM_EPS) * nw[:, vs[h]]
            outs.append(hn * _sigmoid(load(o_ref, vs[h])))
        y_out = jnp.concatenate(outs, axis=-1).astype(y_ref.dtype)
        if TB >= L:
            y_ref[sl, :] = y_out
        else:
            y_ref[...] = y_out[:TB]
        return carry

    lax.fori_loop(0, max(TB // L, 1), chunk, 0)

    @pl.when(tb == pl.num_programs(1) - 1)
    def _():
        cT_ref[...] = C_scr[...]
        nT_ref[...] = n_scr[...]
        mT_ref[...] = m_scr[...]


def mlstm_recurrence(proj, gate_bias, norm_w, c0, n0, m0, n_seq, T, row0, TB, t_valid):
    H, E, V = C_HEADS, C_QK_DIM, C_V_DIM
    nb = T // TB
    rb0 = row0 // TB
    HE, HV = H * E, H * V

    def cols(cb):
        return lambda b, t: (rb0 + b * nb + t, cb)

    st4 = lambda b, t: (b, 0, 0, 0)
    st3 = lambda b, t: (b, 0, 0)
    return pl.pallas_call(
        functools.partial(_mlstm_body, L=CHUNK, TB=TB, t_valid=t_valid), grid=(n_seq, nb),
        in_specs=[pl.BlockSpec((TB, HE), cols(0)), pl.BlockSpec((TB, HE), cols(1)),
                  pl.BlockSpec((TB, HV), cols(2 * HE // HV)), pl.BlockSpec((TB, HV), cols(2 * HE // HV + 1)),
                  pl.BlockSpec((TB, LANES), cols((2 * HE + 2 * HV) // LANES)),
                  pl.BlockSpec((1, LANES), lambda b, t: (0, 0)), pl.BlockSpec((1, HV), lambda b, t: (0, 0)),
                  pl.BlockSpec((None, H, V, E), st4), pl.BlockSpec((None, H, E), st3),
                  pl.BlockSpec((None, H, LANES), st3)],
        out_specs=[pl.BlockSpec((TB, HV), lambda b, t: (b * nb + t, 0)),
                   pl.BlockSpec((None, H, V, E), st4), pl.BlockSpec((None, H, E), st3),
                   pl.BlockSpec((None, H, LANES), st3)],
        out_shape=[jax.ShapeDtypeStruct((n_seq * T, HV), _row_dtype(TB)), jax.ShapeDtypeStruct((n_seq, H, V, E), F32),
                   jax.ShapeDtypeStruct((n_seq, H, E), F32), jax.ShapeDtypeStruct((n_seq, H, LANES), F32)],
        scratch_shapes=[pltpu.VMEM((H, V, E), F32), pltpu.VMEM((H, E), F32), pltpu.VMEM((H, LANES), F32)],
        compiler_params=_params(("parallel", "arbitrary")), name="mlstm_recurrence",
    )(proj, proj, proj, proj, proj, gate_bias, norm_w, c0, n0, m0)


def _row_tile(m, target, mult):
    return max(t for t in range(mult, min(m, target) + 1, mult) if m % t == 0)


def _tile(n, target):
    return max(t for t in range(LANES, min(n, target) + 1, LANES) if n % t == 0)


def _pad_cols(w, n):
    return jnp.pad(w, ((0, 0), (0, n - w.shape[1])))


def _pad_rows(w, n):
    return jnp.pad(w, ((0, n - w.shape[0]), (0, 0)))


def kernel(x_prompt, x_sample, state_a_wkv, state_a_shift, cache_b_kv_g0, cache_b_kv_g1, cache_b_kv_g2, state_c_C, state_c_n, state_c_m, rel_bias, norm_ffn1, ffn1_w_in, ffn1_w_out, norm_mix, norm_ffn2, ffn2_w_in, ffn2_w_out, norm_final, a_mu, a_w_rkv, a_w0, a_w1, a_w2, a_a0, a_a1, a_a2, a_g1, a_g2, a_k_k, a_k_a, a_r_k, a_gn_w, a_gn_b, a_w_out, a_v0, a_v1, a_v2, b_w_qkv, b_w_out, c_w_in, c_b_gates, c_norm_w, c_w_out):
    Bp, Tp, D = x_prompt.shape
    Bs, Ts, _ = x_sample.shape
    depth = norm_mix.shape[0]
    Tsp = SAMPLE_PAD
    Mp, Ms = Bp * Tp, Bs * Tsp
    M = Mp + Ms
    TM = _row_tile(M, 768, 16)
    TM_LORA = _row_tile(M, 384, 8)
    TN = _tile(D, 2048)
    TF = _tile(ffn1_w_out.shape[1], 512)
    H_a = D // A_HEAD_DIM
    G, H_b, Dh = len(B_DILATIONS), B_HEADS, B_HEAD_DIM
    bf = lambda w: w.astype(BF16)

    x = jnp.concatenate([x_prompt.reshape(Mp, D),
                         jnp.pad(x_sample, ((0, 0), (0, Tsp - Ts), (0, 0))).reshape(Ms, D)], axis=0)

    def last_rows(t):
        return (jnp.stack([t[(b + 1) * Tp - 1] for b in range(Bp)]),
                jnp.stack([t[Mp + b * Tsp + Ts - 1] for b in range(Bs)]))

    qi = jnp.arange(B_BLK)[:, None]
    kj = jnp.arange(2 * B_BLK)[None, :]
    step = qi + B_BLK - kj
    step_ok = (step >= 0) & (step <= B_BLK)
    m_desc = B_BLK - jnp.arange(B_BLK + 8)
    bias_p, bias_s = [], []
    buckets = jnp.arange(N_BUCKETS)
    for gi, dil in enumerate(B_DILATIONS):
        tab = rel_bias[:, gi * H_b:(gi + 1) * H_b].astype(F32)
        hot = (_rel_bucket(jnp.clip(step, 0, B_BLK) * dil)[None] == buckets[:, None, None]).astype(F32)
        bp = jnp.einsum("nh,nqk->hqk", tab, hot, precision=HIGHEST)
        bias_p.append(jnp.where(step_ok[None], bp, -jnp.inf))
        bs = tab[_rel_bucket(jnp.maximum(m_desc, 0) * dil)]
        bias_s.append(jnp.broadcast_to(bs[:, :, None], (B_BLK + 8, H_b, Dh)))
    bias_p, bias_s = jnp.stack(bias_p), jnp.stack(bias_s)
    w_rkv = bf(a_w_rkv)

    outs_a_wkv, outs_a_shift, outs_c = ([], []), ([], []), ([], [], [], [], [], [])
    outs_b = [([], []) for _ in range(G)]
    v_first = None
    for i in range(depth):
        x = ffn(x, norm_ffn1[i], ffn1_w_in, ffn1_w_out, i, TM, TF)
        kind, j = i % 3, i // 3
        if kind == 0:
            u, u_prev = rmsnorm_shift(x, norm_mix[i], state_a_shift[j], Tp, Mp, Tsp, TM)
            mu = a_mu[j]
            rkv = rwkv_rkv(u, u_prev, mu[jnp.array([0, 2, 3])][:, None, :], w_rkv, j, TM, TN)
            lr = LANES
            w_br = (bf(_pad_cols(a_w1[j], lr)), bf(_pad_rows(a_w2[j], lr)), a_w0[j].reshape(1, D))
            a_br = (bf(_pad_cols(a_a1[j], lr)), bf(_pad_rows(a_a2[j], lr)), a_a0[j].reshape(1, D))
            g_br = (bf(a_g1[j]), bf(a_g2[j]))
            v_br = None
            if j > 0:
                v_br = (bf(_pad_cols(a_v1[j - 1], lr)), bf(_pad_rows(a_v2[j - 1], lr)), a_v0[j - 1].reshape(1, D))
            lora = rwkv_lora(u, u_prev, mu[jnp.array([1, 4, 5, 3])], w_br, a_br, g_br, v_br, TM_LORA)
            lw, a_lr, gate = lora[:3]
            vres = None if j == 0 else (v_first, lora[3])
            if j == 0:
                v_first = rkv
            par = tuple(p.reshape(1, D) for p in (a_k_k[j], a_k_a[j], a_r_k[j], a_gn_w[j], a_gn_b[j]))
            s0p = jnp.zeros((Bp * H_a, A_HEAD_DIM, A_HEAD_DIM), F32)
            s0s = state_a_wkv[j].reshape(Bs * H_a, A_HEAD_DIM, A_HEAD_DIM)
            yp, sp = rwkv_recurrence(rkv, lw, a_lr, gate, vres, par, s0p, Bp, Tp, 0, 128, None)
            ys, ss = rwkv_recurrence(rkv, lw, a_lr, gate, vres, par, s0s, Bs, Tsp, Mp, Tsp, Ts)
            x = matmul((yp, ys), bf(a_w_out[j]), TM, TN, residual=x)
            outs_a_wkv[0].append(sp.reshape(Bp, H_a, A_HEAD_DIM, A_HEAD_DIM))
            outs_a_wkv[1].append(ss.reshape(Bs, H_a, A_HEAD_DIM, A_HEAD_DIM))
            sh_p, sh_s = last_rows(u)
            outs_a_shift[0].append(sh_p)
            outs_a_shift[1].append(sh_s)
        elif kind == 1:
            qkv = matmul(x, bf(b_w_qkv[j]), TM, _tile(b_w_qkv.shape[2], 2048), norm_g=norm_mix[i])
            caches = [c[j].reshape(Bs, B_BLK, d, 2, H_b, Dh)
                      for c, d in zip((cache_b_kv_g0, cache_b_kv_g1, cache_b_kv_g2), B_DILATIONS)]
            op = attn_prompt(qkv, bias_p, Bp, Tp)
            os_ = attn_sample(qkv[Mp:], caches, bias_s, Bs, Tsp, 0, Ts)
            x = matmul((op, os_.reshape(Ms, H_b * Dh)), bf(b_w_out[j]), TM, TN, residual=x)
            for gi in range(G):
                keep = min(B_WINDOWS[gi], Tp)
                c0, c1 = (gi * 3 + 1) * H_b * Dh, (gi * 3 + 3) * H_b * Dh
                kv_p = jnp.stack([lax.slice(qkv, ((b + 1) * Tp - keep, c0), ((b + 1) * Tp, c1)) for b in range(Bp)])
                kv_s = lax.slice(qkv, (Mp, c0), (M, c1)).reshape(Bs, Tsp, c1 - c0)[:, :Ts]
                outs_b[gi][0].append(kv_p.reshape(Bp, keep, 2, H_b, Dh))
                outs_b[gi][1].append(kv_s.reshape(Bs, Ts, 2, H_b, Dh))
        else:
            H, E, V = C_HEADS, C_QK_DIM, C_V_DIM
            n_in = c_w_in.shape[2]
            n_pad = -(-n_in // LANES) * LANES
            proj = matmul(x, bf(_pad_cols(c_w_in[j], n_pad)), TM, _tile(n_pad, 1024), norm_g=norm_mix[i])
            gbias = _pad_cols(c_b_gates[j].reshape(1, 2 * H), LANES)
            nw = c_norm_w[j].reshape(1, H * V)
            zc = (jnp.zeros((Bp, H, V, E), F32), jnp.zeros((Bp, H, E), F32), jnp.zeros((Bp, H, LANES), F32))
            sc = (state_c_C[j], state_c_n[j], jnp.broadcast_to(state_c_m[j][:, :, None], (Bs, H, LANES)))
            hp, cp, np_, mp = mlstm_recurrence(proj, gbias, nw, *zc, Bp, Tp, 0, 256, None)
            hs, cs, ns, ms = mlstm_recurrence(proj, gbias, nw, *sc, Bs, Tsp, Mp, Tsp, Ts)
            x = matmul((hp, hs), bf(c_w_out[j]), TM, TN, residual=x)
            for lst, val in zip(outs_c, (cp, cs, np_, ns, mp[:, :, 0], ms[:, :, 0])):
                lst.append(val)
        x = ffn(x, norm_ffn2[i], ffn2_w_in, ffn2_w_out, i, TM, TF)

    y_prompt = rmsnorm(x, norm_final, _row_tile(Mp, 512, 8), 0, Mp).reshape(Bp, Tp, D)
    y_sample = rmsnorm(x, norm_final, Ms, Mp, Ms).reshape(Bs, Tsp, D)[:, :Ts]
    st = jnp.stack
    return (y_prompt, y_sample, st(outs_a_wkv[0]), st(outs_a_wkv[1]), st(outs_a_shift[0]), st(outs_a_shift[1]),
            st(outs_b[0][0]), st(outs_b[0][1]), st(outs_b[1][0]), st(outs_b[1][1]), st(outs_b[2][0]), st(outs_b[2][1]),
            st(outs_c[0]), st(outs_c[1]), st(outs_c[2]), st(outs_c[3]), st(outs_c[4]), st(outs_c[5]))
```

```python
import functools
import math

import jax
import jax.numpy as jnp
from jax import lax
from jax.experimental import pallas as pl
from jax.experimental.pallas import tpu as pltpu

F32 = jnp.float32
BF16 = jnp.bfloat16
HIGHEST = lax.Precision.HIGHEST

NORM_EPS = 1e-6
A_HEAD_DIM = 64
A_GN_EPS = 64e-5
A_REC_LANES = 2048
B_WINDOWS = (128, 512, 2048)
B_DILATIONS = (1, 4, 16)
B_HEADS = 16
B_HEAD_DIM = 128
B_BLK = 128
B_UNROLL = 4
N_BUCKETS = 32
BUCKET_MAX_DIST = 2048
C_HEADS = 8
C_QK_DIM = 128
C_V_DIM = 256
C_GATE_CAP = 15.0
CHUNK = 64
SAMPLE_PAD = 8
LANES = 128
VMEM_LIMIT = 56 * 1024 * 1024


def _row_dtype(rows):
    return BF16 if rows % 16 == 0 else F32


def _params(sem):
    return pltpu.CompilerParams(dimension_semantics=sem, vmem_limit_bytes=VMEM_LIMIT)


def _dot(a, b, precision=None):
    return jnp.dot(a, b, preferred_element_type=F32, precision=precision)


def _dot_nt(a, b):
    return lax.dot_general(a, b, (((1,), (1,)), ((), ())), preferred_element_type=F32)


def _dot_tn(a, b):
    return lax.dot_general(a, b, (((0,), (0,)), ((), ())), preferred_element_type=F32)


def _bf16_operands(dot):
    return lambda a, b: dot(a.astype(BF16), b.astype(BF16))


def _sigmoid(x):
    return 1.0 / (1.0 + jnp.exp(-x))


def _rms(x, g):
    ms = jnp.mean(x * x, axis=-1, keepdims=True)
    return x * lax.rsqrt(ms + NORM_EPS) * g


def _rmsnorm_body(x_ref, g_ref, o_ref):
    o_ref[...] = _rms(x_ref[...], g_ref[...]).astype(o_ref.dtype)


def rmsnorm(x, g, tm, row0, rows):
    D = x.shape[1]
    rb0 = row0 // tm
    return pl.pallas_call(
        _rmsnorm_body, grid=(rows // tm,),
        in_specs=[pl.BlockSpec((tm, D), lambda i: (rb0 + i, 0)), pl.BlockSpec((1, D), lambda i: (0, 0))],
        out_specs=pl.BlockSpec((tm, D), lambda i: (i, 0)),
        out_shape=jax.ShapeDtypeStruct((rows, D), F32),
        compiler_params=_params(("parallel",)), name="rmsnorm",
    )(x, g.reshape(1, D))


def _rmsnorm_shift_body(x_ref, xp_ref, g_ref, st_ref, u_ref, up_ref, *, tm, grp, seq_len, seq_rows):
    i = pl.program_id(0)
    g = g_ref[...]
    u = _rms(x_ref[...], g)
    u_ref[...] = u
    up_ref[...] = pltpu.roll(u, 1, axis=0)
    tail = _rms(xp_ref[...], g)[-1:, :]
    n_late = st_ref.shape[0]
    for k in range(tm // grp):
        row0 = i * tm + k * grp
        late = row0 >= seq_rows
        is_start = late | (lax.rem(row0, seq_len) == 0)
        state = st_ref[pl.ds(jnp.clip((row0 - seq_rows) // grp, 0, n_late - 1), 1), :]
        before = tail if k == 0 else u[k * grp - 1:k * grp, :]
        up_ref[k * grp:k * grp + 1, :] = jnp.where(is_start, jnp.where(late, state, 0.0), before)


def rmsnorm_shift(x, g, late_states, seq_len, seq_rows, grp, tm):
    M, D = x.shape
    sub = 8
    return pl.pallas_call(
        functools.partial(_rmsnorm_shift_body, tm=tm, grp=grp, seq_len=seq_len, seq_rows=seq_rows), grid=(M // tm,),
        in_specs=[pl.BlockSpec((tm, D), lambda i: (i, 0)),
                  pl.BlockSpec((sub, D), lambda i: (jnp.maximum(i * (tm // sub) - 1, 0), 0)),
                  pl.BlockSpec((1, D), lambda i: (0, 0)),
                  pl.BlockSpec(late_states.shape, lambda i: (0, 0))],
        out_specs=[pl.BlockSpec((tm, D), lambda i: (i, 0))] * 2,
        out_shape=[jax.ShapeDtypeStruct((M, D), F32)] * 2,
        compiler_params=_params(("parallel",)), name="rmsnorm_shift",
    )(x, x, g.reshape(1, D), late_states)


def _ffn_body(x_ref, g_ref, wg_ref, wu_ref, wo_ref, o_ref, xn_ref):
    j = pl.program_id(1)

    @pl.when(j == 0)
    def _():
        xn_ref[...] = _rms(x_ref[...], g_ref[...]).astype(BF16)
        o_ref[...] = jnp.zeros_like(o_ref)

    xn = xn_ref[...]
    gate = _dot(xn, wg_ref[...].astype(BF16))
    up = _dot(xn, wu_ref[...].astype(BF16))
    h = (gate * _sigmoid(gate) * up).astype(BF16)
    o_ref[...] += _dot(h, wo_ref[...].astype(BF16))

    @pl.when(j == pl.num_programs(1) - 1)
    def _():
        o_ref[...] = x_ref[...] + 0.5 * o_ref[...]


def ffn(x, g, w_in, w_out, layer, tm, tf):
    M, D = x.shape
    Fh = w_out.shape[1]
    nf = Fh // tf
    return pl.pallas_call(
        _ffn_body, grid=(M // tm, nf),
        in_specs=[pl.BlockSpec((tm, D), lambda i, j: (i, 0)),
                  pl.BlockSpec((1, D), lambda i, j: (0, 0)),
                  pl.BlockSpec((None, D, tf), lambda i, j: (layer, 0, j)),
                  pl.BlockSpec((None, D, tf), lambda i, j: (layer, 0, j + nf)),
                  pl.BlockSpec((None, tf, D), lambda i, j: (layer, j, 0))],
        out_specs=pl.BlockSpec((tm, D), lambda i, j: (i, 0)),
        out_shape=jax.ShapeDtypeStruct((M, D), F32),
        scratch_shapes=[pltpu.VMEM((tm, D), BF16)],
        compiler_params=_params(("parallel", "arbitrary")), name="ffn",
    )(x, g.reshape(1, D), w_in, w_in, w_out)


def _mm_body(*refs, nb0, has_norm, has_res):
    it = iter(refs)
    x_ref = next(it)
    x1_ref = next(it) if nb0 is not None else None
    g_ref = next(it) if has_norm else None
    w_ref = next(it)
    res_ref = next(it) if has_res else None
    o_ref = next(it)
    xs_ref = next(it)

    def stage(ref, rows=slice(None), dst=slice(None)):
        x = ref[rows, :].astype(F32)
        if has_norm:
            x = _rms(x, g_ref[...])
        xs_ref[dst, :] = x.astype(BF16)

    @pl.when(pl.program_id(1) == 0)
    def _():
        if nb0 is None:
            stage(x_ref)
        else:
            rem = xs_ref.shape[0] - x1_ref.shape[0]
            pl.when(pl.program_id(0) < nb0)(lambda: stage(x_ref))

            @pl.when(pl.program_id(0) == nb0)
            def _():
                if rem > 0:
                    stage(x_ref, slice(0, rem), slice(0, rem))
                stage(x1_ref, slice(None), slice(rem, None))

    acc = _dot(xs_ref[...], w_ref[...])
    if has_res:
        acc = res_ref[...] + acc
    o_ref[...] = acc.astype(o_ref.dtype)


def matmul(x, w, tm, tn, norm_g=None, residual=None, out_dtype=F32):
    nb0 = None
    if isinstance(x, tuple):
        x0, x1 = x
        nb0 = x0.shape[0] // tm
        M, K = x0.shape[0] + x1.shape[0], x0.shape[1]
        assert M == (nb0 + 1) * tm and x1.shape[0] <= tm and x1.shape[0] % 16 == 0
        args = [x0, x1]
        last0 = -(-x0.shape[0] // tm) - 1
        specs = [pl.BlockSpec((tm, K), lambda i, j: (jnp.minimum(i, last0), 0)),
                 pl.BlockSpec(x1.shape, lambda i, j: (0, 0))]
    else:
        M, K = x.shape
        args = [x]
        specs = [pl.BlockSpec((tm, K), lambda i, j: (i, 0))]
    N = w.shape[1]
    if norm_g is not None:
        args.append(norm_g.reshape(1, K))
        specs.append(pl.BlockSpec((1, K), lambda i, j: (0, 0)))
    args.append(w)
    specs.append(pl.BlockSpec((K, tn), lambda i, j: (0, j)))
    if residual is not None:
        args.append(residual)
        specs.append(pl.BlockSpec((tm, tn), lambda i, j: (i, j)))
    return pl.pallas_call(
        functools.partial(_mm_body, nb0=nb0, has_norm=norm_g is not None, has_res=residual is not None),
        grid=(M // tm, N // tn), in_specs=specs,
        out_specs=pl.BlockSpec((tm, tn), lambda i, j: (i, j)),
        out_shape=jax.ShapeDtypeStruct((M, N), out_dtype),
        scratch_shapes=[pltpu.VMEM((tm, K), BF16)],
        compiler_params=_params(("parallel", "arbitrary")), name="matmul",
    )(*args)


def _rkv_body(u_ref, up_ref, mu_ref, w_ref, o_ref, xs_ref):
    @pl.when(pl.program_id(2) == 0)
    def _():
        u = u_ref[...]
        xs_ref[...] = (u + (up_ref[...] - u) * mu_ref[...]).astype(BF16)

    o_ref[...] = _dot(xs_ref[...], w_ref[...])


def rwkv_rkv(u, u_prev, mu3, w, layer, tm, tn):
    M, D = u.shape
    return pl.pallas_call(
        _rkv_body, grid=(M // tm, 3, D // tn),
        in_specs=[pl.BlockSpec((tm, D), lambda i, k, j: (i, 0)),
                  pl.BlockSpec((tm, D), lambda i, k, j: (i, 0)),
                  pl.BlockSpec((None, 1, D), lambda i, k, j: (k, 0, 0)),
                  pl.BlockSpec((None, None, D, tn), lambda i, k, j: (layer, k, 0, j))],
        out_specs=pl.BlockSpec((None, tm, tn), lambda i, k, j: (k, i, j)),
        out_shape=jax.ShapeDtypeStruct((3, M, D), F32),
        scratch_shapes=[pltpu.VMEM((tm, D), BF16)],
        compiler_params=_params(("parallel", "arbitrary", "arbitrary")), name="rwkv_rkv",
    )(u, u_prev, mu3, w)


def _lora_body(*refs, has_vres):
    it = iter(refs)
    u_ref, up_ref, mu_ref = next(it), next(it), next(it)
    w1, w2, w0 = next(it), next(it), next(it)
    a1, a2, a0 = next(it), next(it), next(it)
    g1, g2 = next(it), next(it)
    if has_vres:
        v1, v2, v0 = next(it), next(it), next(it)
    lw_ref, a_ref, g_ref = next(it), next(it), next(it)
    nu_ref = next(it) if has_vres else None

    u = u_ref[...]
    du = up_ref[...] - u

    def mix(n):
        return (u + du * mu_ref[n:n + 1, :]).astype(BF16)

    hw = jnp.tanh(_dot(mix(0), w1[...])).astype(BF16)
    w_pre = w0[...] + _dot(hw, w2[...])
    softplus = jnp.maximum(-w_pre, 0.0) + jnp.log(1.0 + jnp.exp(-jnp.abs(w_pre)))
    lw_ref[...] = -jnp.exp(-softplus - 0.5)
    ha = _dot(mix(1), a1[...]).astype(BF16)
    a_ref[...] = _sigmoid(a0[...] + _dot(ha, a2[...]))
    hg = _sigmoid(_dot(mix(2), g1[...])).astype(BF16)
    g_ref[...] = _dot(hg, g2[...])
    if has_vres:
        hv = _dot(mix(3), v1[...]).astype(BF16)
        nu_ref[...] = _sigmoid(v0[...] + _dot(hv, v2[...]))


def rwkv_lora(u, u_prev, mu4, w, a, g, v, tm):
    M, D = u.shape
    has_vres = v is not None
    row = lambda i: (i, 0)
    full = lambda i: (0, 0)
    args = [u, u_prev, mu4]
    specs = [pl.BlockSpec((tm, D), row), pl.BlockSpec((tm, D), row), pl.BlockSpec(mu4.shape, full)]
    for t in (w, a, g) + ((v,) if has_vres else ()):
        for m in t:
            args.append(m)
            specs.append(pl.BlockSpec(m.shape, full))
    n_out = 4 if has_vres else 3
    return pl.pallas_call(
        functools.partial(_lora_body, has_vres=has_vres), grid=(M // tm,), in_specs=specs,
        out_specs=[pl.BlockSpec((tm, D), row)] * n_out,
        out_shape=[jax.ShapeDtypeStruct((M, D), F32)] * n_out,
        compiler_params=_params(("parallel",)), name="rwkv_lora",
    )(*args)


def _rwkv_rec_body(*refs, L, TB, hb, t_valid, has_vres):
    N = A_HEAD_DIM
    it = iter(refs)
    r_ref, k_ref, v_ref, lw_ref, a_ref, g_ref = (next(it) for _ in range(6))
    if has_vres:
        vf_ref, nu_ref = next(it), next(it)
    kk_ref, ka_ref, rk_ref, gnw_ref, gnb_ref, s0_ref = (next(it) for _ in range(6))
    y_ref, sT_ref, S_scr = next(it), next(it), next(it)
    tb = pl.program_id(2)

    assert L == N and 2 * N == LANES

    @pl.when(tb == 0)
    def _():
        for p in range(hb // 2):
            S_scr[p] = jnp.concatenate([s0_ref[2 * p], s0_ref[2 * p + 1]], axis=1)

    row = lax.broadcasted_iota(jnp.int32, (L, L), 0)
    col = lax.broadcasted_iota(jnp.int32, (L, L), 1)
    tril = (row >= col).astype(F32)
    row1 = lax.broadcasted_iota(jnp.int32, (L, 2 * N), 0)
    lane1 = lax.broadcasted_iota(jnp.int32, (L, 2 * N), 1)
    head0_lane = lane1 < N
    eye = ((lane1 & (N - 1)) == row1).astype(F32)
    row2 = lax.broadcasted_iota(jnp.int32, (2 * L, 2 * N), 0)
    lane2 = lax.broadcasted_iota(jnp.int32, (2 * L, 2 * N), 1)
    mask2 = (lane2 & (N - 1)) < jnp.where(row2 < L, row2, row2 - L + 1)
    bd_mask = (row2 // L) == (lane2 // N)
    ones_bd = bd_mask.astype(BF16)
    ones_bd2 = jnp.concatenate([ones_bd, ones_bd], axis=0)
    n_sq = int(math.log2(L)) - 1
    kk_p, ka_p, rk_p, gnw, gnb = kk_ref[...], ka_ref[...], rk_ref[...], gnw_ref[...], gnb_ref[...]

    def chunk(c, carry):
        sl = pl.ds(pl.multiple_of(c * L, L), L)

        def load(ref):
            if TB >= L:
                return ref[sl, :]
            return jnp.concatenate([ref[...], jnp.zeros((L - TB, ref.shape[1]), F32)], axis=0)

        r, k, v, lw, a, g = (load(ref) for ref in (r_ref, k_ref, v_ref, lw_ref, a_ref, g_ref))
        if has_vres:
            v = v + (load(vf_ref) - v) * load(nu_ref)
        if t_valid is not None:
            t_idx = tb * TB + c * L + lax.broadcasted_iota(jnp.int32, (L, 1), 0)
            valid = t_idx < t_valid
            r, k, v, lw = (jnp.where(valid, t, 0.0) for t in (r, k, v, lw))
        cum = _dot(tril, lw, HIGHEST)
        cum_end = cum[L - 1:L, :]
        w_cur, w_prev, w_inv, w_rem, w_end = (jnp.exp(cum), jnp.exp(cum - lw), jnp.exp(-cum),
                                              jnp.exp(cum_end - cum), jnp.exp(cum_end))
        mm, mm_nt, mm_tn = (_bf16_operands(d) for d in (_dot, _dot_nt, _dot_tn))
        pairs = range(hb // 2)
        ps = [slice(p * LANES, (p + 1) * LANES) for p in pairs]

        def head_sum(x):
            x_hi = x.astype(BF16)
            x_lo = (x - x_hi.astype(F32)).astype(BF16)
            return _dot(jnp.concatenate([x_hi, x_lo], axis=1), ones_bd2)

        def bdiag(x):
            return jnp.where(bd_mask, jnp.concatenate([x, x], axis=0), 0.0)

        kkp = [k[:, s] * kk_p[:, s] for s in ps]
        k2 = [k[:, s] * (1.0 + (a[:, s] - 1.0) * ka_p[:, s]) for s in ps]
        sums = [head_sum(jnp.concatenate([kkp[p] * kkp[p], r[:, ps[p]] * k2[p] * rk_p[:, ps[p]]], axis=0))
                for p in pairs]
        kk = [kkp[p] / jnp.maximum(jnp.sqrt(sums[p][:L]), 1e-12) for p in pairs]
        b = [kk[p] * a[:, ps[p]] for p in pairs]
        lhs2 = [jnp.concatenate([kk[p] * w_prev[:, ps[p]], r[:, ps[p]] * w_cur[:, ps[p]]], axis=0) for p in pairs]
        kd = [k2[p] * w_inv[:, ps[p]] for p in pairs]
        bd = [b[p] * w_inv[:, ps[p]] for p in pairs]
        kend = [k2[p] * w_rem[:, ps[p]] for p in pairs]
        bend = [b[p] * w_rem[:, ps[p]] for p in pairs]
        kkd = [x[:L] for x in lhs2]
        rd = [x[L:] for x in lhs2]
        a_kb2 = [mm_nt(lhs2[p], jnp.concatenate([bdiag(kd[p]), bdiag(bd[p])], axis=0)) for p in pairs]
        a_k = [jnp.where(mask2, x[:, :LANES], 0.0) for x in a_kb2]
        a_b = [jnp.where(mask2, x[:, LANES:], 0.0) for x in a_kb2]
        a_kb = [x[:L] for x in a_b]
        a_rb = [x[L:] for x in a_b]
        a_v = [mm(a_k[p], bdiag(v[:, ps[p]])) for p in pairs]
        t_inv = [eye - x for x in a_kb]
        pw = [mm(x, bdiag(x)) for x in a_kb]
        for _ in range(n_sq - 1):
            both = [mm(jnp.concatenate([pw[p], t_inv[p]], axis=0), bdiag(pw[p])) for p in pairs]
            t_inv = [t_inv[p] + both[p][L:] for p in pairs]
            pw = [x[:L] for x in both]
        t_inv = [t_inv[p] + mm(t_inv[p], bdiag(pw[p])) for p in pairs]
        ktcu = [mm(t_inv[p], jnp.concatenate([bdiag(kkd[p]), bdiag(a_v[p][:L])], axis=1)) for p in pairs]
        k_t = [x[:, :LANES] for x in ktcu]
        c_u = [x[:, LANES:] for x in ktcu]
        ykc = [mm(a_rb[p], jnp.concatenate([bdiag(k_t[p]), bdiag(c_u[p])], axis=1)) for p in pairs]
        y_k = [rd[p] - ykc[p][:, :LANES] for p in pairs]
        y_c = [a_v[p][L:] - ykc[p][:, LANES:] for p in pairs]
        S = [S_scr[p] for p in pairs]
        y = [mm_nt(y_k[p], bdiag(S[p])) + y_c[p] for p in pairs]
        upd = [mm_tn(jnp.concatenate([jnp.concatenate([v[:, ps[p]], jnp.zeros_like(k_t[p])], axis=1),
                                      jnp.concatenate([-c_u[p], k_t[p]], axis=1)], axis=0),
                     jnp.concatenate([kend[p], bend[p]], axis=0)) for p in pairs]
        S_c = [jnp.where(head0_lane, x[:N], x[N:LANES]) for x in upd]
        ktb = [jnp.where(bd_mask, x[LANES:], 0.0) for x in upd]
        for p in pairs:
            S_scr[p] = S[p] * w_end[:, ps[p]] - mm(S[p], ktb[p]) + S_c[p]
        mean = [head_sum(y[p]) * (1.0 / N) for p in pairs]
        var = [head_sum(jnp.square(y[p] - mean[p])) * (1.0 / N) for p in pairs]
        bonus = [sums[p][L:] * v[:, ps[p]] for p in pairs]
        outs = [((y[p] - mean[p]) * lax.rsqrt(var[p] + A_GN_EPS) * gnw[:, ps[p]] + gnb[:, ps[p]] + bonus[p])
                * g[:, ps[p]] for p in pairs]
        y_out = jnp.concatenate(outs, axis=-1).astype(y_ref.dtype)
        if TB >= L:
            y_ref[sl, :] = y_out
        else:
            y_ref[...] = y_out[:TB]
        return carry

    lax.fori_loop(0, max(TB // L, 1), chunk, 0)

    @pl.when(tb == pl.num_programs(2) - 1)
    def _():
        for p in range(hb // 2):
            S = S_scr[p]
            sT_ref[2 * p] = S[:, :N]
            sT_ref[2 * p + 1] = S[:, N:]


def rwkv_recurrence(rkv, lw, a, g, vres, params, s0, n_seq, T, row0, TB, t_valid):
    _, M, D = rkv.shape
    N = A_HEAD_DIM
    LW = min(A_REC_LANES, D)
    hb = LW // N
    nb = T // TB
    rb0 = row0 // TB
    has_vres = vres is not None
    seq = lambda b, h, t: (rb0 + b * nb + t, h)
    args, specs = [], []
    for n in range(3):
        args.append(rkv)
        specs.append(pl.BlockSpec((None, TB, LW), lambda b, h, t, n=n: (n, rb0 + b * nb + t, h)))
    for x in (lw, a, g):
        args.append(x)
        specs.append(pl.BlockSpec((TB, LW), seq))
    if has_vres:
        args += list(vres)
        specs += [pl.BlockSpec((None, TB, LW), lambda b, h, t: (2, rb0 + b * nb + t, h)), pl.BlockSpec((TB, LW), seq)]
    for p in params:
        args.append(p)
        specs.append(pl.BlockSpec((1, LW), lambda b, h, t: (0, h)))
    args.append(s0)
    specs.append(pl.BlockSpec((hb, N, N), lambda b, h, t: (b * (D // LW) + h, 0, 0)))
    return pl.pallas_call(
        functools.partial(_rwkv_rec_body, L=CHUNK, TB=TB, hb=hb, t_valid=t_valid, has_vres=has_vres),
        grid=(n_seq, D // LW, nb), in_specs=specs,
        out_specs=[pl.BlockSpec((TB, LW), lambda b, h, t: (b * nb + t, h)),
                   pl.BlockSpec((hb, N, N), lambda b, h, t: (b * (D // LW) + h, 0, 0))],
        out_shape=[jax.ShapeDtypeStruct((n_seq * T, D), _row_dtype(TB)),
                   jax.ShapeDtypeStruct((n_seq * (D // N), N, N), F32)],
        scratch_shapes=[pltpu.VMEM((hb // 2, N, 2 * N), F32)],
        compiler_params=_params(("parallel", "parallel", "arbitrary")), name="rwkv_recurrence",
    )(*args)


def _rel_bucket(dist):
    exact = N_BUCKETS // 2
    d = jnp.maximum(dist, 1).astype(F32)
    log_b = exact + (jnp.log(d / exact) / math.log(BUCKET_MAX_DIST / exact) * (N_BUCKETS - exact)).astype(jnp.int32)
    return jnp.where(dist < exact, dist, jnp.minimum(log_b, N_BUCKETS - 1))


def _attn_prompt_body(q_ref, k_ref, v_ref, bias_ref, o_ref, acc_ref, m_ref, l_ref, kd_ref, vd_ref, *, T):
    step = pl.program_id(2)
    blk = B_BLK
    G = len(B_DILATIONS)
    scale = B_HEAD_DIM ** -0.5
    first_keys = lax.broadcasted_iota(jnp.int32, (blk, 2 * blk), 1) < blk
    ones_cols = jnp.ones((2 * blk, B_HEAD_DIM), BF16)

    for si, dil in enumerate(reversed(B_DILATIONS)):
        @pl.when(step == si)
        def _(gi=si, dil=dil):
            span = blk * dil
            res_rows = T // dil + blk
            bias = bias_ref[...]

            def where(idx):
                n = idx // dil
                r = idx - n * dil
                return n, n * span + r, pl.multiple_of(r * res_rows + n * blk, blk)

            for r in range(dil):
                kd_ref[r * res_rows:r * res_rows + blk, :] = jnp.zeros((blk, B_HEAD_DIM), BF16)
                vd_ref[r * res_rows:r * res_rows + blk, :] = jnp.zeros((blk, B_HEAD_DIM), BF16)

            def stage(it, carry):
                for u in range(B_UNROLL):
                    _, start, dst = where(it * B_UNROLL + u)
                    rows = pl.ds(start, blk, stride=dil)
                    kd_ref[pl.ds(dst + blk, blk), :] = k_ref[rows, :].astype(BF16)
                    vd_ref[pl.ds(dst + blk, blk), :] = v_ref[rows, :].astype(BF16)
                return carry

            lax.fori_loop(0, T // (blk * B_UNROLL), stage, 0)

            def blocks(it, carry):
                us = range(B_UNROLL)
                pos = [where(it * B_UNROLL + u) for u in us]
                n = [x[0] for x in pos]
                cur = [pl.ds(x[1], blk, stride=dil) for x in pos]
                q = [(q_ref[cur[u], :] * scale).astype(BF16) for u in us]
                kcat = [kd_ref[pl.ds(x[2], 2 * blk), :] for x in pos]
                vcat = [vd_ref[pl.ds(x[2], 2 * blk), :] for x in pos]
                logits = [_dot_nt(q[u], kcat[u]) + bias for u in us]
                logits = [jnp.where(first_keys & (n[u] == 0), -jnp.inf, logits[u]) for u in us]
                mx = [jnp.max(x, axis=-1, keepdims=True) for x in logits]
                p = [jnp.exp(logits[u] - mx[u]) for u in us]
                pvd = [_dot(p[u].astype(BF16), jnp.concatenate([vcat[u], ones_cols], axis=1)) for u in us]
                pv = [x[:, :B_HEAD_DIM] for x in pvd]
                den = [x[:, B_HEAD_DIM:] for x in pvd]
                if gi > 0:
                    m_old = [m_ref[cur[u], :] for u in us]
                    l_old = [l_ref[cur[u], :] for u in us]
                    acc_old = [acc_ref[cur[u], :] for u in us]
                    m_new = [jnp.maximum(m_old[u], mx[u]) for u in us]
                    c_old = [jnp.exp(m_old[u] - m_new[u]) for u in us]
                    c_new = [jnp.exp(mx[u] - m_new[u]) for u in us]
                    pv = [acc_old[u] * c_old[u] + pv[u] * c_new[u] for u in us]
                    den = [l_old[u] * c_old[u] + den[u] * c_new[u] for u in us]
                    mx = m_new
                for u in us:
                    acc_ref[cur[u], :] = pv[u]
                    m_ref[cur[u], :] = mx[u]
                    l_ref[cur[u], :] = den[u]
                return carry

            lax.fori_loop(0, T // (blk * B_UNROLL), blocks, 0)

    @pl.when(step == G - 1)
    def _():
        o_ref[...] = (acc_ref[...] / l_ref[...]).astype(o_ref.dtype)


def attn_prompt(qkv, bias, n_seq, T):
    H, Dh, G = B_HEADS, B_HEAD_DIM, len(B_DILATIONS)

    def col(which):
        return lambda b, h, s: (b, ((G - 1 - s) * 3 + which) * H + h)

    staged_rows = T + B_BLK * max(B_DILATIONS)
    return pl.pallas_call(
        functools.partial(_attn_prompt_body, T=T), grid=(n_seq, H, G),
        in_specs=[pl.BlockSpec((T, Dh), col(0)), pl.BlockSpec((T, Dh), col(1)), pl.BlockSpec((T, Dh), col(2)),
                  pl.BlockSpec((None, None, B_BLK, 2 * B_BLK), lambda b, h, s: (G - 1 - s, h, 0, 0))],
        out_specs=pl.BlockSpec((T, Dh), lambda b, h, s: (b, h)),
        out_shape=jax.ShapeDtypeStruct((n_seq * T, H * Dh), BF16),
        scratch_shapes=[pltpu.VMEM((T, Dh), F32), pltpu.VMEM((T, 1), F32), pltpu.VMEM((T, Dh), F32),
                        pltpu.VMEM((staged_rows, Dh), BF16), pltpu.VMEM((staged_rows, Dh), BF16)],
        compiler_params=_params(("parallel", "parallel", "arbitrary")), name="attn_prompt",
    )(qkv, qkv, qkv, bias)


def _attn_sample_body(q_ref, k_ref, v_ref, c0_ref, c1_ref, c2_ref, bias_ref, o_ref, *, t_valid):
    blk = B_BLK
    scale = B_HEAD_DIM ** -0.5
    caches = (c0_ref, c1_ref, c2_ref)
    o_ref[...] = jnp.zeros_like(o_ref)
    for t in range(t_valid):
        m_run = l_run = acc = None
        for gi, dil in enumerate(B_DILATIONS):
            q = q_ref[t, gi] * scale
            c_ref = caches[gi]
            if dil == 1:
                kcat = jnp.concatenate([c_ref[t:, 0, 0], k_ref[:t + 1, gi]], axis=0)
                vcat = jnp.concatenate([c_ref[t:, 0, 1], v_ref[:t + 1, gi]], axis=0)
            else:
                kcat = jnp.concatenate([c_ref[:, t, 0], k_ref[t:t + 1, gi]], axis=0)
                vcat = jnp.concatenate([c_ref[:, t, 1], v_ref[t:t + 1, gi]], axis=0)
            logits = jnp.sum(q[None] * kcat, axis=-1, keepdims=True) + bias_ref[gi, :blk + 1]
            mx = jnp.max(logits, axis=0)
            p = jnp.exp(logits - mx[None])
            den = jnp.sum(p, axis=0)
            pv = jnp.sum(p * vcat, axis=0)
            if gi == 0:
                m_run, l_run, acc = mx, den, pv
            else:
                m_new = jnp.maximum(m_run, mx)
                c_old, c_new = jnp.exp(m_run - m_new), jnp.exp(mx - m_new)
                acc = acc * c_old + pv * c_new
                l_run = l_run * c_old + den * c_new
                m_run = m_new
        o_ref[t] = (acc / l_run).astype(o_ref.dtype)


def attn_sample(qkv, caches, bias, n_seq, T, row0, t_valid):
    H, Dh, G = B_HEADS, B_HEAD_DIM, len(B_DILATIONS)
    rb0 = row0 // T
    assert t_valid <= min(d for d in B_DILATIONS if d > 1)
    q5 = qkv.reshape(qkv.shape[0], G, 3, H, Dh)

    def spec(which):
        return pl.BlockSpec((T, G, None, H, Dh), lambda b: (b + rb0, 0, which, 0, 0))

    cache_specs = [pl.BlockSpec((None, B_BLK, min(d, t_valid), 2, H, Dh), lambda b: (b, 0, 0, 0, 0, 0))
                   for d in B_DILATIONS]
    return pl.pallas_call(
        functools.partial(_attn_sample_body, t_valid=t_valid), grid=(n_seq,),
        in_specs=[spec(0), spec(1), spec(2)] + cache_specs + [pl.BlockSpec(bias.shape, lambda b: (0, 0, 0, 0))],
        out_specs=pl.BlockSpec((T, H, Dh), lambda b: (b, 0, 0)),
        out_shape=jax.ShapeDtypeStruct((n_seq * T, H, Dh), BF16),
        compiler_params=_params(("parallel",)), name="attn_sample",
    )(q5, q5, q5, *caches, bias)


def _mlstm_body(q_ref, k_ref, v_ref, o_ref, gate_ref, gb_ref, nw_ref, c0_ref, n0_ref, m0_ref,
                y_ref, cT_ref, nT_ref, mT_ref, C_scr, n_scr, m_scr, *, L, TB, t_valid):
    H, E, V = C_HEADS, C_QK_DIM, C_V_DIM
    tb = pl.program_id(1)

    @pl.when(tb == 0)
    def _():
        C_scr[...] = c0_ref[...]
        n_scr[...] = n0_ref[...]
        m_scr[...] = m0_ref[...]

    row = lax.broadcasted_iota(jnp.int32, (L, L), 0)
    col = lax.broadcasted_iota(jnp.int32, (L, L), 1)
    causal = row >= col
    tril = causal.astype(F32)
    gb = gb_ref[...]
    nw = nw_ref[...]

    def chunk(c, carry):
        sl = pl.ds(pl.multiple_of(c * L, L), L)

        def load(ref, cols=slice(None)):
            if TB >= L:
                return ref[sl, cols]
            x = ref[:, cols]
            return jnp.concatenate([x, jnp.zeros((L - TB, x.shape[1]), F32)], axis=0)

        gact = C_GATE_CAP * jnp.tanh((load(gate_ref) + gb) / C_GATE_CAP)
        lf = jnp.minimum(gact, 0.0) - jnp.log(1.0 + jnp.exp(-jnp.abs(gact)))
        ig = gact
        valid = None
        if t_valid is not None:
            t_idx = tb * TB + c * L + lax.broadcasted_iota(jnp.int32, (L, 1), 0)
            valid = t_idx < t_valid
            ig = jnp.where(valid, ig, -1e30)
            lf = jnp.where(valid, lf, 0.0)
        bcum = _dot(tril, lf, HIGHEST)
        ig_t = ig.T
        bcum_t = bcum.T
        heads = range(H)
        es = [slice(h * E, (h + 1) * E) for h in heads]
        vs = [slice(h * V, (h + 1) * V) for h in heads]
        b_col = [bcum[:, H + h:H + h + 1] for h in heads]
        b_row = [bcum_t[H + h:H + h + 1, :] for h in heads]
        ig_col = [ig[:, h:h + 1] for h in heads]
        ig_row = [ig_t[h:h + 1, :] for h in heads]
        q = [load(q_ref, s) for s in es]
        k = [load(k_ref, s) * (E ** -0.5) for s in es]
        v = [load(v_ref, s) for s in vs]
        if valid is not None:
            q, k, v = ([jnp.where(valid, t, 0.0) for t in ts] for ts in (q, k, v))
        m_prev = [m_scr[h:h + 1, 0:1] for h in heads]
        n_prev = [n_scr[h:h + 1, :] for h in heads]
        C = [C_scr[h] for h in heads]
        dm = [jnp.where(causal, b_col[h] - b_row[h] + ig_row[h], -jnp.inf) for h in heads]
        inter = [b_col[h] + m_prev[h] for h in heads]
        mt = [jnp.maximum(inter[h], jnp.max(dm[h], axis=-1, keepdims=True)) for h in heads]
        w_d = [jnp.exp(dm[h] - mt[h]) for h in heads]
        w_i = [jnp.exp(inter[h] - mt[h]) for h in heads]
        qb, kb, vb = ([t.astype(BF16) for t in ts] for ts in (q, k, v))
        sc = [_dot_nt(qb[h], kb[h]) * w_d[h] for h in heads]
        qc = [_dot_nt(qb[h], C[h].astype(BF16)) for h in heads]
        num = [_dot(sc[h].astype(BF16), vb[h]) + w_i[h] * qc[h] for h in heads]
        den = [jnp.sum(sc[h], axis=-1, keepdims=True) + w_i[h] * jnp.sum(q[h] * n_prev[h], axis=-1, keepdims=True)
               for h in heads]
        hh = [num[h] / jnp.maximum(jnp.abs(den[h]), jnp.exp(-mt[h])) for h in heads]
        m_new = [x[L - 1:L, :] for x in mt]
        b_end = [x[L - 1:L, :] for x in b_col]
        w_s = [jnp.exp(b_end[h] - b_col[h] + ig_col[h] - m_new[h]) for h in heads]
        dec = [jnp.exp(b_end[h] + m_prev[h] - m_new[h]) for h in heads]
        c_upd = [_dot_tn((w_s[h] * v[h]).astype(BF16), kb[h]) for h in heads]
        outs = []
        for h in heads:
            C_scr[h] = dec[h] * C[h] + c_upd[h]
            n_scr[h:h + 1, :] = dec[h] * n_prev[h] + jnp.sum(w_s[h] * k[h], axis=0, keepdims=True)
            m_scr[h:h + 1, :] = jnp.broadcast_to(m_new[h], (1, LANES))
            hn = hh[h] * lax.rsqrt(jnp.mean(hh[h] * hh[h], axis=-1, keepdims=True) + NORM_EPS) * nw[:, vs[h]]
            outs.append(hn * _sigmoid(load(o_ref, vs[h])))
        y_out = jnp.concatenate(outs, axis=-1).astype(y_ref.dtype)
        if TB >= L:
            y_ref[sl, :] = y_out
        else:
            y_ref[...] = y_out[:TB]
        return carry

    lax.fori_loop(0, max(TB // L, 1), chunk, 0)

    @pl.when(tb == pl.num_programs(1) - 1)
    def _():
        cT_ref[...] = C_scr[...]
        nT_ref[...] = n_scr[...]
        mT_ref[...] = m_scr[...]


def mlstm_recurrence(proj, gate_bias, norm_w, c0, n0, m0, n_seq, T, row0, TB, t_valid):
    H, E, V = C_HEADS, C_QK_DIM, C_V_DIM
    nb = T // TB
    rb0 = row0 // TB
    HE, HV = H * E, H * V

    def cols(cb):
        return lambda b, t: (rb0 + b * nb + t, cb)

    st4 = lambda b, t: (b, 0, 0, 0)
    st3 = lambda b, t: (b, 0, 0)
    return pl.pallas_call(
        functools.partial(_mlstm_body, L=CHUNK, TB=TB, t_valid=t_valid), grid=(n_seq, nb),
        in_specs=[pl.BlockSpec((TB, HE), cols(0)), pl.BlockSpec((TB, HE), cols(1)),
                  pl.BlockSpec((TB, HV), cols(2 * HE // HV)), pl.BlockSpec((TB, HV), cols(2 * HE // HV + 1)),
                  pl.BlockSpec((TB, LANES), cols((2 * HE + 2 * HV) // LANES)),
                  pl.BlockSpec((1, LANES), lambda b, t: (0, 0)), pl.BlockSpec((1, HV), lambda b, t: (0, 0)),
                  pl.BlockSpec((None, H, V, E), st4), pl.BlockSpec((None, H, E), st3),
                  pl.BlockSpec((None, H, LANES), st3)],
        out_specs=[pl.BlockSpec((TB, HV), lambda b, t: (b * nb + t, 0)),
                   pl.BlockSpec((None, H, V, E), st4), pl.BlockSpec((None, H, E), st3),
                   pl.BlockSpec((None, H, LANES), st3)],
        out_shape=[jax.ShapeDtypeStruct((n_seq * T, HV), _row_dtype(TB)), jax.ShapeDtypeStruct((n_seq, H, V, E), F32),
                   jax.ShapeDtypeStruct((n_seq, H, E), F32), jax.ShapeDtypeStruct((n_seq, H, LANES), F32)],
        scratch_shapes=[pltpu.VMEM((H, V, E), F32), pltpu.VMEM((H, E), F32), pltpu.VMEM((H, LANES), F32)],
        compiler_params=_params(("parallel", "arbitrary")), name="mlstm_recurrence",
    )(proj, proj, proj, proj, proj, gate_bias, norm_w, c0, n0, m0)


def _row_tile(m, target, mult):
    return max(t for t in range(mult, min(m, target) + 1, mult) if m % t == 0)


def _tile(n, target):
    return max(t for t in range(LANES, min(n, target) + 1, LANES) if n % t == 0)


def _pad_cols(w, n):
    return jnp.pad(w, ((0, 0), (0, n - w.shape[1])))


def _pad_rows(w, n):
    return jnp.pad(w, ((0, n - w.shape[0]), (0, 0)))


def kernel(x_prompt, x_sample, state_a_wkv, state_a_shift, cache_b_kv_g0, cache_b_kv_g1, cache_b_kv_g2, state_c_C, state_c_n, state_c_m, rel_bias, norm_ffn1, ffn1_w_in, ffn1_w_out, norm_mix, norm_ffn2, ffn2_w_in, ffn2_w_out, norm_final, a_mu, a_w_rkv, a_w0, a_w1, a_w2, a_a0, a_a1, a_a2, a_g1, a_g2, a_k_k, a_k_a, a_r_k, a_gn_w, a_gn_b, a_w_out, a_v0, a_v1, a_v2, b_w_qkv, b_w_out, c_w_in, c_b_gates, c_norm_w, c_w_out):
    Bp, Tp, D = x_prompt.shape
    Bs, Ts, _ = x_sample.shape
    depth = norm_mix.shape[0]
    Tsp = SAMPLE_PAD
    Mp, Ms = Bp * Tp, Bs * Tsp
    M = Mp + Ms
    TM = _row_tile(M, 768, 16)
    TM_LORA = _row_tile(M, 384, 8)
    TN = _tile(D, 2048)
    TF = _tile(ffn1_w_out.shape[1], 512)
    H_a = D // A_HEAD_DIM
    G, H_b, Dh = len(B_DILATIONS), B_HEADS, B_HEAD_DIM
    bf = lambda w: w.astype(BF16)

    x = jnp.concatenate([x_prompt.reshape(Mp, D),
                         jnp.pad(x_sample, ((0, 0), (0, Tsp - Ts), (0, 0))).reshape(Ms, D)], axis=0)

    def last_rows(t):
        return (jnp.stack([t[(b + 1) * Tp - 1] for b in range(Bp)]),
                jnp.stack([t[Mp + b * Tsp + Ts - 1] for b in range(Bs)]))

    qi = jnp.arange(B_BLK)[:, None]
    kj = jnp.arange(2 * B_BLK)[None, :]
    step = qi + B_BLK - kj
    step_ok = (step >= 0) & (step <= B_BLK)
    m_desc = B_BLK - jnp.arange(B_BLK + 8)
    bias_p, bias_s = [], []
    buckets = jnp.arange(N_BUCKETS)
    for gi, dil in enumerate(B_DILATIONS):
        tab = rel_bias[:, gi * H_b:(gi + 1) * H_b].astype(F32)
        hot = (_rel_bucket(jnp.clip(step, 0, B_BLK) * dil)[None] == buckets[:, None, None]).astype(F32)
        bp = jnp.einsum("nh,nqk->hqk", tab, hot, precision=HIGHEST)
        bias_p.append(jnp.where(step_ok[None], bp, -jnp.inf))
        bs = tab[_rel_bucket(jnp.maximum(m_desc, 0) * dil)]
        bias_s.append(jnp.broadcast_to(bs[:, :, None], (B_BLK + 8, H_b, Dh)))
    bias_p, bias_s = jnp.stack(bias_p), jnp.stack(bias_s)
    w_rkv = bf(a_w_rkv)

    outs_a_wkv, outs_a_shift, outs_c = ([], []), ([], []), ([], [], [], [], [], [])
    outs_b = [([], []) for _ in range(G)]
    v_first = None
    for i in range(depth):
        x = ffn(x, norm_ffn1[i], ffn1_w_in, ffn1_w_out, i, TM, TF)
        kind, j = i % 3, i // 3
        if kind == 0:
            u, u_prev = rmsnorm_shift(x, norm_mix[i], state_a_shift[j], Tp, Mp, Tsp, TM)
            mu = a_mu[j]
            rkv = rwkv_rkv(u, u_prev, mu[jnp.array([0, 2, 3])][:, None, :], w_rkv, j, TM, TN)
            lr = LANES
            w_br = (bf(_pad_cols(a_w1[j], lr)), bf(_pad_rows(a_w2[j], lr)), a_w0[j].reshape(1, D))
            a_br = (bf(_pad_cols(a_a1[j], lr)), bf(_pad_rows(a_a2[j], lr)), a_a0[j].reshape(1, D))
            g_br = (bf(a_g1[j]), bf(a_g2[j]))
            v_br = None
            if j > 0:
                v_br = (bf(_pad_cols(a_v1[j - 1], lr)), bf(_pad_rows(a_v2[j - 1], lr)), a_v0[j - 1].reshape(1, D))
            lora = rwkv_lora(u, u_prev, mu[jnp.array([1, 4, 5, 3])], w_br, a_br, g_br, v_br, TM_LORA)
            lw, a_lr, gate = lora[:3]
            vres = None if j == 0 else (v_first, lora[3])
            if j == 0:
                v_first = rkv
            par = tuple(p.reshape(1, D) for p in (a_k_k[j], a_k_a[j], a_r_k[j], a_gn_w[j], a_gn_b[j]))
            s0p = jnp.zeros((Bp * H_a, A_HEAD_DIM, A_HEAD_DIM), F32)
            s0s = state_a_wkv[j].reshape(Bs * H_a, A_HEAD_DIM, A_HEAD_DIM)
            yp, sp = rwkv_recurrence(rkv, lw, a_lr, gate, vres, par, s0p, Bp, Tp, 0, 128, None)
            ys, ss = rwkv_recurrence(rkv, lw, a_lr, gate, vres, par, s0s, Bs, Tsp, Mp, Tsp, Ts)
            x = matmul((yp, ys), bf(a_w_out[j]), TM, TN, residual=x)
            outs_a_wkv[0].append(sp.reshape(Bp, H_a, A_HEAD_DIM, A_HEAD_DIM))
            outs_a_wkv[1].append(ss.reshape(Bs, H_a, A_HEAD_DIM, A_HEAD_DIM))
            sh_p, sh_s = last_rows(u)
            outs_a_shift[0].append(sh_p)
            outs_a_shift[1].append(sh_s)
        elif kind == 1:
            qkv = matmul(x, bf(b_w_qkv[j]), TM, _tile(b_w_qkv.shape[2], 2048), norm_g=norm_mix[i])
            caches = [c[j].reshape(Bs, B_BLK, d, 2, H_b, Dh)
                      for c, d in zip((cache_b_kv_g0, cache_b_kv_g1, cache_b_kv_g2), B_DILATIONS)]
            op = attn_prompt(qkv, bias_p, Bp, Tp)
            os_ = attn_sample(qkv[Mp:], caches, bias_s, Bs, Tsp, 0, Ts)
            x = matmul((op, os_.reshape(Ms, H_b * Dh)), bf(b_w_out[j]), TM, TN, residual=x)
            for gi in range(G):
                keep = min(B_WINDOWS[gi], Tp)
                c0, c1 = (gi * 3 + 1) * H_b * Dh, (gi * 3 + 3) * H_b * Dh
                kv_p = jnp.stack([lax.slice(qkv, ((b + 1) * Tp - keep, c0), ((b + 1) * Tp, c1)) for b in range(Bp)])
                kv_s = lax.slice(qkv, (Mp, c0), (M, c1)).reshape(Bs, Tsp, c1 - c0)[:, :Ts]
                outs_b[gi][0].append(kv_p.reshape(Bp, keep, 2, H_b, Dh))
                outs_b[gi][1].append(kv_s.reshape(Bs, Ts, 2, H_b, Dh))
        else:
            H, E, V = C_HEADS, C_QK_DIM, C_V_DIM
            n_in = c_w_in.shape[2]
            n_pad = -(-n_in // LANES) * LANES
            proj = matmul(x, bf(_pad_cols(c_w_in[j], n_pad)), TM, _tile(n_pad, 1024), norm_g=norm_mix[i])
            gbias = _pad_cols(c_b_gates[j].reshape(1, 2 * H), LANES)
            nw = c_norm_w[j].reshape(1, H * V)
            zc = (jnp.zeros((Bp, H, V, E), F32), jnp.zeros((Bp, H, E), F32), jnp.zeros((Bp, H, LANES), F32))
            sc = (state_c_C[j], state_c_n[j], jnp.broadcast_to(state_c_m[j][:, :, None], (Bs, H, LANES)))
            hp, cp, np_, mp = mlstm_recurrence(proj, gbias, nw, *zc, Bp, Tp, 0, 256, None)
            hs, cs, ns, ms = mlstm_recurrence(proj, gbias, nw, *sc, Bs, Tsp, Mp, Tsp, Ts)
            x = matmul((hp, hs), bf(c_w_out[j]), TM, TN, residual=x)
            for lst, val in zip(outs_c, (cp, cs, np_, ns, mp[:, :, 0], ms[:, :, 0])):
                lst.append(val)
        x = ffn(x, norm_ffn2[i], ffn2_w_in, ffn2_w_out, i, TM, TF)

    y_prompt = rmsnorm(x, norm_final, _row_tile(Mp, 512, 8), 0, Mp).reshape(Bp, Tp, D)
    y_sample = rmsnorm(x, norm_final, Ms, Mp, Ms).reshape(Bs, Tsp, D)[:, :Ts]
    st = jnp.stack
    return (y_prompt, y_sample, st(outs_a_wkv[0]), st(outs_a_wkv[1]), st(outs_a_shift[0]), st(outs_a_shift[1]),
            st(outs_b[0][0]), st(outs_b[0][1]), st(outs_b[1][0]), st(outs_b[1][1]), st(outs_b[2][0]), st(outs_b[2][1]),
            st(outs_c[0]), st(outs_c[1]), st(outs_c[2]), st(outs_c[3]), st(outs_c[4]), st(outs_c[5]))
```

```python
import functools
import math

import jax
import jax.numpy as jnp
from jax import lax
from jax.experimental import pallas as pl
from jax.experimental.pallas import tpu as pltpu

F32 = jnp.float32
BF16 = jnp.bfloat16
HIGHEST = lax.Precision.HIGHEST

NORM_EPS = 1e-6
A_HEAD_DIM = 64
A_GN_EPS = 64e-5
A_REC_LANES = 2048
B_WINDOWS = (128, 512, 2048)
B_DILATIONS = (1, 4, 16)
B_HEADS = 16
B_HEAD_DIM = 128
B_BLK = 128
B_UNROLL = 4
N_BUCKETS = 32
BUCKET_MAX_DIST = 2048
C_HEADS = 8
C_QK_DIM = 128
C_V_DIM = 256
C_GATE_CAP = 15.0
CHUNK = 64
SAMPLE_PAD = 8
LANES = 128
SUBLANES_F32 = 8
SUBLANES_BF16 = 16
VMEM_LIMIT = 56 * 1024 * 1024


def _row_dtype(rows):
    return BF16 if rows % SUBLANES_BF16 == 0 else F32


def _params(sem):
    return pltpu.CompilerParams(dimension_semantics=sem, vmem_limit_bytes=VMEM_LIMIT)


def _dot(a, b, precision=None):
    return jnp.dot(a, b, preferred_element_type=F32, precision=precision)


def _dot_nt(a, b):
    return lax.dot_general(a, b, (((1,), (1,)), ((), ())), preferred_element_type=F32)


def _dot_tn(a, b):
    return lax.dot_general(a, b, (((0,), (0,)), ((), ())), preferred_element_type=F32)


def _bf16_operands(dot):
    return lambda a, b: dot(a.astype(BF16), b.astype(BF16))


def _sigmoid(x):
    return 1.0 / (1.0 + jnp.exp(-x))


def _rms(x, g):
    ms = jnp.mean(x * x, axis=-1, keepdims=True)
    return x * lax.rsqrt(ms + NORM_EPS) * g


def _rmsnorm_body(x_ref, g_ref, o_ref):
    o_ref[...] = _rms(x_ref[...], g_ref[...]).astype(o_ref.dtype)


def rmsnorm(x, g, tm, row0, rows):
    D = x.shape[1]
    rb0 = row0 // tm
    return pl.pallas_call(
        _rmsnorm_body, grid=(rows // tm,),
        in_specs=[pl.BlockSpec((tm, D), lambda i: (rb0 + i, 0)), pl.BlockSpec((1, D), lambda i: (0, 0))],
        out_specs=pl.BlockSpec((tm, D), lambda i: (i, 0)),
        out_shape=jax.ShapeDtypeStruct((rows, D), F32),
        compiler_params=_params(("parallel",)), name="rmsnorm",
    )(x, g.reshape(1, D))


def _rmsnorm_shift_body(x_ref, xp_ref, g_ref, st_ref, u_ref, up_ref, *, tm, grp, seq_len, seq_rows):
    i = pl.program_id(0)
    g = g_ref[...]
    u = _rms(x_ref[...], g)
    u_ref[...] = u
    up_ref[...] = pltpu.roll(u, 1, axis=0)
    tail = _rms(xp_ref[...], g)[-1:, :]
    n_late = st_ref.shape[0]
    for k in range(tm // grp):
        row0 = i * tm + k * grp
        late = row0 >= seq_rows
        is_start = late | (lax.rem(row0, seq_len) == 0)
        state = st_ref[pl.ds(jnp.clip((row0 - seq_rows) // grp, 0, n_late - 1), 1), :]
        before = tail if k == 0 else u[k * grp - 1:k * grp, :]
        up_ref[k * grp:k * grp + 1, :] = jnp.where(is_start, jnp.where(late, state, 0.0), before)


def rmsnorm_shift(x, g, late_states, seq_len, seq_rows, grp, tm):
    M, D = x.shape
    sub = SUBLANES_F32
    return pl.pallas_call(
        functools.partial(_rmsnorm_shift_body, tm=tm, grp=grp, seq_len=seq_len, seq_rows=seq_rows), grid=(M // tm,),
        in_specs=[pl.BlockSpec((tm, D), lambda i: (i, 0)),
                  pl.BlockSpec((sub, D), lambda i: (jnp.maximum(i * (tm // sub) - 1, 0), 0)),
                  pl.BlockSpec((1, D), lambda i: (0, 0)),
                  pl.BlockSpec(late_states.shape, lambda i: (0, 0))],
        out_specs=[pl.BlockSpec((tm, D), lambda i: (i, 0))] * 2,
        out_shape=[jax.ShapeDtypeStruct((M, D), F32)] * 2,
        compiler_params=_params(("parallel",)), name="rmsnorm_shift",
    )(x, x, g.reshape(1, D), late_states)


def _ffn_body(x_ref, g_ref, wg_ref, wu_ref, wo_ref, o_ref, xn_ref):
    j = pl.program_id(1)

    @pl.when(j == 0)
    def _():
        xn_ref[...] = _rms(x_ref[...], g_ref[...]).astype(BF16)
        o_ref[...] = jnp.zeros_like(o_ref)

    xn = xn_ref[...]
    gate = _dot(xn, wg_ref[...].astype(BF16))
    up = _dot(xn, wu_ref[...].astype(BF16))
    h = (gate * _sigmoid(gate) * up).astype(BF16)
    o_ref[...] += _dot(h, wo_ref[...].astype(BF16))

    @pl.when(j == pl.num_programs(1) - 1)
    def _():
        o_ref[...] = x_ref[...] + 0.5 * o_ref[...]


def ffn(x, g, w_in, w_out, layer, tm, tf):
    M, D = x.shape
    Fh = w_out.shape[1]
    nf = Fh // tf
    return pl.pallas_call(
        _ffn_body, grid=(M // tm, nf),
        in_specs=[pl.BlockSpec((tm, D), lambda i, j: (i, 0)),
                  pl.BlockSpec((1, D), lambda i, j: (0, 0)),
                  pl.BlockSpec((None, D, tf), lambda i, j: (layer, 0, j)),
                  pl.BlockSpec((None, D, tf), lambda i, j: (layer, 0, j + nf)),
                  pl.BlockSpec((None, tf, D), lambda i, j: (layer, j, 0))],
        out_specs=pl.BlockSpec((tm, D), lambda i, j: (i, 0)),
        out_shape=jax.ShapeDtypeStruct((M, D), F32),
        scratch_shapes=[pltpu.VMEM((tm, D), BF16)],
        compiler_params=_params(("parallel", "arbitrary")), name="ffn",
    )(x, g.reshape(1, D), w_in, w_in, w_out)


def _mm_body(*refs, nb0, has_norm, has_res):
    it = iter(refs)
    x_ref = next(it)
    x1_ref = next(it) if nb0 is not None else None
    g_ref = next(it) if has_norm else None
    w_ref = next(it)
    res_ref = next(it) if has_res else None
    o_ref = next(it)
    xs_ref = next(it)

    def stage(ref, rows=slice(None), dst=slice(None)):
        x = ref[rows, :].astype(F32)
        if has_norm:
            x = _rms(x, g_ref[...])
        xs_ref[dst, :] = x.astype(BF16)

    @pl.when(pl.program_id(1) == 0)
    def _():
        if nb0 is None:
            stage(x_ref)
        else:
            rem = xs_ref.shape[0] - x1_ref.shape[0]
            pl.when(pl.program_id(0) < nb0)(lambda: stage(x_ref))

            @pl.when(pl.program_id(0) == nb0)
            def _():
                if rem > 0:
                    stage(x_ref, slice(0, rem), slice(0, rem))
                stage(x1_ref, slice(None), slice(rem, None))

    acc = _dot(xs_ref[...], w_ref[...])
    if has_res:
        acc = res_ref[...] + acc
    o_ref[...] = acc.astype(o_ref.dtype)


def matmul(x, w, tm, tn, norm_g=None, residual=None, out_dtype=F32):
    nb0 = None
    if isinstance(x, tuple):
        x0, x1 = x
        nb0 = x0.shape[0] // tm
        M, K = x0.shape[0] + x1.shape[0], x0.shape[1]
        assert M == (nb0 + 1) * tm and x1.shape[0] <= tm and x1.shape[0] % SUBLANES_BF16 == 0
        args = [x0, x1]
        last0 = -(-x0.shape[0] // tm) - 1
        specs = [pl.BlockSpec((tm, K), lambda i, j: (jnp.minimum(i, last0), 0)),
                 pl.BlockSpec(x1.shape, lambda i, j: (0, 0))]
    else:
        M, K = x.shape
        args = [x]
        specs = [pl.BlockSpec((tm, K), lambda i, j: (i, 0))]
    N = w.shape[1]
    if norm_g is not None:
        args.append(norm_g.reshape(1, K))
        specs.append(pl.BlockSpec((1, K), lambda i, j: (0, 0)))
    args.append(w)
    specs.append(pl.BlockSpec((K, tn), lambda i, j: (0, j)))
    if residual is not None:
        args.append(residual)
        specs.append(pl.BlockSpec((tm, tn), lambda i, j: (i, j)))
    return pl.pallas_call(
        functools.partial(_mm_body, nb0=nb0, has_norm=norm_g is not None, has_res=residual is not None),
        grid=(M // tm, N // tn), in_specs=specs,
        out_specs=pl.BlockSpec((tm, tn), lambda i, j: (i, j)),
        out_shape=jax.ShapeDtypeStruct((M, N), out_dtype),
        scratch_shapes=[pltpu.VMEM((tm, K), BF16)],
        compiler_params=_params(("parallel", "arbitrary")), name="matmul",
    )(*args)


def _rkv_body(u_ref, up_ref, mu_ref, w_ref, o_ref, xs_ref):
    @pl.when(pl.program_id(2) == 0)
    def _():
        u = u_ref[...]
        xs_ref[...] = (u + (up_ref[...] - u) * mu_ref[...]).astype(BF16)

    o_ref[...] = _dot(xs_ref[...], w_ref[...])


def rwkv_rkv(u, u_prev, mu3, w, layer, tm, tn):
    M, D = u.shape
    return pl.pallas_call(
        _rkv_body, grid=(M // tm, 3, D // tn),
        in_specs=[pl.BlockSpec((tm, D), lambda i, k, j: (i, 0)),
                  pl.BlockSpec((tm, D), lambda i, k, j: (i, 0)),
                  pl.BlockSpec((None, 1, D), lambda i, k, j: (k, 0, 0)),
                  pl.BlockSpec((None, None, D, tn), lambda i, k, j: (layer, k, 0, j))],
        out_specs=pl.BlockSpec((None, tm, tn), lambda i, k, j: (k, i, j)),
        out_shape=jax.ShapeDtypeStruct((3, M, D), F32),
        scratch_shapes=[pltpu.VMEM((tm, D), BF16)],
        compiler_params=_params(("parallel", "arbitrary", "arbitrary")), name="rwkv_rkv",
    )(u, u_prev, mu3, w)


def _lora_body(*refs, has_vres):
    it = iter(refs)
    u_ref, up_ref, mu_ref = next(it), next(it), next(it)
    w1, w2, w0 = next(it), next(it), next(it)
    a1, a2, a0 = next(it), next(it), next(it)
    g1, g2 = next(it), next(it)
    if has_vres:
        v1, v2, v0 = next(it), next(it), next(it)
    lw_ref, a_ref, g_ref = next(it), next(it), next(it)
    nu_ref = next(it) if has_vres else None

    u = u_ref[...]
    du = up_ref[...] - u

    def mix(n):
        return (u + du * mu_ref[n:n + 1, :]).astype(BF16)

    hw = jnp.tanh(_dot(mix(0), w1[...])).astype(BF16)
    w_pre = w0[...] + _dot(hw, w2[...])
    softplus = jnp.maximum(-w_pre, 0.0) + jnp.log(1.0 + jnp.exp(-jnp.abs(w_pre)))
    lw_ref[...] = -jnp.exp(-softplus - 0.5)
    ha = _dot(mix(1), a1[...]).astype(BF16)
    a_ref[...] = _sigmoid(a0[...] + _dot(ha, a2[...]))
    hg = _sigmoid(_dot(mix(2), g1[...])).astype(BF16)
    g_ref[...] = _dot(hg, g2[...])
    if has_vres:
        hv = _dot(mix(3), v1[...]).astype(BF16)
        nu_ref[...] = _sigmoid(v0[...] + _dot(hv, v2[...]))


def rwkv_lora(u, u_prev, mu4, w, a, g, v, tm):
    M, D = u.shape
    has_vres = v is not None
    row = lambda i: (i, 0)
    full = lambda i: (0, 0)
    args = [u, u_prev, mu4]
    specs = [pl.BlockSpec((tm, D), row), pl.BlockSpec((tm, D), row), pl.BlockSpec(mu4.shape, full)]
    for t in (w, a, g) + ((v,) if has_vres else ()):
        for m in t:
            args.append(m)
            specs.append(pl.BlockSpec(m.shape, full))
    n_out = 4 if has_vres else 3
    return pl.pallas_call(
        functools.partial(_lora_body, has_vres=has_vres), grid=(M // tm,), in_specs=specs,
        out_specs=[pl.BlockSpec((tm, D), row)] * n_out,
        out_shape=[jax.ShapeDtypeStruct((M, D), F32)] * n_out,
        compiler_params=_params(("parallel",)), name="rwkv_lora",
    )(*args)


def _rwkv_rec_body(*refs, L, TB, hb, t_valid, has_vres):
    N = A_HEAD_DIM
    it = iter(refs)
    r_ref, k_ref, v_ref, lw_ref, a_ref, g_ref = (next(it) for _ in range(6))
    if has_vres:
        vf_ref, nu_ref = next(it), next(it)
    kk_ref, ka_ref, rk_ref, gnw_ref, gnb_ref, s0_ref = (next(it) for _ in range(6))
    y_ref, sT_ref, S_scr = next(it), next(it), next(it)
    tb = pl.program_id(2)

    assert L == N and 2 * N == LANES

    @pl.when(tb == 0)
    def _():
        for p in range(hb // 2):
            S_scr[p] = jnp.concatenate([s0_ref[2 * p], s0_ref[2 * p + 1]], axis=1)

    row = lax.broadcasted_iota(jnp.int32, (L, L), 0)
    col = lax.broadcasted_iota(jnp.int32, (L, L), 1)
    tril = (row >= col).astype(F32)
    row1 = lax.broadcasted_iota(jnp.int32, (L, 2 * N), 0)
    lane1 = lax.broadcasted_iota(jnp.int32, (L, 2 * N), 1)
    head0_lane = lane1 < N
    eye = ((lane1 & (N - 1)) == row1).astype(F32)
    row2 = lax.broadcasted_iota(jnp.int32, (2 * L, 2 * N), 0)
    lane2 = lax.broadcasted_iota(jnp.int32, (2 * L, 2 * N), 1)
    mask2 = (lane2 & (N - 1)) < jnp.where(row2 < L, row2, row2 - L + 1)
    bd_mask = (row2 // L) == (lane2 // N)
    ones_bd = bd_mask.astype(BF16)
    ones_bd2 = jnp.concatenate([ones_bd, ones_bd], axis=0)
    n_sq = int(math.log2(L)) - 1
    kk_p, ka_p, rk_p, gnw, gnb = kk_ref[...], ka_ref[...], rk_ref[...], gnw_ref[...], gnb_ref[...]

    def chunk(c, carry):
        sl = pl.ds(pl.multiple_of(c * L, L), L)

        def load(ref):
            if TB >= L:
                return ref[sl, :]
            return jnp.concatenate([ref[...], jnp.zeros((L - TB, ref.shape[1]), F32)], axis=0)

        r, k, v, lw, a, g = (load(ref) for ref in (r_ref, k_ref, v_ref, lw_ref, a_ref, g_ref))
        if has_vres:
            v = v + (load(vf_ref) - v) * load(nu_ref)
        if t_valid is not None:
            t_idx = tb * TB + c * L + lax.broadcasted_iota(jnp.int32, (L, 1), 0)
            valid = t_idx < t_valid
            r, k, v, lw = (jnp.where(valid, t, 0.0) for t in (r, k, v, lw))
        cum = _dot(tril, lw, HIGHEST)
        cum_end = cum[L - 1:L, :]
        w_cur, w_prev, w_inv, w_rem, w_end = (jnp.exp(cum), jnp.exp(cum - lw), jnp.exp(-cum),
                                              jnp.exp(cum_end - cum), jnp.exp(cum_end))
        mm, mm_nt, mm_tn = (_bf16_operands(d) for d in (_dot, _dot_nt, _dot_tn))
        pairs = range(hb // 2)
        ps = [slice(p * LANES, (p + 1) * LANES) for p in pairs]

        def head_sum(x):
            x_hi = x.astype(BF16)
            x_lo = (x - x_hi.astype(F32)).astype(BF16)
            return _dot(jnp.concatenate([x_hi, x_lo], axis=1), ones_bd2)

        def bdiag(x):
            return jnp.where(bd_mask, jnp.concatenate([x, x], axis=0), 0.0)

        kkp = [k[:, s] * kk_p[:, s] for s in ps]
        k2 = [k[:, s] * (1.0 + (a[:, s] - 1.0) * ka_p[:, s]) for s in ps]
        sums = [head_sum(jnp.concatenate([kkp[p] * kkp[p], r[:, ps[p]] * k2[p] * rk_p[:, ps[p]]], axis=0))
                for p in pairs]
        kk = [kkp[p] / jnp.maximum(jnp.sqrt(sums[p][:L]), 1e-12) for p in pairs]
        b = [kk[p] * a[:, ps[p]] for p in pairs]
        lhs2 = [jnp.concatenate([kk[p] * w_prev[:, ps[p]], r[:, ps[p]] * w_cur[:, ps[p]]], axis=0) for p in pairs]
        kd = [k2[p] * w_inv[:, ps[p]] for p in pairs]
        bd = [b[p] * w_inv[:, ps[p]] for p in pairs]
        kend = [k2[p] * w_rem[:, ps[p]] for p in pairs]
        bend = [b[p] * w_rem[:, ps[p]] for p in pairs]
        kkd = [x[:L] for x in lhs2]
        rd = [x[L:] for x in lhs2]
        a_kb2 = [mm_nt(lhs2[p], jnp.concatenate([bdiag(kd[p]), bdiag(bd[p])], axis=0)) for p in pairs]
        a_k = [jnp.where(mask2, x[:, :LANES], 0.0) for x in a_kb2]
        a_b = [jnp.where(mask2, x[:, LANES:], 0.0) for x in a_kb2]
        a_kb = [x[:L] for x in a_b]
        a_rb = [x[L:] for x in a_b]
        a_v = [mm(a_k[p], bdiag(v[:, ps[p]])) for p in pairs]
        t_inv = [eye - x for x in a_kb]
        pw = [mm(x, bdiag(x)) for x in a_kb]
        for _ in range(n_sq - 1):
            both = [mm(jnp.concatenate([pw[p], t_inv[p]], axis=0), bdiag(pw[p])) for p in pairs]
            t_inv = [t_inv[p] + both[p][L:] for p in pairs]
            pw = [x[:L] for x in both]
        t_inv = [t_inv[p] + mm(t_inv[p], bdiag(pw[p])) for p in pairs]
        ktcu = [mm(t_inv[p], jnp.concatenate([bdiag(kkd[p]), bdiag(a_v[p][:L])], axis=1)) for p in pairs]
        k_t = [x[:, :LANES] for x in ktcu]
        c_u = [x[:, LANES:] for x in ktcu]
        ykc = [mm(a_rb[p], jnp.concatenate([bdiag(k_t[p]), bdiag(c_u[p])], axis=1)) for p in pairs]
        y_k = [rd[p] - ykc[p][:, :LANES] for p in pairs]
        y_c = [a_v[p][L:] - ykc[p][:, LANES:] for p in pairs]
        S = [S_scr[p] for p in pairs]
        y = [mm_nt(y_k[p], bdiag(S[p])) + y_c[p] for p in pairs]
        upd = [mm_tn(jnp.concatenate([jnp.concatenate([v[:, ps[p]], jnp.zeros_like(k_t[p])], axis=1),
                                      jnp.concatenate([-c_u[p], k_t[p]], axis=1)], axis=0),
                     jnp.concatenate([kend[p], bend[p]], axis=0)) for p in pairs]
        S_c = [jnp.where(head0_lane, x[:N], x[N:LANES]) for x in upd]
        ktb = [jnp.where(bd_mask, x[LANES:], 0.0) for x in upd]
        for p in pairs:
            S_scr[p] = S[p] * w_end[:, ps[p]] - mm(S[p], ktb[p]) + S_c[p]
        mean = [head_sum(y[p]) * (1.0 / N) for p in pairs]
        var = [head_sum(jnp.square(y[p] - mean[p])) * (1.0 / N) for p in pairs]
        bonus = [sums[p][L:] * v[:, ps[p]] for p in pairs]
        outs = [((y[p] - mean[p]) * lax.rsqrt(var[p] + A_GN_EPS) * gnw[:, ps[p]] + gnb[:, ps[p]] + bonus[p])
                * g[:, ps[p]] for p in pairs]
        y_out = jnp.concatenate(outs, axis=-1).astype(y_ref.dtype)
        if TB >= L:
            y_ref[sl, :] = y_out
        else:
            y_ref[...] = y_out[:TB]
        return carry

    lax.fori_loop(0, max(TB // L, 1), chunk, 0)

    @pl.when(tb == pl.num_programs(2) - 1)
    def _():
        for p in range(hb // 2):
            S = S_scr[p]
            sT_ref[2 * p] = S[:, :N]
            sT_ref[2 * p + 1] = S[:, N:]


def rwkv_recurrence(rkv, lw, a, g, vres, params, s0, n_seq, T, row0, TB, t_valid):
    _, M, D = rkv.shape
    N = A_HEAD_DIM
    LW = min(A_REC_LANES, D)
    hb = LW // N
    nb = T // TB
    rb0 = row0 // TB
    has_vres = vres is not None
    seq = lambda b, h, t: (rb0 + b * nb + t, h)
    args, specs = [], []
    for n in range(3):
        args.append(rkv)
        specs.append(pl.BlockSpec((None, TB, LW), lambda b, h, t, n=n: (n, rb0 + b * nb + t, h)))
    for x in (lw, a, g):
        args.append(x)
        specs.append(pl.BlockSpec((TB, LW), seq))
    if has_vres:
        args += list(vres)
        specs += [pl.BlockSpec((None, TB, LW), lambda b, h, t: (2, rb0 + b * nb + t, h)), pl.BlockSpec((TB, LW), seq)]
    for p in params:
        args.append(p)
        specs.append(pl.BlockSpec((1, LW), lambda b, h, t: (0, h)))
    args.append(s0)
    specs.append(pl.BlockSpec((hb, N, N), lambda b, h, t: (b * (D // LW) + h, 0, 0)))
    return pl.pallas_call(
        functools.partial(_rwkv_rec_body, L=CHUNK, TB=TB, hb=hb, t_valid=t_valid, has_vres=has_vres),
        grid=(n_seq, D // LW, nb), in_specs=specs,
        out_specs=[pl.BlockSpec((TB, LW), lambda b, h, t: (b * nb + t, h)),
                   pl.BlockSpec((hb, N, N), lambda b, h, t: (b * (D // LW) + h, 0, 0))],
        out_shape=[jax.ShapeDtypeStruct((n_seq * T, D), _row_dtype(TB)),
                   jax.ShapeDtypeStruct((n_seq * (D // N), N, N), F32)],
        scratch_shapes=[pltpu.VMEM((hb // 2, N, 2 * N), F32)],
        compiler_params=_params(("parallel", "parallel", "arbitrary")), name="rwkv_recurrence",
    )(*args)


def _rel_bucket(dist):
    exact = N_BUCKETS // 2
    d = jnp.maximum(dist, 1).astype(F32)
    log_b = exact + (jnp.log(d / exact) / math.log(BUCKET_MAX_DIST / exact) * (N_BUCKETS - exact)).astype(jnp.int32)
    return jnp.where(dist < exact, dist, jnp.minimum(log_b, N_BUCKETS - 1))


def _attn_prompt_body(q_ref, k_ref, v_ref, bias_ref, o_ref, acc_ref, m_ref, l_ref, kd_ref, vd_ref, *, T):
    step = pl.program_id(2)
    blk = B_BLK
    G = len(B_DILATIONS)
    scale = B_HEAD_DIM ** -0.5
    first_keys = lax.broadcasted_iota(jnp.int32, (blk, 2 * blk), 1) < blk
    ones_cols = jnp.ones((2 * blk, B_HEAD_DIM), BF16)

    for si, dil in enumerate(reversed(B_DILATIONS)):
        @pl.when(step == si)
        def _(gi=si, dil=dil):
            span = blk * dil
            unroll = B_UNROLL if gi > 0 else 2 * B_UNROLL
            res_rows = T // dil + blk
            bias = bias_ref[...]

            def where(idx):
                n = idx // dil
                r = idx - n * dil
                return n, n * span + r, pl.multiple_of(r * res_rows + n * blk, blk)

            for r in range(dil):
                kd_ref[r * res_rows:r * res_rows + blk, :] = jnp.zeros((blk, B_HEAD_DIM), BF16)
                vd_ref[r * res_rows:r * res_rows + blk, :] = jnp.zeros((blk, B_HEAD_DIM), BF16)

            def stage(it, carry):
                for u in range(unroll):
                    _, start, dst = where(it * unroll + u)
                    rows = pl.ds(start, blk, stride=dil)
                    kd_ref[pl.ds(dst + blk, blk), :] = k_ref[rows, :].astype(BF16)
                    vd_ref[pl.ds(dst + blk, blk), :] = v_ref[rows, :].astype(BF16)
                return carry

            lax.fori_loop(0, T // (blk * unroll), stage, 0)

            def blocks(it, carry):
                us = range(unroll)
                pos = [where(it * unroll + u) for u in us]
                n = [x[0] for x in pos]
                cur = [pl.ds(x[1], blk, stride=dil) for x in pos]
                q = [(q_ref[cur[u], :] * scale).astype(BF16) for u in us]
                kcat = [kd_ref[pl.ds(x[2], 2 * blk), :] for x in pos]
                vcat = [vd_ref[pl.ds(x[2], 2 * blk), :] for x in pos]
                logits = [_dot_nt(q[u], kcat[u]) + bias for u in us]
                logits = [jnp.where(first_keys & (n[u] == 0), -jnp.inf, logits[u]) for u in us]
                mx = [jnp.max(x, axis=-1, keepdims=True) for x in logits]
                p = [jnp.exp(logits[u] - mx[u]) for u in us]
                pvd = [_dot(p[u].astype(BF16), jnp.concatenate([vcat[u], ones_cols], axis=1)) for u in us]
                pv = [x[:, :B_HEAD_DIM] for x in pvd]
                den = [x[:, B_HEAD_DIM:] for x in pvd]
                if gi > 0:
                    m_old = [m_ref[cur[u], :] for u in us]
                    l_old = [l_ref[cur[u], :] for u in us]
                    acc_old = [acc_ref[cur[u], :] for u in us]
                    m_new = [jnp.maximum(m_old[u], mx[u]) for u in us]
                    c_old = [jnp.exp(m_old[u] - m_new[u]) for u in us]
                    c_new = [jnp.exp(mx[u] - m_new[u]) for u in us]
                    pv = [acc_old[u] * c_old[u] + pv[u] * c_new[u] for u in us]
                    den = [l_old[u] * c_old[u] + den[u] * c_new[u] for u in us]
                    mx = m_new
                for u in us:
                    acc_ref[cur[u], :] = pv[u]
                    m_ref[cur[u], :] = mx[u]
                    l_ref[cur[u], :] = den[u]
                return carry

            lax.fori_loop(0, T // (blk * unroll), blocks, 0)

    @pl.when(step == G - 1)
    def _():
        o_ref[...] = (acc_ref[...] / l_ref[...]).astype(o_ref.dtype)


def attn_prompt(qkv, bias, n_seq, T):
    H, Dh, G = B_HEADS, B_HEAD_DIM, len(B_DILATIONS)

    def col(which):
        return lambda b, h, s: (b, ((G - 1 - s) * 3 + which) * H + h)

    staged_rows = T + B_BLK * max(B_DILATIONS)
    return pl.pallas_call(
        functools.partial(_attn_prompt_body, T=T), grid=(n_seq, H, G),
        in_specs=[pl.BlockSpec((T, Dh), col(0)), pl.BlockSpec((T, Dh), col(1)), pl.BlockSpec((T, Dh), col(2)),
                  pl.BlockSpec((None, None, B_BLK, 2 * B_BLK), lambda b, h, s: (G - 1 - s, h, 0, 0))],
        out_specs=pl.BlockSpec((T, Dh), lambda b, h, s: (b, h)),
        out_shape=jax.ShapeDtypeStruct((n_seq * T, H * Dh), BF16),
        scratch_shapes=[pltpu.VMEM((T, Dh), F32), pltpu.VMEM((T, 1), F32), pltpu.VMEM((T, Dh), F32),
                        pltpu.VMEM((staged_rows, Dh), BF16), pltpu.VMEM((staged_rows, Dh), BF16)],
        compiler_params=_params(("parallel", "parallel", "arbitrary")), name="attn_prompt",
    )(qkv, qkv, qkv, bias)


def _attn_sample_body(q_ref, k_ref, v_ref, c0_ref, c1_ref, c2_ref, bias_ref, o_ref, *, t_valid):
    blk = B_BLK
    scale = B_HEAD_DIM ** -0.5
    caches = (c0_ref, c1_ref, c2_ref)
    o_ref[...] = jnp.zeros_like(o_ref)
    for t in range(t_valid):
        m_run = l_run = acc = None
        for gi, dil in enumerate(B_DILATIONS):
            q = q_ref[t, gi] * scale
            c_ref = caches[gi]
            if dil == 1:
                kcat = jnp.concatenate([c_ref[t:, 0, 0], k_ref[:t + 1, gi]], axis=0)
                vcat = jnp.concatenate([c_ref[t:, 0, 1], v_ref[:t + 1, gi]], axis=0)
            else:
                kcat = jnp.concatenate([c_ref[:, t, 0], k_ref[t:t + 1, gi]], axis=0)
                vcat = jnp.concatenate([c_ref[:, t, 1], v_ref[t:t + 1, gi]], axis=0)
            logits = jnp.sum(q[None] * kcat, axis=-1, keepdims=True) + bias_ref[gi, :blk + 1]
            mx = jnp.max(logits, axis=0)
            p = jnp.exp(logits - mx[None])
            den = jnp.sum(p, axis=0)
            pv = jnp.sum(p * vcat, axis=0)
            if gi == 0:
                m_run, l_run, acc = mx, den, pv
            else:
                m_new = jnp.maximum(m_run, mx)
                c_old, c_new = jnp.exp(m_run - m_new), jnp.exp(mx - m_new)
                acc = acc * c_old + pv * c_new
                l_run = l_run * c_old + den * c_new
                m_run = m_new
        o_ref[t] = (acc / l_run).astype(o_ref.dtype)


def attn_sample(qkv, caches, bias, n_seq, T, row0, t_valid):
    H, Dh, G = B_HEADS, B_HEAD_DIM, len(B_DILATIONS)
    rb0 = row0 // T
    assert t_valid <= min(d for d in B_DILATIONS if d > 1)
    q5 = qkv.reshape(qkv.shape[0], G, 3, H, Dh)

    def spec(which):
        return pl.BlockSpec((T, G, None, H, Dh), lambda b: (b + rb0, 0, which, 0, 0))

    cache_specs = [pl.BlockSpec((None, B_BLK, min(d, t_valid), 2, H, Dh), lambda b: (b, 0, 0, 0, 0, 0))
                   for d in B_DILATIONS]
    return pl.pallas_call(
        functools.partial(_attn_sample_body, t_valid=t_valid), grid=(n_seq,),
        in_specs=[spec(0), spec(1), spec(2)] + cache_specs + [pl.BlockSpec(bias.shape, lambda b: (0, 0, 0, 0))],
        out_specs=pl.BlockSpec((T, H, Dh), lambda b: (b, 0, 0)),
        out_shape=jax.ShapeDtypeStruct((n_seq * T, H, Dh), BF16),
        compiler_params=_params(("parallel",)), name="attn_sample",
    )(q5, q5, q5, *caches, bias)


def _mlstm_body(q_ref, k_ref, v_ref, o_ref, gate_ref, gb_ref, nw_ref, c0_ref, n0_ref, m0_ref,
                y_ref, cT_ref, nT_ref, mT_ref, C_scr, n_scr, m_scr, *, L, TB, t_valid):
    H, E, V = C_HEADS, C_QK_DIM, C_V_DIM
    tb = pl.program_id(1)

    @pl.when(tb == 0)
    def _():
        C_scr[...] = c0_ref[...]
        n_scr[...] = n0_ref[...]
        m_scr[...] = m0_ref[...]

    row = lax.broadcasted_iota(jnp.int32, (L, L), 0)
    col = lax.broadcasted_iota(jnp.int32, (L, L), 1)
    causal = row >= col
    tril = causal.astype(F32)
    gb = gb_ref[...]
    nw = nw_ref[...]

    def chunk(c, carry):
        sl = pl.ds(pl.multiple_of(c * L, L), L)

        def load(ref, cols=slice(None)):
            if TB >= L:
                return ref[sl, cols]
            x = ref[:, cols]
            return jnp.concatenate([x, jnp.zeros((L - TB, x.shape[1]), F32)], axis=0)

        gact = C_GATE_CAP * jnp.tanh((load(gate_ref) + gb) / C_GATE_CAP)
        lf = jnp.minimum(gact, 0.0) - jnp.log(1.0 + jnp.exp(-jnp.abs(gact)))
        ig = gact
        valid = None
        if t_valid is not None:
            t_idx = tb * TB + c * L + lax.broadcasted_iota(jnp.int32, (L, 1), 0)
            valid = t_idx < t_valid
            ig = jnp.where(valid, ig, -1e30)
            lf = jnp.where(valid, lf, 0.0)
        bcum = _dot(tril, lf, HIGHEST)
        ig_t = ig.T
        bcum_t = bcum.T
        heads = range(H)
        es = [slice(h * E, (h + 1) * E) for h in heads]
        vs = [slice(h * V, (h + 1) * V) for h in heads]
        b_col = [bcum[:, H + h:H + h + 1] for h in heads]
        b_row = [bcum_t[H + h:H + h + 1, :] for h in heads]
        ig_col = [ig[:, h:h + 1] for h in heads]
        ig_row = [ig_t[h:h + 1, :] for h in heads]
        q = [load(q_ref, s) for s in es]
        k = [load(k_ref, s) * (E ** -0.5) for s in es]
        v = [load(v_ref, s) for s in vs]
        if valid is not None:
            q, k, v = ([jnp.where(valid, t, 0.0) for t in ts] for ts in (q, k, v))
        m_prev = [m_scr[h:h + 1, 0:1] for h in heads]
        n_prev = [n_scr[h:h + 1, :] for h in heads]
        C = [C_scr[h] for h in heads]
        dm = [jnp.where(causal, b_col[h] - b_row[h] + ig_row[h], -jnp.inf) for h in heads]
        inter = [b_col[h] + m_prev[h] for h in heads]
        mt = [jnp.maximum(inter[h], jnp.max(dm[h], axis=-1, keepdims=True)) for h in heads]
        w_d = [jnp.exp(dm[h] - mt[h]) for h in heads]
        w_i = [jnp.exp(inter[h] - mt[h]) for h in heads]
        qb, kb, vb = ([t.astype(BF16) for t in ts] for ts in (q, k, v))
        sc = [_dot_nt(qb[h], kb[h]) * w_d[h] for h in heads]
        qc = [_dot_nt(qb[h], C[h].astype(BF16)) for h in heads]
        num = [_dot(sc[h].astype(BF16), vb[h]) + w_i[h] * qc[h] for h in heads]
        den = [jnp.sum(sc[h], axis=-1, keepdims=True) + w_i[h] * jnp.sum(q[h] * n_prev[h], axis=-1, keepdims=True)
               for h in heads]
        hh = [num[h] / jnp.maximum(jnp.abs(den[h]), jnp.exp(-mt[h])) for h in heads]
        m_new = [x[L - 1:L, :] for x in mt]
        b_end = [x[L - 1:L, :] for x in b_col]
        w_s = [jnp.exp(b_end[h] - b_col[h] + ig_col[h] - m_new[h]) for h in heads]
        dec = [jnp.exp(b_end[h] + m_prev[h] - m_new[h]) for h in heads]
        c_upd = [_dot_tn((w_s[h] * v[h]).astype(BF16), kb[h]) for h in heads]
        outs = []
        for h in heads:
            C_scr[h] = dec[h] * C[h] + c_upd[h]
            n_scr[h:h + 1, :] = dec[h] * n_prev[h] + jnp.sum(w_s[h] * k[h], axis=0, keepdims=True)
            m_scr[h:h + 1, :] = jnp.broadcast_to(m_new[h], (1, LANES))
            hn = hh[h] * lax.rsqrt(jnp.mean(hh[h] * hh[h], axis=-1, keepdims=True) + NORM_EPS) * nw[:, vs[h]]
            outs.append(hn * _sigmoid(load(o_ref, vs[h])))
        y_out = jnp.concatenate(outs, axis=-1).astype(y_ref.dtype)
        if TB >= L:
            y_ref[sl, :] = y_out
        else:
            y_ref[...] = y_out[:TB]
        return carry

    lax.fori_loop(0, max(TB // L, 1), chunk, 0)

    @pl.when(tb == pl.num_programs(1) - 1)
    def _():
        cT_ref[...] = C_scr[...]
        nT_ref[...] = n_scr[...]
        mT_ref[...] = m_scr[...]


def mlstm_recurrence(proj, gate_bias, norm_w, c0, n0, m0, n_seq, T, row0, TB, t_valid):
    H, E, V = C_HEADS, C_QK_DIM, C_V_DIM
    nb = T // TB
    rb0 = row0 // TB
    HE, HV = H * E, H * V

    def cols(cb):
        return lambda b, t: (rb0 + b * nb + t, cb)

    st4 = lambda b, t: (b, 0, 0, 0)
    st3 = lambda b, t: (b, 0, 0)
    return pl.pallas_call(
        functools.partial(_mlstm_body, L=CHUNK, TB=TB, t_valid=t_valid), grid=(n_seq, nb),
        in_specs=[pl.BlockSpec((TB, HE), cols(0)), pl.BlockSpec((TB, HE), cols(1)),
                  pl.BlockSpec((TB, HV), cols(2 * HE // HV)), pl.BlockSpec((TB, HV), cols(2 * HE // HV + 1)),
                  pl.BlockSpec((TB, LANES), cols((2 * HE + 2 * HV) // LANES)),
                  pl.BlockSpec((1, LANES), lambda b, t: (0, 0)), pl.BlockSpec((1, HV), lambda b, t: (0, 0)),
                  pl.BlockSpec((None, H, V, E), st4), pl.BlockSpec((None, H, E), st3),
                  pl.BlockSpec((None, H, LANES), st3)],
        out_specs=[pl.BlockSpec((TB, HV), lambda b, t: (b * nb + t, 0)),
                   pl.BlockSpec((None, H, V, E), st4), pl.BlockSpec((None, H, E), st3),
                   pl.BlockSpec((None, H, LANES), st3)],
        out_shape=[jax.ShapeDtypeStruct((n_seq * T, HV), _row_dtype(TB)), jax.ShapeDtypeStruct((n_seq, H, V, E), F32),
                   jax.ShapeDtypeStruct((n_seq, H, E), F32), jax.ShapeDtypeStruct((n_seq, H, LANES), F32)],
        scratch_shapes=[pltpu.VMEM((H, V, E), F32), pltpu.VMEM((H, E), F32), pltpu.VMEM((H, LANES), F32)],
        compiler_params=_params(("parallel", "arbitrary")), name="mlstm_recurrence",
    )(proj, proj, proj, proj, proj, gate_bias, norm_w, c0, n0, m0)


def _row_tile(m, target, mult):
    return max(t for t in range(mult, min(m, target) + 1, mult) if m % t == 0)


def _tile(n, target):
    return max(t for t in range(LANES, min(n, target) + 1, LANES) if n % t == 0)


def _pad_cols(w, n):
    return jnp.pad(w, ((0, 0), (0, n - w.shape[1])))


def _pad_rows(w, n):
    return jnp.pad(w, ((0, n - w.shape[0]), (0, 0)))


def kernel(x_prompt, x_sample, state_a_wkv, state_a_shift, cache_b_kv_g0, cache_b_kv_g1, cache_b_kv_g2, state_c_C, state_c_n, state_c_m, rel_bias, norm_ffn1, ffn1_w_in, ffn1_w_out, norm_mix, norm_ffn2, ffn2_w_in, ffn2_w_out, norm_final, a_mu, a_w_rkv, a_w0, a_w1, a_w2, a_a0, a_a1, a_a2, a_g1, a_g2, a_k_k, a_k_a, a_r_k, a_gn_w, a_gn_b, a_w_out, a_v0, a_v1, a_v2, b_w_qkv, b_w_out, c_w_in, c_b_gates, c_norm_w, c_w_out):
    Bp, Tp, D = x_prompt.shape
    Bs, Ts, _ = x_sample.shape
    depth = norm_mix.shape[0]
    Tsp = SAMPLE_PAD
    Mp, Ms = Bp * Tp, Bs * Tsp
    M = Mp + Ms
    TM = _row_tile(M, 768, SUBLANES_BF16)
    TM_LORA = _row_tile(M, 384, SUBLANES_F32)
    TM_OUT = _row_tile(Mp, 512, SUBLANES_F32)
    TN = _tile(D, 2048)
    TN_QKV = _tile(b_w_qkv.shape[2], 2048)
    TF = _tile(ffn1_w_out.shape[1], 512)
    TB_A, TB_C = min(Tp, 128), min(Tp, 256)
    H_a = D // A_HEAD_DIM
    G, H_b, Dh = len(B_DILATIONS), B_HEADS, B_HEAD_DIM
    bf = lambda w: w.astype(BF16)

    x = jnp.concatenate([x_prompt.reshape(Mp, D),
                         jnp.pad(x_sample, ((0, 0), (0, Tsp - Ts), (0, 0))).reshape(Ms, D)], axis=0)

    def last_rows(t):
        return (jnp.stack([t[(b + 1) * Tp - 1] for b in range(Bp)]),
                jnp.stack([t[Mp + b * Tsp + Ts - 1] for b in range(Bs)]))

    qi = jnp.arange(B_BLK)[:, None]
    kj = jnp.arange(2 * B_BLK)[None, :]
    step = qi + B_BLK - kj
    step_ok = (step >= 0) & (step <= B_BLK)
    m_desc = B_BLK - jnp.arange(B_BLK + 8)
    bias_p, bias_s = [], []
    buckets = jnp.arange(N_BUCKETS)
    for gi, dil in enumerate(B_DILATIONS):
        tab = rel_bias[:, gi * H_b:(gi + 1) * H_b].astype(F32)
        hot = (_rel_bucket(jnp.clip(step, 0, B_BLK) * dil)[None] == buckets[:, None, None]).astype(F32)
        bp = jnp.einsum("nh,nqk->hqk", tab, hot, precision=HIGHEST)
        bias_p.append(jnp.where(step_ok[None], bp, -jnp.inf))
        bs = tab[_rel_bucket(jnp.maximum(m_desc, 0) * dil)]
        bias_s.append(jnp.broadcast_to(bs[:, :, None], (B_BLK + 8, H_b, Dh)))
    bias_p, bias_s = jnp.stack(bias_p), jnp.stack(bias_s)
    w_rkv = bf(a_w_rkv)

    outs_a_wkv, outs_a_shift, outs_c = ([], []), ([], []), ([], [], [], [], [], [])
    outs_b = [([], []) for _ in range(G)]
    v_first = None
    for i in range(depth):
        x = ffn(x, norm_ffn1[i], ffn1_w_in, ffn1_w_out, i, TM, TF)
        kind, j = i % 3, i // 3
        if kind == 0:
            u, u_prev = rmsnorm_shift(x, norm_mix[i], state_a_shift[j], Tp, Mp, Tsp, TM)
            mu = a_mu[j]
            rkv = rwkv_rkv(u, u_prev, mu[jnp.array([0, 2, 3])][:, None, :], w_rkv, j, TM, TN)
            lr = LANES
            w_br = (bf(_pad_cols(a_w1[j], lr)), bf(_pad_rows(a_w2[j], lr)), a_w0[j].reshape(1, D))
            a_br = (bf(_pad_cols(a_a1[j], lr)), bf(_pad_rows(a_a2[j], lr)), a_a0[j].reshape(1, D))
            g_br = (bf(a_g1[j]), bf(a_g2[j]))
            v_br = None
            if j > 0:
                v_br = (bf(_pad_cols(a_v1[j - 1], lr)), bf(_pad_rows(a_v2[j - 1], lr)), a_v0[j - 1].reshape(1, D))
            lora = rwkv_lora(u, u_prev, mu[jnp.array([1, 4, 5, 3])], w_br, a_br, g_br, v_br, TM_LORA)
            lw, a_lr, gate = lora[:3]
            vres = None if j == 0 else (v_first, lora[3])
            if j == 0:
                v_first = rkv
            par = tuple(p.reshape(1, D) for p in (a_k_k[j], a_k_a[j], a_r_k[j], a_gn_w[j], a_gn_b[j]))
            s0p = jnp.zeros((Bp * H_a, A_HEAD_DIM, A_HEAD_DIM), F32)
            s0s = state_a_wkv[j].reshape(Bs * H_a, A_HEAD_DIM, A_HEAD_DIM)
            yp, sp = rwkv_recurrence(rkv, lw, a_lr, gate, vres, par, s0p, Bp, Tp, 0, TB_A, None)
            ys, ss = rwkv_recurrence(rkv, lw, a_lr, gate, vres, par, s0s, Bs, Tsp, Mp, Tsp, Ts)
            x = matmul((yp, ys), bf(a_w_out[j]), TM, TN, residual=x)
            outs_a_wkv[0].append(sp.reshape(Bp, H_a, A_HEAD_DIM, A_HEAD_DIM))
            outs_a_wkv[1].append(ss.reshape(Bs, H_a, A_HEAD_DIM, A_HEAD_DIM))
            sh_p, sh_s = last_rows(u)
            outs_a_shift[0].append(sh_p)
            outs_a_shift[1].append(sh_s)
        elif kind == 1:
            qkv = matmul(x, bf(b_w_qkv[j]), TM, TN_QKV, norm_g=norm_mix[i])
            caches = [c[j].reshape(Bs, B_BLK, d, 2, H_b, Dh)
                      for c, d in zip((cache_b_kv_g0, cache_b_kv_g1, cache_b_kv_g2), B_DILATIONS)]
            op = attn_prompt(qkv, bias_p, Bp, Tp)
            os_ = attn_sample(qkv[Mp:], caches, bias_s, Bs, Tsp, 0, Ts)
            x = matmul((op, os_.reshape(Ms, H_b * Dh)), bf(b_w_out[j]), TM, TN, residual=x)
            for gi in range(G):
                keep = min(B_WINDOWS[gi], Tp)
                c0, c1 = (gi * 3 + 1) * H_b * Dh, (gi * 3 + 3) * H_b * Dh
                kv_p = jnp.stack([lax.slice(qkv, ((b + 1) * Tp - keep, c0), ((b + 1) * Tp, c1)) for b in range(Bp)])
                kv_s = lax.slice(qkv, (Mp, c0), (M, c1)).reshape(Bs, Tsp, c1 - c0)[:, :Ts]
                outs_b[gi][0].append(kv_p.reshape(Bp, keep, 2, H_b, Dh))
                outs_b[gi][1].append(kv_s.reshape(Bs, Ts, 2, H_b, Dh))
        else:
            H, E, V = C_HEADS, C_QK_DIM, C_V_DIM
            n_in = c_w_in.shape[2]
            n_pad = -(-n_in // LANES) * LANES
            proj = matmul(x, bf(_pad_cols(c_w_in[j], n_pad)), TM, _tile(n_pad, 1024), norm_g=norm_mix[i])
            gbias = _pad_cols(c_b_gates[j].reshape(1, 2 * H), LANES)
            nw = c_norm_w[j].reshape(1, H * V)
            zc = (jnp.zeros((Bp, H, V, E), F32), jnp.zeros((Bp, H, E), F32), jnp.zeros((Bp, H, LANES), F32))
            sc = (state_c_C[j], state_c_n[j], jnp.broadcast_to(state_c_m[j][:, :, None], (Bs, H, LANES)))
            hp, cp, np_, mp = mlstm_recurrence(proj, gbias, nw, *zc, Bp, Tp, 0, TB_C, None)
            hs, cs, ns, ms = mlstm_recurrence(proj, gbias, nw, *sc, Bs, Tsp, Mp, Tsp, Ts)
            x = matmul((hp, hs), bf(c_w_out[j]), TM, TN, residual=x)
            for lst, val in zip(outs_c, (cp, cs, np_, ns, mp[:, :, 0], ms[:, :, 0])):
                lst.append(val)
        x = ffn(x, norm_ffn2[i], ffn2_w_in, ffn2_w_out, i, TM, TF)

    y_prompt = rmsnorm(x, norm_final, TM_OUT, 0, Mp).reshape(Bp, Tp, D)
    y_sample = rmsnorm(x, norm_final, Ms, Mp, Ms).reshape(Bs, Tsp, D)[:, :Ts]
    st = jnp.stack
    return (y_prompt, y_sample, st(outs_a_wkv[0]), st(outs_a_wkv[1]), st(outs_a_shift[0]), st(outs_a_shift[1]),
            st(outs_b[0][0]), st(outs_b[0][1]), st(outs_b[1][0]), st(outs_b[1][1]), st(outs_b[2][0]), st(outs_b[2][1]),
            st(outs_c[0]), st(outs_c[1]), st(outs_c[2]), st(outs_c[3]), st(outs_c[4]), st(outs_c[5]))
```

```python
import functools
import math

import jax
import jax.numpy as jnp
from jax import lax
from jax.experimental import pallas as pl
from jax.experimental.pallas import tpu as pltpu

F32 = jnp.float32
BF16 = jnp.bfloat16
HIGHEST = lax.Precision.HIGHEST

NORM_EPS = 1e-6
A_HEAD_DIM = 64
A_GN_EPS = 64e-5
A_REC_LANES = 2048
B_WINDOWS = (128, 512, 2048)
B_DILATIONS = (1, 4, 16)
B_HEADS = 16
B_HEAD_DIM = 128
B_BLK = 128
B_UNROLL = 4
N_BUCKETS = 32
BUCKET_MAX_DIST = 2048
C_HEADS = 8
C_QK_DIM = 128
C_V_DIM = 256
C_GATE_CAP = 15.0
CHUNK = 64
SAMPLE_PAD = 8
LANES = 128
SUBLANES_F32 = 8
SUBLANES_BF16 = 16
VMEM_LIMIT = 56 * 1024 * 1024


def _row_dtype(rows):
    return BF16 if rows % SUBLANES_BF16 == 0 else F32


def _params(sem):
    return pltpu.CompilerParams(dimension_semantics=sem, vmem_limit_bytes=VMEM_LIMIT)


def _dot(a, b, precision=None):
    return jnp.dot(a, b, preferred_element_type=F32, precision=precision)


def _dot_nt(a, b):
    return lax.dot_general(a, b, (((1,), (1,)), ((), ())), preferred_element_type=F32)


def _dot_tn(a, b):
    return lax.dot_general(a, b, (((0,), (0,)), ((), ())), preferred_element_type=F32)


def _bf16_operands(dot):
    return lambda a, b: dot(a.astype(BF16), b.astype(BF16))


def _sigmoid(x):
    return 1.0 / (1.0 + jnp.exp(-x))


def _rms(x, g):
    ms = jnp.mean(x * x, axis=-1, keepdims=True)
    return x * lax.rsqrt(ms + NORM_EPS) * g


def _rmsnorm_body(x_ref, g_ref, o_ref):
    o_ref[...] = _rms(x_ref[...], g_ref[...]).astype(o_ref.dtype)


def rmsnorm(x, g, tm, row0, rows):
    D = x.shape[1]
    rb0 = row0 // tm
    return pl.pallas_call(
        _rmsnorm_body, grid=(rows // tm,),
        in_specs=[pl.BlockSpec((tm, D), lambda i: (rb0 + i, 0)), pl.BlockSpec((1, D), lambda i: (0, 0))],
        out_specs=pl.BlockSpec((tm, D), lambda i: (i, 0)),
        out_shape=jax.ShapeDtypeStruct((rows, D), F32),
        compiler_params=_params(("parallel",)), name="rmsnorm",
    )(x, g.reshape(1, D))


def _rmsnorm_shift_body(x_ref, xp_ref, g_ref, st_ref, u_ref, up_ref, *, tm, grp, seq_len, seq_rows):
    i = pl.program_id(0)
    g = g_ref[...]
    u = _rms(x_ref[...], g)
    u_ref[...] = u
    up_ref[...] = pltpu.roll(u, 1, axis=0)
    tail = _rms(xp_ref[...], g)[-1:, :]
    n_late = st_ref.shape[0]
    for k in range(tm // grp):
        row0 = i * tm + k * grp
        late = row0 >= seq_rows
        is_start = late | (lax.rem(row0, seq_len) == 0)
        state = st_ref[pl.ds(jnp.clip((row0 - seq_rows) // grp, 0, n_late - 1), 1), :]
        before = tail if k == 0 else u[k * grp - 1:k * grp, :]
        up_ref[k * grp:k * grp + 1, :] = jnp.where(is_start, jnp.where(late, state, 0.0), before)


def rmsnorm_shift(x, g, late_states, seq_len, seq_rows, grp, tm):
    M, D = x.shape
    sub = SUBLANES_F32
    return pl.pallas_call(
        functools.partial(_rmsnorm_shift_body, tm=tm, grp=grp, seq_len=seq_len, seq_rows=seq_rows), grid=(M // tm,),
        in_specs=[pl.BlockSpec((tm, D), lambda i: (i, 0)),
                  pl.BlockSpec((sub, D), lambda i: (jnp.maximum(i * (tm // sub) - 1, 0), 0)),
                  pl.BlockSpec((1, D), lambda i: (0, 0)),
                  pl.BlockSpec(late_states.shape, lambda i: (0, 0))],
        out_specs=[pl.BlockSpec((tm, D), lambda i: (i, 0))] * 2,
        out_shape=[jax.ShapeDtypeStruct((M, D), F32)] * 2,
        compiler_params=_params(("parallel",)), name="rmsnorm_shift",
    )(x, x, g.reshape(1, D), late_states)


def _ffn_body(x_ref, g_ref, wg_ref, wu_ref, wo_ref, o_ref, xn_ref):
    j = pl.program_id(1)

    @pl.when(j == 0)
    def _():
        xn_ref[...] = _rms(x_ref[...], g_ref[...]).astype(BF16)
        o_ref[...] = jnp.zeros_like(o_ref)

    xn = xn_ref[...]
    gate = _dot(xn, wg_ref[...].astype(BF16))
    up = _dot(xn, wu_ref[...].astype(BF16))
    h = (gate * _sigmoid(gate) * up).astype(BF16)
    o_ref[...] += _dot(h, wo_ref[...].astype(BF16))

    @pl.when(j == pl.num_programs(1) - 1)
    def _():
        o_ref[...] = x_ref[...] + 0.5 * o_ref[...]


def ffn(x, g, w_in, w_out, layer, tm, tf):
    M, D = x.shape
    Fh = w_out.shape[1]
    nf = Fh // tf
    return pl.pallas_call(
        _ffn_body, grid=(M // tm, nf),
        in_specs=[pl.BlockSpec((tm, D), lambda i, j: (i, 0)),
                  pl.BlockSpec((1, D), lambda i, j: (0, 0)),
                  pl.BlockSpec((None, D, tf), lambda i, j: (layer, 0, j)),
                  pl.BlockSpec((None, D, tf), lambda i, j: (layer, 0, j + nf)),
                  pl.BlockSpec((None, tf, D), lambda i, j: (layer, j, 0))],
        out_specs=pl.BlockSpec((tm, D), lambda i, j: (i, 0)),
        out_shape=jax.ShapeDtypeStruct((M, D), F32),
        scratch_shapes=[pltpu.VMEM((tm, D), BF16)],
        compiler_params=_params(("parallel", "arbitrary")), name="ffn",
    )(x, g.reshape(1, D), w_in, w_in, w_out)


def _mm_body(*refs, nb0, has_norm, has_res):
    it = iter(refs)
    x_ref = next(it)
    x1_ref = next(it) if nb0 is not None else None
    g_ref = next(it) if has_norm else None
    w_ref = next(it)
    res_ref = next(it) if has_res else None
    o_ref = next(it)
    xs_ref = next(it)

    def stage(ref, rows=slice(None), dst=slice(None)):
        x = ref[rows, :].astype(F32)
        if has_norm:
            x = _rms(x, g_ref[...])
        xs_ref[dst, :] = x.astype(BF16)

    @pl.when(pl.program_id(1) == 0)
    def _():
        if nb0 is None:
            stage(x_ref)
        else:
            rem = xs_ref.shape[0] - x1_ref.shape[0]
            pl.when(pl.program_id(0) < nb0)(lambda: stage(x_ref))

            @pl.when(pl.program_id(0) == nb0)
            def _():
                if rem > 0:
                    stage(x_ref, slice(0, rem), slice(0, rem))
                stage(x1_ref, slice(None), slice(rem, None))

    acc = _dot(xs_ref[...], w_ref[...])
    if has_res:
        acc = res_ref[...] + acc
    o_ref[...] = acc.astype(o_ref.dtype)


def matmul(x, w, tm, tn, norm_g=None, residual=None, out_dtype=F32):
    nb0 = None
    if isinstance(x, tuple):
        x0, x1 = x
        nb0 = x0.shape[0] // tm
        M, K = x0.shape[0] + x1.shape[0], x0.shape[1]
        assert M == (nb0 + 1) * tm and x1.shape[0] <= tm and x1.shape[0] % SUBLANES_BF16 == 0
        args = [x0, x1]
        last0 = -(-x0.shape[0] // tm) - 1
        specs = [pl.BlockSpec((tm, K), lambda i, j: (jnp.minimum(i, last0), 0)),
                 pl.BlockSpec(x1.shape, lambda i, j: (0, 0))]
    else:
        M, K = x.shape
        args = [x]
        specs = [pl.BlockSpec((tm, K), lambda i, j: (i, 0))]
    N = w.shape[1]
    if norm_g is not None:
        args.append(norm_g.reshape(1, K))
        specs.append(pl.BlockSpec((1, K), lambda i, j: (0, 0)))
    args.append(w)
    specs.append(pl.BlockSpec((K, tn), lambda i, j: (0, j)))
    if residual is not None:
        args.append(residual)
        specs.append(pl.BlockSpec((tm, tn), lambda i, j: (i, j)))
    return pl.pallas_call(
        functools.partial(_mm_body, nb0=nb0, has_norm=norm_g is not None, has_res=residual is not None),
        grid=(M // tm, N // tn), in_specs=specs,
        out_specs=pl.BlockSpec((tm, tn), lambda i, j: (i, j)),
        out_shape=jax.ShapeDtypeStruct((M, N), out_dtype),
        scratch_shapes=[pltpu.VMEM((tm, K), BF16)],
        compiler_params=_params(("parallel", "arbitrary")), name="matmul",
    )(*args)


def _rkv_body(u_ref, up_ref, mu_ref, w_ref, o_ref, xs_ref):
    @pl.when(pl.program_id(2) == 0)
    def _():
        u = u_ref[...]
        xs_ref[...] = (u + (up_ref[...] - u) * mu_ref[...]).astype(BF16)

    o_ref[...] = _dot(xs_ref[...], w_ref[...])


def rwkv_rkv(u, u_prev, mu3, w, layer, tm, tn):
    M, D = u.shape
    return pl.pallas_call(
        _rkv_body, grid=(M // tm, 3, D // tn),
        in_specs=[pl.BlockSpec((tm, D), lambda i, k, j: (i, 0)),
                  pl.BlockSpec((tm, D), lambda i, k, j: (i, 0)),
                  pl.BlockSpec((None, 1, D), lambda i, k, j: (k, 0, 0)),
                  pl.BlockSpec((None, None, D, tn), lambda i, k, j: (layer, k, 0, j))],
        out_specs=pl.BlockSpec((None, tm, tn), lambda i, k, j: (k, i, j)),
        out_shape=jax.ShapeDtypeStruct((3, M, D), F32),
        scratch_shapes=[pltpu.VMEM((tm, D), BF16)],
        compiler_params=_params(("parallel", "arbitrary", "arbitrary")), name="rwkv_rkv",
    )(u, u_prev, mu3, w)


def _lora_body(*refs, has_vres):
    it = iter(refs)
    u_ref, up_ref, mu_ref = next(it), next(it), next(it)
    w1, w2, w0 = next(it), next(it), next(it)
    a1, a2, a0 = next(it), next(it), next(it)
    g1, g2 = next(it), next(it)
    if has_vres:
        v1, v2, v0 = next(it), next(it), next(it)
    lw_ref, a_ref, g_ref = next(it), next(it), next(it)
    nu_ref = next(it) if has_vres else None

    u = u_ref[...]
    du = up_ref[...] - u

    def mix(n):
        return (u + du * mu_ref[n:n + 1, :]).astype(BF16)

    hw = jnp.tanh(_dot(mix(0), w1[...])).astype(BF16)
    w_pre = w0[...] + _dot(hw, w2[...])
    softplus = jnp.maximum(-w_pre, 0.0) + jnp.log(1.0 + jnp.exp(-jnp.abs(w_pre)))
    lw_ref[...] = -jnp.exp(-softplus - 0.5)
    ha = _dot(mix(1), a1[...]).astype(BF16)
    a_ref[...] = _sigmoid(a0[...] + _dot(ha, a2[...]))
    hg = _sigmoid(_dot(mix(2), g1[...])).astype(BF16)
    g_ref[...] = _dot(hg, g2[...])
    if has_vres:
        hv = _dot(mix(3), v1[...]).astype(BF16)
        nu_ref[...] = _sigmoid(v0[...] + _dot(hv, v2[...]))


def rwkv_lora(u, u_prev, mu4, w, a, g, v, tm):
    M, D = u.shape
    has_vres = v is not None
    row = lambda i: (i, 0)
    full = lambda i: (0, 0)
    args = [u, u_prev, mu4]
    specs = [pl.BlockSpec((tm, D), row), pl.BlockSpec((tm, D), row), pl.BlockSpec(mu4.shape, full)]
    for t in (w, a, g) + ((v,) if has_vres else ()):
        for m in t:
            args.append(m)
            specs.append(pl.BlockSpec(m.shape, full))
    n_out = 4 if has_vres else 3
    return pl.pallas_call(
        functools.partial(_lora_body, has_vres=has_vres), grid=(M // tm,), in_specs=specs,
        out_specs=[pl.BlockSpec((tm, D), row)] * n_out,
        out_shape=[jax.ShapeDtypeStruct((M, D), F32)] * n_out,
        compiler_params=_params(("parallel",)), name="rwkv_lora",
    )(*args)


def _rwkv_rec_body(*refs, L, TB, hb, t_valid, has_vres):
    N = A_HEAD_DIM
    it = iter(refs)
    r_ref, k_ref, v_ref, lw_ref, a_ref, g_ref = (next(it) for _ in range(6))
    if has_vres:
        vf_ref, nu_ref = next(it), next(it)
    kk_ref, ka_ref, rk_ref, gnw_ref, gnb_ref, s0_ref = (next(it) for _ in range(6))
    y_ref, sT_ref, S_scr = next(it), next(it), next(it)
    tb = pl.program_id(2)

    assert L == N and 2 * N == LANES

    @pl.when(tb == 0)
    def _():
        for p in range(hb // 2):
            S_scr[p] = jnp.concatenate([s0_ref[2 * p], s0_ref[2 * p + 1]], axis=1)

    row = lax.broadcasted_iota(jnp.int32, (L, L), 0)
    col = lax.broadcasted_iota(jnp.int32, (L, L), 1)
    tril = (row >= col).astype(F32)
    row1 = lax.broadcasted_iota(jnp.int32, (L, 2 * N), 0)
    lane1 = lax.broadcasted_iota(jnp.int32, (L, 2 * N), 1)
    head0_lane = lane1 < N
    eye = ((lane1 & (N - 1)) == row1).astype(F32)
    row2 = lax.broadcasted_iota(jnp.int32, (2 * L, 2 * N), 0)
    lane2 = lax.broadcasted_iota(jnp.int32, (2 * L, 2 * N), 1)
    mask2 = (lane2 & (N - 1)) < jnp.where(row2 < L, row2, row2 - L + 1)
    bd_mask = (row2 // L) == (lane2 // N)
    ones_bd = bd_mask.astype(BF16)
    ones_bd2 = jnp.concatenate([ones_bd, ones_bd], axis=0)
    n_sq = int(math.log2(L)) - 1
    kk_p, ka_p, rk_p, gnw, gnb = kk_ref[...], ka_ref[...], rk_ref[...], gnw_ref[...], gnb_ref[...]

    def chunk(c, carry):
        sl = pl.ds(pl.multiple_of(c * L, L), L)

        def load(ref):
            if TB >= L:
                return ref[sl, :]
            return jnp.concatenate([ref[...], jnp.zeros((L - TB, ref.shape[1]), F32)], axis=0)

        r, k, v, lw, a, g = (load(ref) for ref in (r_ref, k_ref, v_ref, lw_ref, a_ref, g_ref))
        if has_vres:
            v = v + (load(vf_ref) - v) * load(nu_ref)
        if t_valid is not None:
            t_idx = tb * TB + c * L + lax.broadcasted_iota(jnp.int32, (L, 1), 0)
            valid = t_idx < t_valid
            r, k, v, lw = (jnp.where(valid, t, 0.0) for t in (r, k, v, lw))
        cum = _dot(tril, lw, HIGHEST)
        cum_end = cum[L - 1:L, :]
        w_cur, w_prev, w_inv, w_rem, w_end = (jnp.exp(cum), jnp.exp(cum - lw), jnp.exp(-cum),
                                              jnp.exp(cum_end - cum), jnp.exp(cum_end))
        mm, mm_nt, mm_tn = (_bf16_operands(d) for d in (_dot, _dot_nt, _dot_tn))
        pairs = range(hb // 2)
        ps = [slice(p * LANES, (p + 1) * LANES) for p in pairs]

        def head_sum(x):
            x_hi = x.astype(BF16)
            x_lo = (x - x_hi.astype(F32)).astype(BF16)
            return _dot(jnp.concatenate([x_hi, x_lo], axis=1), ones_bd2)

        def bdiag(x):
            return jnp.where(bd_mask, jnp.concatenate([x, x], axis=0), 0.0)

        kkp = [k[:, s] * kk_p[:, s] for s in ps]
        k2 = [k[:, s] * (1.0 + (a[:, s] - 1.0) * ka_p[:, s]) for s in ps]
        sums = [head_sum(jnp.concatenate([kkp[p] * kkp[p], r[:, ps[p]] * k2[p] * rk_p[:, ps[p]]], axis=0))
                for p in pairs]
        kk = [kkp[p] / jnp.maximum(jnp.sqrt(sums[p][:L]), 1e-12) for p in pairs]
        b = [kk[p] * a[:, ps[p]] for p in pairs]
        lhs2 = [jnp.concatenate([kk[p] * w_prev[:, ps[p]], r[:, ps[p]] * w_cur[:, ps[p]]], axis=0) for p in pairs]
        kd = [k2[p] * w_inv[:, ps[p]] for p in pairs]
        bd = [b[p] * w_inv[:, ps[p]] for p in pairs]
        kend = [k2[p] * w_rem[:, ps[p]] for p in pairs]
        bend = [b[p] * w_rem[:, ps[p]] for p in pairs]
        kkd = [x[:L] for x in lhs2]
        rd = [x[L:] for x in lhs2]
        a_kb2 = [mm_nt(lhs2[p], jnp.concatenate([bdiag(kd[p]), bdiag(bd[p])], axis=0)) for p in pairs]
        a_k = [jnp.where(mask2, x[:, :LANES], 0.0) for x in a_kb2]
        a_b = [jnp.where(mask2, x[:, LANES:], 0.0) for x in a_kb2]
        a_kb = [x[:L] for x in a_b]
        a_rb = [x[L:] for x in a_b]
        a_v = [mm(a_k[p], bdiag(v[:, ps[p]])) for p in pairs]
        t_inv = [eye - x for x in a_kb]
        pw = [mm(x, bdiag(x)) for x in a_kb]
        for _ in range(n_sq - 1):
            both = [mm(jnp.concatenate([pw[p], t_inv[p]], axis=0), bdiag(pw[p])) for p in pairs]
            t_inv = [t_inv[p] + both[p][L:] for p in pairs]
            pw = [x[:L] for x in both]
        t_inv = [t_inv[p] + mm(t_inv[p], bdiag(pw[p])) for p in pairs]
        ktcu = [mm(t_inv[p], jnp.concatenate([bdiag(kkd[p]), bdiag(a_v[p][:L])], axis=1)) for p in pairs]
        k_t = [x[:, :LANES] for x in ktcu]
        c_u = [x[:, LANES:] for x in ktcu]
        ykc = [mm(a_rb[p], jnp.concatenate([bdiag(k_t[p]), bdiag(c_u[p])], axis=1)) for p in pairs]
        y_k = [rd[p] - ykc[p][:, :LANES] for p in pairs]
        y_c = [a_v[p][L:] - ykc[p][:, LANES:] for p in pairs]
        S = [S_scr[p] for p in pairs]
        y = [mm_nt(y_k[p], bdiag(S[p])) + y_c[p] for p in pairs]
        upd = [mm_tn(jnp.concatenate([jnp.concatenate([v[:, ps[p]], jnp.zeros_like(k_t[p])], axis=1),
                                      jnp.concatenate([-c_u[p], k_t[p]], axis=1)], axis=0),
                     jnp.concatenate([kend[p], bend[p]], axis=0)) for p in pairs]
        S_c = [jnp.where(head0_lane, x[:N], x[N:LANES]) for x in upd]
        ktb = [jnp.where(bd_mask, x[LANES:], 0.0) for x in upd]
        for p in pairs:
            S_scr[p] = S[p] * w_end[:, ps[p]] - mm(S[p], ktb[p]) + S_c[p]
        mean = [head_sum(y[p]) * (1.0 / N) for p in pairs]
        var = [head_sum(jnp.square(y[p] - mean[p])) * (1.0 / N) for p in pairs]
        bonus = [sums[p][L:] * v[:, ps[p]] for p in pairs]
        outs = [((y[p] - mean[p]) * lax.rsqrt(var[p] + A_GN_EPS) * gnw[:, ps[p]] + gnb[:, ps[p]] + bonus[p])
                * g[:, ps[p]] for p in pairs]
        y_out = jnp.concatenate(outs, axis=-1).astype(y_ref.dtype)
        if TB >= L:
            y_ref[sl, :] = y_out
        else:
            y_ref[...] = y_out[:TB]
        return carry

    lax.fori_loop(0, max(TB // L, 1), chunk, 0)

    @pl.when(tb == pl.num_programs(2) - 1)
    def _():
        for p in range(hb // 2):
            S = S_scr[p]
            sT_ref[2 * p] = S[:, :N]
            sT_ref[2 * p + 1] = S[:, N:]


def rwkv_recurrence(rkv, lw, a, g, vres, params, s0, n_seq, T, row0, TB, t_valid):
    _, M, D = rkv.shape
    N = A_HEAD_DIM
    LW = min(A_REC_LANES, D)
    hb = LW // N
    nb = T // TB
    rb0 = row0 // TB
    has_vres = vres is not None
    seq = lambda b, h, t: (rb0 + b * nb + t, h)
    args, specs = [], []
    for n in range(3):
        args.append(rkv)
        specs.append(pl.BlockSpec((None, TB, LW), lambda b, h, t, n=n: (n, rb0 + b * nb + t, h)))
    for x in (lw, a, g):
        args.append(x)
        specs.append(pl.BlockSpec((TB, LW), seq))
    if has_vres:
        args += list(vres)
        specs += [pl.BlockSpec((None, TB, LW), lambda b, h, t: (2, rb0 + b * nb + t, h)), pl.BlockSpec((TB, LW), seq)]
    for p in params:
        args.append(p)
        specs.append(pl.BlockSpec((1, LW), lambda b, h, t: (0, h)))
    args.append(s0)
    specs.append(pl.BlockSpec((hb, N, N), lambda b, h, t: (b * (D // LW) + h, 0, 0)))
    return pl.pallas_call(
        functools.partial(_rwkv_rec_body, L=CHUNK, TB=TB, hb=hb, t_valid=t_valid, has_vres=has_vres),
        grid=(n_seq, D // LW, nb), in_specs=specs,
        out_specs=[pl.BlockSpec((TB, LW), lambda b, h, t: (b * nb + t, h)),
                   pl.BlockSpec((hb, N, N), lambda b, h, t: (b * (D // LW) + h, 0, 0))],
        out_shape=[jax.ShapeDtypeStruct((n_seq * T, D), _row_dtype(TB)),
                   jax.ShapeDtypeStruct((n_seq * (D // N), N, N), F32)],
        scratch_shapes=[pltpu.VMEM((hb // 2, N, 2 * N), F32)],
        compiler_params=_params(("parallel", "parallel", "arbitrary")), name="rwkv_recurrence",
    )(*args)


def _rel_bucket(dist):
    exact = N_BUCKETS // 2
    d = jnp.maximum(dist, 1).astype(F32)
    log_b = exact + (jnp.log(d / exact) / math.log(BUCKET_MAX_DIST / exact) * (N_BUCKETS - exact)).astype(jnp.int32)
    return jnp.where(dist < exact, dist, jnp.minimum(log_b, N_BUCKETS - 1))


def _attn_prompt_body(q_ref, k_ref, v_ref, bias_ref, o_ref, acc_ref, m_ref, l_ref, kd_ref, vd_ref, *, T):
    step = pl.program_id(2)
    blk = B_BLK
    G = len(B_DILATIONS)
    scale = B_HEAD_DIM ** -0.5
    first_keys = lax.broadcasted_iota(jnp.int32, (blk, 2 * blk), 1) < blk
    ones_cols = jnp.ones((2 * blk, B_HEAD_DIM), BF16)

    for si, dil in enumerate(reversed(B_DILATIONS)):
        @pl.when(step == si)
        def _(gi=si, dil=dil):
            span = blk * dil
            unroll = B_UNROLL if gi > 0 else 2 * B_UNROLL
            res_rows = T // dil + blk
            bias = bias_ref[...]

            def where(idx):
                n = idx // dil
                r = idx - n * dil
                return n, n * span + r, pl.multiple_of(r * res_rows + n * blk, blk)

            for r in range(dil):
                kd_ref[r * res_rows:r * res_rows + blk, :] = jnp.zeros((blk, B_HEAD_DIM), BF16)
                vd_ref[r * res_rows:r * res_rows + blk, :] = jnp.zeros((blk, B_HEAD_DIM), BF16)

            def stage(it, carry):
                for u in range(unroll):
                    _, start, dst = where(it * unroll + u)
                    rows = pl.ds(start, blk, stride=dil)
                    kd_ref[pl.ds(dst + blk, blk), :] = k_ref[rows, :].astype(BF16)
                    vd_ref[pl.ds(dst + blk, blk), :] = v_ref[rows, :].astype(BF16)
                return carry

            lax.fori_loop(0, T // (blk * unroll), stage, 0)

            def blocks(it, carry):
                us = range(unroll)
                pos = [where(it * unroll + u) for u in us]
                n = [x[0] for x in pos]
                cur = [pl.ds(x[1], blk, stride=dil) for x in pos]
                q = [(q_ref[cur[u], :] * scale).astype(BF16) for u in us]
                kcat = [kd_ref[pl.ds(x[2], 2 * blk), :] for x in pos]
                vcat = [vd_ref[pl.ds(x[2], 2 * blk), :] for x in pos]
                logits = [_dot_nt(q[u], kcat[u]) + bias for u in us]
                logits = [jnp.where(first_keys & (n[u] == 0), -jnp.inf, logits[u]) for u in us]
                mx = [jnp.max(x, axis=-1, keepdims=True) for x in logits]
                p = [jnp.exp(logits[u] - mx[u]) for u in us]
                pvd = [_dot(p[u].astype(BF16), jnp.concatenate([vcat[u], ones_cols], axis=1)) for u in us]
                pv = [x[:, :B_HEAD_DIM] for x in pvd]
                den = [x[:, B_HEAD_DIM:] for x in pvd]
                if gi > 0:
                    m_old = [m_ref[cur[u], :] for u in us]
                    l_old = [l_ref[cur[u], :] for u in us]
                    acc_old = [acc_ref[cur[u], :] for u in us]
                    m_new = [jnp.maximum(m_old[u], mx[u]) for u in us]
                    c_old = [jnp.exp(m_old[u] - m_new[u]) for u in us]
                    c_new = [jnp.exp(mx[u] - m_new[u]) for u in us]
                    pv = [acc_old[u] * c_old[u] + pv[u] * c_new[u] for u in us]
                    den = [l_old[u] * c_old[u] + den[u] * c_new[u] for u in us]
                    mx = m_new
                for u in us:
                    acc_ref[cur[u], :] = pv[u]
                    m_ref[cur[u], :] = mx[u]
                    l_ref[cur[u], :] = den[u]
                return carry

            lax.fori_loop(0, T // (blk * unroll), blocks, 0)

    @pl.when(step == G - 1)
    def _():
        o_ref[...] = (acc_ref[...] / l_ref[...]).astype(o_ref.dtype)


def attn_prompt(qkv, bias, n_seq, T):
    H, Dh, G = B_HEADS, B_HEAD_DIM, len(B_DILATIONS)

    def col(which):
        return lambda b, h, s: (b, ((G - 1 - s) * 3 + which) * H + h)

    staged_rows = T + B_BLK * max(B_DILATIONS)
    return pl.pallas_call(
        functools.partial(_attn_prompt_body, T=T), grid=(n_seq, H, G),
        in_specs=[pl.BlockSpec((T, Dh), col(0)), pl.BlockSpec((T, Dh), col(1)), pl.BlockSpec((T, Dh), col(2)),
                  pl.BlockSpec((None, None, B_BLK, 2 * B_BLK), lambda b, h, s: (G - 1 - s, h, 0, 0))],
        out_specs=pl.BlockSpec((T, Dh), lambda b, h, s: (b, h)),
        out_shape=jax.ShapeDtypeStruct((n_seq * T, H * Dh), BF16),
        scratch_shapes=[pltpu.VMEM((T, Dh), F32), pltpu.VMEM((T, 1), F32), pltpu.VMEM((T, Dh), F32),
                        pltpu.VMEM((staged_rows, Dh), BF16), pltpu.VMEM((staged_rows, Dh), BF16)],
        compiler_params=_params(("parallel", "parallel", "arbitrary")), name="attn_prompt",
    )(qkv, qkv, qkv, bias)


def _attn_sample_body(q_ref, k_ref, v_ref, c0_ref, c1_ref, c2_ref, bias_ref, o_ref, *, t_valid):
    blk = B_BLK
    scale = B_HEAD_DIM ** -0.5
    caches = (c0_ref, c1_ref, c2_ref)
    o_ref[...] = jnp.zeros_like(o_ref)
    for t in range(t_valid):
        m_run = l_run = acc = None
        for gi, dil in enumerate(B_DILATIONS):
            q = q_ref[t, gi] * scale
            c_ref = caches[gi]
            if dil == 1:
                kcat = jnp.concatenate([c_ref[t:, 0, 0], k_ref[:t + 1, gi]], axis=0)
                vcat = jnp.concatenate([c_ref[t:, 0, 1], v_ref[:t + 1, gi]], axis=0)
            else:
                kcat = jnp.concatenate([c_ref[:, t, 0], k_ref[t:t + 1, gi]], axis=0)
                vcat = jnp.concatenate([c_ref[:, t, 1], v_ref[t:t + 1, gi]], axis=0)
            logits = jnp.sum(q[None] * kcat, axis=-1, keepdims=True) + bias_ref[gi, :blk + 1]
            mx = jnp.max(logits, axis=0)
            p = jnp.exp(logits - mx[None])
            den = jnp.sum(p, axis=0)
            pv = jnp.sum(p * vcat, axis=0)
            if gi == 0:
                m_run, l_run, acc = mx, den, pv
            else:
                m_new = jnp.maximum(m_run, mx)
                c_old, c_new = jnp.exp(m_run - m_new), jnp.exp(mx - m_new)
                acc = acc * c_old + pv * c_new
                l_run = l_run * c_old + den * c_new
                m_run = m_new
        o_ref[t] = (acc / l_run).astype(o_ref.dtype)


def attn_sample(qkv, caches, bias, n_seq, T, row0, t_valid):
    H, Dh, G = B_HEADS, B_HEAD_DIM, len(B_DILATIONS)
    rb0 = row0 // T
    assert t_valid <= min(d for d in B_DILATIONS if d > 1)
    q5 = qkv.reshape(qkv.shape[0], G, 3, H, Dh)

    def spec(which):
        return pl.BlockSpec((T, G, None, H, Dh), lambda b: (b + rb0, 0, which, 0, 0))

    cache_specs = [pl.BlockSpec((None, B_BLK, min(d, t_valid), 2, H, Dh), lambda b: (b, 0, 0, 0, 0, 0))
                   for d in B_DILATIONS]
    return pl.pallas_call(
        functools.partial(_attn_sample_body, t_valid=t_valid), grid=(n_seq,),
        in_specs=[spec(0), spec(1), spec(2)] + cache_specs + [pl.BlockSpec(bias.shape, lambda b: (0, 0, 0, 0))],
        out_specs=pl.BlockSpec((T, H, Dh), lambda b: (b, 0, 0)),
        out_shape=jax.ShapeDtypeStruct((n_seq * T, H, Dh), BF16),
        compiler_params=_params(("parallel",)), name="attn_sample",
    )(q5, q5, q5, *caches, bias)


def _mlstm_body(q_ref, k_ref, v_ref, o_ref, gate_ref, gb_ref, nw_ref, c0_ref, n0_ref, m0_ref,
                y_ref, cT_ref, nT_ref, mT_ref, C_scr, n_scr, m_scr, *, L, TB, t_valid):
    H, E, V = C_HEADS, C_QK_DIM, C_V_DIM
    tb = pl.program_id(1)

    @pl.when(tb == 0)
    def _():
        C_scr[...] = c0_ref[...]
        n_scr[...] = n0_ref[...]
        m_scr[...] = m0_ref[...]

    row = lax.broadcasted_iota(jnp.int32, (L, L), 0)
    col = lax.broadcasted_iota(jnp.int32, (L, L), 1)
    causal = row >= col
    tril = causal.astype(F32)
    gb = gb_ref[...]
    nw = nw_ref[...]

    def chunk(c, carry):
        sl = pl.ds(pl.multiple_of(c * L, L), L)

        def load(ref, cols=slice(None)):
            if TB >= L:
                return ref[sl, cols]
            x = ref[:, cols]
            return jnp.concatenate([x, jnp.zeros((L - TB, x.shape[1]), F32)], axis=0)

        gact = C_GATE_CAP * jnp.tanh((load(gate_ref) + gb) / C_GATE_CAP)
        lf = jnp.minimum(gact, 0.0) - jnp.log(1.0 + jnp.exp(-jnp.abs(gact)))
        ig = gact
        valid = None
        if t_valid is not None:
            t_idx = tb * TB + c * L + lax.broadcasted_iota(jnp.int32, (L, 1), 0)
            valid = t_idx < t_valid
            ig = jnp.where(valid, ig, -1e30)
            lf = jnp.where(valid, lf, 0.0)
        bcum = _dot(tril, lf, HIGHEST)
        ig_t = ig.T
        bcum_t = bcum.T
        heads = range(H)
        es = [slice(h * E, (h + 1) * E) for h in heads]
        vs = [slice(h * V, (h + 1) * V) for h in heads]
        b_col = [bcum[:, H + h:H + h + 1] for h in heads]
        b_row = [bcum_t[H + h:H + h + 1, :] for h in heads]
        ig_col = [ig[:, h:h + 1] for h in heads]
        ig_row = [ig_t[h:h + 1, :] for h in heads]
        q = [load(q_ref, s) for s in es]
        k = [load(k_ref, s) * (E ** -0.5) for s in es]
        v = [load(v_ref, s) for s in vs]
        if valid is not None:
            q, k, v = ([jnp.where(valid, t, 0.0) for t in ts] for ts in (q, k, v))
        m_prev = [m_scr[h:h + 1, 0:1] for h in heads]
        n_prev = [n_scr[h:h + 1, :] for h in heads]
        C = [C_scr[h] for h in heads]
        dm = [jnp.where(causal, b_col[h] - b_row[h] + ig_row[h], -jnp.inf) for h in heads]
        inter = [b_col[h] + m_prev[h] for h in heads]
        mt = [jnp.maximum(inter[h], jnp.max(dm[h], axis=-1, keepdims=True)) for h in heads]
        w_d = [jnp.exp(dm[h] - mt[h]) for h in heads]
        w_i = [jnp.exp(inter[h] - mt[h]) for h in heads]
        qb, kb, vb = ([t.astype(BF16) for t in ts] for ts in (q, k, v))
        sc = [_dot_nt(qb[h], kb[h]) * w_d[h] for h in heads]
        qc = [_dot_nt(qb[h], C[h].astype(BF16)) for h in heads]
        num = [_dot(sc[h].astype(BF16), vb[h]) + w_i[h] * qc[h] for h in heads]
        den = [jnp.sum(sc[h], axis=-1, keepdims=True) + w_i[h] * jnp.sum(q[h] * n_prev[h], axis=-1, keepdims=True)
               for h in heads]
        hh = [num[h] / jnp.maximum(jnp.abs(den[h]), jnp.exp(-mt[h])) for h in heads]
        m_new = [x[L - 1:L, :] for x in mt]
        b_end = [x[L - 1:L, :] for x in b_col]
        w_s = [jnp.exp(b_end[h] - b_col[h] + ig_col[h] - m_new[h]) for h in heads]
        dec = [jnp.exp(b_end[h] + m_prev[h] - m_new[h]) for h in heads]
        c_upd = [_dot_tn((w_s[h] * v[h]).astype(BF16), kb[h]) for h in heads]
        outs = []
        for h in heads:
            C_scr[h] = dec[h] * C[h] + c_upd[h]
            n_scr[h:h + 1, :] = dec[h] * n_prev[h] + jnp.sum(w_s[h] * k[h], axis=0, keepdims=True)
            m_scr[h:h + 1, :] = jnp.broadcast_to(m_new[h], (1, LANES))
            hn = hh[h] * lax.rsqrt(jnp.mean(hh[h] * hh[h], axis=-1, keepdims=True) + NORM_EPS) * nw[:, vs[h]]
            outs.append(hn * _sigmoid(load(o_ref, vs[h])))
        y_out = jnp.concatenate(outs, axis=-1).astype(y_ref.dtype)
        if TB >= L:
            y_ref[sl, :] = y_out
        else:
            y_ref[...] = y_out[:TB]
        return carry

    lax.fori_loop(0, max(TB // L, 1), chunk, 0)

    @pl.when(tb == pl.num_programs(1) - 1)
    def _():
        cT_ref[...] = C_scr[...]
        nT_ref[...] = n_scr[...]
        mT_ref[...] = m_scr[...]


def mlstm_recurrence(proj, gate_bias, norm_w, c0, n0, m0, n_seq, T, row0, TB, t_valid):
    H, E, V = C_HEADS, C_QK_DIM, C_V_DIM
    nb = T // TB
    rb0 = row0 // TB
    HE, HV = H * E, H * V

    def cols(cb):
        return lambda b, t: (rb0 + b * nb + t, cb)

    st4 = lambda b, t: (b, 0, 0, 0)
    st3 = lambda b, t: (b, 0, 0)
    return pl.pallas_call(
        functools.partial(_mlstm_body, L=CHUNK, TB=TB, t_valid=t_valid), grid=(n_seq, nb),
        in_specs=[pl.BlockSpec((TB, HE), cols(0)), pl.BlockSpec((TB, HE), cols(1)),
                  pl.BlockSpec((TB, HV), cols(2 * HE // HV)), pl.BlockSpec((TB, HV), cols(2 * HE // HV + 1)),
                  pl.BlockSpec((TB, LANES), cols((2 * HE + 2 * HV) // LANES)),
                  pl.BlockSpec((1, LANES), lambda b, t: (0, 0)), pl.BlockSpec((1, HV), lambda b, t: (0, 0)),
                  pl.BlockSpec((None, H, V, E), st4), pl.BlockSpec((None, H, E), st3),
                  pl.BlockSpec((None, H, LANES), st3)],
        out_specs=[pl.BlockSpec((TB, HV), lambda b, t: (b * nb + t, 0)),
                   pl.BlockSpec((None, H, V, E), st4), pl.BlockSpec((None, H, E), st3),
                   pl.BlockSpec((None, H, LANES), st3)],
        out_shape=[jax.ShapeDtypeStruct((n_seq * T, HV), _row_dtype(TB)), jax.ShapeDtypeStruct((n_seq, H, V, E), F32),
                   jax.ShapeDtypeStruct((n_seq, H, E), F32), jax.ShapeDtypeStruct((n_seq, H, LANES), F32)],
        scratch_shapes=[pltpu.VMEM((H, V, E), F32), pltpu.VMEM((H, E), F32), pltpu.VMEM((H, LANES), F32)],
        compiler_params=_params(("parallel", "arbitrary")), name="mlstm_recurrence",
    )(proj, proj, proj, proj, proj, gate_bias, norm_w, c0, n0, m0)


def _row_tile(m, target, mult):
    return max(t for t in range(mult, min(m, target) + 1, mult) if m % t == 0)


def _tile(n, target):
    return max(t for t in range(LANES, min(n, target) + 1, LANES) if n % t == 0)


def _pad_cols(w, n):
    return jnp.pad(w, ((0, 0), (0, n - w.shape[1])))


def _pad_rows(w, n):
    return jnp.pad(w, ((0, n - w.shape[0]), (0, 0)))


def kernel(x_prompt, x_sample, state_a_wkv, state_a_shift, cache_b_kv_g0, cache_b_kv_g1, cache_b_kv_g2, state_c_C, state_c_n, state_c_m, rel_bias, norm_ffn1, ffn1_w_in, ffn1_w_out, norm_mix, norm_ffn2, ffn2_w_in, ffn2_w_out, norm_final, a_mu, a_w_rkv, a_w0, a_w1, a_w2, a_a0, a_a1, a_a2, a_g1, a_g2, a_k_k, a_k_a, a_r_k, a_gn_w, a_gn_b, a_w_out, a_v0, a_v1, a_v2, b_w_qkv, b_w_out, c_w_in, c_b_gates, c_norm_w, c_w_out):
    Bp, Tp, D = x_prompt.shape
    Bs, Ts, _ = x_sample.shape
    depth = norm_mix.shape[0]
    Tsp = SAMPLE_PAD
    Mp, Ms = Bp * Tp, Bs * Tsp
    M = Mp + Ms
    TM = _row_tile(M, 768, SUBLANES_BF16)
    TM_LORA = _row_tile(M, 384, SUBLANES_F32)
    TM_OUT = _row_tile(Mp, 512, SUBLANES_F32)
    TN = _tile(D, 2048)
    TN_QKV = _tile(b_w_qkv.shape[2], 2048)
    TF = _tile(ffn1_w_out.shape[1], 512)
    TB_A, TB_C = min(Tp, 128), min(Tp, 256)
    H_a = D // A_HEAD_DIM
    G, H_b, Dh = len(B_DILATIONS), B_HEADS, B_HEAD_DIM
    bf = lambda w: w.astype(BF16)

    x = jnp.concatenate([x_prompt.reshape(Mp, D),
                         jnp.pad(x_sample, ((0, 0), (0, Tsp - Ts), (0, 0))).reshape(Ms, D)], axis=0)

    def last_rows(t):
        return (jnp.stack([t[(b + 1) * Tp - 1] for b in range(Bp)]),
                jnp.stack([t[Mp + b * Tsp + Ts - 1] for b in range(Bs)]))

    qi = jnp.arange(B_BLK)[:, None]
    kj = jnp.arange(2 * B_BLK)[None, :]
    step = qi + B_BLK - kj
    step_ok = (step >= 0) & (step <= B_BLK)
    m_desc = B_BLK - jnp.arange(B_BLK + 8)
    bias_p, bias_s = [], []
    buckets = jnp.arange(N_BUCKETS)
    for gi, dil in enumerate(B_DILATIONS):
        tab = rel_bias[:, gi * H_b:(gi + 1) * H_b].astype(F32)
        hot = (_rel_bucket(jnp.clip(step, 0, B_BLK) * dil)[None] == buckets[:, None, None]).astype(F32)
        bp = jnp.einsum("nh,nqk->hqk", tab, hot, precision=HIGHEST)
        bias_p.append(jnp.where(step_ok[None], bp, -jnp.inf))
        bs = tab[_rel_bucket(jnp.maximum(m_desc, 0) * dil)]
        bias_s.append(jnp.broadcast_to(bs[:, :, None], (B_BLK + 8, H_b, Dh)))
    bias_p, bias_s = jnp.stack(bias_p), jnp.stack(bias_s)
    w_rkv = bf(a_w_rkv)
    ffn1_out, ffn2_out = bf(ffn1_w_out), bf(ffn2_w_out)

    outs_a_wkv, outs_a_shift, outs_c = ([], []), ([], []), ([], [], [], [], [], [])
    outs_b = [([], []) for _ in range(G)]
    v_first = None
    for i in range(depth):
        x = ffn(x, norm_ffn1[i], ffn1_w_in, ffn1_out, i, TM, TF)
        kind, j = i % 3, i // 3
        if kind == 0:
            u, u_prev = rmsnorm_shift(x, norm_mix[i], state_a_shift[j], Tp, Mp, Tsp, TM)
            mu = a_mu[j]
            rkv = rwkv_rkv(u, u_prev, mu[jnp.array([0, 2, 3])][:, None, :], w_rkv, j, TM, TN)
            lr = LANES
            w_br = (bf(_pad_cols(a_w1[j], lr)), bf(_pad_rows(a_w2[j], lr)), a_w0[j].reshape(1, D))
            a_br = (bf(_pad_cols(a_a1[j], lr)), bf(_pad_rows(a_a2[j], lr)), a_a0[j].reshape(1, D))
            g_br = (bf(a_g1[j]), bf(a_g2[j]))
            v_br = None
            if j > 0:
                v_br = (bf(_pad_cols(a_v1[j - 1], lr)), bf(_pad_rows(a_v2[j - 1], lr)), a_v0[j - 1].reshape(1, D))
            lora = rwkv_lora(u, u_prev, mu[jnp.array([1, 4, 5, 3])], w_br, a_br, g_br, v_br, TM_LORA)
            lw, a_lr, gate = lora[:3]
            vres = None if j == 0 else (v_first, lora[3])
            if j == 0:
                v_first = rkv
            par = tuple(p.reshape(1, D) for p in (a_k_k[j], a_k_a[j], a_r_k[j], a_gn_w[j], a_gn_b[j]))
            s0p = jnp.zeros((Bp * H_a, A_HEAD_DIM, A_HEAD_DIM), F32)
            s0s = state_a_wkv[j].reshape(Bs * H_a, A_HEAD_DIM, A_HEAD_DIM)
            yp, sp = rwkv_recurrence(rkv, lw, a_lr, gate, vres, par, s0p, Bp, Tp, 0, TB_A, None)
            ys, ss = rwkv_recurrence(rkv, lw, a_lr, gate, vres, par, s0s, Bs, Tsp, Mp, Tsp, Ts)
            x = matmul((yp, ys), bf(a_w_out[j]), TM, TN, residual=x)
            outs_a_wkv[0].append(sp.reshape(Bp, H_a, A_HEAD_DIM, A_HEAD_DIM))
            outs_a_wkv[1].append(ss.reshape(Bs, H_a, A_HEAD_DIM, A_HEAD_DIM))
            sh_p, sh_s = last_rows(u)
            outs_a_shift[0].append(sh_p)
            outs_a_shift[1].append(sh_s)
        elif kind == 1:
            qkv = matmul(x, bf(b_w_qkv[j]), TM, TN_QKV, norm_g=norm_mix[i])
            caches = [c[j].reshape(Bs, B_BLK, d, 2, H_b, Dh)
                      for c, d in zip((cache_b_kv_g0, cache_b_kv_g1, cache_b_kv_g2), B_DILATIONS)]
            op = attn_prompt(qkv, bias_p, Bp, Tp)
            os_ = attn_sample(qkv[Mp:], caches, bias_s, Bs, Tsp, 0, Ts)
            x = matmul((op, os_.reshape(Ms, H_b * Dh)), bf(b_w_out[j]), TM, TN, residual=x)
            for gi in range(G):
                keep = min(B_WINDOWS[gi], Tp)
                c0, c1 = (gi * 3 + 1) * H_b * Dh, (gi * 3 + 3) * H_b * Dh
                kv_p = jnp.stack([lax.slice(qkv, ((b + 1) * Tp - keep, c0), ((b + 1) * Tp, c1)) for b in range(Bp)])
                kv_s = lax.slice(qkv, (Mp, c0), (M, c1)).reshape(Bs, Tsp, c1 - c0)[:, :Ts]
                outs_b[gi][0].append(kv_p.reshape(Bp, keep, 2, H_b, Dh))
                outs_b[gi][1].append(kv_s.reshape(Bs, Ts, 2, H_b, Dh))
        else:
            H, E, V = C_HEADS, C_QK_DIM, C_V_DIM
            n_in = c_w_in.shape[2]
            n_pad = -(-n_in // LANES) * LANES
            proj = matmul(x, bf(_pad_cols(c_w_in[j], n_pad)), TM, _tile(n_pad, 1024), norm_g=norm_mix[i])
            gbias = _pad_cols(c_b_gates[j].reshape(1, 2 * H), LANES)
            nw = c_norm_w[j].reshape(1, H * V)
            zc = (jnp.zeros((Bp, H, V, E), F32), jnp.zeros((Bp, H, E), F32), jnp.zeros((Bp, H, LANES), F32))
            sc = (state_c_C[j], state_c_n[j], jnp.broadcast_to(state_c_m[j][:, :, None], (Bs, H, LANES)))
            hp, cp, np_, mp = mlstm_recurrence(proj, gbias, nw, *zc, Bp, Tp, 0, TB_C, None)
            hs, cs, ns, ms = mlstm_recurrence(proj, gbias, nw, *sc, Bs, Tsp, Mp, Tsp, Ts)
            x = matmul((hp, hs), bf(c_w_out[j]), TM, TN, residual=x)
            for lst, val in zip(outs_c, (cp, cs, np_, ns, mp[:, :, 0], ms[:, :, 0])):
                lst.append(val)
        x = ffn(x, norm_ffn2[i], ffn2_w_in, ffn2_out, i, TM, TF)

    y_prompt = rmsnorm(x, norm_final, TM_OUT, 0, Mp).reshape(Bp, Tp, D)
    y_sample = rmsnorm(x, norm_final, Ms, Mp, Ms).reshape(Bs, Tsp, D)[:, :Ts]
    st = jnp.stack
    return (y_prompt, y_sample, st(outs_a_wkv[0]), st(outs_a_wkv[1]), st(outs_a_shift[0]), st(outs_a_shift[1]),
            st(outs_b[0][0]), st(outs_b[0][1]), st(outs_b[1][0]), st(outs_b[1][1]), st(outs_b[2][0]), st(outs_b[2][1]),
            st(outs_c[0]), st(outs_c[1]), st(outs_c[2]), st(outs_c[3]), st(outs_c[4]), st(outs_c[5]))
```

```python
import functools
import math

import jax
import jax.numpy as jnp
from jax import lax
from jax.experimental import pallas as pl
from jax.experimental.pallas import tpu as pltpu

F32 = jnp.float32
BF16 = jnp.bfloat16
HIGHEST = lax.Precision.HIGHEST

NORM_EPS = 1e-6
A_HEAD_DIM = 64
A_GN_EPS = 64e-5
A_REC_LANES = 2048
B_WINDOWS = (128, 512, 2048)
B_DILATIONS = (1, 4, 16)
B_HEADS = 16
B_HEAD_DIM = 128
B_BLK = 128
B_UNROLL = 4
N_BUCKETS = 32
BUCKET_MAX_DIST = 2048
C_HEADS = 8
C_QK_DIM = 128
C_V_DIM = 256
C_GATE_CAP = 15.0
CHUNK = 64
SAMPLE_PAD = 8
LANES = 128
SUBLANES_F32 = 8
SUBLANES_BF16 = 16
VMEM_LIMIT = 56 * 1024 * 1024


def _row_dtype(rows):
    return BF16 if rows % SUBLANES_BF16 == 0 else F32


def _params(sem):
    return pltpu.CompilerParams(dimension_semantics=sem, vmem_limit_bytes=VMEM_LIMIT)


def _dot(a, b, precision=None):
    return jnp.dot(a, b, preferred_element_type=F32, precision=precision)


def _dot_nt(a, b):
    return lax.dot_general(a, b, (((1,), (1,)), ((), ())), preferred_element_type=F32)


def _dot_tn(a, b):
    return lax.dot_general(a, b, (((0,), (0,)), ((), ())), preferred_element_type=F32)


def _bf16_operands(dot):
    return lambda a, b: dot(a.astype(BF16), b.astype(BF16))


def _sigmoid(x):
    return 1.0 / (1.0 + jnp.exp(-x))


def _rms(x, g):
    ms = jnp.mean(x * x, axis=-1, keepdims=True)
    return x * lax.rsqrt(ms + NORM_EPS) * g


def _rmsnorm_body(x_ref, g_ref, o_ref):
    o_ref[...] = _rms(x_ref[...], g_ref[...]).astype(o_ref.dtype)


def rmsnorm(x, g, tm, row0, rows):
    D = x.shape[1]
    rb0 = row0 // tm
    return pl.pallas_call(
        _rmsnorm_body, grid=(rows // tm,),
        in_specs=[pl.BlockSpec((tm, D), lambda i: (rb0 + i, 0)), pl.BlockSpec((1, D), lambda i: (0, 0))],
        out_specs=pl.BlockSpec((tm, D), lambda i: (i, 0)),
        out_shape=jax.ShapeDtypeStruct((rows, D), F32),
        compiler_params=_params(("parallel",)), name="rmsnorm",
    )(x, g.reshape(1, D))


def _rmsnorm_shift_body(x_ref, xp_ref, g_ref, st_ref, u_ref, up_ref, *, tm, grp, seq_len, seq_rows):
    i = pl.program_id(0)
    g = g_ref[...]
    u = _rms(x_ref[...], g)
    u_ref[...] = u
    up_ref[...] = pltpu.roll(u, 1, axis=0)
    tail = _rms(xp_ref[...], g)[-1:, :]
    n_late = st_ref.shape[0]
    for k in range(tm // grp):
        row0 = i * tm + k * grp
        late = row0 >= seq_rows
        is_start = late | (lax.rem(row0, seq_len) == 0)
        state = st_ref[pl.ds(jnp.clip((row0 - seq_rows) // grp, 0, n_late - 1), 1), :]
        before = tail if k == 0 else u[k * grp - 1:k * grp, :]
        up_ref[k * grp:k * grp + 1, :] = jnp.where(is_start, jnp.where(late, state, 0.0), before)


def rmsnorm_shift(x, g, late_states, seq_len, seq_rows, grp, tm):
    M, D = x.shape
    sub = SUBLANES_F32
    return pl.pallas_call(
        functools.partial(_rmsnorm_shift_body, tm=tm, grp=grp, seq_len=seq_len, seq_rows=seq_rows), grid=(M // tm,),
        in_specs=[pl.BlockSpec((tm, D), lambda i: (i, 0)),
                  pl.BlockSpec((sub, D), lambda i: (jnp.maximum(i * (tm // sub) - 1, 0), 0)),
                  pl.BlockSpec((1, D), lambda i: (0, 0)),
                  pl.BlockSpec(late_states.shape, lambda i: (0, 0))],
        out_specs=[pl.BlockSpec((tm, D), lambda i: (i, 0))] * 2,
        out_shape=[jax.ShapeDtypeStruct((M, D), F32)] * 2,
        compiler_params=_params(("parallel",)), name="rmsnorm_shift",
    )(x, x, g.reshape(1, D), late_states)


def _ffn_body(x_ref, g_ref, wg_ref, wu_ref, wo_ref, o_ref, xn_ref):
    j = pl.program_id(1)

    @pl.when(j == 0)
    def _():
        xn_ref[...] = _rms(x_ref[...], g_ref[...]).astype(BF16)
        o_ref[...] = jnp.zeros_like(o_ref)

    xn = xn_ref[...]
    gate = _dot(xn, wg_ref[...].astype(BF16))
    up = _dot(xn, wu_ref[...].astype(BF16))
    h = (gate * _sigmoid(gate) * up).astype(BF16)
    o_ref[...] += _dot(h, wo_ref[...].astype(BF16))

    @pl.when(j == pl.num_programs(1) - 1)
    def _():
        o_ref[...] = x_ref[...] + 0.5 * o_ref[...]


def ffn(x, g, w_in, w_out, layer, tm, tf):
    M, D = x.shape
    Fh = w_out.shape[1]
    nf = Fh // tf
    return pl.pallas_call(
        _ffn_body, grid=(M // tm, nf),
        in_specs=[pl.BlockSpec((tm, D), lambda i, j: (i, 0)),
                  pl.BlockSpec((1, D), lambda i, j: (0, 0)),
                  pl.BlockSpec((None, D, tf), lambda i, j: (layer, 0, j)),
                  pl.BlockSpec((None, D, tf), lambda i, j: (layer, 0, j + nf)),
                  pl.BlockSpec((None, tf, D), lambda i, j: (layer, j, 0))],
        out_specs=pl.BlockSpec((tm, D), lambda i, j: (i, 0)),
        out_shape=jax.ShapeDtypeStruct((M, D), F32),
        scratch_shapes=[pltpu.VMEM((tm, D), BF16)],
        compiler_params=_params(("parallel", "arbitrary")), name="ffn",
    )(x, g.reshape(1, D), w_in, w_in, w_out)


def _mm_body(*refs, nb0, has_norm, has_res):
    it = iter(refs)
    x_ref = next(it)
    x1_ref = next(it) if nb0 is not None else None
    g_ref = next(it) if has_norm else None
    w_ref = next(it)
    res_ref = next(it) if has_res else None
    o_ref = next(it)
    xs_ref = next(it)

    def stage(ref, rows=slice(None), dst=slice(None)):
        x = ref[rows, :].astype(F32)
        if has_norm:
            x = _rms(x, g_ref[...])
        xs_ref[dst, :] = x.astype(BF16)

    @pl.when(pl.program_id(1) == 0)
    def _():
        if nb0 is None:
            stage(x_ref)
        else:
            rem = xs_ref.shape[0] - x1_ref.shape[0]
            pl.when(pl.program_id(0) < nb0)(lambda: stage(x_ref))

            @pl.when(pl.program_id(0) == nb0)
            def _():
                if rem > 0:
                    stage(x_ref, slice(0, rem), slice(0, rem))
                stage(x1_ref, slice(None), slice(rem, None))

    acc = _dot(xs_ref[...], w_ref[...])
    if has_res:
        acc = res_ref[...] + acc
    o_ref[...] = acc.astype(o_ref.dtype)


def matmul(x, w, tm, tn, norm_g=None, residual=None, out_dtype=F32):
    nb0 = None
    if isinstance(x, tuple):
        x0, x1 = x
        nb0 = x0.shape[0] // tm
        M, K = x0.shape[0] + x1.shape[0], x0.shape[1]
        assert M == (nb0 + 1) * tm and x1.shape[0] <= tm and x1.shape[0] % SUBLANES_BF16 == 0
        args = [x0, x1]
        last0 = -(-x0.shape[0] // tm) - 1
        specs = [pl.BlockSpec((tm, K), lambda i, j: (jnp.minimum(i, last0), 0)),
                 pl.BlockSpec(x1.shape, lambda i, j: (0, 0))]
    else:
        M, K = x.shape
        args = [x]
        specs = [pl.BlockSpec((tm, K), lambda i, j: (i, 0))]
    N = w.shape[1]
    if norm_g is not None:
        args.append(norm_g.reshape(1, K))
        specs.append(pl.BlockSpec((1, K), lambda i, j: (0, 0)))
    args.append(w)
    specs.append(pl.BlockSpec((K, tn), lambda i, j: (0, j)))
    if residual is not None:
        args.append(residual)
        specs.append(pl.BlockSpec((tm, tn), lambda i, j: (i, j)))
    return pl.pallas_call(
        functools.partial(_mm_body, nb0=nb0, has_norm=norm_g is not None, has_res=residual is not None),
        grid=(M // tm, N // tn), in_specs=specs,
        out_specs=pl.BlockSpec((tm, tn), lambda i, j: (i, j)),
        out_shape=jax.ShapeDtypeStruct((M, N), out_dtype),
        scratch_shapes=[pltpu.VMEM((tm, K), BF16)],
        compiler_params=_params(("parallel", "arbitrary")), name="matmul",
    )(*args)


def _rkv_body(u_ref, up_ref, mu_ref, w_ref, o_ref, xs_ref):
    @pl.when(pl.program_id(2) == 0)
    def _():
        u = u_ref[...]
        xs_ref[...] = (u + (up_ref[...] - u) * mu_ref[...]).astype(BF16)

    o_ref[...] = _dot(xs_ref[...], w_ref[...])


def rwkv_rkv(u, u_prev, mu3, w, layer, tm, tn):
    M, D = u.shape
    return pl.pallas_call(
        _rkv_body, grid=(M // tm, 3, D // tn),
        in_specs=[pl.BlockSpec((tm, D), lambda i, k, j: (i, 0)),
                  pl.BlockSpec((tm, D), lambda i, k, j: (i, 0)),
                  pl.BlockSpec((None, 1, D), lambda i, k, j: (k, 0, 0)),
                  pl.BlockSpec((None, None, D, tn), lambda i, k, j: (layer, k, 0, j))],
        out_specs=pl.BlockSpec((None, tm, tn), lambda i, k, j: (k, i, j)),
        out_shape=jax.ShapeDtypeStruct((3, M, D), F32),
        scratch_shapes=[pltpu.VMEM((tm, D), BF16)],
        compiler_params=_params(("parallel", "arbitrary", "arbitrary")), name="rwkv_rkv",
    )(u, u_prev, mu3, w)


def _lora_body(*refs, has_vres):
    it = iter(refs)
    u_ref, up_ref, mu_ref = next(it), next(it), next(it)
    w1, w2, w0 = next(it), next(it), next(it)
    a1, a2, a0 = next(it), next(it), next(it)
    g1, g2 = next(it), next(it)
    if has_vres:
        v1, v2, v0 = next(it), next(it), next(it)
    lw_ref, a_ref, g_ref = next(it), next(it), next(it)
    nu_ref = next(it) if has_vres else None

    u = u_ref[...]
    du = up_ref[...] - u

    def mix(n):
        return (u + du * mu_ref[n:n + 1, :]).astype(BF16)

    hw = jnp.tanh(_dot(mix(0), w1[...])).astype(BF16)
    w_pre = w0[...] + _dot(hw, w2[...])
    softplus = jnp.maximum(-w_pre, 0.0) + jnp.log(1.0 + jnp.exp(-jnp.abs(w_pre)))
    lw_ref[...] = -jnp.exp(-softplus - 0.5)
    ha = _dot(mix(1), a1[...]).astype(BF16)
    a_ref[...] = _sigmoid(a0[...] + _dot(ha, a2[...]))
    hg = _sigmoid(_dot(mix(2), g1[...])).astype(BF16)
    g_ref[...] = _dot(hg, g2[...])
    if has_vres:
        hv = _dot(mix(3), v1[...]).astype(BF16)
        nu_ref[...] = _sigmoid(v0[...] + _dot(hv, v2[...]))


def rwkv_lora(u, u_prev, mu4, w, a, g, v, tm):
    M, D = u.shape
    has_vres = v is not None
    row = lambda i: (i, 0)
    full = lambda i: (0, 0)
    args = [u, u_prev, mu4]
    specs = [pl.BlockSpec((tm, D), row), pl.BlockSpec((tm, D), row), pl.BlockSpec(mu4.shape, full)]
    for t in (w, a, g) + ((v,) if has_vres else ()):
        for m in t:
            args.append(m)
            specs.append(pl.BlockSpec(m.shape, full))
    n_out = 4 if has_vres else 3
    return pl.pallas_call(
        functools.partial(_lora_body, has_vres=has_vres), grid=(M // tm,), in_specs=specs,
        out_specs=[pl.BlockSpec((tm, D), row)] * n_out,
        out_shape=[jax.ShapeDtypeStruct((M, D), F32)] * n_out,
        compiler_params=_params(("parallel",)), name="rwkv_lora",
    )(*args)


def _rwkv_rec_body(*refs, L, TB, hb, t_valid, has_vres):
    N = A_HEAD_DIM
    it = iter(refs)
    r_ref, k_ref, v_ref, lw_ref, a_ref, g_ref = (next(it) for _ in range(6))
    if has_vres:
        vf_ref, nu_ref = next(it), next(it)
    kk_ref, ka_ref, rk_ref, gnw_ref, gnb_ref, s0_ref = (next(it) for _ in range(6))
    y_ref, sT_ref, S_scr = next(it), next(it), next(it)
    tb = pl.program_id(2)

    assert L == N and 2 * N == LANES

    @pl.when(tb == 0)
    def _():
        for p in range(hb // 2):
            S_scr[p] = jnp.concatenate([s0_ref[2 * p], s0_ref[2 * p + 1]], axis=1)

    row = lax.broadcasted_iota(jnp.int32, (L, L), 0)
    col = lax.broadcasted_iota(jnp.int32, (L, L), 1)
    tril = (row >= col).astype(F32)
    row1 = lax.broadcasted_iota(jnp.int32, (L, 2 * N), 0)
    lane1 = lax.broadcasted_iota(jnp.int32, (L, 2 * N), 1)
    head0_lane = lane1 < N
    eye = ((lane1 & (N - 1)) == row1).astype(F32)
    row2 = lax.broadcasted_iota(jnp.int32, (2 * L, 2 * N), 0)
    lane2 = lax.broadcasted_iota(jnp.int32, (2 * L, 2 * N), 1)
    mask2 = (lane2 & (N - 1)) < jnp.where(row2 < L, row2, row2 - L + 1)
    bd_mask = (row2 // L) == (lane2 // N)
    ones_bd = bd_mask.astype(BF16)
    ones_bd2 = jnp.concatenate([ones_bd, ones_bd], axis=0)
    n_sq = int(math.log2(L)) - 1
    kk_p, ka_p, rk_p, gnw, gnb = kk_ref[...], ka_ref[...], rk_ref[...], gnw_ref[...], gnb_ref[...]

    def chunk(c, carry):
        sl = pl.ds(pl.multiple_of(c * L, L), L)

        def load(ref):
            if TB >= L:
                return ref[sl, :]
            return jnp.concatenate([ref[...], jnp.zeros((L - TB, ref.shape[1]), F32)], axis=0)

        r, k, v, lw, a, g = (load(ref) for ref in (r_ref, k_ref, v_ref, lw_ref, a_ref, g_ref))
        if has_vres:
            v = v + (load(vf_ref) - v) * load(nu_ref)
        if t_valid is not None:
            t_idx = tb * TB + c * L + lax.broadcasted_iota(jnp.int32, (L, 1), 0)
            valid = t_idx < t_valid
            r, k, v, lw = (jnp.where(valid, t, 0.0) for t in (r, k, v, lw))
        cum = _dot(tril, lw, HIGHEST)
        cum_end = cum[L - 1:L, :]
        w_cur, w_prev, w_inv, w_rem, w_end = (jnp.exp(cum), jnp.exp(cum - lw), jnp.exp(-cum),
                                              jnp.exp(cum_end - cum), jnp.exp(cum_end))
        mm, mm_nt, mm_tn = (_bf16_operands(d) for d in (_dot, _dot_nt, _dot_tn))
        pairs = range(hb // 2)
        ps = [slice(p * LANES, (p + 1) * LANES) for p in pairs]

        def head_sum(x):
            x_hi = x.astype(BF16)
            x_lo = (x - x_hi.astype(F32)).astype(BF16)
            return _dot(jnp.concatenate([x_hi, x_lo], axis=1), ones_bd2)

        def bdiag(x):
            return jnp.where(bd_mask, jnp.concatenate([x, x], axis=0), 0.0)

        kkp = [k[:, s] * kk_p[:, s] for s in ps]
        k2 = [k[:, s] * (1.0 + (a[:, s] - 1.0) * ka_p[:, s]) for s in ps]
        sums = [head_sum(jnp.concatenate([kkp[p] * kkp[p], r[:, ps[p]] * k2[p] * rk_p[:, ps[p]]], axis=0))
                for p in pairs]
        kk = [kkp[p] / jnp.maximum(jnp.sqrt(sums[p][:L]), 1e-12) for p in pairs]
        b = [kk[p] * a[:, ps[p]] for p in pairs]
        lhs2 = [jnp.concatenate([kk[p] * w_prev[:, ps[p]], r[:, ps[p]] * w_cur[:, ps[p]]], axis=0) for p in pairs]
        kd = [k2[p] * w_inv[:, ps[p]] for p in pairs]
        bd = [b[p] * w_inv[:, ps[p]] for p in pairs]
        kend = [k2[p] * w_rem[:, ps[p]] for p in pairs]
        bend = [b[p] * w_rem[:, ps[p]] for p in pairs]
        kkd = [x[:L] for x in lhs2]
        rd = [x[L:] for x in lhs2]
        a_kb2 = [mm_nt(lhs2[p], jnp.concatenate([bdiag(kd[p]), bdiag(bd[p])], axis=0)) for p in pairs]
        a_k = [jnp.where(mask2, x[:, :LANES], 0.0) for x in a_kb2]
        a_b = [jnp.where(mask2, x[:, LANES:], 0.0) for x in a_kb2]
        a_kb = [x[:L] for x in a_b]
        a_rb = [x[L:] for x in a_b]
        a_v = [mm(a_k[p], bdiag(v[:, ps[p]])) for p in pairs]
        t_inv = [eye - x for x in a_kb]
        pw = [mm(x, bdiag(x)) for x in a_kb]
        for _ in range(n_sq - 1):
            both = [mm(jnp.concatenate([pw[p], t_inv[p]], axis=0), bdiag(pw[p])) for p in pairs]
            t_inv = [t_inv[p] + both[p][L:] for p in pairs]
            pw = [x[:L] for x in both]
        t_inv = [t_inv[p] + mm(t_inv[p], bdiag(pw[p])) for p in pairs]
        ktcu = [mm(t_inv[p], jnp.concatenate([bdiag(kkd[p]), bdiag(a_v[p][:L])], axis=1)) for p in pairs]
        k_t = [x[:, :LANES] for x in ktcu]
        c_u = [x[:, LANES:] for x in ktcu]
        ykc = [mm(a_rb[p], jnp.concatenate([bdiag(k_t[p]), bdiag(c_u[p])], axis=1)) for p in pairs]
        y_k = [rd[p] - ykc[p][:, :LANES] for p in pairs]
        y_c = [a_v[p][L:] - ykc[p][:, LANES:] for p in pairs]
        S = [S_scr[p] for p in pairs]
        y = [mm_nt(y_k[p], bdiag(S[p])) + y_c[p] for p in pairs]
        upd = [mm_tn(jnp.concatenate([jnp.concatenate([v[:, ps[p]], jnp.zeros_like(k_t[p])], axis=1),
                                      jnp.concatenate([-c_u[p], k_t[p]], axis=1)], axis=0),
                     jnp.concatenate([kend[p], bend[p]], axis=0)) for p in pairs]
        S_c = [jnp.where(head0_lane, x[:N], x[N:LANES]) for x in upd]
        ktb = [jnp.where(bd_mask, x[LANES:], 0.0) for x in upd]
        for p in pairs:
            S_scr[p] = S[p] * w_end[:, ps[p]] - mm(S[p], ktb[p]) + S_c[p]
        mean = [head_sum(y[p]) * (1.0 / N) for p in pairs]
        var = [head_sum(jnp.square(y[p] - mean[p])) * (1.0 / N) for p in pairs]
        bonus = [sums[p][L:] * v[:, ps[p]] for p in pairs]
        outs = [((y[p] - mean[p]) * lax.rsqrt(var[p] + A_GN_EPS) * gnw[:, ps[p]] + gnb[:, ps[p]] + bonus[p])
                * g[:, ps[p]] for p in pairs]
        y_out = jnp.concatenate(outs, axis=-1).astype(y_ref.dtype)
        if TB >= L:
            y_ref[sl, :] = y_out
        else:
            y_ref[...] = y_out[:TB]
        return carry

    lax.fori_loop(0, max(TB // L, 1), chunk, 0)

    @pl.when(tb == pl.num_programs(2) - 1)
    def _():
        for p in range(hb // 2):
            S = S_scr[p]
            sT_ref[2 * p] = S[:, :N]
            sT_ref[2 * p + 1] = S[:, N:]


def rwkv_recurrence(rkv, lw, a, g, vres, params, s0, n_seq, T, row0, TB, t_valid):
    _, M, D = rkv.shape
    N = A_HEAD_DIM
    LW = min(A_REC_LANES, D)
    hb = LW // N
    nb = T // TB
    rb0 = row0 // TB
    has_vres = vres is not None
    seq = lambda b, h, t: (rb0 + b * nb + t, h)
    args, specs = [], []
    for n in range(3):
        args.append(rkv)
        specs.append(pl.BlockSpec((None, TB, LW), lambda b, h, t, n=n: (n, rb0 + b * nb + t, h)))
    for x in (lw, a, g):
        args.append(x)
        specs.append(pl.BlockSpec((TB, LW), seq))
    if has_vres:
        args += list(vres)
        specs += [pl.BlockSpec((None, TB, LW), lambda b, h, t: (2, rb0 + b * nb + t, h)), pl.BlockSpec((TB, LW), seq)]
    for p in params:
        args.append(p)
        specs.append(pl.BlockSpec((1, LW), lambda b, h, t: (0, h)))
    args.append(s0)
    specs.append(pl.BlockSpec((hb, N, N), lambda b, h, t: (b * (D // LW) + h, 0, 0)))
    return pl.pallas_call(
        functools.partial(_rwkv_rec_body, L=CHUNK, TB=TB, hb=hb, t_valid=t_valid, has_vres=has_vres),
        grid=(n_seq, D // LW, nb), in_specs=specs,
        out_specs=[pl.BlockSpec((TB, LW), lambda b, h, t: (b * nb + t, h)),
                   pl.BlockSpec((hb, N, N), lambda b, h, t: (b * (D // LW) + h, 0, 0))],
        out_shape=[jax.ShapeDtypeStruct((n_seq * T, D), _row_dtype(TB)),
                   jax.ShapeDtypeStruct((n_seq * (D // N), N, N), F32)],
        scratch_shapes=[pltpu.VMEM((hb // 2, N, 2 * N), F32)],
        compiler_params=_params(("parallel", "parallel", "arbitrary")), name="rwkv_recurrence",
    )(*args)


def _rel_bucket(dist):
    exact = N_BUCKETS // 2
    d = jnp.maximum(dist, 1).astype(F32)
    log_b = exact + (jnp.log(d / exact) / math.log(BUCKET_MAX_DIST / exact) * (N_BUCKETS - exact)).astype(jnp.int32)
    return jnp.where(dist < exact, dist, jnp.minimum(log_b, N_BUCKETS - 1))


def _attn_prompt_body(q_ref, k_ref, v_ref, bias_ref, o_ref, acc_ref, m_ref, l_ref, kd_ref, vd_ref, *, T):
    step = pl.program_id(2)
    blk = B_BLK
    G = len(B_DILATIONS)
    scale = B_HEAD_DIM ** -0.5
    first_keys = lax.broadcasted_iota(jnp.int32, (blk, 2 * blk), 1) < blk
    ones_cols = jnp.ones((2 * blk, B_HEAD_DIM), BF16)

    for si, dil in enumerate(reversed(B_DILATIONS)):
        @pl.when(step == si)
        def _(gi=si, dil=dil):
            span = blk * dil
            unroll = B_UNROLL if gi > 0 else 2 * B_UNROLL
            res_rows = T // dil + blk
            bias = bias_ref[...]

            def where(idx):
                n = idx // dil
                r = idx - n * dil
                return n, n * span + r, pl.multiple_of(r * res_rows + n * blk, blk)

            for r in range(dil):
                kd_ref[r * res_rows:r * res_rows + blk, :] = jnp.zeros((blk, B_HEAD_DIM), BF16)
                vd_ref[r * res_rows:r * res_rows + blk, :] = jnp.zeros((blk, B_HEAD_DIM), BF16)

            def stage(it, carry):
                for u in range(unroll):
                    _, start, dst = where(it * unroll + u)
                    rows = pl.ds(start, blk, stride=dil)
                    kd_ref[pl.ds(dst + blk, blk), :] = k_ref[rows, :].astype(BF16)
                    vd_ref[pl.ds(dst + blk, blk), :] = v_ref[rows, :].astype(BF16)
                return carry

            lax.fori_loop(0, T // (blk * unroll), stage, 0)

            def blocks(it, carry):
                us = range(unroll)
                pos = [where(it * unroll + u) for u in us]
                n = [x[0] for x in pos]
                cur = [pl.ds(x[1], blk, stride=dil) for x in pos]
                q = [(q_ref[cur[u], :] * scale).astype(BF16) for u in us]
                kcat = [kd_ref[pl.ds(x[2], 2 * blk), :] for x in pos]
                vcat = [vd_ref[pl.ds(x[2], 2 * blk), :] for x in pos]
                logits = [_dot_nt(q[u], kcat[u]) + bias for u in us]
                logits = [jnp.where(first_keys & (n[u] == 0), -jnp.inf, logits[u]) for u in us]
                mx = [jnp.max(x, axis=-1, keepdims=True) for x in logits]
                p = [jnp.exp(logits[u] - mx[u]) for u in us]
                pvd = [_dot(p[u].astype(BF16), jnp.concatenate([vcat[u], ones_cols], axis=1)) for u in us]
                pv = [x[:, :B_HEAD_DIM] for x in pvd]
                den = [x[:, B_HEAD_DIM:] for x in pvd]
                if gi > 0:
                    m_old = [m_ref[cur[u], :] for u in us]
                    l_old = [l_ref[cur[u], :] for u in us]
                    acc_old = [acc_ref[cur[u], :] for u in us]
                    m_new = [jnp.maximum(m_old[u], mx[u]) for u in us]
                    c_old = [jnp.exp(m_old[u] - m_new[u]) for u in us]
                    c_new = [jnp.exp(mx[u] - m_new[u]) for u in us]
                    pv = [acc_old[u] * c_old[u] + pv[u] * c_new[u] for u in us]
                    den = [l_old[u] * c_old[u] + den[u] * c_new[u] for u in us]
                    mx = m_new
                for u in us:
                    acc_ref[cur[u], :] = pv[u]
                    m_ref[cur[u], :] = mx[u]
                    l_ref[cur[u], :] = den[u]
                return carry

            lax.fori_loop(0, T // (blk * unroll), blocks, 0)

    @pl.when(step == G - 1)
    def _():
        o_ref[...] = (acc_ref[...] / l_ref[...]).astype(o_ref.dtype)


def attn_prompt(qkv, bias, n_seq, T):
    H, Dh, G = B_HEADS, B_HEAD_DIM, len(B_DILATIONS)

    def col(which):
        return lambda b, h, s: (b, ((G - 1 - s) * 3 + which) * H + h)

    staged_rows = T + B_BLK * max(B_DILATIONS)
    return pl.pallas_call(
        functools.partial(_attn_prompt_body, T=T), grid=(n_seq, H, G),
        in_specs=[pl.BlockSpec((T, Dh), col(0)), pl.BlockSpec((T, Dh), col(1)), pl.BlockSpec((T, Dh), col(2)),
                  pl.BlockSpec((None, None, B_BLK, 2 * B_BLK), lambda b, h, s: (G - 1 - s, h, 0, 0))],
        out_specs=pl.BlockSpec((T, Dh), lambda b, h, s: (b, h)),
        out_shape=jax.ShapeDtypeStruct((n_seq * T, H * Dh), BF16),
        scratch_shapes=[pltpu.VMEM((T, Dh), F32), pltpu.VMEM((T, 1), F32), pltpu.VMEM((T, Dh), F32),
                        pltpu.VMEM((staged_rows, Dh), BF16), pltpu.VMEM((staged_rows, Dh), BF16)],
        compiler_params=_params(("parallel", "parallel", "arbitrary")), name="attn_prompt",
    )(qkv, qkv, qkv, bias)


def _attn_sample_body(q_ref, k_ref, v_ref, c0_ref, c1_ref, c2_ref, bias_ref, o_ref, *, t_valid):
    blk = B_BLK
    scale = B_HEAD_DIM ** -0.5
    caches = (c0_ref, c1_ref, c2_ref)
    o_ref[...] = jnp.zeros_like(o_ref)
    for t in range(t_valid):
        m_run = l_run = acc = None
        for gi, dil in enumerate(B_DILATIONS):
            q = q_ref[t, gi] * scale
            c_ref = caches[gi]
            if dil == 1:
                kcat = jnp.concatenate([c_ref[t:, 0, 0], k_ref[:t + 1, gi]], axis=0)
                vcat = jnp.concatenate([c_ref[t:, 0, 1], v_ref[:t + 1, gi]], axis=0)
            else:
                kcat = jnp.concatenate([c_ref[:, t, 0], k_ref[t:t + 1, gi]], axis=0)
                vcat = jnp.concatenate([c_ref[:, t, 1], v_ref[t:t + 1, gi]], axis=0)
            logits = jnp.sum(q[None] * kcat, axis=-1, keepdims=True) + bias_ref[gi, :blk + 1]
            mx = jnp.max(logits, axis=0)
            p = jnp.exp(logits - mx[None])
            den = jnp.sum(p, axis=0)
            pv = jnp.sum(p * vcat, axis=0)
            if gi == 0:
                m_run, l_run, acc = mx, den, pv
            else:
                m_new = jnp.maximum(m_run, mx)
                c_old, c_new = jnp.exp(m_run - m_new), jnp.exp(mx - m_new)
                acc = acc * c_old + pv * c_new
                l_run = l_run * c_old + den * c_new
                m_run = m_new
        o_ref[t] = (acc / l_run).astype(o_ref.dtype)


def attn_sample(qkv, caches, bias, n_seq, T, row0, t_valid):
    H, Dh, G = B_HEADS, B_HEAD_DIM, len(B_DILATIONS)
    rb0 = row0 // T
    assert t_valid <= min(d for d in B_DILATIONS if d > 1)
    q5 = qkv.reshape(qkv.shape[0], G, 3, H, Dh)

    def spec(which):
        return pl.BlockSpec((T, G, None, H, Dh), lambda b: (b + rb0, 0, which, 0, 0))

    cache_specs = [pl.BlockSpec((None, B_BLK, min(d, t_valid), 2, H, Dh), lambda b: (b, 0, 0, 0, 0, 0))
                   for d in B_DILATIONS]
    return pl.pallas_call(
        functools.partial(_attn_sample_body, t_valid=t_valid), grid=(n_seq,),
        in_specs=[spec(0), spec(1), spec(2)] + cache_specs + [pl.BlockSpec(bias.shape, lambda b: (0, 0, 0, 0))],
        out_specs=pl.BlockSpec((T, H, Dh), lambda b: (b, 0, 0)),
        out_shape=jax.ShapeDtypeStruct((n_seq * T, H, Dh), BF16),
        compiler_params=_params(("parallel",)), name="attn_sample",
    )(q5, q5, q5, *caches, bias)


def _mlstm_body(q_ref, k_ref, v_ref, o_ref, gate_ref, gb_ref, nw_ref, c0_ref, n0_ref, m0_ref,
                y_ref, cT_ref, nT_ref, mT_ref, C_scr, n_scr, m_scr, *, L, TB, t_valid):
    H, E, V = C_HEADS, C_QK_DIM, C_V_DIM
    tb = pl.program_id(1)

    @pl.when(tb == 0)
    def _():
        C_scr[...] = c0_ref[...]
        n_scr[...] = n0_ref[...]
        m_scr[...] = m0_ref[...]

    row = lax.broadcasted_iota(jnp.int32, (L, L), 0)
    col = lax.broadcasted_iota(jnp.int32, (L, L), 1)
    causal = row >= col
    tril = causal.astype(F32)
    gb = gb_ref[...]
    nw = nw_ref[...]

    def chunk(c, carry):
        sl = pl.ds(pl.multiple_of(c * L, L), L)

        def load(ref, cols=slice(None)):
            if TB >= L:
                return ref[sl, cols]
            x = ref[:, cols]
            return jnp.concatenate([x, jnp.zeros((L - TB, x.shape[1]), F32)], axis=0)

        gact = C_GATE_CAP * jnp.tanh((load(gate_ref) + gb) / C_GATE_CAP)
        lf = jnp.minimum(gact, 0.0) - jnp.log(1.0 + jnp.exp(-jnp.abs(gact)))
        ig = gact
        valid = None
        if t_valid is not None:
            t_idx = tb * TB + c * L + lax.broadcasted_iota(jnp.int32, (L, 1), 0)
            valid = t_idx < t_valid
            ig = jnp.where(valid, ig, -1e30)
            lf = jnp.where(valid, lf, 0.0)
        bcum = _dot(tril, lf, HIGHEST)
        ig_t = ig.T
        bcum_t = bcum.T
        heads = range(H)
        es = [slice(h * E, (h + 1) * E) for h in heads]
        vs = [slice(h * V, (h + 1) * V) for h in heads]
        b_col = [bcum[:, H + h:H + h + 1] for h in heads]
        b_row = [bcum_t[H + h:H + h + 1, :] for h in heads]
        ig_col = [ig[:, h:h + 1] for h in heads]
        ig_row = [ig_t[h:h + 1, :] for h in heads]
        q = [load(q_ref, s) for s in es]
        k = [load(k_ref, s) * (E ** -0.5) for s in es]
        v = [load(v_ref, s) for s in vs]
        if valid is not None:
            q, k, v = ([jnp.where(valid, t, 0.0) for t in ts] for ts in (q, k, v))
        m_prev = [m_scr[h:h + 1, 0:1] for h in heads]
        n_prev = [n_scr[h:h + 1, :] for h in heads]
        C = [C_scr[h] for h in heads]
        dm = [jnp.where(causal, b_col[h] - b_row[h] + ig_row[h], -jnp.inf) for h in heads]
        inter = [b_col[h] + m_prev[h] for h in heads]
        mt = [jnp.maximum(inter[h], jnp.max(dm[h], axis=-1, keepdims=True)) for h in heads]
        w_d = [jnp.exp(dm[h] - mt[h]) for h in heads]
        w_i = [jnp.exp(inter[h] - mt[h]) for h in heads]
        qb, kb, vb = ([t.astype(BF16) for t in ts] for ts in (q, k, v))
        sc = [_dot_nt(qb[h], kb[h]) * w_d[h] for h in heads]
        qc = [_dot_nt(qb[h], C[h].astype(BF16)) for h in heads]
        num = [_dot(sc[h].astype(BF16), vb[h]) + w_i[h] * qc[h] for h in heads]
        den = [jnp.sum(sc[h], axis=-1, keepdims=True) + w_i[h] * jnp.sum(q[h] * n_prev[h], axis=-1, keepdims=True)
               for h in heads]
        hh = [num[h] / jnp.maximum(jnp.abs(den[h]), jnp.exp(-mt[h])) for h in heads]
        m_new = [x[L - 1:L, :] for x in mt]
        b_end = [x[L - 1:L, :] for x in b_col]
        w_s = [jnp.exp(b_end[h] - b_col[h] + ig_col[h] - m_new[h]) for h in heads]
        dec = [jnp.exp(b_end[h] + m_prev[h] - m_new[h]) for h in heads]
        c_upd = [_dot_tn((w_s[h] * v[h]).astype(BF16), kb[h]) for h in heads]
        outs = []
        for h in heads:
            C_scr[h] = dec[h] * C[h] + c_upd[h]
            n_scr[h:h + 1, :] = dec[h] * n_prev[h] + jnp.sum(w_s[h] * k[h], axis=0, keepdims=True)
            m_scr[h:h + 1, :] = jnp.broadcast_to(m_new[h], (1, LANES))
            hn = hh[h] * lax.rsqrt(jnp.mean(hh[h] * hh[h], axis=-1, keepdims=True) + NORM_EPS) * nw[:, vs[h]]
            outs.append(hn * _sigmoid(load(o_ref, vs[h])))
        y_out = jnp.concatenate(outs, axis=-1).astype(y_ref.dtype)
        if TB >= L:
            y_ref[sl, :] = y_out
        else:
            y_ref[...] = y_out[:TB]
        return carry

    lax.fori_loop(0, max(TB // L, 1), chunk, 0)

    @pl.when(tb == pl.num_programs(1) - 1)
    def _():
        cT_ref[...] = C_scr[...]
        nT_ref[...] = n_scr[...]
        mT_ref[...] = m_scr[...]


def mlstm_recurrence(proj, gate_bias, norm_w, c0, n0, m0, n_seq, T, row0, TB, t_valid):
    H, E, V = C_HEADS, C_QK_DIM, C_V_DIM
    nb = T // TB
    rb0 = row0 // TB
    HE, HV = H * E, H * V

    def cols(cb):
        return lambda b, t: (rb0 + b * nb + t, cb)

    st4 = lambda b, t: (b, 0, 0, 0)
    st3 = lambda b, t: (b, 0, 0)
    return pl.pallas_call(
        functools.partial(_mlstm_body, L=CHUNK, TB=TB, t_valid=t_valid), grid=(n_seq, nb),
        in_specs=[pl.BlockSpec((TB, HE), cols(0)), pl.BlockSpec((TB, HE), cols(1)),
                  pl.BlockSpec((TB, HV), cols(2 * HE // HV)), pl.BlockSpec((TB, HV), cols(2 * HE // HV + 1)),
                  pl.BlockSpec((TB, LANES), cols((2 * HE + 2 * HV) // LANES)),
                  pl.BlockSpec((1, LANES), lambda b, t: (0, 0)), pl.BlockSpec((1, HV), lambda b, t: (0, 0)),
                  pl.BlockSpec((None, H, V, E), st4), pl.BlockSpec((None, H, E), st3),
                  pl.BlockSpec((None, H, LANES), st3)],
        out_specs=[pl.BlockSpec((TB, HV), lambda b, t: (b * nb + t, 0)),
                   pl.BlockSpec((None, H, V, E), st4), pl.BlockSpec((None, H, E), st3),
                   pl.BlockSpec((None, H, LANES), st3)],
        out_shape=[jax.ShapeDtypeStruct((n_seq * T, HV), _row_dtype(TB)), jax.ShapeDtypeStruct((n_seq, H, V, E), F32),
                   jax.ShapeDtypeStruct((n_seq, H, E), F32), jax.ShapeDtypeStruct((n_seq, H, LANES), F32)],
        scratch_shapes=[pltpu.VMEM((H, V, E), F32), pltpu.VMEM((H, E), F32), pltpu.VMEM((H, LANES), F32)],
        compiler_params=_params(("parallel", "arbitrary")), name="mlstm_recurrence",
    )(proj, proj, proj, proj, proj, gate_bias, norm_w, c0, n0, m0)


def _row_tile(m, target, mult):
    return max(t for t in range(mult, min(m, target) + 1, mult) if m % t == 0)


def _tile(n, target):
    return max(t for t in range(LANES, min(n, target) + 1, LANES) if n % t == 0)


def _pad_cols(w, n):
    return jnp.pad(w, ((0, 0), (0, n - w.shape[1])))


def _pad_rows(w, n):
    return jnp.pad(w, ((0, n - w.shape[0]), (0, 0)))


def kernel(x_prompt, x_sample, state_a_wkv, state_a_shift, cache_b_kv_g0, cache_b_kv_g1, cache_b_kv_g2, state_c_C, state_c_n, state_c_m, rel_bias, norm_ffn1, ffn1_w_in, ffn1_w_out, norm_mix, norm_ffn2, ffn2_w_in, ffn2_w_out, norm_final, a_mu, a_w_rkv, a_w0, a_w1, a_w2, a_a0, a_a1, a_a2, a_g1, a_g2, a_k_k, a_k_a, a_r_k, a_gn_w, a_gn_b, a_w_out, a_v0, a_v1, a_v2, b_w_qkv, b_w_out, c_w_in, c_b_gates, c_norm_w, c_w_out):
    Bp, Tp, D = x_prompt.shape
    Bs, Ts, _ = x_sample.shape
    depth = norm_mix.shape[0]
    Tsp = SAMPLE_PAD
    Mp, Ms = Bp * Tp, Bs * Tsp
    M = Mp + Ms
    TM = _row_tile(M, 768, SUBLANES_BF16)
    TM_LORA = _row_tile(M, 384, SUBLANES_F32)
    TM_OUT = _row_tile(Mp, 512, SUBLANES_F32)
    TN = _tile(D, 2048)
    TN_QKV = _tile(b_w_qkv.shape[2], 2048)
    TF = _tile(ffn1_w_out.shape[1], 512)
    TB_A, TB_C = min(Tp, 256), min(Tp, 512)
    H_a = D // A_HEAD_DIM
    G, H_b, Dh = len(B_DILATIONS), B_HEADS, B_HEAD_DIM
    bf = lambda w: w.astype(BF16)

    x = jnp.concatenate([x_prompt.reshape(Mp, D),
                         jnp.pad(x_sample, ((0, 0), (0, Tsp - Ts), (0, 0))).reshape(Ms, D)], axis=0)

    def last_rows(t):
        return (jnp.stack([t[(b + 1) * Tp - 1] for b in range(Bp)]),
                jnp.stack([t[Mp + b * Tsp + Ts - 1] for b in range(Bs)]))

    qi = jnp.arange(B_BLK)[:, None]
    kj = jnp.arange(2 * B_BLK)[None, :]
    step = qi + B_BLK - kj
    step_ok = (step >= 0) & (step <= B_BLK)
    m_desc = B_BLK - jnp.arange(B_BLK + 8)
    bias_p, bias_s = [], []
    buckets = jnp.arange(N_BUCKETS)
    for gi, dil in enumerate(B_DILATIONS):
        tab = rel_bias[:, gi * H_b:(gi + 1) * H_b].astype(F32)
        hot = (_rel_bucket(jnp.clip(step, 0, B_BLK) * dil)[None] == buckets[:, None, None]).astype(F32)
        bp = jnp.einsum("nh,nqk->hqk", tab, hot, precision=HIGHEST)
        bias_p.append(jnp.where(step_ok[None], bp, -jnp.inf))
        bs = tab[_rel_bucket(jnp.maximum(m_desc, 0) * dil)]
        bias_s.append(jnp.broadcast_to(bs[:, :, None], (B_BLK + 8, H_b, Dh)))
    bias_p, bias_s = jnp.stack(bias_p), jnp.stack(bias_s)
    w_rkv = bf(a_w_rkv)

    outs_a_wkv, outs_a_shift, outs_c = ([], []), ([], []), ([], [], [], [], [], [])
    outs_b = [([], []) for _ in range(G)]
    v_first = None
    for i in range(depth):
        x = ffn(x, norm_ffn1[i], ffn1_w_in, ffn1_w_out, i, TM, TF)
        kind, j = i % 3, i // 3
        if kind == 0:
            u, u_prev = rmsnorm_shift(x, norm_mix[i], state_a_shift[j], Tp, Mp, Tsp, TM)
            mu = a_mu[j]
            rkv = rwkv_rkv(u, u_prev, mu[jnp.array([0, 2, 3])][:, None, :], w_rkv, j, TM, TN)
            lr = LANES
            w_br = (bf(_pad_cols(a_w1[j], lr)), bf(_pad_rows(a_w2[j], lr)), a_w0[j].reshape(1, D))
            a_br = (bf(_pad_cols(a_a1[j], lr)), bf(_pad_rows(a_a2[j], lr)), a_a0[j].reshape(1, D))
            g_br = (bf(a_g1[j]), bf(a_g2[j]))
            v_br = None
            if j > 0:
                v_br = (bf(_pad_cols(a_v1[j - 1], lr)), bf(_pad_rows(a_v2[j - 1], lr)), a_v0[j - 1].reshape(1, D))
            lora = rwkv_lora(u, u_prev, mu[jnp.array([1, 4, 5, 3])], w_br, a_br, g_br, v_br, TM_LORA)
            lw, a_lr, gate = lora[:3]
            vres = None if j == 0 else (v_first, lora[3])
            if j == 0:
                v_first = rkv
            par = tuple(p.reshape(1, D) for p in (a_k_k[j], a_k_a[j], a_r_k[j], a_gn_w[j], a_gn_b[j]))
            s0p = jnp.zeros((Bp * H_a, A_HEAD_DIM, A_HEAD_DIM), F32)
            s0s = state_a_wkv[j].reshape(Bs * H_a, A_HEAD_DIM, A_HEAD_DIM)
            yp, sp = rwkv_recurrence(rkv, lw, a_lr, gate, vres, par, s0p, Bp, Tp, 0, TB_A, None)
            ys, ss = rwkv_recurrence(rkv, lw, a_lr, gate, vres, par, s0s, Bs, Tsp, Mp, Tsp, Ts)
            x = matmul((yp, ys), bf(a_w_out[j]), TM, TN, residual=x)
            outs_a_wkv[0].append(sp.reshape(Bp, H_a, A_HEAD_DIM, A_HEAD_DIM))
            outs_a_wkv[1].append(ss.reshape(Bs, H_a, A_HEAD_DIM, A_HEAD_DIM))
            sh_p, sh_s = last_rows(u)
            outs_a_shift[0].append(sh_p)
            outs_a_shift[1].append(sh_s)
        elif kind == 1:
            qkv = matmul(x, bf(b_w_qkv[j]), TM, TN_QKV, norm_g=norm_mix[i])
            caches = [c[j].reshape(Bs, B_BLK, d, 2, H_b, Dh)
                      for c, d in zip((cache_b_kv_g0, cache_b_kv_g1, cache_b_kv_g2), B_DILATIONS)]
            op = attn_prompt(qkv, bias_p, Bp, Tp)
            os_ = attn_sample(qkv[Mp:], caches, bias_s, Bs, Tsp, 0, Ts)
            x = matmul((op, os_.reshape(Ms, H_b * Dh)), bf(b_w_out[j]), TM, TN, residual=x)
            for gi in range(G):
                keep = min(B_WINDOWS[gi], Tp)
                c0, c1 = (gi * 3 + 1) * H_b * Dh, (gi * 3 + 3) * H_b * Dh
                kv_p = jnp.stack([lax.slice(qkv, ((b + 1) * Tp - keep, c0), ((b + 1) * Tp, c1)) for b in range(Bp)])
                kv_s = lax.slice(qkv, (Mp, c0), (M, c1)).reshape(Bs, Tsp, c1 - c0)[:, :Ts]
                outs_b[gi][0].append(kv_p.reshape(Bp, keep, 2, H_b, Dh))
                outs_b[gi][1].append(kv_s.reshape(Bs, Ts, 2, H_b, Dh))
        else:
            H, E, V = C_HEADS, C_QK_DIM, C_V_DIM
            n_in = c_w_in.shape[2]
            n_pad = -(-n_in // LANES) * LANES
            proj = matmul(x, bf(_pad_cols(c_w_in[j], n_pad)), TM, _tile(n_pad, 1024), norm_g=norm_mix[i])
            gbias = _pad_cols(c_b_gates[j].reshape(1, 2 * H), LANES)
            nw = c_norm_w[j].reshape(1, H * V)
            zc = (jnp.zeros((Bp, H, V, E), F32), jnp.zeros((Bp, H, E), F32), jnp.zeros((Bp, H, LANES), F32))
            sc = (state_c_C[j], state_c_n[j], jnp.broadcast_to(state_c_m[j][:, :, None], (Bs, H, LANES)))
            hp, cp, np_, mp = mlstm_recurrence(proj, gbias, nw, *zc, Bp, Tp, 0, TB_C, None)
            hs, cs, ns, ms = mlstm_recurrence(proj, gbias, nw, *sc, Bs, Tsp, Mp, Tsp, Ts)
            x = matmul((hp, hs), bf(c_w_out[j]), TM, TN, residual=x)
            for lst, val in zip(outs_c, (cp, cs, np_, ns, mp[:, :, 0], ms[:, :, 0])):
                lst.append(val)
        x = ffn(x, norm_ffn2[i], ffn2_w_in, ffn2_w_out, i, TM, TF)

    y_prompt = rmsnorm(x, norm_final, TM_OUT, 0, Mp).reshape(Bp, Tp, D)
    y_sample = rmsnorm(x, norm_final, Ms, Mp, Ms).reshape(Bs, Tsp, D)[:, :Ts]
    st = jnp.stack
    return (y_prompt, y_sample, st(outs_a_wkv[0]), st(outs_a_wkv[1]), st(outs_a_shift[0]), st(outs_a_shift[1]),
            st(outs_b[0][0]), st(outs_b[0][1]), st(outs_b[1][0]), st(outs_b[1][1]), st(outs_b[2][0]), st(outs_b[2][1]),
            st(outs_c[0]), st(outs_c[1]), st(outs_c[2]), st(outs_c[3]), st(outs_c[4]), st(outs_c[5]))
```

```python
import functools
import math

import jax
import jax.numpy as jnp
from jax import lax
from jax.experimental import pallas as pl
from jax.experimental.pallas import tpu as pltpu

F32 = jnp.float32
BF16 = jnp.bfloat16
HIGHEST = lax.Precision.HIGHEST

NORM_EPS = 1e-6
A_HEAD_DIM = 64
A_GN_EPS = 64e-5
A_REC_LANES = 2048
B_WINDOWS = (128, 512, 2048)
B_DILATIONS = (1, 4, 16)
B_HEADS = 16
B_HEAD_DIM = 128
B_BLK = 128
B_UNROLL = 4
N_BUCKETS = 32
BUCKET_MAX_DIST = 2048
C_HEADS = 8
C_QK_DIM = 128
C_V_DIM = 256
C_GATE_CAP = 15.0
CHUNK = 64
SAMPLE_PAD = 8
LANES = 128
SUBLANES_F32 = 8
SUBLANES_BF16 = 16
VMEM_LIMIT = 56 * 1024 * 1024


def _row_dtype(rows):
    return BF16 if rows % SUBLANES_BF16 == 0 else F32


def _params(sem):
    return pltpu.CompilerParams(dimension_semantics=sem, vmem_limit_bytes=VMEM_LIMIT)


def _dot(a, b, precision=None):
    return jnp.dot(a, b, preferred_element_type=F32, precision=precision)


def _dot_nt(a, b):
    return lax.dot_general(a, b, (((1,), (1,)), ((), ())), preferred_element_type=F32)


def _dot_tn(a, b):
    return lax.dot_general(a, b, (((0,), (0,)), ((), ())), preferred_element_type=F32)


def _bf16_operands(dot):
    return lambda a, b: dot(a.astype(BF16), b.astype(BF16))


def _sigmoid(x):
    return 1.0 / (1.0 + jnp.exp(-x))


def _rms(x, g):
    ms = jnp.mean(x * x, axis=-1, keepdims=True)
    return x * lax.rsqrt(ms + NORM_EPS) * g


def _rmsnorm_body(x_ref, g_ref, o_ref):
    o_ref[...] = _rms(x_ref[...], g_ref[...]).astype(o_ref.dtype)


def rmsnorm(x, g, tm, row0, rows):
    D = x.shape[1]
    rb0 = row0 // tm
    return pl.pallas_call(
        _rmsnorm_body, grid=(rows // tm,),
        in_specs=[pl.BlockSpec((tm, D), lambda i: (rb0 + i, 0)), pl.BlockSpec((1, D), lambda i: (0, 0))],
        out_specs=pl.BlockSpec((tm, D), lambda i: (i, 0)),
        out_shape=jax.ShapeDtypeStruct((rows, D), F32),
        compiler_params=_params(("parallel",)), name="rmsnorm",
    )(x, g.reshape(1, D))


def _rmsnorm_shift_body(x_ref, xp_ref, g_ref, st_ref, u_ref, up_ref, *, tm, grp, seq_len, seq_rows):
    i = pl.program_id(0)
    g = g_ref[...]
    u = _rms(x_ref[...], g)
    u_ref[...] = u
    up_ref[...] = pltpu.roll(u, 1, axis=0)
    tail = _rms(xp_ref[...], g)[-1:, :]
    n_late = st_ref.shape[0]
    for k in range(tm // grp):
        row0 = i * tm + k * grp
        late = row0 >= seq_rows
        is_start = late | (lax.rem(row0, seq_len) == 0)
        state = st_ref[pl.ds(jnp.clip((row0 - seq_rows) // grp, 0, n_late - 1), 1), :]
        before = tail if k == 0 else u[k * grp - 1:k * grp, :]
        up_ref[k * grp:k * grp + 1, :] = jnp.where(is_start, jnp.where(late, state, 0.0), before)


def rmsnorm_shift(x, g, late_states, seq_len, seq_rows, grp, tm):
    M, D = x.shape
    sub = SUBLANES_F32
    return pl.pallas_call(
        functools.partial(_rmsnorm_shift_body, tm=tm, grp=grp, seq_len=seq_len, seq_rows=seq_rows), grid=(M // tm,),
        in_specs=[pl.BlockSpec((tm, D), lambda i: (i, 0)),
                  pl.BlockSpec((sub, D), lambda i: (jnp.maximum(i * (tm // sub) - 1, 0), 0)),
                  pl.BlockSpec((1, D), lambda i: (0, 0)),
                  pl.BlockSpec(late_states.shape, lambda i: (0, 0))],
        out_specs=[pl.BlockSpec((tm, D), lambda i: (i, 0))] * 2,
        out_shape=[jax.ShapeDtypeStruct((M, D), F32)] * 2,
        compiler_params=_params(("parallel",)), name="rmsnorm_shift",
    )(x, x, g.reshape(1, D), late_states)


def _ffn_body(x_ref, g_ref, wg_ref, wu_ref, wo_ref, o_ref, xn_ref):
    j = pl.program_id(1)

    @pl.when(j == 0)
    def _():
        xn_ref[...] = _rms(x_ref[...], g_ref[...]).astype(BF16)
        o_ref[...] = jnp.zeros_like(o_ref)

    xn = xn_ref[...]
    gate = _dot(xn, wg_ref[...].astype(BF16))
    up = _dot(xn, wu_ref[...].astype(BF16))
    h = (gate * _sigmoid(gate) * up).astype(BF16)
    o_ref[...] += _dot(h, wo_ref[...].astype(BF16))

    @pl.when(j == pl.num_programs(1) - 1)
    def _():
        o_ref[...] = x_ref[...] + 0.5 * o_ref[...]


def _ffn_piped_body(x_ref, g_ref, win_ref, wout_ref, o_ref, xn_ref, *, layer, nf, tf):
    D = x_ref.shape[1]
    xn_ref[...] = _rms(x_ref[...], g_ref[...]).astype(BF16)
    o_ref[...] = jnp.zeros_like(o_ref)

    def step(wg_ref, wu_ref, wo_ref):
        xn = xn_ref[...]
        gate = _dot(xn, wg_ref[...].astype(BF16))
        up = _dot(xn, wu_ref[...].astype(BF16))
        h = (gate * _sigmoid(gate) * up).astype(BF16)
        o_ref[...] += _dot(h, wo_ref[...].astype(BF16))

    deep = pl.Buffered(3)
    pltpu.emit_pipeline(
        step, grid=(nf,),
        in_specs=[pl.BlockSpec((None, D, tf), lambda j: (layer, 0, j), pipeline_mode=deep),
                  pl.BlockSpec((None, D, tf), lambda j: (layer, 0, j + nf), pipeline_mode=deep),
                  pl.BlockSpec((None, tf, D), lambda j: (layer, j, 0), pipeline_mode=deep)],
    )(win_ref, win_ref, wout_ref)
    o_ref[...] = x_ref[...] + 0.5 * o_ref[...]


def ffn_piped(x, g, w_in, w_out, layer, tm, tf):
    M, D = x.shape
    nf = w_out.shape[1] // tf
    return pl.pallas_call(
        functools.partial(_ffn_piped_body, layer=layer, nf=nf, tf=tf), grid=(M // tm,),
        in_specs=[pl.BlockSpec((tm, D), lambda i: (i, 0)), pl.BlockSpec((1, D), lambda i: (0, 0)),
                  pl.BlockSpec(memory_space=pl.ANY), pl.BlockSpec(memory_space=pl.ANY)],
        out_specs=pl.BlockSpec((tm, D), lambda i: (i, 0)),
        out_shape=jax.ShapeDtypeStruct((M, D), F32),
        scratch_shapes=[pltpu.VMEM((tm, D), BF16)],
        compiler_params=_params(("arbitrary",)), name="ffn_piped",
    )(x, g.reshape(1, D), w_in, w_out)


def ffn(x, g, w_in, w_out, layer, tm, tf):
    M, D = x.shape
    Fh = w_out.shape[1]
    nf = Fh // tf
    return pl.pallas_call(
        _ffn_body, grid=(M // tm, nf),
        in_specs=[pl.BlockSpec((tm, D), lambda i, j: (i, 0)),
                  pl.BlockSpec((1, D), lambda i, j: (0, 0)),
                  pl.BlockSpec((None, D, tf), lambda i, j: (layer, 0, j)),
                  pl.BlockSpec((None, D, tf), lambda i, j: (layer, 0, j + nf)),
                  pl.BlockSpec((None, tf, D), lambda i, j: (layer, j, 0))],
        out_specs=pl.BlockSpec((tm, D), lambda i, j: (i, 0)),
        out_shape=jax.ShapeDtypeStruct((M, D), F32),
        scratch_shapes=[pltpu.VMEM((tm, D), BF16)],
        compiler_params=_params(("parallel", "arbitrary")), name="ffn",
    )(x, g.reshape(1, D), w_in, w_in, w_out)


def _mm_body(*refs, nb0, has_norm, has_res):
    it = iter(refs)
    x_ref = next(it)
    x1_ref = next(it) if nb0 is not None else None
    g_ref = next(it) if has_norm else None
    w_ref = next(it)
    res_ref = next(it) if has_res else None
    o_ref = next(it)
    xs_ref = next(it)

    def stage(ref, rows=slice(None), dst=slice(None)):
        x = ref[rows, :].astype(F32)
        if has_norm:
            x = _rms(x, g_ref[...])
        xs_ref[dst, :] = x.astype(BF16)

    @pl.when(pl.program_id(1) == 0)
    def _():
        if nb0 is None:
            stage(x_ref)
        else:
            rem = xs_ref.shape[0] - x1_ref.shape[0]
            pl.when(pl.program_id(0) < nb0)(lambda: stage(x_ref))

            @pl.when(pl.program_id(0) == nb0)
            def _():
                if rem > 0:
                    stage(x_ref, slice(0, rem), slice(0, rem))
                stage(x1_ref, slice(None), slice(rem, None))

    acc = _dot(xs_ref[...], w_ref[...])
    if has_res:
        acc = res_ref[...] + acc
    o_ref[...] = acc.astype(o_ref.dtype)


def matmul(x, w, tm, tn, norm_g=None, residual=None, out_dtype=F32):
    nb0 = None
    if isinstance(x, tuple):
        x0, x1 = x
        nb0 = x0.shape[0] // tm
        M, K = x0.shape[0] + x1.shape[0], x0.shape[1]
        assert M == (nb0 + 1) * tm and x1.shape[0] <= tm and x1.shape[0] % SUBLANES_BF16 == 0
        args = [x0, x1]
        last0 = -(-x0.shape[0] // tm) - 1
        specs = [pl.BlockSpec((tm, K), lambda i, j: (jnp.minimum(i, last0), 0)),
                 pl.BlockSpec(x1.shape, lambda i, j: (0, 0))]
    else:
        M, K = x.shape
        args = [x]
        specs = [pl.BlockSpec((tm, K), lambda i, j: (i, 0))]
    N = w.shape[1]
    if norm_g is not None:
        args.append(norm_g.reshape(1, K))
        specs.append(pl.BlockSpec((1, K), lambda i, j: (0, 0)))
    args.append(w)
    specs.append(pl.BlockSpec((K, tn), lambda i, j: (0, j)))
    if residual is not None:
        args.append(residual)
        specs.append(pl.BlockSpec((tm, tn), lambda i, j: (i, j)))
    return pl.pallas_call(
        functools.partial(_mm_body, nb0=nb0, has_norm=norm_g is not None, has_res=residual is not None),
        grid=(M // tm, N // tn), in_specs=specs,
        out_specs=pl.BlockSpec((tm, tn), lambda i, j: (i, j)),
        out_shape=jax.ShapeDtypeStruct((M, N), out_dtype),
        scratch_shapes=[pltpu.VMEM((tm, K), BF16)],
        compiler_params=_params(("parallel", "arbitrary")), name="matmul",
    )(*args)


def _rkv_body(u_ref, up_ref, mu_ref, w_ref, o_ref, xs_ref):
    @pl.when(pl.program_id(2) == 0)
    def _():
        u = u_ref[...]
        xs_ref[...] = (u + (up_ref[...] - u) * mu_ref[...]).astype(BF16)

    o_ref[...] = _dot(xs_ref[...], w_ref[...])


def rwkv_rkv(u, u_prev, mu3, w, layer, tm, tn):
    M, D = u.shape
    return pl.pallas_call(
        _rkv_body, grid=(M // tm, 3, D // tn),
        in_specs=[pl.BlockSpec((tm, D), lambda i, k, j: (i, 0)),
                  pl.BlockSpec((tm, D), lambda i, k, j: (i, 0)),
                  pl.BlockSpec((None, 1, D), lambda i, k, j: (k, 0, 0)),
                  pl.BlockSpec((None, None, D, tn), lambda i, k, j: (layer, k, 0, j))],
        out_specs=pl.BlockSpec((None, tm, tn), lambda i, k, j: (k, i, j)),
        out_shape=jax.ShapeDtypeStruct((3, M, D), F32),
        scratch_shapes=[pltpu.VMEM((tm, D), BF16)],
        compiler_params=_params(("parallel", "arbitrary", "arbitrary")), name="rwkv_rkv",
    )(u, u_prev, mu3, w)


def _lora_body(*refs, has_vres):
    it = iter(refs)
    u_ref, up_ref, mu_ref = next(it), next(it), next(it)
    w1, w2, w0 = next(it), next(it), next(it)
    a1, a2, a0 = next(it), next(it), next(it)
    g1, g2 = next(it), next(it)
    if has_vres:
        v1, v2, v0 = next(it), next(it), next(it)
    lw_ref, a_ref, g_ref = next(it), next(it), next(it)
    nu_ref = next(it) if has_vres else None

    u = u_ref[...]
    du = up_ref[...] - u

    def mix(n):
        return (u + du * mu_ref[n:n + 1, :]).astype(BF16)

    hw = jnp.tanh(_dot(mix(0), w1[...])).astype(BF16)
    w_pre = w0[...] + _dot(hw, w2[...])
    softplus = jnp.maximum(-w_pre, 0.0) + jnp.log(1.0 + jnp.exp(-jnp.abs(w_pre)))
    lw_ref[...] = -jnp.exp(-softplus - 0.5)
    ha = _dot(mix(1), a1[...]).astype(BF16)
    a_ref[...] = _sigmoid(a0[...] + _dot(ha, a2[...]))
    hg = _sigmoid(_dot(mix(2), g1[...])).astype(BF16)
    g_ref[...] = _dot(hg, g2[...])
    if has_vres:
        hv = _dot(mix(3), v1[...]).astype(BF16)
        nu_ref[...] = _sigmoid(v0[...] + _dot(hv, v2[...]))


def rwkv_lora(u, u_prev, mu4, w, a, g, v, tm):
    M, D = u.shape
    has_vres = v is not None
    row = lambda i: (i, 0)
    full = lambda i: (0, 0)
    args = [u, u_prev, mu4]
    specs = [pl.BlockSpec((tm, D), row), pl.BlockSpec((tm, D), row), pl.BlockSpec(mu4.shape, full)]
    for t in (w, a, g) + ((v,) if has_vres else ()):
        for m in t:
            args.append(m)
            specs.append(pl.BlockSpec(m.shape, full))
    n_out = 4 if has_vres else 3
    return pl.pallas_call(
        functools.partial(_lora_body, has_vres=has_vres), grid=(M // tm,), in_specs=specs,
        out_specs=[pl.BlockSpec((tm, D), row)] * n_out,
        out_shape=[jax.ShapeDtypeStruct((M, D), F32)] * n_out,
        compiler_params=_params(("parallel",)), name="rwkv_lora",
    )(*args)


def _rwkv_rec_body(*refs, L, TB, hb, t_valid, has_vres):
    N = A_HEAD_DIM
    it = iter(refs)
    r_ref, k_ref, v_ref, lw_ref, a_ref, g_ref = (next(it) for _ in range(6))
    if has_vres:
        vf_ref, nu_ref = next(it), next(it)
    kk_ref, ka_ref, rk_ref, gnw_ref, gnb_ref, s0_ref = (next(it) for _ in range(6))
    y_ref, sT_ref, S_scr = next(it), next(it), next(it)
    tb = pl.program_id(2)

    assert L == N and 2 * N == LANES

    @pl.when(tb == 0)
    def _():
        for p in range(hb // 2):
            S_scr[p] = jnp.concatenate([s0_ref[2 * p], s0_ref[2 * p + 1]], axis=1)

    row = lax.broadcasted_iota(jnp.int32, (L, L), 0)
    col = lax.broadcasted_iota(jnp.int32, (L, L), 1)
    tril = (row >= col).astype(F32)
    row1 = lax.broadcasted_iota(jnp.int32, (L, 2 * N), 0)
    lane1 = lax.broadcasted_iota(jnp.int32, (L, 2 * N), 1)
    head0_lane = lane1 < N
    eye = ((lane1 & (N - 1)) == row1).astype(F32)
    row2 = lax.broadcasted_iota(jnp.int32, (2 * L, 2 * N), 0)
    lane2 = lax.broadcasted_iota(jnp.int32, (2 * L, 2 * N), 1)
    mask2 = (lane2 & (N - 1)) < jnp.where(row2 < L, row2, row2 - L + 1)
    bd_mask = (row2 // L) == (lane2 // N)
    ones_bd = bd_mask.astype(BF16)
    ones_bd2 = jnp.concatenate([ones_bd, ones_bd], axis=0)
    n_sq = int(math.log2(L)) - 1
    kk_p, ka_p, rk_p, gnw, gnb = kk_ref[...], ka_ref[...], rk_ref[...], gnw_ref[...], gnb_ref[...]

    def chunk(c, carry):
        sl = pl.ds(pl.multiple_of(c * L, L), L)

        def load(ref):
            if TB >= L:
                return ref[sl, :]
            return jnp.concatenate([ref[...], jnp.zeros((L - TB, ref.shape[1]), F32)], axis=0)

        r, k, v, lw, a, g = (load(ref) for ref in (r_ref, k_ref, v_ref, lw_ref, a_ref, g_ref))
        if has_vres:
            v = v + (load(vf_ref) - v) * load(nu_ref)
        if t_valid is not None:
            t_idx = tb * TB + c * L + lax.broadcasted_iota(jnp.int32, (L, 1), 0)
            valid = t_idx < t_valid
            r, k, v, lw = (jnp.where(valid, t, 0.0) for t in (r, k, v, lw))
        cum = _dot(tril, lw, HIGHEST)
        cum_end = cum[L - 1:L, :]
        w_cur, w_prev, w_inv, w_rem, w_end = (jnp.exp(cum), jnp.exp(cum - lw), jnp.exp(-cum),
                                              jnp.exp(cum_end - cum), jnp.exp(cum_end))
        mm, mm_nt, mm_tn = (_bf16_operands(d) for d in (_dot, _dot_nt, _dot_tn))
        pairs = range(hb // 2)
        ps = [slice(p * LANES, (p + 1) * LANES) for p in pairs]

        def head_sum(x):
            x_hi = x.astype(BF16)
            x_lo = (x - x_hi.astype(F32)).astype(BF16)
            return _dot(jnp.concatenate([x_hi, x_lo], axis=1), ones_bd2)

        def bdiag(x):
            return jnp.where(bd_mask, jnp.concatenate([x, x], axis=0), 0.0)

        kkp = [k[:, s] * kk_p[:, s] for s in ps]
        k2 = [k[:, s] * (1.0 + (a[:, s] - 1.0) * ka_p[:, s]) for s in ps]
        sums = [head_sum(jnp.concatenate([kkp[p] * kkp[p], r[:, ps[p]] * k2[p] * rk_p[:, ps[p]]], axis=0))
                for p in pairs]
        kk = [kkp[p] / jnp.maximum(jnp.sqrt(sums[p][:L]), 1e-12) for p in pairs]
        b = [kk[p] * a[:, ps[p]] for p in pairs]
        lhs2 = [jnp.concatenate([kk[p] * w_prev[:, ps[p]], r[:, ps[p]] * w_cur[:, ps[p]]], axis=0) for p in pairs]
        kd = [k2[p] * w_inv[:, ps[p]] for p in pairs]
        bd = [b[p] * w_inv[:, ps[p]] for p in pairs]
        kend = [k2[p] * w_rem[:, ps[p]] for p in pairs]
        bend = [b[p] * w_rem[:, ps[p]] for p in pairs]
        kkd = [x[:L] for x in lhs2]
        rd = [x[L:] for x in lhs2]
        a_kb2 = [mm_nt(lhs2[p], jnp.concatenate([bdiag(kd[p]), bdiag(bd[p])], axis=0)) for p in pairs]
        a_k = [jnp.where(mask2, x[:, :LANES], 0.0) for x in a_kb2]
        a_b = [jnp.where(mask2, x[:, LANES:], 0.0) for x in a_kb2]
        a_kb = [x[:L] for x in a_b]
        a_rb = [x[L:] for x in a_b]
        a_v = [mm(a_k[p], bdiag(v[:, ps[p]])) for p in pairs]
        t_inv = [eye - x for x in a_kb]
        pw = [mm(x, bdiag(x)) for x in a_kb]
        for _ in range(n_sq - 1):
            both = [mm(jnp.concatenate([pw[p], t_inv[p]], axis=0), bdiag(pw[p])) for p in pairs]
            t_inv = [t_inv[p] + both[p][L:] for p in pairs]
            pw = [x[:L] for x in both]
        t_inv = [t_inv[p] + mm(t_inv[p], bdiag(pw[p])) for p in pairs]
        ktcu = [mm(t_inv[p], jnp.concatenate([bdiag(kkd[p]), bdiag(a_v[p][:L])], axis=1)) for p in pairs]
        k_t = [x[:, :LANES] for x in ktcu]
        c_u = [x[:, LANES:] for x in ktcu]
        ykc = [mm(a_rb[p], jnp.concatenate([bdiag(k_t[p]), bdiag(c_u[p])], axis=1)) for p in pairs]
        y_k = [rd[p] - ykc[p][:, :LANES] for p in pairs]
        y_c = [a_v[p][L:] - ykc[p][:, LANES:] for p in pairs]
        S = [S_scr[p] for p in pairs]
        y = [mm_nt(y_k[p], bdiag(S[p])) + y_c[p] for p in pairs]
        upd = [mm_tn(jnp.concatenate([jnp.concatenate([v[:, ps[p]], jnp.zeros_like(k_t[p])], axis=1),
                                      jnp.concatenate([-c_u[p], k_t[p]], axis=1)], axis=0),
                     jnp.concatenate([kend[p], bend[p]], axis=0)) for p in pairs]
        S_c = [jnp.where(head0_lane, x[:N], x[N:LANES]) for x in upd]
        ktb = [jnp.where(bd_mask, x[LANES:], 0.0) for x in upd]
        for p in pairs:
            S_scr[p] = S[p] * w_end[:, ps[p]] - mm(S[p], ktb[p]) + S_c[p]
        mean = [head_sum(y[p]) * (1.0 / N) for p in pairs]
        var = [head_sum(jnp.square(y[p] - mean[p])) * (1.0 / N) for p in pairs]
        bonus = [sums[p][L:] * v[:, ps[p]] for p in pairs]
        outs = [((y[p] - mean[p]) * lax.rsqrt(var[p] + A_GN_EPS) * gnw[:, ps[p]] + gnb[:, ps[p]] + bonus[p])
                * g[:, ps[p]] for p in pairs]
        y_out = jnp.concatenate(outs, axis=-1).astype(y_ref.dtype)
        if TB >= L:
            y_ref[sl, :] = y_out
        else:
            y_ref[...] = y_out[:TB]
        return carry

    lax.fori_loop(0, max(TB // L, 1), chunk, 0)

    @pl.when(tb == pl.num_programs(2) - 1)
    def _():
        for p in range(hb // 2):
            S = S_scr[p]
            sT_ref[2 * p] = S[:, :N]
            sT_ref[2 * p + 1] = S[:, N:]


def rwkv_recurrence(rkv, lw, a, g, vres, params, s0, n_seq, T, row0, TB, t_valid):
    _, M, D = rkv.shape
    N = A_HEAD_DIM
    LW = min(A_REC_LANES, D)
    hb = LW // N
    nb = T // TB
    rb0 = row0 // TB
    has_vres = vres is not None
    seq = lambda b, h, t: (rb0 + b * nb + t, h)
    args, specs = [], []
    for n in range(3):
        args.append(rkv)
        specs.append(pl.BlockSpec((None, TB, LW), lambda b, h, t, n=n: (n, rb0 + b * nb + t, h)))
    for x in (lw, a, g):
        args.append(x)
        specs.append(pl.BlockSpec((TB, LW), seq))
    if has_vres:
        args += list(vres)
        specs += [pl.BlockSpec((None, TB, LW), lambda b, h, t: (2, rb0 + b * nb + t, h)), pl.BlockSpec((TB, LW), seq)]
    for p in params:
        args.append(p)
        specs.append(pl.BlockSpec((1, LW), lambda b, h, t: (0, h)))
    args.append(s0)
    specs.append(pl.BlockSpec((hb, N, N), lambda b, h, t: (b * (D // LW) + h, 0, 0)))
    return pl.pallas_call(
        functools.partial(_rwkv_rec_body, L=CHUNK, TB=TB, hb=hb, t_valid=t_valid, has_vres=has_vres),
        grid=(n_seq, D // LW, nb), in_specs=specs,
        out_specs=[pl.BlockSpec((TB, LW), lambda b, h, t: (b * nb + t, h)),
                   pl.BlockSpec((hb, N, N), lambda b, h, t: (b * (D // LW) + h, 0, 0))],
        out_shape=[jax.ShapeDtypeStruct((n_seq * T, D), _row_dtype(TB)),
                   jax.ShapeDtypeStruct((n_seq * (D // N), N, N), F32)],
        scratch_shapes=[pltpu.VMEM((hb // 2, N, 2 * N), F32)],
        compiler_params=_params(("parallel", "parallel", "arbitrary")), name="rwkv_recurrence",
    )(*args)


def _rel_bucket(dist):
    exact = N_BUCKETS // 2
    d = jnp.maximum(dist, 1).astype(F32)
    log_b = exact + (jnp.log(d / exact) / math.log(BUCKET_MAX_DIST / exact) * (N_BUCKETS - exact)).astype(jnp.int32)
    return jnp.where(dist < exact, dist, jnp.minimum(log_b, N_BUCKETS - 1))


def _attn_prompt_body(q_ref, k_ref, v_ref, bias_ref, o_ref, acc_ref, m_ref, l_ref, kd_ref, vd_ref, *, T):
    step = pl.program_id(2)
    blk = B_BLK
    G = len(B_DILATIONS)
    scale = B_HEAD_DIM ** -0.5
    first_keys = lax.broadcasted_iota(jnp.int32, (blk, 2 * blk), 1) < blk
    ones_cols = jnp.ones((2 * blk, B_HEAD_DIM), BF16)

    for si, dil in enumerate(reversed(B_DILATIONS)):
        @pl.when(step == si)
        def _(gi=si, dil=dil):
            span = blk * dil
            unroll = B_UNROLL if gi > 0 else 2 * B_UNROLL
            res_rows = T // dil + blk
            bias = bias_ref[...]

            def where(idx):
                n = idx // dil
                r = idx - n * dil
                return n, n * span + r, pl.multiple_of(r * res_rows + n * blk, blk)

            for r in range(dil):
                kd_ref[r * res_rows:r * res_rows + blk, :] = jnp.zeros((blk, B_HEAD_DIM), BF16)
                vd_ref[r * res_rows:r * res_rows + blk, :] = jnp.zeros((blk, B_HEAD_DIM), BF16)

            def stage(it, carry):
                for u in range(unroll):
                    _, start, dst = where(it * unroll + u)
                    rows = pl.ds(start, blk, stride=dil)
                    kd_ref[pl.ds(dst + blk, blk), :] = k_ref[rows, :].astype(BF16)
                    vd_ref[pl.ds(dst + blk, blk), :] = v_ref[rows, :].astype(BF16)
                return carry

            lax.fori_loop(0, T // (blk * unroll), stage, 0)

            def blocks(it, carry):
                us = range(unroll)
                pos = [where(it * unroll + u) for u in us]
                n = [x[0] for x in pos]
                cur = [pl.ds(x[1], blk, stride=dil) for x in pos]
                q = [(q_ref[cur[u], :] * scale).astype(BF16) for u in us]
                kcat = [kd_ref[pl.ds(x[2], 2 * blk), :] for x in pos]
                vcat = [vd_ref[pl.ds(x[2], 2 * blk), :] for x in pos]
                logits = [_dot_nt(q[u], kcat[u]) + bias for u in us]
                logits = [jnp.where(first_keys & (n[u] == 0), -jnp.inf, logits[u]) for u in us]
                mx = [jnp.max(x, axis=-1, keepdims=True) for x in logits]
                p = [jnp.exp(logits[u] - mx[u]) for u in us]
                pvd = [_dot(p[u].astype(BF16), jnp.concatenate([vcat[u], ones_cols], axis=1)) for u in us]
                pv = [x[:, :B_HEAD_DIM] for x in pvd]
                den = [x[:, B_HEAD_DIM:] for x in pvd]
                if gi > 0:
                    m_old = [m_ref[cur[u], :] for u in us]
                    l_old = [l_ref[cur[u], :] for u in us]
                    acc_old = [acc_ref[cur[u], :] for u in us]
                    m_new = [jnp.maximum(m_old[u], mx[u]) for u in us]
                    c_old = [jnp.exp(m_old[u] - m_new[u]) for u in us]
                    c_new = [jnp.exp(mx[u] - m_new[u]) for u in us]
                    pv = [acc_old[u] * c_old[u] + pv[u] * c_new[u] for u in us]
                    den = [l_old[u] * c_old[u] + den[u] * c_new[u] for u in us]
                    mx = m_new
                for u in us:
                    acc_ref[cur[u], :] = pv[u]
                    m_ref[cur[u], :] = mx[u]
                    l_ref[cur[u], :] = den[u]
                return carry

            lax.fori_loop(0, T // (blk * unroll), blocks, 0)

    @pl.when(step == G - 1)
    def _():
        o_ref[...] = (acc_ref[...] / l_ref[...]).astype(o_ref.dtype)


def attn_prompt(qkv, bias, n_seq, T):
    H, Dh, G = B_HEADS, B_HEAD_DIM, len(B_DILATIONS)

    def col(which):
        return lambda b, h, s: (b, ((G - 1 - s) * 3 + which) * H + h)

    staged_rows = T + B_BLK * max(B_DILATIONS)
    return pl.pallas_call(
        functools.partial(_attn_prompt_body, T=T), grid=(n_seq, H, G),
        in_specs=[pl.BlockSpec((T, Dh), col(0)), pl.BlockSpec((T, Dh), col(1)), pl.BlockSpec((T, Dh), col(2)),
                  pl.BlockSpec((None, None, B_BLK, 2 * B_BLK), lambda b, h, s: (G - 1 - s, h, 0, 0))],
        out_specs=pl.BlockSpec((T, Dh), lambda b, h, s: (b, h)),
        out_shape=jax.ShapeDtypeStruct((n_seq * T, H * Dh), BF16),
        scratch_shapes=[pltpu.VMEM((T, Dh), F32), pltpu.VMEM((T, 1), F32), pltpu.VMEM((T, Dh), F32),
                        pltpu.VMEM((staged_rows, Dh), BF16), pltpu.VMEM((staged_rows, Dh), BF16)],
        compiler_params=_params(("parallel", "parallel", "arbitrary")), name="attn_prompt",
    )(qkv, qkv, qkv, bias)


def _attn_sample_body(q_ref, k_ref, v_ref, c0_ref, c1_ref, c2_ref, bias_ref, o_ref, *, t_valid):
    blk = B_BLK
    scale = B_HEAD_DIM ** -0.5
    caches = (c0_ref, c1_ref, c2_ref)
    o_ref[...] = jnp.zeros_like(o_ref)
    for t in range(t_valid):
        m_run = l_run = acc = None
        for gi, dil in enumerate(B_DILATIONS):
            q = q_ref[t, gi] * scale
            c_ref = caches[gi]
            if dil == 1:
                kcat = jnp.concatenate([c_ref[t:, 0, 0], k_ref[:t + 1, gi]], axis=0)
                vcat = jnp.concatenate([c_ref[t:, 0, 1], v_ref[:t + 1, gi]], axis=0)
            else:
                kcat = jnp.concatenate([c_ref[:, t, 0], k_ref[t:t + 1, gi]], axis=0)
                vcat = jnp.concatenate([c_ref[:, t, 1], v_ref[t:t + 1, gi]], axis=0)
            logits = jnp.sum(q[None] * kcat, axis=-1, keepdims=True) + bias_ref[gi, :blk + 1]
            mx = jnp.max(logits, axis=0)
            p = jnp.exp(logits - mx[None])
            den = jnp.sum(p, axis=0)
            pv = jnp.sum(p * vcat, axis=0)
            if gi == 0:
                m_run, l_run, acc = mx, den, pv
            else:
                m_new = jnp.maximum(m_run, mx)
                c_old, c_new = jnp.exp(m_run - m_new), jnp.exp(mx - m_new)
                acc = acc * c_old + pv * c_new
                l_run = l_run * c_old + den * c_new
                m_run = m_new
        o_ref[t] = (acc / l_run).astype(o_ref.dtype)


def attn_sample(qkv, caches, bias, n_seq, T, row0, t_valid):
    H, Dh, G = B_HEADS, B_HEAD_DIM, len(B_DILATIONS)
    rb0 = row0 // T
    assert t_valid <= min(d for d in B_DILATIONS if d > 1)
    q5 = qkv.reshape(qkv.shape[0], G, 3, H, Dh)

    def spec(which):
        return pl.BlockSpec((T, G, None, H, Dh), lambda b: (b + rb0, 0, which, 0, 0))

    cache_specs = [pl.BlockSpec((None, B_BLK, min(d, t_valid), 2, H, Dh), lambda b: (b, 0, 0, 0, 0, 0))
                   for d in B_DILATIONS]
    return pl.pallas_call(
        functools.partial(_attn_sample_body, t_valid=t_valid), grid=(n_seq,),
        in_specs=[spec(0), spec(1), spec(2)] + cache_specs + [pl.BlockSpec(bias.shape, lambda b: (0, 0, 0, 0))],
        out_specs=pl.BlockSpec((T, H, Dh), lambda b: (b, 0, 0)),
        out_shape=jax.ShapeDtypeStruct((n_seq * T, H, Dh), BF16),
        compiler_params=_params(("parallel",)), name="attn_sample",
    )(q5, q5, q5, *caches, bias)


def _mlstm_body(q_ref, k_ref, v_ref, o_ref, gate_ref, gb_ref, nw_ref, c0_ref, n0_ref, m0_ref,
                y_ref, cT_ref, nT_ref, mT_ref, C_scr, n_scr, m_scr, *, L, TB, t_valid):
    H, E, V = C_HEADS, C_QK_DIM, C_V_DIM
    tb = pl.program_id(1)

    @pl.when(tb == 0)
    def _():
        C_scr[...] = c0_ref[...]
        n_scr[...] = n0_ref[...]
        m_scr[...] = m0_ref[...]

    row = lax.broadcasted_iota(jnp.int32, (L, L), 0)
    col = lax.broadcasted_iota(jnp.int32, (L, L), 1)
    causal = row >= col
    tril = causal.astype(F32)
    gb = gb_ref[...]
    nw = nw_ref[...]

    def chunk(c, carry):
        sl = pl.ds(pl.multiple_of(c * L, L), L)

        def load(ref, cols=slice(None)):
            if TB >= L:
                return ref[sl, cols]
            x = ref[:, cols]
            return jnp.concatenate([x, jnp.zeros((L - TB, x.shape[1]), F32)], axis=0)

        gact = C_GATE_CAP * jnp.tanh((load(gate_ref) + gb) / C_GATE_CAP)
        lf = jnp.minimum(gact, 0.0) - jnp.log(1.0 + jnp.exp(-jnp.abs(gact)))
        ig = gact
        valid = None
        if t_valid is not None:
            t_idx = tb * TB + c * L + lax.broadcasted_iota(jnp.int32, (L, 1), 0)
            valid = t_idx < t_valid
            ig = jnp.where(valid, ig, -1e30)
            lf = jnp.where(valid, lf, 0.0)
        bcum = _dot(tril, lf, HIGHEST)
        ig_t = ig.T
        bcum_t = bcum.T
        heads = range(H)
        es = [slice(h * E, (h + 1) * E) for h in heads]
        vs = [slice(h * V, (h + 1) * V) for h in heads]
        b_col = [bcum[:, H + h:H + h + 1] for h in heads]
        b_row = [bcum_t[H + h:H + h + 1, :] for h in heads]
        ig_col = [ig[:, h:h + 1] for h in heads]
        ig_row = [ig_t[h:h + 1, :] for h in heads]
        q = [load(q_ref, s) for s in es]
        k = [load(k_ref, s) * (E ** -0.5) for s in es]
        v = [load(v_ref, s) for s in vs]
        if valid is not None:
            q, k, v = ([jnp.where(valid, t, 0.0) for t in ts] for ts in (q, k, v))
        m_prev = [m_scr[h:h + 1, 0:1] for h in heads]
        n_prev = [n_scr[h:h + 1, :] for h in heads]
        C = [C_scr[h] for h in heads]
        dm = [jnp.where(causal, b_col[h] - b_row[h] + ig_row[h], -jnp.inf) for h in heads]
        inter = [b_col[h] + m_prev[h] for h in heads]
        mt = [jnp.maximum(inter[h], jnp.max(dm[h], axis=-1, keepdims=True)) for h in heads]
        w_d = [jnp.exp(dm[h] - mt[h]) for h in heads]
        w_i = [jnp.exp(inter[h] - mt[h]) for h in heads]
        qb, kb, vb = ([t.astype(BF16) for t in ts] for ts in (q, k, v))
        sc = [_dot_nt(qb[h], kb[h]) * w_d[h] for h in heads]
        qc = [_dot_nt(qb[h], C[h].astype(BF16)) for h in heads]
        num = [_dot(sc[h].astype(BF16), vb[h]) + w_i[h] * qc[h] for h in heads]
        den = [jnp.sum(sc[h], axis=-1, keepdims=True) + w_i[h] * jnp.sum(q[h] * n_prev[h], axis=-1, keepdims=True)
               for h in heads]
        hh = [num[h] / jnp.maximum(jnp.abs(den[h]), jnp.exp(-mt[h])) for h in heads]
        m_new = [x[L - 1:L, :] for x in mt]
        b_end = [x[L - 1:L, :] for x in b_col]
        w_s = [jnp.exp(b_end[h] - b_col[h] + ig_col[h] - m_new[h]) for h in heads]
        dec = [jnp.exp(b_end[h] + m_prev[h] - m_new[h]) for h in heads]
        c_upd = [_dot_tn((w_s[h] * v[h]).astype(BF16), kb[h]) for h in heads]
        outs = []
        for h in heads:
            C_scr[h] = dec[h] * C[h] + c_upd[h]
            n_scr[h:h + 1, :] = dec[h] * n_prev[h] + jnp.sum(w_s[h] * k[h], axis=0, keepdims=True)
            m_scr[h:h + 1, :] = jnp.broadcast_to(m_new[h], (1, LANES))
            hn = hh[h] * lax.rsqrt(jnp.mean(hh[h] * hh[h], axis=-1, keepdims=True) + NORM_EPS) * nw[:, vs[h]]
            outs.append(hn * _sigmoid(load(o_ref, vs[h])))
        y_out = jnp.concatenate(outs, axis=-1).astype(y_ref.dtype)
        if TB >= L:
            y_ref[sl, :] = y_out
        else:
            y_ref[...] = y_out[:TB]
        return carry

    lax.fori_loop(0, max(TB // L, 1), chunk, 0)

    @pl.when(tb == pl.num_programs(1) - 1)
    def _():
        cT_ref[...] = C_scr[...]
        nT_ref[...] = n_scr[...]
        mT_ref[...] = m_scr[...]


def mlstm_recurrence(proj, gate_bias, norm_w, c0, n0, m0, n_seq, T, row0, TB, t_valid):
    H, E, V = C_HEADS, C_QK_DIM, C_V_DIM
    nb = T // TB
    rb0 = row0 // TB
    HE, HV = H * E, H * V

    def cols(cb):
        return lambda b, t: (rb0 + b * nb + t, cb)

    st4 = lambda b, t: (b, 0, 0, 0)
    st3 = lambda b, t: (b, 0, 0)
    return pl.pallas_call(
        functools.partial(_mlstm_body, L=CHUNK, TB=TB, t_valid=t_valid), grid=(n_seq, nb),
        in_specs=[pl.BlockSpec((TB, HE), cols(0)), pl.BlockSpec((TB, HE), cols(1)),
                  pl.BlockSpec((TB, HV), cols(2 * HE // HV)), pl.BlockSpec((TB, HV), cols(2 * HE // HV + 1)),
                  pl.BlockSpec((TB, LANES), cols((2 * HE + 2 * HV) // LANES)),
                  pl.BlockSpec((1, LANES), lambda b, t: (0, 0)), pl.BlockSpec((1, HV), lambda b, t: (0, 0)),
                  pl.BlockSpec((None, H, V, E), st4), pl.BlockSpec((None, H, E), st3),
                  pl.BlockSpec((None, H, LANES), st3)],
        out_specs=[pl.BlockSpec((TB, HV), lambda b, t: (b * nb + t, 0)),
                   pl.BlockSpec((None, H, V, E), st4), pl.BlockSpec((None, H, E), st3),
                   pl.BlockSpec((None, H, LANES), st3)],
        out_shape=[jax.ShapeDtypeStruct((n_seq * T, HV), _row_dtype(TB)), jax.ShapeDtypeStruct((n_seq, H, V, E), F32),
                   jax.ShapeDtypeStruct((n_seq, H, E), F32), jax.ShapeDtypeStruct((n_seq, H, LANES), F32)],
        scratch_shapes=[pltpu.VMEM((H, V, E), F32), pltpu.VMEM((H, E), F32), pltpu.VMEM((H, LANES), F32)],
        compiler_params=_params(("parallel", "arbitrary")), name="mlstm_recurrence",
    )(proj, proj, proj, proj, proj, gate_bias, norm_w, c0, n0, m0)


def _row_tile(m, target, mult):
    return max(t for t in range(mult, min(m, target) + 1, mult) if m % t == 0)


def _tile(n, target):
    return max(t for t in range(LANES, min(n, target) + 1, LANES) if n % t == 0)


def _pad_cols(w, n):
    return jnp.pad(w, ((0, 0), (0, n - w.shape[1])))


def _pad_rows(w, n):
    return jnp.pad(w, ((0, n - w.shape[0]), (0, 0)))


def kernel(x_prompt, x_sample, state_a_wkv, state_a_shift, cache_b_kv_g0, cache_b_kv_g1, cache_b_kv_g2, state_c_C, state_c_n, state_c_m, rel_bias, norm_ffn1, ffn1_w_in, ffn1_w_out, norm_mix, norm_ffn2, ffn2_w_in, ffn2_w_out, norm_final, a_mu, a_w_rkv, a_w0, a_w1, a_w2, a_a0, a_a1, a_a2, a_g1, a_g2, a_k_k, a_k_a, a_r_k, a_gn_w, a_gn_b, a_w_out, a_v0, a_v1, a_v2, b_w_qkv, b_w_out, c_w_in, c_b_gates, c_norm_w, c_w_out):
    Bp, Tp, D = x_prompt.shape
    Bs, Ts, _ = x_sample.shape
    depth = norm_mix.shape[0]
    Tsp = SAMPLE_PAD
    Mp, Ms = Bp * Tp, Bs * Tsp
    M = Mp + Ms
    TM = _row_tile(M, 768, SUBLANES_BF16)
    TM_LORA = _row_tile(M, 384, SUBLANES_F32)
    TM_OUT = _row_tile(Mp, 512, SUBLANES_F32)
    TN = _tile(D, 2048)
    TN_QKV = _tile(b_w_qkv.shape[2], 2048)
    TF = _tile(ffn1_w_out.shape[1], 512)
    TB_A, TB_C = min(Tp, 256), min(Tp, 512)
    H_a = D // A_HEAD_DIM
    G, H_b, Dh = len(B_DILATIONS), B_HEADS, B_HEAD_DIM
    bf = lambda w: w.astype(BF16)

    x = jnp.concatenate([x_prompt.reshape(Mp, D),
                         jnp.pad(x_sample, ((0, 0), (0, Tsp - Ts), (0, 0))).reshape(Ms, D)], axis=0)

    def last_rows(t):
        return (jnp.stack([t[(b + 1) * Tp - 1] for b in range(Bp)]),
                jnp.stack([t[Mp + b * Tsp + Ts - 1] for b in range(Bs)]))

    qi = jnp.arange(B_BLK)[:, None]
    kj = jnp.arange(2 * B_BLK)[None, :]
    step = qi + B_BLK - kj
    step_ok = (step >= 0) & (step <= B_BLK)
    m_desc = B_BLK - jnp.arange(B_BLK + 8)
    bias_p, bias_s = [], []
    buckets = jnp.arange(N_BUCKETS)
    for gi, dil in enumerate(B_DILATIONS):
        tab = rel_bias[:, gi * H_b:(gi + 1) * H_b].astype(F32)
        hot = (_rel_bucket(jnp.clip(step, 0, B_BLK) * dil)[None] == buckets[:, None, None]).astype(F32)
        bp = jnp.einsum("nh,nqk->hqk", tab, hot, precision=HIGHEST)
        bias_p.append(jnp.where(step_ok[None], bp, -jnp.inf))
        bs = tab[_rel_bucket(jnp.maximum(m_desc, 0) * dil)]
        bias_s.append(jnp.broadcast_to(bs[:, :, None], (B_BLK + 8, H_b, Dh)))
    bias_p, bias_s = jnp.stack(bias_p), jnp.stack(bias_s)
    w_rkv = bf(a_w_rkv)

    outs_a_wkv, outs_a_shift, outs_c = ([], []), ([], []), ([], [], [], [], [], [])
    outs_b = [([], []) for _ in range(G)]
    v_first = None
    for i in range(depth):
        x = ffn_piped(x, norm_ffn1[i], ffn1_w_in, ffn1_w_out, i, TM, TF // 2)
        kind, j = i % 3, i // 3
        if kind == 0:
            u, u_prev = rmsnorm_shift(x, norm_mix[i], state_a_shift[j], Tp, Mp, Tsp, TM)
            mu = a_mu[j]
            rkv = rwkv_rkv(u, u_prev, mu[jnp.array([0, 2, 3])][:, None, :], w_rkv, j, TM, TN)
            lr = LANES
            w_br = (bf(_pad_cols(a_w1[j], lr)), bf(_pad_rows(a_w2[j], lr)), a_w0[j].reshape(1, D))
            a_br = (bf(_pad_cols(a_a1[j], lr)), bf(_pad_rows(a_a2[j], lr)), a_a0[j].reshape(1, D))
            g_br = (bf(a_g1[j]), bf(a_g2[j]))
            v_br = None
            if j > 0:
                v_br = (bf(_pad_cols(a_v1[j - 1], lr)), bf(_pad_rows(a_v2[j - 1], lr)), a_v0[j - 1].reshape(1, D))
            lora = rwkv_lora(u, u_prev, mu[jnp.array([1, 4, 5, 3])], w_br, a_br, g_br, v_br, TM_LORA)
            lw, a_lr, gate = lora[:3]
            vres = None if j == 0 else (v_first, lora[3])
            if j == 0:
                v_first = rkv
            par = tuple(p.reshape(1, D) for p in (a_k_k[j], a_k_a[j], a_r_k[j], a_gn_w[j], a_gn_b[j]))
            s0p = jnp.zeros((Bp * H_a, A_HEAD_DIM, A_HEAD_DIM), F32)
            s0s = state_a_wkv[j].reshape(Bs * H_a, A_HEAD_DIM, A_HEAD_DIM)
            yp, sp = rwkv_recurrence(rkv, lw, a_lr, gate, vres, par, s0p, Bp, Tp, 0, TB_A, None)
            ys, ss = rwkv_recurrence(rkv, lw, a_lr, gate, vres, par, s0s, Bs, Tsp, Mp, Tsp, Ts)
            x = matmul((yp, ys), bf(a_w_out[j]), TM, TN, residual=x)
            outs_a_wkv[0].append(sp.reshape(Bp, H_a, A_HEAD_DIM, A_HEAD_DIM))
            outs_a_wkv[1].append(ss.reshape(Bs, H_a, A_HEAD_DIM, A_HEAD_DIM))
            sh_p, sh_s = last_rows(u)
            outs_a_shift[0].append(sh_p)
            outs_a_shift[1].append(sh_s)
        elif kind == 1:
            qkv = matmul(x, bf(b_w_qkv[j]), TM, TN_QKV, norm_g=norm_mix[i])
            caches = [c[j].reshape(Bs, B_BLK, d, 2, H_b, Dh)
                      for c, d in zip((cache_b_kv_g0, cache_b_kv_g1, cache_b_kv_g2), B_DILATIONS)]
            op = attn_prompt(qkv, bias_p, Bp, Tp)
            os_ = attn_sample(qkv[Mp:], caches, bias_s, Bs, Tsp, 0, Ts)
            x = matmul((op, os_.reshape(Ms, H_b * Dh)), bf(b_w_out[j]), TM, TN, residual=x)
            for gi in range(G):
                keep = min(B_WINDOWS[gi], Tp)
                c0, c1 = (gi * 3 + 1) * H_b * Dh, (gi * 3 + 3) * H_b * Dh
                kv_p = jnp.stack([lax.slice(qkv, ((b + 1) * Tp - keep, c0), ((b + 1) * Tp, c1)) for b in range(Bp)])
                kv_s = lax.slice(qkv, (Mp, c0), (M, c1)).reshape(Bs, Tsp, c1 - c0)[:, :Ts]
                outs_b[gi][0].append(kv_p.reshape(Bp, keep, 2, H_b, Dh))
                outs_b[gi][1].append(kv_s.reshape(Bs, Ts, 2, H_b, Dh))
        else:
            H, E, V = C_HEADS, C_QK_DIM, C_V_DIM
            n_in = c_w_in.shape[2]
            n_pad = -(-n_in // LANES) * LANES
            proj = matmul(x, bf(_pad_cols(c_w_in[j], n_pad)), TM, _tile(n_pad, 1024), norm_g=norm_mix[i])
            gbias = _pad_cols(c_b_gates[j].reshape(1, 2 * H), LANES)
            nw = c_norm_w[j].reshape(1, H * V)
            zc = (jnp.zeros((Bp, H, V, E), F32), jnp.zeros((Bp, H, E), F32), jnp.zeros((Bp, H, LANES), F32))
            sc = (state_c_C[j], state_c_n[j], jnp.broadcast_to(state_c_m[j][:, :, None], (Bs, H, LANES)))
            hp, cp, np_, mp = mlstm_recurrence(proj, gbias, nw, *zc, Bp, Tp, 0, TB_C, None)
            hs, cs, ns, ms = mlstm_recurrence(proj, gbias, nw, *sc, Bs, Tsp, Mp, Tsp, Ts)
            x = matmul((hp, hs), bf(c_w_out[j]), TM, TN, residual=x)
            for lst, val in zip(outs_c, (cp, cs, np_, ns, mp[:, :, 0], ms[:, :, 0])):
                lst.append(val)
        x = ffn_piped(x, norm_ffn2[i], ffn2_w_in, ffn2_w_out, i, TM, TF // 2)

    y_prompt = rmsnorm(x, norm_final, TM_OUT, 0, Mp).reshape(Bp, Tp, D)
    y_sample = rmsnorm(x, norm_final, Ms, Mp, Ms).reshape(Bs, Tsp, D)[:, :Ts]
    st = jnp.stack
    return (y_prompt, y_sample, st(outs_a_wkv[0]), st(outs_a_wkv[1]), st(outs_a_shift[0]), st(outs_a_shift[1]),
            st(outs_b[0][0]), st(outs_b[0][1]), st(outs_b[1][0]), st(outs_b[1][1]), st(outs_b[2][0]), st(outs_b[2][1]),
            st(outs_c[0]), st(outs_c[1]), st(outs_c[2]), st(outs_c[3]), st(outs_c[4]), st(outs_c[5]))
```
